```python
import math
import jax, jax.numpy as jnp
from jax import lax
import numpy as np

D_MODEL = 1024
BATCH = 2
SEQ = 8192
DEPTH = 1
DEC_BATCH = 4
DEC_SEQ = 4096
PAST_LEN = 128

ATTN_HEADS = 8
QK_DIM = 64
V_DIM = 2 * QK_DIM
ATTN_WIDTH = ATTN_HEADS * V_DIM
ROPE_THETA = 10000.0
Q_BLOCK = 128

SSD_EXPAND = 2
D_INNER = SSD_EXPAND * D_MODEL
SSD_HEAD_DIM = 64
SSD_HEADS = D_INNER // SSD_HEAD_DIM
SSD_GROUPS = 4
D_STATE = 128
CONV_K = 5
CONV_CH = D_INNER + 2 * SSD_GROUPS * D_STATE
CHUNK = 128

N_EXPERT_GROUPS = 4
EXPERTS_PER_GROUP = 8
N_EXPERTS = N_EXPERT_GROUPS * EXPERTS_PER_GROUP
TOP_K_INNER = 2
EXPERT_FF = 512
MOE_BLOCK = 128

RMS_EPS = 1e-6

Q_COLS = ATTN_HEADS * 2 * QK_DIM
K_COLS = ATTN_HEADS * 2 * QK_DIM
V_COLS = ATTN_WIDTH
Z_COLS = D_INNER
XBC_COLS = CONV_CH
DT_COLS = 2 * SSD_HEADS
GATE_COLS = 2 * D_MODEL
IN_COLS = Q_COLS + K_COLS + V_COLS + Z_COLS + XBC_COLS + DT_COLS + GATE_COLS
IN_SPLITS = [Q_COLS,
             Q_COLS + K_COLS,
             Q_COLS + K_COLS + V_COLS,
             Q_COLS + K_COLS + V_COLS + Z_COLS,
             Q_COLS + K_COLS + V_COLS + Z_COLS + XBC_COLS,
             Q_COLS + K_COLS + V_COLS + Z_COLS + XBC_COLS + DT_COLS]

kernel_name = "hybrid_diffattn_ssd_hmoe_encoder"


def rmsnorm(x, g):
    xf = x.astype(jnp.float32)
    r = lax.rsqrt(jnp.mean(xf * xf, axis=-1, keepdims=True) + RMS_EPS)
    return (xf * r).astype(x.dtype) * g


def lambda_init_fn(layer):
    return 0.8 - 0.6 * math.exp(-0.3 * layer)


def rope_tables(S):
    pos = jnp.arange(S, dtype=jnp.float32)
    inv = 1.0 / (ROPE_THETA ** (jnp.arange(0, QK_DIM, 2, dtype=jnp.float32) / QK_DIM))
    ang = pos[:, None] * inv[None, :]
    return jnp.cos(ang), jnp.sin(ang)


def apply_rope(x, cos, sin):
    half = x.shape[-1] // 2
    x1, x2 = x[..., :half], x[..., half:]
    c = cos[None, :, None, None, :].astype(x.dtype)
    s = sin[None, :, None, None, :].astype(x.dtype)
    return jnp.concatenate([x1 * c - x2 * s, x2 * c + x1 * s], axis=-1)


def diff_attention(q, k, v, lam):
    B, S = q.shape[0], q.shape[1]
    nb = S // Q_BLOCK
    qb = jnp.moveaxis(q.reshape(B, nb, Q_BLOCK, ATTN_HEADS, 2, QK_DIM), 1, 0)
    scale = QK_DIM ** -0.5

    def one_block(qblk):
        s = jnp.einsum('bqhcd,bkhcd->bhcqk', qblk, k,
                       preferred_element_type=jnp.float32) * scale
        p = jax.nn.softmax(s, axis=-1)
        a = p[:, :, 0] - lam * p[:, :, 1]
        return jnp.einsum('bhqk,bkhe->bqhe', a.astype(v.dtype), v)

    o = lax.map(one_block, qb)
    return jnp.moveaxis(o, 0, 1).reshape(B, S, ATTN_HEADS, V_DIM)


def centred_conv(u, w, b):
    pad = (CONV_K - 1) // 2
    out = lax.conv_general_dilated(u, w[:, None, :].astype(u.dtype), window_strides=(1,),
                                   padding=[(pad, pad)],
                                   dimension_numbers=('NWC', 'WIO', 'NWC'),
                                   feature_group_count=u.shape[-1])
    return out + b


def ssd_chunked(X, dt, A, Bm, Cm):
    b, L, h, p = X.shape
    g, n = Bm.shape[2], Bm.shape[3]
    r = h // g
    c = L // CHUNK
    l = CHUNK
    Xd = (X.astype(jnp.float32) * dt[..., None]).reshape(b, c, l, g, r, p)
    Ad = (dt * A).reshape(b, c, l, g, r).transpose(0, 3, 4, 1, 2)
    Bc = Bm.astype(jnp.float32).reshape(b, c, l, g, n)
    Cc = Cm.astype(jnp.float32).reshape(b, c, l, g, n)
    A_cum = jnp.cumsum(Ad, axis=-1)
    seg = A_cum[..., :, None] - A_cum[..., None, :]
    mask = jnp.tril(jnp.ones((l, l), dtype=bool))
    Lmat = jnp.exp(jnp.where(mask, seg, -jnp.inf))
    CB = jnp.einsum('bclgn,bcsgn->bcgls', Cc, Bc)
    Y_diag = jnp.einsum('bcgls,bgrcls,bcsgrp->bclgrp', CB, Lmat, Xd)
    decay_states = jnp.exp(A_cum[..., -1:] - A_cum)
    states = jnp.einsum('bclgn,bgrcl,bclgrp->bcgrpn', Bc, decay_states, Xd)
    chunk_decay = jnp.exp(A_cum[..., -1])

    def step(carry, inp):
        st, dec = inp
        return carry * dec[..., None, None] + st, carry

    init = jnp.zeros((b, g, r, p, n), jnp.float32)
    _, states_in = lax.scan(step, init, (jnp.moveaxis(states, 1, 0),
                                         jnp.moveaxis(chunk_decay, 3, 0)))
    states_in = jnp.moveaxis(states_in, 0, 1)
    Y_off = jnp.einsum('bclgn,bcgrpn,bgrcl->bclgrp', Cc, states_in, jnp.exp(A_cum))
    return (Y_diag + Y_off).reshape(b, L, h, p)


def ssd_branch(z, xbc, dt_raw, conv_w, conv_b, dt_bias, a_log, d_skip, ssd_norm_g, w_ssd_o):
    B, S, _ = z.shape
    xbc = jax.nn.silu(centred_conv(xbc, conv_w, conv_b))
    xs, bm, cm = jnp.split(xbc, [D_INNER, D_INNER + SSD_GROUPS * D_STATE], axis=-1)
    xs = xs.reshape(B, S, SSD_HEADS, SSD_HEAD_DIM)
    bm = bm.reshape(B, S, SSD_GROUPS, D_STATE)
    cm = cm.reshape(B, S, SSD_GROUPS, D_STATE)
    dt = jax.nn.softplus(dt_raw.astype(jnp.float32).reshape(B, S, 2, SSD_HEADS)
                         + dt_bias.astype(jnp.float32))
    A = -jnp.exp(a_log.astype(jnp.float32))
    y_f = ssd_chunked(xs, dt[:, :, 0], A[0], bm, cm)
    flip = lambda t: jnp.flip(t, axis=1)
    y_b = flip(ssd_chunked(flip(xs), flip(dt[:, :, 1]), A[1], flip(bm), flip(cm)))
    y = (y_f + y_b).astype(xs.dtype) + d_skip[:, None] * xs
    y = y.reshape(B, S, D_INNER) * jax.nn.silu(z)
    y = rmsnorm(y.reshape(B, S, SSD_GROUPS, D_INNER // SSD_GROUPS),
                ssd_norm_g.reshape(SSD_GROUPS, D_INNER // SSD_GROUPS)).reshape(B, S, D_INNER)
    return y @ w_ssd_o


def hier_moe(h, w_group, b_group, w_router, b_router, w_gate_e, w_up_e, w_down_e):
    B, S, D = h.shape
    T = B * S
    ht = h.reshape(T, D)
    g_logits = (ht @ w_group).astype(jnp.float32) + b_group.astype(jnp.float32)
    g_prob = jax.nn.softmax(g_logits, axis=-1)
    g_sel = jnp.argmax(g_logits, axis=-1)
    g_w = jnp.take_along_axis(g_prob, g_sel[:, None], axis=-1)[:, 0]
    e_logits = ((ht @ w_router).astype(jnp.float32) + b_router.astype(jnp.float32)
                ).reshape(T, N_EXPERT_GROUPS, EXPERTS_PER_GROUP)
    e_in = jnp.take_along_axis(e_logits, g_sel[:, None, None], axis=1)[:, 0]
    top_v, top_i = lax.top_k(e_in, TOP_K_INNER)
    top_w = jax.nn.softmax(top_v, axis=-1) * g_w[:, None]
    expert_idx = g_sel[:, None].astype(jnp.int32) * EXPERTS_PER_GROUP + top_i.astype(jnp.int32)
    n_slots = T * TOP_K_INNER
    flat_e = expert_idx.reshape(-1)
    flat_w = top_w.reshape(-1)
    flat_tok = jnp.arange(n_slots, dtype=jnp.int32) // TOP_K_INNER
    order = jnp.argsort(flat_e)
    se, stok, sw = flat_e[order], flat_tok[order], flat_w[order]
    counts = jnp.bincount(flat_e, length=N_EXPERTS)
    padded = ((counts + MOE_BLOCK - 1) // MOE_BLOCK) * MOE_BLOCK
    pad_end = jnp.cumsum(padded)
    pad_start = pad_end - padded
    start = jnp.cumsum(counts) - counts
    dest = pad_start[se] + (jnp.arange(n_slots, dtype=jnp.int32) - start[se])
    cap = n_slots + N_EXPERTS * MOE_BLOCK
    n_blk = cap // MOE_BLOCK
    tok_buf = jnp.zeros((cap,), jnp.int32).at[dest].set(stok)
    w_buf = jnp.zeros((cap,), jnp.float32).at[dest].set(sw)
    blk_e = jnp.minimum(jnp.searchsorted(pad_end, jnp.arange(n_blk) * MOE_BLOCK, side='right'),
                        N_EXPERTS - 1)
    xb = ht[tok_buf].reshape(n_blk, MOE_BLOCK, D)

    def expert_block(args):
        xblk, e = args
        a = jax.nn.silu(xblk @ w_gate_e[e]) * (xblk @ w_up_e[e])
        return a @ w_down_e[e]

    yb = lax.map(expert_block, (xb, blk_e)).reshape(cap, D)
    yb = yb * w_buf[:, None].astype(yb.dtype)
    y = jnp.zeros((T, D), yb.dtype).at[tok_buf].add(yb)
    return y.reshape(B, S, D).astype(h.dtype)


def encoder_layer(x, c, cos, sin, lambda_init, w_ada, b_ada, norm1_g, w_in, q_norm_g, k_norm_g,
                  lambda_q1, lambda_k1, lambda_q2, lambda_k2, attn_subln_g, w_attn_o,
                  conv_w, conv_b, dt_bias, a_log, d_skip, ssd_norm_g, w_ssd_o, w_out,
                  norm2_g, w_group, b_group, w_router, b_router, w_gate_e, w_up_e, w_down_e):
    B, S, _ = x.shape
    mod = jax.nn.silu(c) @ w_ada + b_ada
    shift1, scale1, gate1, shift2, scale2, gate2 = jnp.split(mod[:, None, :], 6, axis=-1)
    h = rmsnorm(x, norm1_g) * (1 + scale1) + shift1
    proj = h @ w_in
    q, k, v, z, xbc, dt_raw, gates = jnp.split(proj, IN_SPLITS, axis=-1)
    q = apply_rope(rmsnorm(q.reshape(B, S, ATTN_HEADS, 2, QK_DIM), q_norm_g), cos, sin)
    k = apply_rope(rmsnorm(k.reshape(B, S, ATTN_HEADS, 2, QK_DIM), k_norm_g), cos, sin)
    v = v.reshape(B, S, ATTN_HEADS, V_DIM)
    f32 = jnp.float32
    lam = (jnp.exp(jnp.sum(lambda_q1.astype(f32) * lambda_k1.astype(f32)))
           - jnp.exp(jnp.sum(lambda_q2.astype(f32) * lambda_k2.astype(f32))) + lambda_init)
    o = diff_attention(q, k, v, lam)
    o = rmsnorm(o, attn_subln_g) * (1.0 - lambda_init)
    attn_d = o.reshape(B, S, ATTN_WIDTH) @ w_attn_o
    ssd_d = ssd_branch(z, xbc, dt_raw, conv_w, conv_b, dt_bias, a_log, d_skip, ssd_norm_g, w_ssd_o)
    gate_a, gate_s = jnp.split(jax.nn.sigmoid(gates), 2, axis=-1)
    mixed = (gate_a * attn_d + gate_s * ssd_d) @ w_out
    x = x + gate1 * mixed
    h2 = rmsnorm(x, norm2_g) * (1 + scale2) + shift2
    x = x + gate2 * hier_moe(h2, w_group, b_group, w_router, b_router, w_gate_e, w_up_e, w_down_e)
    return x


def trunk(x, c, w_ada, b_ada, norm1_g, w_in, q_norm_g, k_norm_g, lambda_q1, lambda_k1,
          lambda_q2, lambda_k2, attn_subln_g, w_attn_o, conv_w, conv_b, dt_bias, a_log,
          d_skip, ssd_norm_g, w_ssd_o, w_out, norm2_g, w_group, b_group, w_router,
          b_router, w_gate_e, w_up_e, w_down_e):
    cos, sin = rope_tables(x.shape[1])
    for l in range(DEPTH):
        x = encoder_layer(x, c, cos, sin, lambda_init_fn(l), w_ada[l], b_ada[l], norm1_g[l],
                          w_in[l], q_norm_g[l], k_norm_g[l], lambda_q1[l], lambda_k1[l],
                          lambda_q2[l], lambda_k2[l], attn_subln_g[l], w_attn_o[l],
                          conv_w[l], conv_b[l], dt_bias[l], a_log[l], d_skip[l],
                          ssd_norm_g[l], w_ssd_o[l], w_out[l], norm2_g[l], w_group[l],
                          b_group[l], w_router[l], b_router[l], w_gate_e[l], w_up_e[l],
                          w_down_e[l])
    return x


def setup_inputs(seed: int = 0) -> dict:
    key = jax.random.key(seed)
    ks = jax.random.split(key, 32)
    f32 = jnp.float32
    L = DEPTH
    nrm = lambda k, shape, s: jax.random.normal(k, shape, f32) * s
    gain = lambda k, shape: 1.0 + 0.02 * jax.random.normal(k, shape, f32)
    u = jax.random.uniform(ks[18], (L, 2, SSD_HEADS), f32)
    dt0 = jnp.exp(u * (math.log(0.1) - math.log(0.001)) + math.log(0.001))
    dt_bias = dt0 + jnp.log(-jnp.expm1(-dt0))
    a_log = jnp.log(jax.random.uniform(ks[19], (L, 2, SSD_HEADS), f32, minval=1.0, maxval=16.0))
    return {
        "x_prompt": nrm(ks[0], (BATCH, SEQ, D_MODEL), 1.0),
        "x_sample": nrm(ks[1], (DEC_BATCH, DEC_SEQ, D_MODEL), 1.0),
        "c_prompt": nrm(ks[2], (BATCH, D_MODEL), 1.0),
        "c_sample": nrm(ks[3], (DEC_BATCH, D_MODEL), 1.0),
        "w_ada": nrm(ks[4], (L, D_MODEL, 6 * D_MODEL), 0.5 * D_MODEL ** -0.5),
        "b_ada": nrm(ks[5], (L, 6 * D_MODEL), 0.01),
        "norm1_g": gain(ks[6], (L, D_MODEL)),
        "w_in": nrm(ks[7], (L, D_MODEL, IN_COLS), D_MODEL ** -0.5),
        "q_norm_g": gain(ks[8], (L, QK_DIM)),
        "k_norm_g": gain(ks[9], (L, QK_DIM)),
        "lambda_q1": nrm(ks[10], (L, QK_DIM), 0.1),
        "lambda_k1": nrm(ks[11], (L, QK_DIM), 0.1),
        "lambda_q2": nrm(ks[12], (L, QK_DIM), 0.1),
        "lambda_k2": nrm(ks[13], (L, QK_DIM), 0.1),
        "attn_subln_g": gain(ks[14], (L, V_DIM)),
        "w_attn_o": nrm(ks[15], (L, ATTN_WIDTH, D_MODEL), ATTN_WIDTH ** -0.5),
        "conv_w": nrm(ks[16], (L, CONV_K, CONV_CH), CONV_K ** -0.5),
        "conv_b": nrm(ks[17], (L, CONV_CH), 0.01),
        "dt_bias": dt_bias,
        "a_log": a_log,
        "d_skip": 1.0 + nrm(ks[20], (L, SSD_HEADS), 0.1),
        "ssd_norm_g": gain(ks[21], (L, D_INNER)),
        "w_ssd_o": nrm(ks[22], (L, D_INNER, D_MODEL), D_INNER ** -0.5),
        "w_out": nrm(ks[23], (L, D_MODEL, D_MODEL), D_MODEL ** -0.5),
        "norm2_g": gain(ks[24], (L, D_MODEL)),
        "w_group": nrm(ks[25], (L, D_MODEL, N_EXPERT_GROUPS), D_MODEL ** -0.5),
        "b_group": nrm(ks[26], (L, N_EXPERT_GROUPS), 0.01),
        "w_router": nrm(ks[27], (L, D_MODEL, N_EXPERTS), D_MODEL ** -0.5),
        "b_router": nrm(ks[28], (L, N_EXPERTS), 0.01),
        "w_gate_e": nrm(ks[29], (L, N_EXPERTS, D_MODEL, EXPERT_FF), D_MODEL ** -0.5),
        "w_up_e": nrm(ks[30], (L, N_EXPERTS, D_MODEL, EXPERT_FF), D_MODEL ** -0.5),
        "w_down_e": nrm(ks[31], (L, N_EXPERTS, EXPERT_FF, D_MODEL), EXPERT_FF ** -0.5),
    }


def reference(x_prompt, x_sample, c_prompt, c_sample, w_ada, b_ada, norm1_g, w_in, q_norm_g,
              k_norm_g, lambda_q1, lambda_k1, lambda_q2, lambda_k2, attn_subln_g, w_attn_o,
              conv_w, conv_b, dt_bias, a_log, d_skip, ssd_norm_g, w_ssd_o, w_out, norm2_g,
              w_group, b_group, w_router, b_router, w_gate_e, w_up_e, w_down_e):
    weights = (w_ada, b_ada, norm1_g, w_in, q_norm_g, k_norm_g, lambda_q1, lambda_k1,
               lambda_q2, lambda_k2, attn_subln_g, w_attn_o, conv_w, conv_b, dt_bias, a_log,
               d_skip, ssd_norm_g, w_ssd_o, w_out, norm2_g, w_group, b_group, w_router,
               b_router, w_gate_e, w_up_e, w_down_e)
    y_prompt = trunk(x_prompt, c_prompt, *weights)
    y_sample = trunk(x_sample, c_sample, *weights)
    return (y_prompt, y_sample)
```

```python
import functools
import math

import jax
import jax.numpy as jnp
from jax import lax
from jax.experimental import pallas as pl
from jax.experimental.pallas import tpu as pltpu

F32 = jnp.float32
BF16 = jnp.bfloat16

D_MODEL = 1024
ATTN_HEADS = 8
QK_DIM = 64
V_DIM = 2 * QK_DIM
ATTN_WIDTH = ATTN_HEADS * V_DIM
ROPE_THETA = 10000.0
D_INNER = 2048
SSD_HEAD_DIM = 64
SSD_HEADS = D_INNER // SSD_HEAD_DIM
SSD_GROUPS = 4
HEADS_PER_GROUP = SSD_HEADS // SSD_GROUPS
D_STATE = 128
CONV_K = 5
CONV_CH = D_INNER + 2 * SSD_GROUPS * D_STATE
CHUNK = 128
N_EXPERT_GROUPS = 4
EXPERTS_PER_GROUP = 8
N_EXPERTS = N_EXPERT_GROUPS * EXPERTS_PER_GROUP
TOP_K_INNER = 2
EXPERT_FF = 512
RMS_EPS = 1e-6
LAMBDA_INIT = 0.8 - 0.6 * math.exp(-0.3 * 0)

LANES = 128
SUBLANES = 8
VMEM_LIMIT = 56 * 1024 * 1024

Q_COLS = ATTN_HEADS * 2 * QK_DIM
K_COLS = Q_COLS
V_COLS = ATTN_WIDTH
Z_COLS = D_INNER
XBC_COLS = CONV_CH
DT_COLS = 2 * SSD_HEADS
GATE_COLS = 2 * D_MODEL
OFF_Q = 0
OFF_K = OFF_Q + Q_COLS
OFF_V = OFF_K + K_COLS
OFF_Z = OFF_V + V_COLS
OFF_XBC = OFF_Z + Z_COLS
OFF_DT = OFF_XBC + XBC_COLS
OFF_GATE = OFF_DT + DT_COLS

EXPERT_BLOCK = 256


def _params(sem):
    return pltpu.CompilerParams(dimension_semantics=sem, vmem_limit_bytes=VMEM_LIMIT)


def _dot(a, b):
    return jnp.dot(a, b, preferred_element_type=F32)


def _dot_tn(a, b):
    return lax.dot_general(a, b, (((0,), (0,)), ((), ())), preferred_element_type=F32)


def _dot_nt(a, b):
    return lax.dot_general(a, b, (((1,), (1,)), ((), ())), preferred_element_type=F32)


def _split3(a):
    hi = a.astype(BF16)
    r = a - hi.astype(F32)
    mid = r.astype(BF16)
    lo = (r - mid.astype(F32)).astype(BF16)
    return hi, mid, lo


def _dot_left01(m01, a):
    hi, mid, lo = _split3(a)
    return _dot(m01, hi) + _dot(m01, mid) + _dot(m01, lo)


def _dot_right01(a, m01):
    hi, mid, lo = _split3(a)
    return _dot(hi, m01) + _dot(mid, m01) + _dot(lo, m01)


def _dot_f32(a, b):
    a0, a1, a2 = _split3(a)
    b0, b1, b2 = _split3(b)
    return (_dot(a0, b0) + (_dot(a0, b1) + _dot(a1, b0))
            + (_dot(a0, b2) + _dot(a2, b0) + _dot(a1, b1)))


def _sigmoid(x):
    return 1.0 / (1.0 + jnp.exp(-x))


def _silu(x):
    return x * _sigmoid(x)


def _softplus(x):
    return jnp.maximum(x, 0.0) + jnp.log1p(jnp.exp(-jnp.abs(x)))


def _ada_kernel(c_ref, w_ref, b_ref, o_ref):
    o_ref[...] = _dot_f32(_silu(c_ref[...]), w_ref[...]) + b_ref[...]


def _ada(c_pad, w_ada, b_ada):
    rows = c_pad.shape[0]
    n = w_ada.shape[1]
    tn = 1024
    return pl.pallas_call(
        _ada_kernel,
        grid=(n // tn,),
        in_specs=[pl.BlockSpec((rows, D_MODEL), lambda j: (0, 0)),
                  pl.BlockSpec((D_MODEL, tn), lambda j: (0, j)),
                  pl.BlockSpec((1, tn), lambda j: (0, j))],
        out_specs=pl.BlockSpec((rows, tn), lambda j: (0, j)),
        out_shape=jax.ShapeDtypeStruct((rows, n), F32),
        compiler_params=_params(("arbitrary",)),
        name="ada_mod",
    )(c_pad, w_ada, b_ada.reshape(1, n))


def _norm_mod_kernel(x_ref, mod_ref, g_ref, o_ref, *, shift_row, scale_row):
    x = x_ref[...]
    r = lax.rsqrt(jnp.mean(x * x, axis=-1, keepdims=True) + RMS_EPS)
    m = mod_ref[0]
    h = ((x * r) * g_ref[...]) * (1.0 + m[scale_row:scale_row + 1]) + m[shift_row:shift_row + 1]
    o_ref[...] = h.astype(o_ref.dtype)


def _norm_mod(x, mod_tiles, g, tm):
    t = x.shape[0]
    return pl.pallas_call(
        functools.partial(_norm_mod_kernel, shift_row=0, scale_row=1),
        grid=(t // tm,),
        in_specs=[pl.BlockSpec((tm, D_MODEL), lambda i: (i, 0)),
                  pl.BlockSpec((1, 6, D_MODEL), lambda i: (i, 0, 0)),
                  pl.BlockSpec((1, D_MODEL), lambda i: (0, 0))],
        out_specs=pl.BlockSpec((tm, D_MODEL), lambda i: (i, 0)),
        out_shape=jax.ShapeDtypeStruct((t, D_MODEL), BF16),
        compiler_params=_params(("parallel",)),
        name="norm1_mod",
    )(x, mod_tiles, g.reshape(1, D_MODEL))


def _mm_kernel(a_ref, w_ref, o_ref):
    o_ref[...] = _dot(a_ref[...], w_ref[...]).astype(o_ref.dtype)


def _matmul(a, w, out_dtype, tm, tn, name):
    m, k = a.shape
    n = w.shape[1]
    return pl.pallas_call(
        _mm_kernel,
        grid=(n // tn, m // tm),
        in_specs=[pl.BlockSpec((tm, k), lambda j, i: (i, 0)),
                  pl.BlockSpec((k, tn), lambda j, i: (0, j))],
        out_specs=pl.BlockSpec((tm, tn), lambda j, i: (i, j)),
        out_shape=jax.ShapeDtypeStruct((m, n), out_dtype),
        compiler_params=_params(("parallel", "parallel")),
        name=name,
    )(a, w)


def _qk_kernel(a_ref, w_ref, g_ref, cos_ref, sin_ref, bd_ref, o_ref):
    acc = _dot(a_ref[...], w_ref[...])
    tm = acc.shape[0]
    lane = lax.broadcasted_iota(jnp.int32, (tm, LANES), 1)
    first = (lane & (QK_DIM // 2)) == 0
    cos = cos_ref[...]
    sin = sin_ref[...]
    g = g_ref[...]
    bd = bd_ref[...]
    for h in range(ATTN_HEADS):
        x = acc[:, h * LANES:(h + 1) * LANES]
        sq = x * x
        hi = sq.astype(BF16)
        lo = (sq - hi.astype(F32)).astype(BF16)
        ss = _dot(hi, bd) + _dot(lo, bd)
        r = lax.rsqrt(ss * (1.0 / QK_DIM) + RMS_EPS)
        xn = (x * r) * g
        partner = jnp.where(first, pltpu.roll(xn, LANES - QK_DIM // 2, 1),
                            pltpu.roll(xn, QK_DIM // 2, 1))
        o_ref[:, h * LANES:(h + 1) * LANES] = (xn * cos + partner * sin).astype(o_ref.dtype)


def _qk_proj(h, w, g128, cos_t, sin_t, bd, tm, name):
    t = h.shape[0]
    n = w.shape[1]
    return pl.pallas_call(
        _qk_kernel,
        grid=(t // tm,),
        in_specs=[pl.BlockSpec((tm, D_MODEL), lambda i: (i, 0)),
                  pl.BlockSpec((D_MODEL, n), lambda i: (0, 0)),
                  pl.BlockSpec((1, LANES), lambda i: (0, 0)),
                  pl.BlockSpec((tm, LANES), lambda i: (i, 0)),
                  pl.BlockSpec((tm, LANES), lambda i: (i, 0)),
                  pl.BlockSpec((LANES, LANES), lambda i: (0, 0))],
        out_specs=pl.BlockSpec((tm, n), lambda i: (i, 0)),
        out_shape=jax.ShapeDtypeStruct((t, n), BF16),
        compiler_params=_params(("parallel",)),
        name=name,
    )(h, w, g128, cos_t, sin_t, bd)


def _vt_kernel(a_ref, w_ref, o_ref, *, tk):
    acc = _dot(a_ref[...], w_ref[...])
    for c in range(acc.shape[0] // tk):
        o_ref[c] = acc[c * tk:(c + 1) * tk, :].T.astype(o_ref.dtype)


def _vt_proj(h, w, tm, tk):
    t = h.shape[0]
    n = w.shape[1]
    return pl.pallas_call(
        functools.partial(_vt_kernel, tk=tk),
        grid=(t // tm,),
        in_specs=[pl.BlockSpec((tm, D_MODEL), lambda i: (i, 0)),
                  pl.BlockSpec((D_MODEL, n), lambda i: (0, 0))],
        out_specs=pl.BlockSpec((tm // tk, n, tk), lambda i: (i, 0, 0)),
        out_shape=jax.ShapeDtypeStruct((t // tk, n, tk), BF16),
        compiler_params=_params(("parallel",)),
        name="v_proj_t",
    )(h, w)


def _attn_kernel(lq1_ref, lk1_ref, lq2_ref, lk2_ref, q_ref, k_ref, vt_ref, *rest, tq, nkv):
    o_ref, q2t_s, m_s, l_s, acc_s = rest[-5:]
    qt = q_ref[...].astype(F32).T
    row = lax.broadcasted_iota(jnp.int32, qt.shape, 0)
    q2t_s[:, :tq] = jnp.where(row < QK_DIM, qt, 0.0).astype(BF16)
    q2t_s[:, tq:] = jnp.where(row >= QK_DIM, qt, 0.0).astype(BF16)
    m_s[...] = jnp.full(m_s.shape, -jnp.inf, F32)
    l_s[...] = jnp.zeros(l_s.shape, F32)
    acc_s[...] = jnp.zeros(acc_s.shape, F32)

    def body(j, carry):
        st = _dot(k_ref[j], q2t_s[...])
        m_old = m_s[...]
        m_new = jnp.maximum(m_old, jnp.max(st, axis=0, keepdims=True))
        alpha = jnp.exp(m_old - m_new)
        p = jnp.exp(st - m_new)
        l_s[...] = alpha * l_s[...] + jnp.sum(p, axis=0, keepdims=True)
        acc_s[...] = alpha * acc_s[...] + _dot(vt_ref[j], p.astype(BF16))
        m_s[...] = m_new
        return carry

    lax.fori_loop(0, nkv, body, 0)
    lam = (jnp.exp(jnp.sum(lq1_ref[...] * lk1_ref[...], axis=1, keepdims=True))
           - jnp.exp(jnp.sum(lq2_ref[...] * lk2_ref[...], axis=1, keepdims=True)) + LAMBDA_INIT)
    ot = acc_s[...] / l_s[...]
    o_ref[...] = (ot[:, :tq] - lam * ot[:, tq:]).T


def _attention(q, k3, vt3, lams, prev_out, tok_off, batch, seq, tq, tk):
    t = q.shape[0]
    nkv = seq // tk
    qb0 = tok_off // tq
    sb0 = tok_off // seq
    nq = seq // tq
    lam_spec = pl.BlockSpec((1, QK_DIM), lambda b, h, i: (0, 0))
    in_specs = [lam_spec, lam_spec, lam_spec, lam_spec,
                pl.BlockSpec((tq, LANES), lambda b, h, i: (qb0 + b * nq + i, h)),
                pl.BlockSpec((nkv, tk, LANES), lambda b, h, i: (sb0 + b, 0, h)),
                pl.BlockSpec((nkv, LANES, tk), lambda b, h, i: (sb0 + b, h, 0))]
    args = list(lams) + [q, k3, vt3]
    aliases = {}
    if prev_out is not None:
        in_specs.append(pl.BlockSpec(memory_space=pl.ANY))
        args.append(prev_out)
        aliases = {len(args) - 1: 0}
    return pl.pallas_call(
        functools.partial(_attn_kernel, tq=tq, nkv=nkv),
        grid=(batch, ATTN_HEADS, nq),
        in_specs=in_specs,
        out_specs=pl.BlockSpec((tq, LANES), lambda b, h, i: (qb0 + b * nq + i, h)),
        out_shape=jax.ShapeDtypeStruct((t, ATTN_WIDTH), F32),
        scratch_shapes=[pltpu.VMEM((LANES, 2 * tq), BF16),
                        pltpu.VMEM((1, 2 * tq), F32),
                        pltpu.VMEM((1, 2 * tq), F32),
                        pltpu.VMEM((LANES, 2 * tq), F32)],
        input_output_aliases=aliases,
        compiler_params=_params(("parallel", "parallel", "arbitrary")),
        name="diff_attention",
    )(*args)


def _conv_kernel(start_ref, end_ref, prev_ref, cur_ref, next_ref, w_ref, b_ref, xs_ref, bc_ref, ext_s, *, tm):
    i = pl.program_id(0)
    halo = SUBLANES
    ext_s[0:halo, :] = jnp.where(start_ref[i] == 1, 0.0, prev_ref[...])
    ext_s[halo:halo + tm, :] = cur_ref[...]
    ext_s[halo + tm:2 * halo + tm, :] = jnp.where(end_ref[i] == 1, 0.0, next_ref[...])
    pad = (CONV_K - 1) // 2
    acc = jnp.broadcast_to(b_ref[...], (tm, CONV_CH))
    for d in range(CONV_K):
        acc = acc + w_ref[d:d + 1, :] * ext_s[halo - pad + d:halo - pad + d + tm, :]
    y = _silu(acc)
    xs_ref[...] = y[:, :D_INNER]
    bc_ref[...] = y[:, D_INNER:].astype(bc_ref.dtype)


def _conv(xbc, conv_w, conv_b, start_flags, end_flags, tm):
    t = xbc.shape[0]
    rb = tm // SUBLANES
    last = t // SUBLANES - 1
    w_pad = jnp.zeros((SUBLANES, CONV_CH), F32).at[:CONV_K].set(conv_w)
    grid_spec = pltpu.PrefetchScalarGridSpec(
        num_scalar_prefetch=2,
        grid=(t // tm,),
        in_specs=[pl.BlockSpec((SUBLANES, CONV_CH), lambda i, s, e: (jnp.maximum(i * rb - 1, 0), 0)),
                  pl.BlockSpec((tm, CONV_CH), lambda i, s, e: (i, 0)),
                  pl.BlockSpec((SUBLANES, CONV_CH), lambda i, s, e: (jnp.minimum((i + 1) * rb, last), 0)),
                  pl.BlockSpec((SUBLANES, CONV_CH), lambda i, s, e: (0, 0)),
                  pl.BlockSpec((1, CONV_CH), lambda i, s, e: (0, 0))],
        out_specs=[pl.BlockSpec((tm, D_INNER), lambda i, s, e: (i, 0)),
                   pl.BlockSpec((tm, CONV_CH - D_INNER), lambda i, s, e: (i, 0))],
        scratch_shapes=[pltpu.VMEM((tm + 2 * SUBLANES, CONV_CH), F32)],
    )
    return pl.pallas_call(
        functools.partial(_conv_kernel, tm=tm),
        grid_spec=grid_spec,
        out_shape=[jax.ShapeDtypeStruct((t, D_INNER), F32),
                   jax.ShapeDtypeStruct((t, CONV_CH - D_INNER), BF16)],
        compiler_params=_params(("arbitrary",)),
        name="conv_silu",
    )(start_flags, end_flags, xbc, xbc, xbc, w_pad, conv_b.reshape(1, CONV_CH))


def _ssd_kernel(idx_ref, reset_ref, xs_ref, bc_ref, dt_ref, dtt_ref, bias_r_ref, bias_c_ref,
                alog_r_ref, alog_c_ref, tril_ref, triu_ref, e01_ref, y_ref, state_s, *, rev):
    i = pl.program_id(0)
    nh = SSD_HEADS

    @pl.when(reset_ref[i] == 1)
    def _():
        state_s[...] = jnp.zeros(state_s.shape, F32)

    xs = xs_ref[...]
    bc = bc_ref[...]
    dtn = _softplus(dt_ref[...] + bias_r_ref[...])
    a = dtn * (-jnp.exp(alog_r_ref[...]))
    pinc = _dot_left01(tril_ref[...], a)
    pex = pinc - a
    tot = pinc[CHUNK - 1:CHUNK, :]

    if rev:
        dec_off = jnp.exp(tot - pex)
        dec_st = jnp.exp(pex)
    else:
        dec_off = jnp.exp(pinc)
        dec_st = jnp.exp(tot - pinc)
    cd = jnp.broadcast_to(jnp.exp(tot), (SUBLANES, LANES))
    stacked = jnp.concatenate([dec_st * dtn, dec_off, cd], axis=0)
    ex = _dot_right01(stacked, e01_ref[...])
    sc_st = ex[:CHUNK]
    sc_off = ex[CHUNK:2 * CHUNK]
    sc_cd = ex[2 * CHUNK:2 * CHUNK + 1]

    w = (xs * sc_st).astype(BF16)
    gw = HEADS_PER_GROUP * SSD_HEAD_DIM
    gn = SSD_GROUPS * D_STATE
    for g in range(SSD_GROUPS):
        bg = bc[:, g * D_STATE:(g + 1) * D_STATE]
        cg = bc[:, gn + g * D_STATE:gn + (g + 1) * D_STATE]
        st = state_s[:, g * gw:(g + 1) * gw]
        y_ref[:, g * gw:(g + 1) * gw] = _dot(cg, st.astype(BF16)) * sc_off[:, g * gw:(g + 1) * gw]
        new = _dot_tn(bg, w[:, g * gw:(g + 1) * gw])
        state_s[:, g * gw:(g + 1) * gw] = st * sc_cd[:, g * gw:(g + 1) * gw] + new

    if rev:
        return

    dtnt = _softplus(dtt_ref[...] + bias_c_ref[...])
    at = dtnt * (-jnp.exp(alog_c_ref[...]))
    pinct = _dot_right01(at, triu_ref[...])
    pext = pinct - at
    li = lax.broadcasted_iota(jnp.int32, (CHUNK, CHUNK), 0)
    si = lax.broadcasted_iota(jnp.int32, (CHUNK, CHUNK), 1)
    lower = si <= li
    strict_lower = si < li
    strict_upper = si > li
    lane = lax.broadcasted_iota(jnp.int32, (CHUNK, LANES), 1)
    first_head = lane < SSD_HEAD_DIM
    xb = xs.astype(BF16)
    for g in range(SSD_GROUPS):
        bg = bc[:, g * D_STATE:(g + 1) * D_STATE]
        cg = bc[:, gn + g * D_STATE:gn + (g + 1) * D_STATE]
        cb = _dot_nt(cg, bg)
        for hp in range(HEADS_PER_GROUP // 2):
            pair = []
            x_pair = xb[:, g * gw + hp * LANES:g * gw + (hp + 1) * LANES]
            for u in range(2):
                h = g * HEADS_PER_GROUP + 2 * hp + u
                arg = jnp.where(lower, pinc[:, h:h + 1] - pinct[h:h + 1, :],
                                pext[nh + h:nh + h + 1, :] - pex[:, nh + h:nh + h + 1])
                dtf = dtnt[h:h + 1, :]
                dtb = dtnt[nh + h:nh + h + 1, :]
                coef = jnp.where(strict_lower, dtf, jnp.where(strict_upper, dtb, dtf + dtb))
                mh = (cb * jnp.exp(arg) * coef).astype(BF16)
                pair.append(_dot(mh, x_pair))
            col = g * gw + hp * LANES
            y_ref[:, col:col + LANES] = y_ref[:, col:col + LANES] + jnp.where(first_head, pair[0], pair[1])


def _ssd(xs, bc, dt, dtt, dt_bias, a_log, idx, reset, rev):
    t = xs.shape[0]
    nc = t // CHUNK
    pad = LANES - DT_COLS
    bias_r = jnp.pad(dt_bias.reshape(1, DT_COLS), ((0, 0), (0, pad)))
    alog_r = jnp.pad(a_log.reshape(1, DT_COLS), ((0, 0), (0, pad)))
    bias_c = bias_r.reshape(LANES, 1)
    alog_c = alog_r.reshape(LANES, 1)
    r = jnp.arange(CHUNK)
    tril = (r[None, :] <= r[:, None]).astype(BF16)
    triu = (r[:, None] <= r[None, :]).astype(BF16)
    lo = SSD_HEADS if rev else 0
    e01 = (jnp.arange(D_INNER)[None, :] // SSD_HEAD_DIM == jnp.arange(LANES)[:, None] - lo).astype(BF16)
    const = lambda shape: pl.BlockSpec(shape, lambda i, ix, rs: (0, 0))
    grid_spec = pltpu.PrefetchScalarGridSpec(
        num_scalar_prefetch=2,
        grid=(nc,),
        in_specs=[pl.BlockSpec((CHUNK, D_INNER), lambda i, ix, rs: (ix[i], 0)),
                  pl.BlockSpec((CHUNK, CONV_CH - D_INNER), lambda i, ix, rs: (ix[i], 0)),
                  pl.BlockSpec((CHUNK, LANES), lambda i, ix, rs: (ix[i], 0)),
                  pl.BlockSpec((LANES, CHUNK), lambda i, ix, rs: (0, ix[i])),
                  const((1, LANES)), const((LANES, 1)), const((1, LANES)), const((LANES, 1)),
                  const((CHUNK, CHUNK)), const((CHUNK, CHUNK)), const((LANES, D_INNER))],
        out_specs=pl.BlockSpec((CHUNK, D_INNER), lambda i, ix, rs: (ix[i], 0)),
        scratch_shapes=[pltpu.VMEM((D_STATE, D_INNER), F32)],
    )
    return pl.pallas_call(
        functools.partial(_ssd_kernel, rev=rev),
        grid_spec=grid_spec,
        out_shape=jax.ShapeDtypeStruct((t, D_INNER), F32),
        compiler_params=_params(("arbitrary",)),
        name="ssd_bwd" if rev else "ssd_fwd",
    )(idx, reset, xs, bc, dt, dtt, bias_r, bias_c, alog_r, alog_c, tril, triu, e01)


def _post_kernel(o_ref, yf_ref, yb_ref, xs_ref, z_ref, gt_ref, x_ref, mod_ref, subg_ref, dskip_ref,
                 ssdg_ref, n2g_ref, wa_ref, ws_ref, wo_ref, x1_ref, h2b_ref, h2f_ref, an_s, yn_s):
    m = mod_ref[0]
    o = o_ref[...]
    for h in range(ATTN_HEADS):
        oh = o[:, h * V_DIM:(h + 1) * V_DIM]
        r = lax.rsqrt(jnp.mean(oh * oh, axis=-1, keepdims=True) + RMS_EPS)
        an_s[:, h * V_DIM:(h + 1) * V_DIM] = (((oh * r) * subg_ref[...]) * (1.0 - LAMBDA_INIT)).astype(BF16)
    attn_d = _dot(an_s[...], wa_ref[...])

    y = (yf_ref[...] + yb_ref[...]) + dskip_ref[...] * xs_ref[...]
    y = y * _silu(z_ref[...])
    gw = D_INNER // SSD_GROUPS
    for g in range(SSD_GROUPS):
        yg = y[:, g * gw:(g + 1) * gw]
        r = lax.rsqrt(jnp.mean(yg * yg, axis=-1, keepdims=True) + RMS_EPS)
        yn_s[:, g * gw:(g + 1) * gw] = ((yg * r) * ssdg_ref[:, g * gw:(g + 1) * gw]).astype(BF16)
    ssd_d = _dot(yn_s[...], ws_ref[...])

    gt = gt_ref[...]
    mix = _sigmoid(gt[:, :D_MODEL]) * attn_d + _sigmoid(gt[:, D_MODEL:]) * ssd_d
    mixed = _dot(mix.astype(BF16), wo_ref[...])
    x1 = x_ref[...] + m[2:3] * mixed
    x1_ref[...] = x1
    r = lax.rsqrt(jnp.mean(x1 * x1, axis=-1, keepdims=True) + RMS_EPS)
    h2 = ((x1 * r) * n2g_ref[...]) * (1.0 + m[4:5]) + m[3:4]
    h2f_ref[...] = h2
    h2b_ref[...] = h2.astype(BF16)


def _post(o, yf, yb, xs, z, gates, x, mod_tiles, subg, dskip, ssdg, n2g, wa, ws, wo, tm):
    t = x.shape[0]
    tok = lambda n: pl.BlockSpec((tm, n), lambda i: (i, 0))
    const = lambda a: pl.BlockSpec(a.shape, lambda i: (0, 0))
    return pl.pallas_call(
        _post_kernel,
        grid=(t // tm,),
        in_specs=[tok(ATTN_WIDTH), tok(D_INNER), tok(D_INNER), tok(D_INNER), tok(D_INNER), tok(GATE_COLS),
                  tok(D_MODEL), pl.BlockSpec((1, 6, D_MODEL), lambda i: (i, 0, 0)),
                  const(subg), const(dskip), const(ssdg), const(n2g), const(wa), const(ws), const(wo)],
        out_specs=[tok(D_MODEL), tok(D_MODEL), tok(D_MODEL)],
        out_shape=[jax.ShapeDtypeStruct((t, D_MODEL), F32),
                   jax.ShapeDtypeStruct((t, D_MODEL), BF16),
                   jax.ShapeDtypeStruct((t, D_MODEL), F32)],
        scratch_shapes=[pltpu.VMEM((tm, ATTN_WIDTH), BF16), pltpu.VMEM((tm, D_INNER), BF16)],
        compiler_params=_params(("parallel",)),
        name="merge_out_proj",
    )(o, yf, yb, xs, z, gates, x, mod_tiles, subg, dskip, ssdg, n2g, wa, ws, wo)


def _router_kernel(h_ref, w_ref, b_ref, o_ref):
    logits = _dot_f32(h_ref[...], w_ref[...]) + b_ref[...]
    tm = logits.shape[0]
    lane = lax.broadcasted_iota(jnp.int32, (tm, LANES), 1)
    lane_f = lane.astype(F32)
    big = float(LANES)
    neg = -jnp.inf
    gl = jnp.where(lane < N_EXPERT_GROUPS, logits, neg)
    gmax = jnp.max(gl, axis=1, keepdims=True)
    g_sel = jnp.min(jnp.where(gl == gmax, lane_f, big), axis=1, keepdims=True)
    g_w = 1.0 / jnp.sum(jnp.exp(gl - gmax), axis=1, keepdims=True)
    e_lane = lane - N_EXPERT_GROUPS
    e_group = (e_lane >> 3).astype(F32)
    in_group = (e_lane >= 0) & (e_lane < N_EXPERTS) & (e_group == g_sel)
    el = jnp.where(in_group, logits, neg)
    v1 = jnp.max(el, axis=1, keepdims=True)
    i1 = jnp.min(jnp.where(el == v1, lane_f, big), axis=1, keepdims=True)
    el2 = jnp.where(lane_f == i1, neg, el)
    v2 = jnp.max(el2, axis=1, keepdims=True)
    i2 = jnp.min(jnp.where(el2 == v2, lane_f, big), axis=1, keepdims=True)
    e2 = jnp.exp(v2 - v1)
    w1 = g_w / (1.0 + e2)
    w2 = g_w * e2 / (1.0 + e2)
    out = jnp.where(lane == 0, i1 - N_EXPERT_GROUPS,
                    jnp.where(lane == 1, i2 - N_EXPERT_GROUPS,
                              jnp.where(lane == 2, w1, jnp.where(lane == 3, w2, 0.0))))
    o_ref[...] = out


def _router(h2f, w_rt, b_rt, tm):
    t = h2f.shape[0]
    return pl.pallas_call(
        _router_kernel,
        grid=(t // tm,),
        in_specs=[pl.BlockSpec((tm, D_MODEL), lambda i: (i, 0)),
                  pl.BlockSpec((D_MODEL, LANES), lambda i: (0, 0)),
                  pl.BlockSpec((1, LANES), lambda i: (0, 0))],
        out_specs=pl.BlockSpec((tm, LANES), lambda i: (i, 0)),
        out_shape=jax.ShapeDtypeStruct((t, LANES), F32),
        compiler_params=_params(("parallel",)),
        name="router_topk",
    )(h2f, w_rt, b_rt)


def _expert_kernel(be_ref, nu_ref, x_ref, wg_ref, wu_ref, wd_ref, o_ref):
    i = pl.program_id(0)

    @pl.when(i < nu_ref[0])
    def _():
        x = x_ref[...]
        a = _silu(_dot(x, wg_ref[0])) * _dot(x, wu_ref[0])
        o_ref[...] = _dot(a.astype(BF16), wd_ref[0])

    @pl.when(i >= nu_ref[0])
    def _():
        o_ref[...] = jnp.zeros(o_ref.shape, o_ref.dtype)


def _experts(xb, blk_e, n_used, wg, wu, wd):
    cap = xb.shape[0]
    nb = cap // EXPERT_BLOCK
    grid_spec = pltpu.PrefetchScalarGridSpec(
        num_scalar_prefetch=2,
        grid=(nb,),
        in_specs=[pl.BlockSpec((EXPERT_BLOCK, D_MODEL), lambda i, be, nu: (jnp.minimum(i, nu[0] - 1), 0)),
                  pl.BlockSpec((1, D_MODEL, EXPERT_FF), lambda i, be, nu: (be[i], 0, 0)),
                  pl.BlockSpec((1, D_MODEL, EXPERT_FF), lambda i, be, nu: (be[i], 0, 0)),
                  pl.BlockSpec((1, EXPERT_FF, D_MODEL), lambda i, be, nu: (be[i], 0, 0))],
        out_specs=pl.BlockSpec((EXPERT_BLOCK, D_MODEL), lambda i, be, nu: (i, 0)),
    )
    return pl.pallas_call(
        _expert_kernel,
        grid_spec=grid_spec,
        out_shape=jax.ShapeDtypeStruct((cap, D_MODEL), F32),
        compiler_params=_params(("arbitrary",)),
        name="expert_mlp",
    )(blk_e, n_used, xb, wg, wu, wd)


def _final_kernel(x1_ref, mod_ref, rt_ref, g0_ref, g1_ref, o_ref):
    m = mod_ref[0]
    rt = rt_ref[...]
    moe = g0_ref[...] * rt[:, 2:3] + g1_ref[...] * rt[:, 3:4]
    o_ref[...] = x1_ref[...] + m[5:6] * moe


def _final(x1, mod_tiles, rt, g0, g1, tok_off, n_tok, tm):
    b0 = tok_off // tm
    tok = lambda n: pl.BlockSpec((tm, n), lambda i: (b0 + i, 0))
    return pl.pallas_call(
        _final_kernel,
        grid=(n_tok // tm,),
        in_specs=[tok(D_MODEL), pl.BlockSpec((1, 6, D_MODEL), lambda i: (b0 + i, 0, 0)), tok(LANES),
                  tok(D_MODEL), tok(D_MODEL)],
        out_specs=pl.BlockSpec((tm, D_MODEL), lambda i: (i, 0)),
        out_shape=jax.ShapeDtypeStruct((n_tok, D_MODEL), F32),
        compiler_params=_params(("parallel",)),
        name="moe_combine",
    )(x1, mod_tiles, rt, g0, g1)


def _rope_tables(seq):
    pos = jnp.arange(seq, dtype=F32)
    inv = 1.0 / (ROPE_THETA ** (jnp.arange(0, QK_DIM, 2, dtype=F32) / QK_DIM))
    ang = pos[:, None] * inv[None, :]
    cos, sin = jnp.cos(ang), jnp.sin(ang)
    cos_t = jnp.tile(cos, (1, LANES // (QK_DIM // 2)))
    sin_t = jnp.tile(jnp.concatenate([-sin, sin], axis=1), (1, LANES // QK_DIM))
    return cos_t, sin_t


def _dispatch(e_idx, n_tok):
    n_slots = n_tok * TOP_K_INNER
    flat_e = e_idx.reshape(-1)
    onehot = (flat_e[:, None] == jnp.arange(N_EXPERTS, dtype=jnp.int32)[None, :]).astype(jnp.int32)
    csum = jnp.cumsum(onehot, axis=0)
    counts = csum[-1]
    rank = jnp.sum(onehot * csum, axis=1) - 1
    padded = ((counts + EXPERT_BLOCK - 1) // EXPERT_BLOCK) * EXPERT_BLOCK
    pad_end = jnp.cumsum(padded)
    pad_start = pad_end - padded
    dest = (pad_start[flat_e] + rank).astype(jnp.int32)
    cap = n_slots + N_EXPERTS * EXPERT_BLOCK
    nb = cap // EXPERT_BLOCK
    tok = jnp.arange(n_slots, dtype=jnp.int32) // TOP_K_INNER
    tok_buf = jnp.zeros((cap,), jnp.int32).at[dest].set(tok)
    blk_e = jnp.minimum(jnp.searchsorted(pad_end, jnp.arange(nb, dtype=jnp.int32) * EXPERT_BLOCK, side='right'),
                        N_EXPERTS - 1).astype(jnp.int32)
    n_used = (pad_end[-1] // EXPERT_BLOCK).astype(jnp.int32).reshape(1)
    return dest.reshape(n_tok, TOP_K_INNER), tok_buf, blk_e, n_used


def kernel(x_prompt, x_sample, c_prompt, c_sample, w_ada, b_ada, norm1_g, w_in, q_norm_g, k_norm_g, lambda_q1, lambda_k1, lambda_q2, lambda_k2, attn_subln_g, w_attn_o, conv_w, conv_b, dt_bias, a_log, d_skip, ssd_norm_g, w_ssd_o, w_out, norm2_g, w_group, b_group, w_router, b_router, w_gate_e, w_up_e, w_down_e):
    groups = [(x_prompt, c_prompt), (x_sample, c_sample)]
    seqs = [(x.shape[0], x.shape[1]) for x, _ in groups]
    n_tok = sum(b * s for b, s in seqs)
    min_seq = min(s for _, s in seqs)
    tm = min(1024, min_seq)
    tp = min(256, min_seq)
    tq = min(256, min_seq)
    tk = min(512, min_seq)
    layer = 0

    x = jnp.concatenate([g[0].reshape(-1, D_MODEL) for g in groups], axis=0)
    c = jnp.concatenate([g[1] for g in groups], axis=0)
    n_batch = c.shape[0]
    c_pad = jnp.pad(c, ((0, (-n_batch) % SUBLANES), (0, 0)))
    mod = _ada(c_pad, w_ada[layer], b_ada[layer]).reshape(-1, 6, D_MODEL)
    tok_batch = jnp.concatenate([jnp.repeat(jnp.arange(b, dtype=jnp.int32), s) + off
                                 for (b, s), off in zip(seqs, [0, seqs[0][0]])])
    mod_tm = mod[tok_batch[::tm]]
    mod_tp = mod[tok_batch[::tp]]

    h = _norm_mod(x, mod_tm, norm1_g[layer], tm)

    w_in_b = w_in[layer].astype(BF16)
    tabs = [_rope_tables(s) for _, s in seqs]
    cos_t = jnp.concatenate([jnp.tile(tb[0], (b, 1)) for tb, (b, _) in zip(tabs, seqs)], axis=0)
    sin_t = jnp.concatenate([jnp.tile(tb[1], (b, 1)) for tb, (b, _) in zip(tabs, seqs)], axis=0)
    half = jnp.arange(LANES) // QK_DIM
    bd = (half[:, None] == half[None, :]).astype(BF16)
    gq = (jnp.tile(q_norm_g[layer], LANES // QK_DIM) * (QK_DIM ** -0.5)).reshape(1, LANES)
    gk = jnp.tile(k_norm_g[layer], LANES // QK_DIM).reshape(1, LANES)
    q = _qk_proj(h, w_in_b[:, OFF_Q:OFF_K], gq, cos_t, sin_t, bd, tm, "q_proj")
    k = _qk_proj(h, w_in_b[:, OFF_K:OFF_V], gk, cos_t, sin_t, bd, tm, "k_proj")
    vt3 = _vt_proj(h, w_in_b[:, OFF_V:OFF_Z], tm, tk)
    z = _matmul(h, w_in_b[:, OFF_Z:OFF_XBC], F32, tm, 1024, "z_proj")
    xbc = _matmul(h, w_in_b[:, OFF_XBC:OFF_DT], F32, tm, 1024, "xbc_proj")
    w_dt = jnp.pad(w_in_b[:, OFF_DT:OFF_GATE], ((0, 0), (0, LANES - DT_COLS)))
    dt = _matmul(h, w_dt, F32, tm, LANES, "dt_proj")
    gates = _matmul(h, w_in_b[:, OFF_GATE:], F32, tm, 1024, "gate_proj")

    k3 = k.reshape(n_tok // tk, tk, ATTN_WIDTH)
    lams = [v[layer].reshape(1, QK_DIM) for v in (lambda_q1, lambda_k1, lambda_q2, lambda_k2)]
    o = None
    off = 0
    for b, s in seqs:
        o = _attention(q, k3, vt3, lams, o, off, b, s, tq, tk)
        off += b * s

    seq_starts = []
    off = 0
    for b, s in seqs:
        seq_starts += [(off + i * s, s) for i in range(b)]
        off += b * s
    tile_start = jnp.zeros((n_tok // tp,), jnp.int32)
    tile_end = jnp.zeros((n_tok // tp,), jnp.int32)
    nc = n_tok // CHUNK
    chunk_reset = jnp.zeros((nc,), jnp.int32)
    bwd_idx = jnp.zeros((nc,), jnp.int32)
    for st, s in seq_starts:
        tile_start = tile_start.at[st // tp].set(1)
        tile_end = tile_end.at[(st + s) // tp - 1].set(1)
        c0, c1 = st // CHUNK, (st + s) // CHUNK
        chunk_reset = chunk_reset.at[c0].set(1)
        bwd_idx = bwd_idx.at[c0:c1].set(jnp.arange(c1 - 1, c0 - 1, -1, dtype=jnp.int32))
    fwd_idx = jnp.arange(nc, dtype=jnp.int32)
    xs, bcm = _conv(xbc, conv_w[layer], conv_b[layer], tile_start, tile_end, tp)
    dtt = dt.T
    dtb = dt_bias[layer].reshape(-1)
    alg = a_log[layer].reshape(-1)
    yf = _ssd(xs, bcm, dt, dtt, dtb, alg, fwd_idx, chunk_reset, False)
    yb = _ssd(xs, bcm, dt, dtt, dtb, alg, bwd_idx, chunk_reset, True)

    subg = attn_subln_g[layer].reshape(1, V_DIM)
    dskip = jnp.repeat(d_skip[layer], SSD_HEAD_DIM).reshape(1, D_INNER)
    x1, h2b, h2f = _post(o, yf, yb, xs, z, gates, x, mod_tp, subg, dskip,
                         ssd_norm_g[layer].reshape(1, D_INNER), norm2_g[layer].reshape(1, D_MODEL),
                         w_attn_o[layer].astype(BF16), w_ssd_o[layer].astype(BF16),
                         w_out[layer].astype(BF16), tp)

    n_rt = N_EXPERT_GROUPS + N_EXPERTS
    w_rt = jnp.pad(jnp.concatenate([w_group[layer], w_router[layer]], axis=1), ((0, 0), (0, LANES - n_rt)))
    b_rt = jnp.pad(jnp.concatenate([b_group[layer], b_router[layer]]), (0, LANES - n_rt)).reshape(1, LANES)
    rt = _router(h2f, w_rt, b_rt, tp)
    e_idx = rt[:, :TOP_K_INNER].astype(jnp.int32)
    dest, tok_buf, blk_e, n_used = _dispatch(e_idx, n_tok)
    xb = h2b[tok_buf]
    yb_e = _experts(xb, blk_e, n_used, w_gate_e[layer].astype(BF16), w_up_e[layer].astype(BF16),
                    w_down_e[layer].astype(BF16))
    g0 = yb_e[dest[:, 0]]
    g1 = yb_e[dest[:, 1]]

    outs = []
    off = 0
    for (b, s), (xg, _) in zip(seqs, groups):
        y = _final(x1, mod_tp, rt, g0, g1, off, b * s, tp)
        outs.append(y.reshape(xg.shape))
        off += b * s
    return tuple(outs)
```

```python
import functools
import math

import jax
import jax.numpy as jnp
from jax import lax
from jax.experimental import pallas as pl
from jax.experimental.pallas import tpu as pltpu

F32 = jnp.float32
BF16 = jnp.bfloat16

D_MODEL = 1024
ATTN_HEADS = 8
QK_DIM = 64
V_DIM = 2 * QK_DIM
ATTN_WIDTH = ATTN_HEADS * V_DIM
ROPE_THETA = 10000.0
D_INNER = 2048
SSD_HEAD_DIM = 64
SSD_HEADS = D_INNER // SSD_HEAD_DIM
SSD_GROUPS = 4
HEADS_PER_GROUP = SSD_HEADS // SSD_GROUPS
D_STATE = 128
CONV_K = 5
CONV_CH = D_INNER + 2 * SSD_GROUPS * D_STATE
CHUNK = 128
N_EXPERT_GROUPS = 4
EXPERTS_PER_GROUP = 8
N_EXPERTS = N_EXPERT_GROUPS * EXPERTS_PER_GROUP
TOP_K_INNER = 2
EXPERT_FF = 512
RMS_EPS = 1e-6
LAMBDA_INIT = 0.8 - 0.6 * math.exp(-0.3 * 0)

LANES = 128
SUBLANES = 8
VMEM_LIMIT = 56 * 1024 * 1024

Q_COLS = ATTN_HEADS * 2 * QK_DIM
K_COLS = Q_COLS
V_COLS = ATTN_WIDTH
Z_COLS = D_INNER
XBC_COLS = CONV_CH
DT_COLS = 2 * SSD_HEADS
GATE_COLS = 2 * D_MODEL
OFF_Q = 0
OFF_K = OFF_Q + Q_COLS
OFF_V = OFF_K + K_COLS
OFF_Z = OFF_V + V_COLS
OFF_XBC = OFF_Z + Z_COLS
OFF_DT = OFF_XBC + XBC_COLS
OFF_GATE = OFF_DT + DT_COLS

EXPERT_BLOCK = 256
ONES_ROWS = 16
LOG2_E = math.log2(math.e)
KV_UNROLL = 4


def _params(sem):
    return pltpu.CompilerParams(dimension_semantics=sem, vmem_limit_bytes=VMEM_LIMIT)


def _dot(a, b):
    return jnp.dot(a, b, preferred_element_type=F32)


def _dot_tn(a, b):
    return lax.dot_general(a, b, (((0,), (0,)), ((), ())), preferred_element_type=F32)


def _dot_nt(a, b):
    return lax.dot_general(a, b, (((1,), (1,)), ((), ())), preferred_element_type=F32)


def _split3(a):
    hi = a.astype(BF16)
    r = a - hi.astype(F32)
    mid = r.astype(BF16)
    lo = (r - mid.astype(F32)).astype(BF16)
    return hi, mid, lo


def _dot_left01(m01, a):
    hi, mid, lo = _split3(a)
    return _dot(m01, hi) + _dot(m01, mid) + _dot(m01, lo)


def _dot_right01(a, m01):
    hi, mid, lo = _split3(a)
    return _dot(hi, m01) + _dot(mid, m01) + _dot(lo, m01)


def _dot_f32(a, b):
    a0, a1, a2 = _split3(a)
    b0, b1, b2 = _split3(b)
    return (_dot(a0, b0) + (_dot(a0, b1) + _dot(a1, b0))
            + (_dot(a0, b2) + _dot(a2, b0) + _dot(a1, b1)))


def _sigmoid(x):
    return 1.0 / (1.0 + jnp.exp(-x))


def _silu(x):
    return x * _sigmoid(x)


def _softplus(x):
    return jnp.maximum(x, 0.0) + jnp.log1p(jnp.exp(-jnp.abs(x)))


def _ada_kernel(c_ref, w_ref, b_ref, o_ref):
    o_ref[...] = _dot_f32(_silu(c_ref[...]), w_ref[...]) + b_ref[...]


def _ada(c_pad, w_ada, b_ada):
    rows = c_pad.shape[0]
    n = w_ada.shape[1]
    tn = 1024
    return pl.pallas_call(
        _ada_kernel,
        grid=(n // tn,),
        in_specs=[pl.BlockSpec((rows, D_MODEL), lambda j: (0, 0)),
                  pl.BlockSpec((D_MODEL, tn), lambda j: (0, j)),
                  pl.BlockSpec((1, tn), lambda j: (0, j))],
        out_specs=pl.BlockSpec((rows, tn), lambda j: (0, j)),
        out_shape=jax.ShapeDtypeStruct((rows, n), F32),
        compiler_params=_params(("arbitrary",)),
        name="ada_mod",
    )(c_pad, w_ada, b_ada.reshape(1, n))


def _norm_mod_kernel(x_ref, mod_ref, g_ref, o_ref, *, shift_row, scale_row):
    x = x_ref[...]
    r = lax.rsqrt(jnp.mean(x * x, axis=-1, keepdims=True) + RMS_EPS)
    m = mod_ref[0]
    h = ((x * r) * g_ref[...]) * (1.0 + m[scale_row:scale_row + 1]) + m[shift_row:shift_row + 1]
    o_ref[...] = h.astype(o_ref.dtype)


def _norm_mod(x, mod_tiles, g, tm):
    t = x.shape[0]
    return pl.pallas_call(
        functools.partial(_norm_mod_kernel, shift_row=0, scale_row=1),
        grid=(t // tm,),
        in_specs=[pl.BlockSpec((tm, D_MODEL), lambda i: (i, 0)),
                  pl.BlockSpec((1, 6, D_MODEL), lambda i: (i, 0, 0)),
                  pl.BlockSpec((1, D_MODEL), lambda i: (0, 0))],
        out_specs=pl.BlockSpec((tm, D_MODEL), lambda i: (i, 0)),
        out_shape=jax.ShapeDtypeStruct((t, D_MODEL), BF16),
        compiler_params=_params(("parallel",)),
        name="norm1_mod",
    )(x, mod_tiles, g.reshape(1, D_MODEL))


def _mm_kernel(a_ref, w_ref, o_ref):
    o_ref[...] = _dot(a_ref[...], w_ref[...]).astype(o_ref.dtype)


def _matmul(a, w, out_dtype, tm, tn, name):
    m, k = a.shape
    n = w.shape[1]
    return pl.pallas_call(
        _mm_kernel,
        grid=(n // tn, m // tm),
        in_specs=[pl.BlockSpec((tm, k), lambda j, i: (i, 0)),
                  pl.BlockSpec((k, tn), lambda j, i: (0, j))],
        out_specs=pl.BlockSpec((tm, tn), lambda j, i: (i, j)),
        out_shape=jax.ShapeDtypeStruct((m, n), out_dtype),
        compiler_params=_params(("parallel", "parallel")),
        name=name,
    )(a, w)


def _qk_kernel(a_ref, w_ref, g_ref, cos_ref, sin_ref, bd_ref, o_ref):
    acc = _dot(a_ref[...], w_ref[...])
    tm = acc.shape[0]
    lane = lax.broadcasted_iota(jnp.int32, (tm, LANES), 1)
    first = (lane & (QK_DIM // 2)) == 0
    cos = cos_ref[...]
    sin = sin_ref[...]
    g = g_ref[...]
    bd = bd_ref[...]
    for h in range(ATTN_HEADS):
        x = acc[:, h * LANES:(h + 1) * LANES]
        sq = x * x
        hi = sq.astype(BF16)
        lo = (sq - hi.astype(F32)).astype(BF16)
        ss = _dot(hi, bd) + _dot(lo, bd)
        r = lax.rsqrt(ss * (1.0 / QK_DIM) + RMS_EPS)
        xn = (x * r) * g
        partner = jnp.where(first, pltpu.roll(xn, LANES - QK_DIM // 2, 1),
                            pltpu.roll(xn, QK_DIM // 2, 1))
        o_ref[:, h * LANES:(h + 1) * LANES] = (xn * cos + partner * sin).astype(o_ref.dtype)


def _qk_proj(h, w, g128, cos_t, sin_t, bd, tm, name):
    t = h.shape[0]
    n = w.shape[1]
    return pl.pallas_call(
        _qk_kernel,
        grid=(t // tm,),
        in_specs=[pl.BlockSpec((tm, D_MODEL), lambda i: (i, 0)),
                  pl.BlockSpec((D_MODEL, n), lambda i: (0, 0)),
                  pl.BlockSpec((1, LANES), lambda i: (0, 0)),
                  pl.BlockSpec((tm, LANES), lambda i: (i, 0)),
                  pl.BlockSpec((tm, LANES), lambda i: (i, 0)),
                  pl.BlockSpec((LANES, LANES), lambda i: (0, 0))],
        out_specs=pl.BlockSpec((tm, n), lambda i: (i, 0)),
        out_shape=jax.ShapeDtypeStruct((t, n), BF16),
        compiler_params=_params(("parallel",)),
        name=name,
    )(h, w, g128, cos_t, sin_t, bd)


def _vt_kernel(a_ref, w_ref, o_ref, *, tk):
    acc = _dot(a_ref[...], w_ref[...])
    for c in range(acc.shape[0] // tk):
        o_ref[c] = acc[c * tk:(c + 1) * tk, :].T.astype(o_ref.dtype)


def _vt_proj(h, w, tm, tk):
    t = h.shape[0]
    n = w.shape[1]
    return pl.pallas_call(
        functools.partial(_vt_kernel, tk=tk),
        grid=(t // tm,),
        in_specs=[pl.BlockSpec((tm, D_MODEL), lambda i: (i, 0)),
                  pl.BlockSpec((D_MODEL, n), lambda i: (0, 0))],
        out_specs=pl.BlockSpec((tm // tk, n, tk), lambda i: (i, 0, 0)),
        out_shape=jax.ShapeDtypeStruct((t // tk, n, tk), BF16),
        compiler_params=_params(("parallel",)),
        name="v_proj_t",
    )(h, w)


def _attn_kernel(lq1_ref, lk1_ref, lq2_ref, lk2_ref, q_ref, k_ref, vt_ref, *rest, tq, nkv):
    o_ref, q2t_s, s_s, m_s, acc_s = rest[-5:]
    tk = k_ref.shape[1]
    qt = q_ref[...].astype(F32).T
    row = lax.broadcasted_iota(jnp.int32, qt.shape, 0)
    q2t_s[:, :tq] = jnp.where(row < QK_DIM, qt, 0.0).astype(BF16)
    q2t_s[:, tq:] = jnp.where(row >= QK_DIM, qt, 0.0).astype(BF16)
    m_s[...] = jnp.full(m_s.shape, -jnp.inf, F32)
    acc_s[...] = jnp.zeros(acc_s.shape, F32)
    ones = jnp.ones((ONES_ROWS, tk), BF16)
    s_s[0] = _dot(k_ref[0], q2t_s[...])

    def step(j, par, compute_next):
        if compute_next:
            s_s[1 - par] = _dot(k_ref[j + 1], q2t_s[...])
        st = s_s[par]
        m_old = m_s[...]
        m_new = jnp.maximum(m_old, jnp.max(st, axis=0, keepdims=True))
        alpha = jnp.exp2(m_old - m_new)
        p = jnp.exp2(st - m_new).astype(BF16)
        lhs = jnp.concatenate([vt_ref[j], ones], axis=0)
        acc_s[...] = alpha * acc_s[...] + _dot(lhs, p)
        m_s[...] = m_new

    n_trips = (nkv - 1) // KV_UNROLL

    def trip(t, carry):
        for u in range(KV_UNROLL):
            step(KV_UNROLL * t + u, u % 2, True)
        return carry

    lax.fori_loop(0, n_trips, trip, 0)
    for j in range(KV_UNROLL * n_trips, nkv):
        step(j, j % 2, j + 1 < nkv)

    lam = (jnp.exp(jnp.sum(lq1_ref[...] * lk1_ref[...], axis=1, keepdims=True))
           - jnp.exp(jnp.sum(lq2_ref[...] * lk2_ref[...], axis=1, keepdims=True)) + LAMBDA_INIT)
    ot = acc_s[:V_DIM, :] / acc_s[V_DIM:V_DIM + 1, :]
    o_ref[...] = (ot[:, :tq] - lam * ot[:, tq:]).T


def _attention(q, k3, vt3, lams, prev_out, tok_off, batch, seq, tq, tk):
    t = q.shape[0]
    nkv = seq // tk
    qb0 = tok_off // tq
    sb0 = tok_off // seq
    nq = seq // tq
    lam_spec = pl.BlockSpec((1, QK_DIM), lambda b, h, i: (0, 0))
    in_specs = [lam_spec, lam_spec, lam_spec, lam_spec,
                pl.BlockSpec((tq, LANES), lambda b, h, i: (qb0 + b * nq + i, h)),
                pl.BlockSpec((nkv, tk, LANES), lambda b, h, i: (sb0 + b, 0, h)),
                pl.BlockSpec((nkv, LANES, tk), lambda b, h, i: (sb0 + b, h, 0))]
    args = list(lams) + [q, k3, vt3]
    aliases = {}
    if prev_out is not None:
        in_specs.append(pl.BlockSpec(memory_space=pl.ANY))
        args.append(prev_out)
        aliases = {len(args) - 1: 0}
    return pl.pallas_call(
        functools.partial(_attn_kernel, tq=tq, nkv=nkv),
        grid=(batch, ATTN_HEADS, nq),
        in_specs=in_specs,
        out_specs=pl.BlockSpec((tq, LANES), lambda b, h, i: (qb0 + b * nq + i, h)),
        out_shape=jax.ShapeDtypeStruct((t, ATTN_WIDTH), F32),
        scratch_shapes=[pltpu.VMEM((LANES, 2 * tq), BF16),
                        pltpu.VMEM((2, tk, 2 * tq), F32),
                        pltpu.VMEM((1, 2 * tq), F32),
                        pltpu.VMEM((V_DIM + ONES_ROWS, 2 * tq), F32)],
        input_output_aliases=aliases,
        compiler_params=_params(("parallel", "parallel", "arbitrary")),
        name="diff_attention",
    )(*args)


def _conv_kernel(start_ref, end_ref, prev_ref, cur_ref, next_ref, w_ref, b_ref, xs_ref, bc_ref, ext_s, *, tm):
    i = pl.program_id(0)
    halo = SUBLANES
    ext_s[0:halo, :] = jnp.where(start_ref[i] == 1, 0.0, prev_ref[...])
    ext_s[halo:halo + tm, :] = cur_ref[...]
    ext_s[halo + tm:2 * halo + tm, :] = jnp.where(end_ref[i] == 1, 0.0, next_ref[...])
    pad = (CONV_K - 1) // 2
    acc = jnp.broadcast_to(b_ref[...], (tm, CONV_CH))
    for d in range(CONV_K):
        acc = acc + w_ref[d:d + 1, :] * ext_s[halo - pad + d:halo - pad + d + tm, :]
    y = _silu(acc)
    xs_ref[...] = y[:, :D_INNER]
    bc_ref[...] = y[:, D_INNER:].astype(bc_ref.dtype)


def _conv(xbc, conv_w, conv_b, start_flags, end_flags, tm):
    t = xbc.shape[0]
    rb = tm // SUBLANES
    last = t // SUBLANES - 1
    w_pad = jnp.zeros((SUBLANES, CONV_CH), F32).at[:CONV_K].set(conv_w)
    grid_spec = pltpu.PrefetchScalarGridSpec(
        num_scalar_prefetch=2,
        grid=(t // tm,),
        in_specs=[pl.BlockSpec((SUBLANES, CONV_CH), lambda i, s, e: (jnp.maximum(i * rb - 1, 0), 0)),
                  pl.BlockSpec((tm, CONV_CH), lambda i, s, e: (i, 0)),
                  pl.BlockSpec((SUBLANES, CONV_CH), lambda i, s, e: (jnp.minimum((i + 1) * rb, last), 0)),
                  pl.BlockSpec((SUBLANES, CONV_CH), lambda i, s, e: (0, 0)),
                  pl.BlockSpec((1, CONV_CH), lambda i, s, e: (0, 0))],
        out_specs=[pl.BlockSpec((tm, D_INNER), lambda i, s, e: (i, 0)),
                   pl.BlockSpec((tm, CONV_CH - D_INNER), lambda i, s, e: (i, 0))],
        scratch_shapes=[pltpu.VMEM((tm + 2 * SUBLANES, CONV_CH), F32)],
    )
    return pl.pallas_call(
        functools.partial(_conv_kernel, tm=tm),
        grid_spec=grid_spec,
        out_shape=[jax.ShapeDtypeStruct((t, D_INNER), F32),
                   jax.ShapeDtypeStruct((t, CONV_CH - D_INNER), BF16)],
        compiler_params=_params(("arbitrary",)),
        name="conv_silu",
    )(start_flags, end_flags, xbc, xbc, xbc, w_pad, conv_b.reshape(1, CONV_CH))


def _ssd_kernel(idx_ref, reset_ref, xs_ref, bc_ref, dt_ref, dtt_ref, bias_r_ref, bias_c_ref,
                alog_r_ref, alog_c_ref, tril_ref, triu_ref, e01_ref, y_ref, state_s, *, rev):
    i = pl.program_id(0)
    nh = SSD_HEADS

    @pl.when(reset_ref[i] == 1)
    def _():
        state_s[...] = jnp.zeros(state_s.shape, F32)

    xs = xs_ref[...]
    bc = bc_ref[...]
    dtn = _softplus(dt_ref[...] + bias_r_ref[...])
    a = dtn * (-jnp.exp(alog_r_ref[...]))
    pinc = _dot_left01(tril_ref[...], a)
    pex = pinc - a
    tot = pinc[CHUNK - 1:CHUNK, :]

    if rev:
        dec_off = jnp.exp(tot - pex)
        dec_st = jnp.exp(pex)
    else:
        dec_off = jnp.exp(pinc)
        dec_st = jnp.exp(tot - pinc)
    cd = jnp.broadcast_to(jnp.exp(tot), (SUBLANES, LANES))
    stacked = jnp.concatenate([dec_st * dtn, dec_off, cd], axis=0)
    ex = _dot_right01(stacked, e01_ref[...])
    sc_st = ex[:CHUNK]
    sc_off = ex[CHUNK:2 * CHUNK]
    sc_cd = ex[2 * CHUNK:2 * CHUNK + 1]

    w = (xs * sc_st).astype(BF16)
    gw = HEADS_PER_GROUP * SSD_HEAD_DIM
    gn = SSD_GROUPS * D_STATE
    for g in range(SSD_GROUPS):
        bg = bc[:, g * D_STATE:(g + 1) * D_STATE]
        cg = bc[:, gn + g * D_STATE:gn + (g + 1) * D_STATE]
        st = state_s[:, g * gw:(g + 1) * gw]
        y_ref[:, g * gw:(g + 1) * gw] = _dot(cg, st.astype(BF16)) * sc_off[:, g * gw:(g + 1) * gw]
        new = _dot_tn(bg, w[:, g * gw:(g + 1) * gw])
        state_s[:, g * gw:(g + 1) * gw] = st * sc_cd[:, g * gw:(g + 1) * gw] + new

    if rev:
        return

    dtnt = _softplus(dtt_ref[...] + bias_c_ref[...])
    at = dtnt * (-jnp.exp(alog_c_ref[...]))
    pinct = _dot_right01(at, triu_ref[...])
    pext = pinct - at
    li = lax.broadcasted_iota(jnp.int32, (CHUNK, CHUNK), 0)
    si = lax.broadcasted_iota(jnp.int32, (CHUNK, CHUNK), 1)
    lower = si <= li
    strict_lower = si < li
    strict_upper = si > li
    lane = lax.broadcasted_iota(jnp.int32, (CHUNK, LANES), 1)
    first_head = lane < SSD_HEAD_DIM
    xb = xs.astype(BF16)
    for g in range(SSD_GROUPS):
        bg = bc[:, g * D_STATE:(g + 1) * D_STATE]
        cg = bc[:, gn + g * D_STATE:gn + (g + 1) * D_STATE]
        cb = _dot_nt(cg, bg)
        for hp in range(HEADS_PER_GROUP // 2):
            pair = []
            x_pair = xb[:, g * gw + hp * LANES:g * gw + (hp + 1) * LANES]
            for u in range(2):
                h = g * HEADS_PER_GROUP + 2 * hp + u
                arg = jnp.where(lower, pinc[:, h:h + 1] - pinct[h:h + 1, :],
                                pext[nh + h:nh + h + 1, :] - pex[:, nh + h:nh + h + 1])
                dtf = dtnt[h:h + 1, :]
                dtb = dtnt[nh + h:nh + h + 1, :]
                coef = jnp.where(strict_lower, dtf, jnp.where(strict_upper, dtb, dtf + dtb))
                mh = (cb * jnp.exp(arg) * coef).astype(BF16)
                pair.append(_dot(mh, x_pair))
            col = g * gw + hp * LANES
            y_ref[:, col:col + LANES] = y_ref[:, col:col + LANES] + jnp.where(first_head, pair[0], pair[1])


def _ssd(xs, bc, dt, dtt, dt_bias, a_log, idx, reset, rev):
    t = xs.shape[0]
    nc = t // CHUNK
    pad = LANES - DT_COLS
    bias_r = jnp.pad(dt_bias.reshape(1, DT_COLS), ((0, 0), (0, pad)))
    alog_r = jnp.pad(a_log.reshape(1, DT_COLS), ((0, 0), (0, pad)))
    bias_c = bias_r.reshape(LANES, 1)
    alog_c = alog_r.reshape(LANES, 1)
    r = jnp.arange(CHUNK)
    tril = (r[None, :] <= r[:, None]).astype(BF16)
    triu = (r[:, None] <= r[None, :]).astype(BF16)
    lo = SSD_HEADS if rev else 0
    e01 = (jnp.arange(D_INNER)[None, :] // SSD_HEAD_DIM == jnp.arange(LANES)[:, None] - lo).astype(BF16)
    const = lambda shape: pl.BlockSpec(shape, lambda i, ix, rs: (0, 0))
    grid_spec = pltpu.PrefetchScalarGridSpec(
        num_scalar_prefetch=2,
        grid=(nc,),
        in_specs=[pl.BlockSpec((CHUNK, D_INNER), lambda i, ix, rs: (ix[i], 0)),
                  pl.BlockSpec((CHUNK, CONV_CH - D_INNER), lambda i, ix, rs: (ix[i], 0)),
                  pl.BlockSpec((CHUNK, LANES), lambda i, ix, rs: (ix[i], 0)),
                  pl.BlockSpec((LANES, CHUNK), lambda i, ix, rs: (0, ix[i])),
                  const((1, LANES)), const((LANES, 1)), const((1, LANES)), const((LANES, 1)),
                  const((CHUNK, CHUNK)), const((CHUNK, CHUNK)), const((LANES, D_INNER))],
        out_specs=pl.BlockSpec((CHUNK, D_INNER), lambda i, ix, rs: (ix[i], 0)),
        scratch_shapes=[pltpu.VMEM((D_STATE, D_INNER), F32)],
    )
    return pl.pallas_call(
        functools.partial(_ssd_kernel, rev=rev),
        grid_spec=grid_spec,
        out_shape=jax.ShapeDtypeStruct((t, D_INNER), F32),
        compiler_params=_params(("arbitrary",)),
        name="ssd_bwd" if rev else "ssd_fwd",
    )(idx, reset, xs, bc, dt, dtt, bias_r, bias_c, alog_r, alog_c, tril, triu, e01)


def _post_kernel(o_ref, yf_ref, yb_ref, xs_ref, z_ref, gt_ref, x_ref, mod_ref, subg_ref, dskip_ref,
                 ssdg_ref, n2g_ref, wa_ref, ws_ref, wo_ref, x1_ref, h2b_ref, h2f_ref, an_s, yn_s):
    m = mod_ref[0]
    o = o_ref[...]
    for h in range(ATTN_HEADS):
        oh = o[:, h * V_DIM:(h + 1) * V_DIM]
        r = lax.rsqrt(jnp.mean(oh * oh, axis=-1, keepdims=True) + RMS_EPS)
        an_s[:, h * V_DIM:(h + 1) * V_DIM] = (((oh * r) * subg_ref[...]) * (1.0 - LAMBDA_INIT)).astype(BF16)
    attn_d = _dot(an_s[...], wa_ref[...])

    y = (yf_ref[...] + yb_ref[...]) + dskip_ref[...] * xs_ref[...]
    y = y * _silu(z_ref[...])
    gw = D_INNER // SSD_GROUPS
    for g in range(SSD_GROUPS):
        yg = y[:, g * gw:(g + 1) * gw]
        r = lax.rsqrt(jnp.mean(yg * yg, axis=-1, keepdims=True) + RMS_EPS)
        yn_s[:, g * gw:(g + 1) * gw] = ((yg * r) * ssdg_ref[:, g * gw:(g + 1) * gw]).astype(BF16)
    ssd_d = _dot(yn_s[...], ws_ref[...])

    gt = gt_ref[...]
    mix = _sigmoid(gt[:, :D_MODEL]) * attn_d + _sigmoid(gt[:, D_MODEL:]) * ssd_d
    mixed = _dot(mix.astype(BF16), wo_ref[...])
    x1 = x_ref[...] + m[2:3] * mixed
    x1_ref[...] = x1
    r = lax.rsqrt(jnp.mean(x1 * x1, axis=-1, keepdims=True) + RMS_EPS)
    h2 = ((x1 * r) * n2g_ref[...]) * (1.0 + m[4:5]) + m[3:4]
    h2f_ref[...] = h2
    h2b_ref[...] = h2.astype(BF16)


def _post(o, yf, yb, xs, z, gates, x, mod_tiles, subg, dskip, ssdg, n2g, wa, ws, wo, tm):
    t = x.shape[0]
    tok = lambda n: pl.BlockSpec((tm, n), lambda i: (i, 0))
    const = lambda a: pl.BlockSpec(a.shape, lambda i: (0, 0))
    return pl.pallas_call(
        _post_kernel,
        grid=(t // tm,),
        in_specs=[tok(ATTN_WIDTH), tok(D_INNER), tok(D_INNER), tok(D_INNER), tok(D_INNER), tok(GATE_COLS),
                  tok(D_MODEL), pl.BlockSpec((1, 6, D_MODEL), lambda i: (i, 0, 0)),
                  const(subg), const(dskip), const(ssdg), const(n2g), const(wa), const(ws), const(wo)],
        out_specs=[tok(D_MODEL), tok(D_MODEL), tok(D_MODEL)],
        out_shape=[jax.ShapeDtypeStruct((t, D_MODEL), F32),
                   jax.ShapeDtypeStruct((t, D_MODEL), BF16),
                   jax.ShapeDtypeStruct((t, D_MODEL), F32)],
        scratch_shapes=[pltpu.VMEM((tm, ATTN_WIDTH), BF16), pltpu.VMEM((tm, D_INNER), BF16)],
        compiler_params=_params(("parallel",)),
        name="merge_out_proj",
    )(o, yf, yb, xs, z, gates, x, mod_tiles, subg, dskip, ssdg, n2g, wa, ws, wo)


def _router_kernel(h_ref, w_ref, b_ref, o_ref):
    logits = _dot_f32(h_ref[...], w_ref[...]) + b_ref[...]
    tm = logits.shape[0]
    lane = lax.broadcasted_iota(jnp.int32, (tm, LANES), 1)
    lane_f = lane.astype(F32)
    big = float(LANES)
    neg = -jnp.inf
    gl = jnp.where(lane < N_EXPERT_GROUPS, logits, neg)
    gmax = jnp.max(gl, axis=1, keepdims=True)
    g_sel = jnp.min(jnp.where(gl == gmax, lane_f, big), axis=1, keepdims=True)
    g_w = 1.0 / jnp.sum(jnp.exp(gl - gmax), axis=1, keepdims=True)
    e_lane = lane - N_EXPERT_GROUPS
    e_group = (e_lane >> 3).astype(F32)
    in_group = (e_lane >= 0) & (e_lane < N_EXPERTS) & (e_group == g_sel)
    el = jnp.where(in_group, logits, neg)
    v1 = jnp.max(el, axis=1, keepdims=True)
    i1 = jnp.min(jnp.where(el == v1, lane_f, big), axis=1, keepdims=True)
    el2 = jnp.where(lane_f == i1, neg, el)
    v2 = jnp.max(el2, axis=1, keepdims=True)
    i2 = jnp.min(jnp.where(el2 == v2, lane_f, big), axis=1, keepdims=True)
    e2 = jnp.exp(v2 - v1)
    w1 = g_w / (1.0 + e2)
    w2 = g_w * e2 / (1.0 + e2)
    out = jnp.where(lane == 0, i1 - N_EXPERT_GROUPS,
                    jnp.where(lane == 1, i2 - N_EXPERT_GROUPS,
                              jnp.where(lane == 2, w1, jnp.where(lane == 3, w2, 0.0))))
    o_ref[...] = out


def _router(h2f, w_rt, b_rt, tm):
    t = h2f.shape[0]
    return pl.pallas_call(
        _router_kernel,
        grid=(t // tm,),
        in_specs=[pl.BlockSpec((tm, D_MODEL), lambda i: (i, 0)),
                  pl.BlockSpec((D_MODEL, LANES), lambda i: (0, 0)),
                  pl.BlockSpec((1, LANES), lambda i: (0, 0))],
        out_specs=pl.BlockSpec((tm, LANES), lambda i: (i, 0)),
        out_shape=jax.ShapeDtypeStruct((t, LANES), F32),
        compiler_params=_params(("parallel",)),
        name="router_topk",
    )(h2f, w_rt, b_rt)


def _expert_kernel(be_ref, nu_ref, x_ref, wg_ref, wu_ref, wd_ref, o_ref):
    i = pl.program_id(0)

    @pl.when(i < nu_ref[0])
    def _():
        x = x_ref[...]
        a = _silu(_dot(x, wg_ref[0])) * _dot(x, wu_ref[0])
        o_ref[...] = _dot(a.astype(BF16), wd_ref[0])

    @pl.when(i >= nu_ref[0])
    def _():
        o_ref[...] = jnp.zeros(o_ref.shape, o_ref.dtype)


def _experts(xb, blk_e, n_used, wg, wu, wd):
    cap = xb.shape[0]
    nb = cap // EXPERT_BLOCK
    grid_spec = pltpu.PrefetchScalarGridSpec(
        num_scalar_prefetch=2,
        grid=(nb,),
        in_specs=[pl.BlockSpec((EXPERT_BLOCK, D_MODEL), lambda i, be, nu: (jnp.minimum(i, nu[0] - 1), 0)),
                  pl.BlockSpec((1, D_MODEL, EXPERT_FF), lambda i, be, nu: (be[i], 0, 0)),
                  pl.BlockSpec((1, D_MODEL, EXPERT_FF), lambda i, be, nu: (be[i], 0, 0)),
                  pl.BlockSpec((1, EXPERT_FF, D_MODEL), lambda i, be, nu: (be[i], 0, 0))],
        out_specs=pl.BlockSpec((EXPERT_BLOCK, D_MODEL), lambda i, be, nu: (i, 0)),
    )
    return pl.pallas_call(
        _expert_kernel,
        grid_spec=grid_spec,
        out_shape=jax.ShapeDtypeStruct((cap, D_MODEL), F32),
        compiler_params=_params(("arbitrary",)),
        name="expert_mlp",
    )(blk_e, n_used, xb, wg, wu, wd)


def _final_kernel(x1_ref, mod_ref, rt_ref, g0_ref, g1_ref, o_ref):
    m = mod_ref[0]
    rt = rt_ref[...]
    moe = g0_ref[...] * rt[:, 2:3] + g1_ref[...] * rt[:, 3:4]
    o_ref[...] = x1_ref[...] + m[5:6] * moe


def _final(x1, mod_tiles, rt, g0, g1, tok_off, n_tok, tm):
    b0 = tok_off // tm
    tok = lambda n: pl.BlockSpec((tm, n), lambda i: (b0 + i, 0))
    return pl.pallas_call(
        _final_kernel,
        grid=(n_tok // tm,),
        in_specs=[tok(D_MODEL), pl.BlockSpec((1, 6, D_MODEL), lambda i: (b0 + i, 0, 0)), tok(LANES),
                  tok(D_MODEL), tok(D_MODEL)],
        out_specs=pl.BlockSpec((tm, D_MODEL), lambda i: (i, 0)),
        out_shape=jax.ShapeDtypeStruct((n_tok, D_MODEL), F32),
        compiler_params=_params(("parallel",)),
        name="moe_combine",
    )(x1, mod_tiles, rt, g0, g1)


def _rope_tables(seq):
    pos = jnp.arange(seq, dtype=F32)
    inv = 1.0 / (ROPE_THETA ** (jnp.arange(0, QK_DIM, 2, dtype=F32) / QK_DIM))
    ang = pos[:, None] * inv[None, :]
    cos, sin = jnp.cos(ang), jnp.sin(ang)
    cos_t = jnp.tile(cos, (1, LANES // (QK_DIM // 2)))
    sin_t = jnp.tile(jnp.concatenate([-sin, sin], axis=1), (1, LANES // QK_DIM))
    return cos_t, sin_t


def _dispatch(e_idx, n_tok):
    n_slots = n_tok * TOP_K_INNER
    flat_e = e_idx.reshape(-1)
    onehot = (flat_e[:, None] == jnp.arange(N_EXPERTS, dtype=jnp.int32)[None, :]).astype(jnp.int32)
    csum = jnp.cumsum(onehot, axis=0)
    counts = csum[-1]
    rank = jnp.sum(onehot * csum, axis=1) - 1
    padded = ((counts + EXPERT_BLOCK - 1) // EXPERT_BLOCK) * EXPERT_BLOCK
    pad_end = jnp.cumsum(padded)
    pad_start = pad_end - padded
    dest = (pad_start[flat_e] + rank).astype(jnp.int32)
    cap = n_slots + N_EXPERTS * EXPERT_BLOCK
    nb = cap // EXPERT_BLOCK
    tok = jnp.arange(n_slots, dtype=jnp.int32) // TOP_K_INNER
    tok_buf = jnp.zeros((cap,), jnp.int32).at[dest].set(tok)
    blk_e = jnp.minimum(jnp.searchsorted(pad_end, jnp.arange(nb, dtype=jnp.int32) * EXPERT_BLOCK, side='right'),
                        N_EXPERTS - 1).astype(jnp.int32)
    n_used = (pad_end[-1] // EXPERT_BLOCK).astype(jnp.int32).reshape(1)
    return dest.reshape(n_tok, TOP_K_INNER), tok_buf, blk_e, n_used


def kernel(x_prompt, x_sample, c_prompt, c_sample, w_ada, b_ada, norm1_g, w_in, q_norm_g, k_norm_g, lambda_q1, lambda_k1, lambda_q2, lambda_k2, attn_subln_g, w_attn_o, conv_w, conv_b, dt_bias, a_log, d_skip, ssd_norm_g, w_ssd_o, w_out, norm2_g, w_group, b_group, w_router, b_router, w_gate_e, w_up_e, w_down_e):
    groups = [(x_prompt, c_prompt), (x_sample, c_sample)]
    seqs = [(x.shape[0], x.shape[1]) for x, _ in groups]
    n_tok = sum(b * s for b, s in seqs)
    min_seq = min(s for _, s in seqs)
    tm = min(1024, min_seq)
    tp = min(256, min_seq)
    tq = min(256, min_seq)
    tk = min(512, min_seq)
    layer = 0

    x = jnp.concatenate([g[0].reshape(-1, D_MODEL) for g in groups], axis=0)
    c = jnp.concatenate([g[1] for g in groups], axis=0)
    n_batch = c.shape[0]
    c_pad = jnp.pad(c, ((0, (-n_batch) % SUBLANES), (0, 0)))
    mod = _ada(c_pad, w_ada[layer], b_ada[layer]).reshape(-1, 6, D_MODEL)
    tok_batch = jnp.concatenate([jnp.repeat(jnp.arange(b, dtype=jnp.int32), s) + off
                                 for (b, s), off in zip(seqs, [0, seqs[0][0]])])
    mod_tm = mod[tok_batch[::tm]]
    mod_tp = mod[tok_batch[::tp]]

    h = _norm_mod(x, mod_tm, norm1_g[layer], tm)

    w_in_b = w_in[layer].astype(BF16)
    tabs = [_rope_tables(s) for _, s in seqs]
    cos_t = jnp.concatenate([jnp.tile(tb[0], (b, 1)) for tb, (b, _) in zip(tabs, seqs)], axis=0)
    sin_t = jnp.concatenate([jnp.tile(tb[1], (b, 1)) for tb, (b, _) in zip(tabs, seqs)], axis=0)
    half = jnp.arange(LANES) // QK_DIM
    bd = (half[:, None] == half[None, :]).astype(BF16)
    gq = (jnp.tile(q_norm_g[layer], LANES // QK_DIM) * (QK_DIM ** -0.5 * LOG2_E)).reshape(1, LANES)
    gk = jnp.tile(k_norm_g[layer], LANES // QK_DIM).reshape(1, LANES)
    q = _qk_proj(h, w_in_b[:, OFF_Q:OFF_K], gq, cos_t, sin_t, bd, tm, "q_proj")
    k = _qk_proj(h, w_in_b[:, OFF_K:OFF_V], gk, cos_t, sin_t, bd, tm, "k_proj")
    vt3 = _vt_proj(h, w_in_b[:, OFF_V:OFF_Z], tm, tk)
    z = _matmul(h, w_in_b[:, OFF_Z:OFF_XBC], F32, tm, 1024, "z_proj")
    xbc = _matmul(h, w_in_b[:, OFF_XBC:OFF_DT], F32, tm, 1024, "xbc_proj")
    w_dt = jnp.pad(w_in_b[:, OFF_DT:OFF_GATE], ((0, 0), (0, LANES - DT_COLS)))
    dt = _matmul(h, w_dt, F32, tm, LANES, "dt_proj")
    gates = _matmul(h, w_in_b[:, OFF_GATE:], F32, tm, 1024, "gate_proj")

    k3 = k.reshape(n_tok // tk, tk, ATTN_WIDTH)
    lams = [v[layer].reshape(1, QK_DIM) for v in (lambda_q1, lambda_k1, lambda_q2, lambda_k2)]
    o = None
    off = 0
    for b, s in seqs:
        o = _attention(q, k3, vt3, lams, o, off, b, s, tq, tk)
        off += b * s

    seq_starts = []
    off = 0
    for b, s in seqs:
        seq_starts += [(off + i * s, s) for i in range(b)]
        off += b * s
    tile_start = jnp.zeros((n_tok // tp,), jnp.int32)
    tile_end = jnp.zeros((n_tok // tp,), jnp.int32)
    nc = n_tok // CHUNK
    chunk_reset = jnp.zeros((nc,), jnp.int32)
    bwd_idx = jnp.zeros((nc,), jnp.int32)
    for st, s in seq_starts:
        tile_start = tile_start.at[st // tp].set(1)
        tile_end = tile_end.at[(st + s) // tp - 1].set(1)
        c0, c1 = st // CHUNK, (st + s) // CHUNK
        chunk_reset = chunk_reset.at[c0].set(1)
        bwd_idx = bwd_idx.at[c0:c1].set(jnp.arange(c1 - 1, c0 - 1, -1, dtype=jnp.int32))
    fwd_idx = jnp.arange(nc, dtype=jnp.int32)
    xs, bcm = _conv(xbc, conv_w[layer], conv_b[layer], tile_start, tile_end, tp)
    dtt = dt.T
    dtb = dt_bias[layer].reshape(-1)
    alg = a_log[layer].reshape(-1)
    yf = _ssd(xs, bcm, dt, dtt, dtb, alg, fwd_idx, chunk_reset, False)
    yb = _ssd(xs, bcm, dt, dtt, dtb, alg, bwd_idx, chunk_reset, True)

    subg = attn_subln_g[layer].reshape(1, V_DIM)
    dskip = jnp.repeat(d_skip[layer], SSD_HEAD_DIM).reshape(1, D_INNER)
    x1, h2b, h2f = _post(o, yf, yb, xs, z, gates, x, mod_tp, subg, dskip,
                         ssd_norm_g[layer].reshape(1, D_INNER), norm2_g[layer].reshape(1, D_MODEL),
                         w_attn_o[layer].astype(BF16), w_ssd_o[layer].astype(BF16),
                         w_out[layer].astype(BF16), tp)

    n_rt = N_EXPERT_GROUPS + N_EXPERTS
    w_rt = jnp.pad(jnp.concatenate([w_group[layer], w_router[layer]], axis=1), ((0, 0), (0, LANES - n_rt)))
    b_rt = jnp.pad(jnp.concatenate([b_group[layer], b_router[layer]]), (0, LANES - n_rt)).reshape(1, LANES)
    rt = _router(h2f, w_rt, b_rt, tp)
    e_idx = rt[:, :TOP_K_INNER].astype(jnp.int32)
    dest, tok_buf, blk_e, n_used = _dispatch(e_idx, n_tok)
    xb = h2b[tok_buf]
    yb_e = _experts(xb, blk_e, n_used, w_gate_e[layer].astype(BF16), w_up_e[layer].astype(BF16),
                    w_down_e[layer].astype(BF16))
    g0 = yb_e[dest[:, 0]]
    g1 = yb_e[dest[:, 1]]

    outs = []
    off = 0
    for (b, s), (xg, _) in zip(seqs, groups):
        y = _final(x1, mod_tp, rt, g0, g1, off, b * s, tp)
        outs.append(y.reshape(xg.shape))
        off += b * s
    return tuple(outs)
```

```python
import functools
import math

import jax
import jax.numpy as jnp
from jax import lax
from jax.experimental import pallas as pl
from jax.experimental.pallas import tpu as pltpu

F32 = jnp.float32
BF16 = jnp.bfloat16

D_MODEL = 1024
ATTN_HEADS = 8
QK_DIM = 64
V_DIM = 2 * QK_DIM
ATTN_WIDTH = ATTN_HEADS * V_DIM
ROPE_THETA = 10000.0
D_INNER = 2048
SSD_HEAD_DIM = 64
SSD_HEADS = D_INNER // SSD_HEAD_DIM
SSD_GROUPS = 4
HEADS_PER_GROUP = SSD_HEADS // SSD_GROUPS
D_STATE = 128
CONV_K = 5
CONV_CH = D_INNER + 2 * SSD_GROUPS * D_STATE
CHUNK = 128
N_EXPERT_GROUPS = 4
EXPERTS_PER_GROUP = 8
N_EXPERTS = N_EXPERT_GROUPS * EXPERTS_PER_GROUP
TOP_K_INNER = 2
EXPERT_FF = 512
RMS_EPS = 1e-6
LAMBDA_INIT = 0.8 - 0.6 * math.exp(-0.3 * 0)

LANES = 128
SUBLANES = 8
VMEM_LIMIT = 56 * 1024 * 1024

Q_COLS = ATTN_HEADS * 2 * QK_DIM
K_COLS = Q_COLS
V_COLS = ATTN_WIDTH
Z_COLS = D_INNER
XBC_COLS = CONV_CH
DT_COLS = 2 * SSD_HEADS
GATE_COLS = 2 * D_MODEL
OFF_Q = 0
OFF_K = OFF_Q + Q_COLS
OFF_V = OFF_K + K_COLS
OFF_Z = OFF_V + V_COLS
OFF_XBC = OFF_Z + Z_COLS
OFF_DT = OFF_XBC + XBC_COLS
OFF_GATE = OFF_DT + DT_COLS

EXPERT_BLOCK = 256
ONES_ROWS = 16
LOG2_E = math.log2(math.e)


def _params(sem):
    return pltpu.CompilerParams(dimension_semantics=sem, vmem_limit_bytes=VMEM_LIMIT)


def _dot(a, b):
    return jnp.dot(a, b, preferred_element_type=F32)


def _dot_tn(a, b):
    return lax.dot_general(a, b, (((0,), (0,)), ((), ())), preferred_element_type=F32)


def _dot_nt(a, b):
    return lax.dot_general(a, b, (((1,), (1,)), ((), ())), preferred_element_type=F32)


def _split3(a):
    hi = a.astype(BF16)
    r = a - hi.astype(F32)
    mid = r.astype(BF16)
    lo = (r - mid.astype(F32)).astype(BF16)
    return hi, mid, lo


def _dot_left01(m01, a):
    hi, mid, lo = _split3(a)
    return _dot(m01, hi) + _dot(m01, mid) + _dot(m01, lo)


def _dot_right01(a, m01):
    hi, mid, lo = _split3(a)
    return _dot(hi, m01) + _dot(mid, m01) + _dot(lo, m01)


def _dot_f32(a, b):
    a0, a1, a2 = _split3(a)
    b0, b1, b2 = _split3(b)
    return (_dot(a0, b0) + (_dot(a0, b1) + _dot(a1, b0))
            + (_dot(a0, b2) + _dot(a2, b0) + _dot(a1, b1)))


def _sigmoid(x):
    return 1.0 / (1.0 + jnp.exp(-x))


def _silu(x):
    return x * _sigmoid(x)


def _softplus(x):
    return jnp.maximum(x, 0.0) + jnp.log1p(jnp.exp(-jnp.abs(x)))


def _ada_kernel(c_ref, w_ref, b_ref, o_ref):
    o_ref[...] = _dot_f32(_silu(c_ref[...]), w_ref[...]) + b_ref[...]


def _ada(c_pad, w_ada, b_ada):
    rows = c_pad.shape[0]
    n = w_ada.shape[1]
    tn = 1024
    return pl.pallas_call(
        _ada_kernel,
        grid=(n // tn,),
        in_specs=[pl.BlockSpec((rows, D_MODEL), lambda j: (0, 0)),
                  pl.BlockSpec((D_MODEL, tn), lambda j: (0, j)),
                  pl.BlockSpec((1, tn), lambda j: (0, j))],
        out_specs=pl.BlockSpec((rows, tn), lambda j: (0, j)),
        out_shape=jax.ShapeDtypeStruct((rows, n), F32),
        compiler_params=_params(("arbitrary",)),
        name="ada_mod",
    )(c_pad, w_ada, b_ada.reshape(1, n))


def _norm_mod_kernel(x_ref, mod_ref, g_ref, o_ref, *, shift_row, scale_row):
    x = x_ref[...]
    r = lax.rsqrt(jnp.mean(x * x, axis=-1, keepdims=True) + RMS_EPS)
    m = mod_ref[0]
    h = ((x * r) * g_ref[...]) * (1.0 + m[scale_row:scale_row + 1]) + m[shift_row:shift_row + 1]
    o_ref[...] = h.astype(o_ref.dtype)


def _norm_mod(x, mod_tiles, g, tm):
    t = x.shape[0]
    return pl.pallas_call(
        functools.partial(_norm_mod_kernel, shift_row=0, scale_row=1),
        grid=(t // tm,),
        in_specs=[pl.BlockSpec((tm, D_MODEL), lambda i: (i, 0)),
                  pl.BlockSpec((1, 6, D_MODEL), lambda i: (i, 0, 0)),
                  pl.BlockSpec((1, D_MODEL), lambda i: (0, 0))],
        out_specs=pl.BlockSpec((tm, D_MODEL), lambda i: (i, 0)),
        out_shape=jax.ShapeDtypeStruct((t, D_MODEL), BF16),
        compiler_params=_params(("parallel",)),
        name="norm1_mod",
    )(x, mod_tiles, g.reshape(1, D_MODEL))


def _mm_kernel(a_ref, w_ref, o_ref):
    o_ref[...] = _dot(a_ref[...], w_ref[...]).astype(o_ref.dtype)


def _matmul(a, w, out_dtype, tm, tn, name):
    m, k = a.shape
    n = w.shape[1]
    return pl.pallas_call(
        _mm_kernel,
        grid=(n // tn, m // tm),
        in_specs=[pl.BlockSpec((tm, k), lambda j, i: (i, 0)),
                  pl.BlockSpec((k, tn), lambda j, i: (0, j))],
        out_specs=pl.BlockSpec((tm, tn), lambda j, i: (i, j)),
        out_shape=jax.ShapeDtypeStruct((m, n), out_dtype),
        compiler_params=_params(("parallel", "parallel")),
        name=name,
    )(a, w)


def _qk_kernel(a_ref, w_ref, g_ref, cos_ref, sin_ref, bd_ref, o_ref):
    acc = _dot(a_ref[...], w_ref[...])
    tm = acc.shape[0]
    lane = lax.broadcasted_iota(jnp.int32, (tm, LANES), 1)
    first = (lane & (QK_DIM // 2)) == 0
    cos = cos_ref[...]
    sin = sin_ref[...]
    g = g_ref[...]
    bd = bd_ref[...]
    for h in range(ATTN_HEADS):
        x = acc[:, h * LANES:(h + 1) * LANES]
        sq = x * x
        hi = sq.astype(BF16)
        lo = (sq - hi.astype(F32)).astype(BF16)
        ss = _dot(hi, bd) + _dot(lo, bd)
        r = lax.rsqrt(ss * (1.0 / QK_DIM) + RMS_EPS)
        xn = (x * r) * g
        partner = jnp.where(first, pltpu.roll(xn, LANES - QK_DIM // 2, 1),
                            pltpu.roll(xn, QK_DIM // 2, 1))
        o_ref[:, h * LANES:(h + 1) * LANES] = (xn * cos + partner * sin).astype(o_ref.dtype)


def _qk_proj(h, w, g128, cos_t, sin_t, bd, tm, name):
    t = h.shape[0]
    n = w.shape[1]
    return pl.pallas_call(
        _qk_kernel,
        grid=(t // tm,),
        in_specs=[pl.BlockSpec((tm, D_MODEL), lambda i: (i, 0)),
                  pl.BlockSpec((D_MODEL, n), lambda i: (0, 0)),
                  pl.BlockSpec((1, LANES), lambda i: (0, 0)),
                  pl.BlockSpec((tm, LANES), lambda i: (i, 0)),
                  pl.BlockSpec((tm, LANES), lambda i: (i, 0)),
                  pl.BlockSpec((LANES, LANES), lambda i: (0, 0))],
        out_specs=pl.BlockSpec((tm, n), lambda i: (i, 0)),
        out_shape=jax.ShapeDtypeStruct((t, n), BF16),
        compiler_params=_params(("parallel",)),
        name=name,
    )(h, w, g128, cos_t, sin_t, bd)


def _vt_kernel(a_ref, w_ref, o_ref, *, tk):
    acc = _dot(a_ref[...], w_ref[...])
    for c in range(acc.shape[0] // tk):
        o_ref[c] = acc[c * tk:(c + 1) * tk, :].T.astype(o_ref.dtype)


def _vt_proj(h, w, tm, tk):
    t = h.shape[0]
    n = w.shape[1]
    return pl.pallas_call(
        functools.partial(_vt_kernel, tk=tk),
        grid=(t // tm,),
        in_specs=[pl.BlockSpec((tm, D_MODEL), lambda i: (i, 0)),
                  pl.BlockSpec((D_MODEL, n), lambda i: (0, 0))],
        out_specs=pl.BlockSpec((tm // tk, n, tk), lambda i: (i, 0, 0)),
        out_shape=jax.ShapeDtypeStruct((t // tk, n, tk), BF16),
        compiler_params=_params(("parallel",)),
        name="v_proj_t",
    )(h, w)


def _attn_kernel(lq1_ref, lk1_ref, lq2_ref, lk2_ref, q_ref, k_ref, vt_ref, o_ref, q2t_s, s_s, m_s, acc_s,
                 *, tq, nkv, unroll):
    tk = k_ref.shape[1]
    qt = q_ref[...].astype(F32).T
    row = lax.broadcasted_iota(jnp.int32, qt.shape, 0)
    q2t_s[:, :tq] = jnp.where(row < QK_DIM, qt, 0.0).astype(BF16)
    q2t_s[:, tq:] = jnp.where(row >= QK_DIM, qt, 0.0).astype(BF16)
    m_s[...] = jnp.full(m_s.shape, -jnp.inf, F32)
    acc_s[...] = jnp.zeros(acc_s.shape, F32)
    ones = jnp.ones((ONES_ROWS, tk), BF16)
    s_s[0] = _dot(k_ref[0], q2t_s[...])

    def step(j, par, compute_next):
        if compute_next:
            s_s[1 - par] = _dot(k_ref[j + 1], q2t_s[...])
        st = s_s[par]
        m_old = m_s[...]
        m_new = jnp.maximum(m_old, jnp.max(st, axis=0, keepdims=True))
        alpha = jnp.exp2(m_old - m_new)
        p = jnp.exp2(st - m_new).astype(BF16)
        lhs = jnp.concatenate([vt_ref[j], ones], axis=0)
        acc_s[...] = alpha * acc_s[...] + _dot(lhs, p)
        m_s[...] = m_new

    n_trips = (nkv - 1) // unroll

    def trip(t, carry):
        for u in range(unroll):
            step(unroll * t + u, u % 2, True)
        return carry

    lax.fori_loop(0, n_trips, trip, 0)
    for j in range(unroll * n_trips, nkv):
        step(j, j % 2, j + 1 < nkv)

    lam = (jnp.exp(jnp.sum(lq1_ref[...] * lk1_ref[...], axis=1, keepdims=True))
           - jnp.exp(jnp.sum(lq2_ref[...] * lk2_ref[...], axis=1, keepdims=True)) + LAMBDA_INIT)
    ot = acc_s[:V_DIM, :] / acc_s[V_DIM:V_DIM + 1, :]
    o_ref[...] = (ot[:, :tq] - lam * ot[:, tq:]).T


def _attention(q, k3, vt3, lams, tok_off, batch, seq, tq, tk):
    nkv = seq // tk
    qb0 = tok_off // tq
    sb0 = tok_off // seq
    nq = seq // tq
    unroll = 4 if nkv >= 16 else 2
    lam_spec = pl.BlockSpec((1, QK_DIM), lambda b, h, i: (0, 0))
    return pl.pallas_call(
        functools.partial(_attn_kernel, tq=tq, nkv=nkv, unroll=unroll),
        grid=(batch, ATTN_HEADS, nq),
        in_specs=[lam_spec, lam_spec, lam_spec, lam_spec,
                  pl.BlockSpec((tq, LANES), lambda b, h, i: (qb0 + b * nq + i, h)),
                  pl.BlockSpec((nkv, tk, LANES), lambda b, h, i: (sb0 + b, 0, h)),
                  pl.BlockSpec((nkv, LANES, tk), lambda b, h, i: (sb0 + b, h, 0))],
        out_specs=pl.BlockSpec((tq, LANES), lambda b, h, i: (b * nq + i, h)),
        out_shape=jax.ShapeDtypeStruct((batch * seq, ATTN_WIDTH), F32),
        scratch_shapes=[pltpu.VMEM((LANES, 2 * tq), BF16),
                        pltpu.VMEM((2, tk, 2 * tq), F32),
                        pltpu.VMEM((1, 2 * tq), F32),
                        pltpu.VMEM((V_DIM + ONES_ROWS, 2 * tq), F32)],
        compiler_params=_params(("parallel", "parallel", "arbitrary")),
        name="diff_attention",
    )(*lams, q, k3, vt3)


def _conv_kernel(start_ref, end_ref, prev_ref, cur_ref, next_ref, w_ref, b_ref, xs_ref, bc_ref, ext_s, *, tm):
    i = pl.program_id(0)
    halo = SUBLANES
    ext_s[0:halo, :] = jnp.where(start_ref[i] == 1, 0.0, prev_ref[...])
    ext_s[halo:halo + tm, :] = cur_ref[...]
    ext_s[halo + tm:2 * halo + tm, :] = jnp.where(end_ref[i] == 1, 0.0, next_ref[...])
    pad = (CONV_K - 1) // 2
    acc = jnp.broadcast_to(b_ref[...], (tm, CONV_CH))
    for d in range(CONV_K):
        acc = acc + w_ref[d:d + 1, :] * ext_s[halo - pad + d:halo - pad + d + tm, :]
    y = _silu(acc)
    xs_ref[...] = y[:, :D_INNER]
    bc_ref[...] = y[:, D_INNER:].astype(bc_ref.dtype)


def _conv(xbc, conv_w, conv_b, start_flags, end_flags, tm):
    t = xbc.shape[0]
    rb = tm // SUBLANES
    last = t // SUBLANES - 1
    w_pad = jnp.zeros((SUBLANES, CONV_CH), F32).at[:CONV_K].set(conv_w)
    grid_spec = pltpu.PrefetchScalarGridSpec(
        num_scalar_prefetch=2,
        grid=(t // tm,),
        in_specs=[pl.BlockSpec((SUBLANES, CONV_CH), lambda i, s, e: (jnp.maximum(i * rb - 1, 0), 0)),
                  pl.BlockSpec((tm, CONV_CH), lambda i, s, e: (i, 0)),
                  pl.BlockSpec((SUBLANES, CONV_CH), lambda i, s, e: (jnp.minimum((i + 1) * rb, last), 0)),
                  pl.BlockSpec((SUBLANES, CONV_CH), lambda i, s, e: (0, 0)),
                  pl.BlockSpec((1, CONV_CH), lambda i, s, e: (0, 0))],
        out_specs=[pl.BlockSpec((tm, D_INNER), lambda i, s, e: (i, 0)),
                   pl.BlockSpec((tm, CONV_CH - D_INNER), lambda i, s, e: (i, 0))],
        scratch_shapes=[pltpu.VMEM((tm + 2 * SUBLANES, CONV_CH), F32)],
    )
    return pl.pallas_call(
        functools.partial(_conv_kernel, tm=tm),
        grid_spec=grid_spec,
        out_shape=[jax.ShapeDtypeStruct((t, D_INNER), F32),
                   jax.ShapeDtypeStruct((t, CONV_CH - D_INNER), BF16)],
        compiler_params=_params(("arbitrary",)),
        name="conv_silu",
    )(start_flags, end_flags, xbc, xbc, xbc, w_pad, conv_b.reshape(1, CONV_CH))


def _ssd_kernel(idx_ref, reset_ref, xs_ref, bc_ref, dt_ref, dtt_ref, bias_r_ref, bias_c_ref,
                alog_r_ref, alog_c_ref, tril_ref, triu_ref, e01_ref, y_ref, state_s, *, rev):
    i = pl.program_id(0)
    nh = SSD_HEADS

    @pl.when(reset_ref[i] == 1)
    def _():
        state_s[...] = jnp.zeros(state_s.shape, F32)

    xs = xs_ref[...]
    bc = bc_ref[...]
    dtn = _softplus(dt_ref[...] + bias_r_ref[...])
    a = dtn * (-jnp.exp(alog_r_ref[...]))
    pinc = _dot_left01(tril_ref[...], a)
    pex = pinc - a
    tot = pinc[CHUNK - 1:CHUNK, :]

    if rev:
        dec_off = jnp.exp(tot - pex)
        dec_st = jnp.exp(pex)
    else:
        dec_off = jnp.exp(pinc)
        dec_st = jnp.exp(tot - pinc)
    cd = jnp.broadcast_to(jnp.exp(tot), (SUBLANES, LANES))
    stacked = jnp.concatenate([dec_st * dtn, dec_off, cd], axis=0)
    ex = _dot_right01(stacked, e01_ref[...])
    sc_st = ex[:CHUNK]
    sc_off = ex[CHUNK:2 * CHUNK]
    sc_cd = ex[2 * CHUNK:2 * CHUNK + 1]

    w = (xs * sc_st).astype(BF16)
    gw = HEADS_PER_GROUP * SSD_HEAD_DIM
    gn = SSD_GROUPS * D_STATE
    for g in range(SSD_GROUPS):
        bg = bc[:, g * D_STATE:(g + 1) * D_STATE]
        cg = bc[:, gn + g * D_STATE:gn + (g + 1) * D_STATE]
        st = state_s[:, g * gw:(g + 1) * gw]
        y_ref[:, g * gw:(g + 1) * gw] = _dot(cg, st.astype(BF16)) * sc_off[:, g * gw:(g + 1) * gw]
        new = _dot_tn(bg, w[:, g * gw:(g + 1) * gw])
        state_s[:, g * gw:(g + 1) * gw] = st * sc_cd[:, g * gw:(g + 1) * gw] + new

    if rev:
        return

    dtnt = _softplus(dtt_ref[...] + bias_c_ref[...])
    at = dtnt * (-jnp.exp(alog_c_ref[...]))
    pinct = _dot_right01(at, triu_ref[...])
    pext = pinct - at
    li = lax.broadcasted_iota(jnp.int32, (CHUNK, CHUNK), 0)
    si = lax.broadcasted_iota(jnp.int32, (CHUNK, CHUNK), 1)
    lower = si <= li
    strict_lower = si < li
    strict_upper = si > li
    lane = lax.broadcasted_iota(jnp.int32, (CHUNK, LANES), 1)
    first_head = lane < SSD_HEAD_DIM
    xb = xs.astype(BF16)
    for g in range(SSD_GROUPS):
        bg = bc[:, g * D_STATE:(g + 1) * D_STATE]
        cg = bc[:, gn + g * D_STATE:gn + (g + 1) * D_STATE]
        cb = _dot_nt(cg, bg)
        for hp in range(HEADS_PER_GROUP // 2):
            pair = []
            x_pair = xb[:, g * gw + hp * LANES:g * gw + (hp + 1) * LANES]
            for u in range(2):
                h = g * HEADS_PER_GROUP + 2 * hp + u
                arg = jnp.where(lower, pinc[:, h:h + 1] - pinct[h:h + 1, :],
                                pext[nh + h:nh + h + 1, :] - pex[:, nh + h:nh + h + 1])
                dtf = dtnt[h:h + 1, :]
                dtb = dtnt[nh + h:nh + h + 1, :]
                coef = jnp.where(strict_lower, dtf, jnp.where(strict_upper, dtb, dtf + dtb))
                mh = (cb * jnp.exp(arg) * coef).astype(BF16)
                pair.append(_dot(mh, x_pair))
            col = g * gw + hp * LANES
            y_ref[:, col:col + LANES] = y_ref[:, col:col + LANES] + jnp.where(first_head, pair[0], pair[1])


def _ssd(xs, bc, dt, dtt, dt_bias, a_log, idx, reset, rev):
    t = xs.shape[0]
    nc = t // CHUNK
    pad = LANES - DT_COLS
    bias_r = jnp.pad(dt_bias.reshape(1, DT_COLS), ((0, 0), (0, pad)))
    alog_r = jnp.pad(a_log.reshape(1, DT_COLS), ((0, 0), (0, pad)))
    bias_c = bias_r.reshape(LANES, 1)
    alog_c = alog_r.reshape(LANES, 1)
    r = jnp.arange(CHUNK)
    tril = (r[None, :] <= r[:, None]).astype(BF16)
    triu = (r[:, None] <= r[None, :]).astype(BF16)
    lo = SSD_HEADS if rev else 0
    e01 = (jnp.arange(D_INNER)[None, :] // SSD_HEAD_DIM == jnp.arange(LANES)[:, None] - lo).astype(BF16)
    const = lambda shape: pl.BlockSpec(shape, lambda i, ix, rs: (0, 0))
    grid_spec = pltpu.PrefetchScalarGridSpec(
        num_scalar_prefetch=2,
        grid=(nc,),
        in_specs=[pl.BlockSpec((CHUNK, D_INNER), lambda i, ix, rs: (ix[i], 0)),
                  pl.BlockSpec((CHUNK, CONV_CH - D_INNER), lambda i, ix, rs: (ix[i], 0)),
                  pl.BlockSpec((CHUNK, LANES), lambda i, ix, rs: (ix[i], 0)),
                  pl.BlockSpec((LANES, CHUNK), lambda i, ix, rs: (0, ix[i])),
                  const((1, LANES)), const((LANES, 1)), const((1, LANES)), const((LANES, 1)),
                  const((CHUNK, CHUNK)), const((CHUNK, CHUNK)), const((LANES, D_INNER))],
        out_specs=pl.BlockSpec((CHUNK, D_INNER), lambda i, ix, rs: (ix[i], 0)),
        scratch_shapes=[pltpu.VMEM((D_STATE, D_INNER), F32)],
    )
    return pl.pallas_call(
        functools.partial(_ssd_kernel, rev=rev),
        grid_spec=grid_spec,
        out_shape=jax.ShapeDtypeStruct((t, D_INNER), F32),
        compiler_params=_params(("arbitrary",)),
        name="ssd_bwd" if rev else "ssd_fwd",
    )(idx, reset, xs, bc, dt, dtt, bias_r, bias_c, alog_r, alog_c, tril, triu, e01)


def _post_kernel(oa_ref, ob_ref, yf_ref, yb_ref, xs_ref, z_ref, gt_ref, x_ref, mod_ref, subg_ref, dskip_ref,
                 ssdg_ref, n2g_ref, wa_ref, ws_ref, wo_ref, x1_ref, h2b_ref, h2f_ref, an_s, yn_s, *, n_first):
    m = mod_ref[0]
    o = jnp.where(pl.program_id(0) < n_first, oa_ref[...], ob_ref[...])
    for h in range(ATTN_HEADS):
        oh = o[:, h * V_DIM:(h + 1) * V_DIM]
        r = lax.rsqrt(jnp.mean(oh * oh, axis=-1, keepdims=True) + RMS_EPS)
        an_s[:, h * V_DIM:(h + 1) * V_DIM] = (((oh * r) * subg_ref[...]) * (1.0 - LAMBDA_INIT)).astype(BF16)
    attn_d = _dot(an_s[...], wa_ref[...])

    y = (yf_ref[...] + yb_ref[...]) + dskip_ref[...] * xs_ref[...]
    y = y * _silu(z_ref[...])
    gw = D_INNER // SSD_GROUPS
    for g in range(SSD_GROUPS):
        yg = y[:, g * gw:(g + 1) * gw]
        r = lax.rsqrt(jnp.mean(yg * yg, axis=-1, keepdims=True) + RMS_EPS)
        yn_s[:, g * gw:(g + 1) * gw] = ((yg * r) * ssdg_ref[:, g * gw:(g + 1) * gw]).astype(BF16)
    ssd_d = _dot(yn_s[...], ws_ref[...])

    gt = gt_ref[...]
    mix = _sigmoid(gt[:, :D_MODEL]) * attn_d + _sigmoid(gt[:, D_MODEL:]) * ssd_d
    mixed = _dot(mix.astype(BF16), wo_ref[...])
    x1 = x_ref[...] + m[2:3] * mixed
    x1_ref[...] = x1
    r = lax.rsqrt(jnp.mean(x1 * x1, axis=-1, keepdims=True) + RMS_EPS)
    h2 = ((x1 * r) * n2g_ref[...]) * (1.0 + m[4:5]) + m[3:4]
    h2f_ref[...] = h2
    h2b_ref[...] = h2.astype(BF16)


def _post(oa, ob, yf, yb, xs, z, gates, x, mod_tiles, subg, dskip, ssdg, n2g, wa, ws, wo, tm):
    t = x.shape[0]
    n_first = oa.shape[0] // tm
    tok = lambda n: pl.BlockSpec((tm, n), lambda i: (i, 0))
    const = lambda a: pl.BlockSpec(a.shape, lambda i: (0, 0))
    return pl.pallas_call(
        functools.partial(_post_kernel, n_first=n_first),
        grid=(t // tm,),
        in_specs=[pl.BlockSpec((tm, ATTN_WIDTH), lambda i: (jnp.minimum(i, n_first - 1), 0)),
                  pl.BlockSpec((tm, ATTN_WIDTH), lambda i: (jnp.maximum(i - n_first, 0), 0)),
                  tok(D_INNER), tok(D_INNER), tok(D_INNER), tok(D_INNER), tok(GATE_COLS),
                  tok(D_MODEL), pl.BlockSpec((1, 6, D_MODEL), lambda i: (i, 0, 0)),
                  const(subg), const(dskip), const(ssdg), const(n2g), const(wa), const(ws), const(wo)],
        out_specs=[tok(D_MODEL), tok(D_MODEL), tok(D_MODEL)],
        out_shape=[jax.ShapeDtypeStruct((t, D_MODEL), F32),
                   jax.ShapeDtypeStruct((t, D_MODEL), BF16),
                   jax.ShapeDtypeStruct((t, D_MODEL), F32)],
        scratch_shapes=[pltpu.VMEM((tm, ATTN_WIDTH), BF16), pltpu.VMEM((tm, D_INNER), BF16)],
        compiler_params=_params(("parallel",)),
        name="merge_out_proj",
    )(oa, ob, yf, yb, xs, z, gates, x, mod_tiles, subg, dskip, ssdg, n2g, wa, ws, wo)


def _router_kernel(h_ref, w_ref, b_ref, o_ref):
    logits = _dot_f32(h_ref[...], w_ref[...]) + b_ref[...]
    tm = logits.shape[0]
    lane = lax.broadcasted_iota(jnp.int32, (tm, LANES), 1)
    lane_f = lane.astype(F32)
    big = float(LANES)
    neg = -jnp.inf
    gl = jnp.where(lane < N_EXPERT_GROUPS, logits, neg)
    gmax = jnp.max(gl, axis=1, keepdims=True)
    g_sel = jnp.min(jnp.where(gl == gmax, lane_f, big), axis=1, keepdims=True)
    g_w = 1.0 / jnp.sum(jnp.exp(gl - gmax), axis=1, keepdims=True)
    e_lane = lane - N_EXPERT_GROUPS
    e_group = (e_lane >> 3).astype(F32)
    in_group = (e_lane >= 0) & (e_lane < N_EXPERTS) & (e_group == g_sel)
    el = jnp.where(in_group, logits, neg)
    v1 = jnp.max(el, axis=1, keepdims=True)
    i1 = jnp.min(jnp.where(el == v1, lane_f, big), axis=1, keepdims=True)
    el2 = jnp.where(lane_f == i1, neg, el)
    v2 = jnp.max(el2, axis=1, keepdims=True)
    i2 = jnp.min(jnp.where(el2 == v2, lane_f, big), axis=1, keepdims=True)
    e2 = jnp.exp(v2 - v1)
    w1 = g_w / (1.0 + e2)
    w2 = g_w * e2 / (1.0 + e2)
    out = jnp.where(lane == 0, i1 - N_EXPERT_GROUPS,
                    jnp.where(lane == 1, i2 - N_EXPERT_GROUPS,
                              jnp.where(lane == 2, w1, jnp.where(lane == 3, w2, 0.0))))
    o_ref[...] = out


def _router(h2f, w_rt, b_rt, tm):
    t = h2f.shape[0]
    return pl.pallas_call(
        _router_kernel,
        grid=(t // tm,),
        in_specs=[pl.BlockSpec((tm, D_MODEL), lambda i: (i, 0)),
                  pl.BlockSpec((D_MODEL, LANES), lambda i: (0, 0)),
                  pl.BlockSpec((1, LANES), lambda i: (0, 0))],
        out_specs=pl.BlockSpec((tm, LANES), lambda i: (i, 0)),
        out_shape=jax.ShapeDtypeStruct((t, LANES), F32),
        compiler_params=_params(("parallel",)),
        name="router_topk",
    )(h2f, w_rt, b_rt)


def _expert_kernel(be_ref, nu_ref, x_ref, wg_ref, wu_ref, wd_ref, o_ref):
    i = pl.program_id(0)

    @pl.when(i < nu_ref[0])
    def _():
        x = x_ref[...]
        a = _silu(_dot(x, wg_ref[0])) * _dot(x, wu_ref[0])
        o_ref[...] = _dot(a.astype(BF16), wd_ref[0])

    @pl.when(i >= nu_ref[0])
    def _():
        o_ref[...] = jnp.zeros(o_ref.shape, o_ref.dtype)


def _experts(xb, blk_e, n_used, wg, wu, wd):
    cap = xb.shape[0]
    nb = cap // EXPERT_BLOCK
    grid_spec = pltpu.PrefetchScalarGridSpec(
        num_scalar_prefetch=2,
        grid=(nb,),
        in_specs=[pl.BlockSpec((EXPERT_BLOCK, D_MODEL), lambda i, be, nu: (jnp.minimum(i, nu[0] - 1), 0)),
                  pl.BlockSpec((1, D_MODEL, EXPERT_FF), lambda i, be, nu: (be[i], 0, 0)),
                  pl.BlockSpec((1, D_MODEL, EXPERT_FF), lambda i, be, nu: (be[i], 0, 0)),
                  pl.BlockSpec((1, EXPERT_FF, D_MODEL), lambda i, be, nu: (be[i], 0, 0))],
        out_specs=pl.BlockSpec((EXPERT_BLOCK, D_MODEL), lambda i, be, nu: (i, 0)),
    )
    return pl.pallas_call(
        _expert_kernel,
        grid_spec=grid_spec,
        out_shape=jax.ShapeDtypeStruct((cap, D_MODEL), F32),
        compiler_params=_params(("arbitrary",)),
        name="expert_mlp",
    )(blk_e, n_used, xb, wg, wu, wd)


def _final_kernel(x1_ref, mod_ref, rt_ref, g0_ref, g1_ref, o_ref):
    m = mod_ref[0]
    rt = rt_ref[...]
    moe = g0_ref[...] * rt[:, 2:3] + g1_ref[...] * rt[:, 3:4]
    o_ref[...] = x1_ref[...] + m[5:6] * moe


def _final(x1, mod_tiles, rt, g0, g1, tok_off, n_tok, tm):
    b0 = tok_off // tm
    tok = lambda n: pl.BlockSpec((tm, n), lambda i: (b0 + i, 0))
    return pl.pallas_call(
        _final_kernel,
        grid=(n_tok // tm,),
        in_specs=[tok(D_MODEL), pl.BlockSpec((1, 6, D_MODEL), lambda i: (b0 + i, 0, 0)), tok(LANES),
                  tok(D_MODEL), tok(D_MODEL)],
        out_specs=pl.BlockSpec((tm, D_MODEL), lambda i: (i, 0)),
        out_shape=jax.ShapeDtypeStruct((n_tok, D_MODEL), F32),
        compiler_params=_params(("parallel",)),
        name="moe_combine",
    )(x1, mod_tiles, rt, g0, g1)


def _rope_tables(seq):
    pos = jnp.arange(seq, dtype=F32)
    inv = 1.0 / (ROPE_THETA ** (jnp.arange(0, QK_DIM, 2, dtype=F32) / QK_DIM))
    ang = pos[:, None] * inv[None, :]
    cos, sin = jnp.cos(ang), jnp.sin(ang)
    cos_t = jnp.tile(cos, (1, LANES // (QK_DIM // 2)))
    sin_t = jnp.tile(jnp.concatenate([-sin, sin], axis=1), (1, LANES // QK_DIM))
    return cos_t, sin_t


def _dispatch(e_idx, n_tok):
    n_slots = n_tok * TOP_K_INNER
    flat_e = e_idx.reshape(-1)
    onehot = (flat_e[:, None] == jnp.arange(N_EXPERTS, dtype=jnp.int32)[None, :]).astype(jnp.int32)
    csum = jnp.cumsum(onehot, axis=0)
    counts = csum[-1]
    rank = jnp.sum(onehot * csum, axis=1) - 1
    padded = ((counts + EXPERT_BLOCK - 1) // EXPERT_BLOCK) * EXPERT_BLOCK
    pad_end = jnp.cumsum(padded)
    pad_start = pad_end - padded
    dest = (pad_start[flat_e] + rank).astype(jnp.int32)
    cap = n_slots + N_EXPERTS * EXPERT_BLOCK
    nb = cap // EXPERT_BLOCK
    tok = jnp.arange(n_slots, dtype=jnp.int32) // TOP_K_INNER
    tok_buf = jnp.zeros((cap,), jnp.int32).at[dest].set(tok)
    blk_e = jnp.minimum(jnp.searchsorted(pad_end, jnp.arange(nb, dtype=jnp.int32) * EXPERT_BLOCK, side='right'),
                        N_EXPERTS - 1).astype(jnp.int32)
    n_used = (pad_end[-1] // EXPERT_BLOCK).astype(jnp.int32).reshape(1)
    return dest.reshape(n_tok, TOP_K_INNER), tok_buf, blk_e, n_used


def kernel(x_prompt, x_sample, c_prompt, c_sample, w_ada, b_ada, norm1_g, w_in, q_norm_g, k_norm_g, lambda_q1, lambda_k1, lambda_q2, lambda_k2, attn_subln_g, w_attn_o, conv_w, conv_b, dt_bias, a_log, d_skip, ssd_norm_g, w_ssd_o, w_out, norm2_g, w_group, b_group, w_router, b_router, w_gate_e, w_up_e, w_down_e):
    groups = [(x_prompt, c_prompt), (x_sample, c_sample)]
    seqs = [(x.shape[0], x.shape[1]) for x, _ in groups]
    n_tok = sum(b * s for b, s in seqs)
    min_seq = min(s for _, s in seqs)
    tm = min(1024, min_seq)
    tp = min(256, min_seq)
    tq = min(512, min_seq)
    tk = min(512, min_seq)
    layer = 0

    x = jnp.concatenate([g[0].reshape(-1, D_MODEL) for g in groups], axis=0)
    c = jnp.concatenate([g[1] for g in groups], axis=0)
    n_batch = c.shape[0]
    c_pad = jnp.pad(c, ((0, (-n_batch) % SUBLANES), (0, 0)))
    mod = _ada(c_pad, w_ada[layer], b_ada[layer]).reshape(-1, 6, D_MODEL)
    tok_batch = jnp.concatenate([jnp.repeat(jnp.arange(b, dtype=jnp.int32), s) + off
                                 for (b, s), off in zip(seqs, [0, seqs[0][0]])])
    mod_tm = mod[tok_batch[::tm]]
    mod_tp = mod[tok_batch[::tp]]

    h = _norm_mod(x, mod_tm, norm1_g[layer], tm)

    w_in_b = w_in[layer].astype(BF16)
    tabs = [_rope_tables(s) for _, s in seqs]
    cos_t = jnp.concatenate([jnp.tile(tb[0], (b, 1)) for tb, (b, _) in zip(tabs, seqs)], axis=0)
    sin_t = jnp.concatenate([jnp.tile(tb[1], (b, 1)) for tb, (b, _) in zip(tabs, seqs)], axis=0)
    half = jnp.arange(LANES) // QK_DIM
    bd = (half[:, None] == half[None, :]).astype(BF16)
    gq = (jnp.tile(q_norm_g[layer], LANES // QK_DIM) * (QK_DIM ** -0.5 * LOG2_E)).reshape(1, LANES)
    gk = jnp.tile(k_norm_g[layer], LANES // QK_DIM).reshape(1, LANES)
    q = _qk_proj(h, w_in_b[:, OFF_Q:OFF_K], gq, cos_t, sin_t, bd, tm, "q_proj")
    k = _qk_proj(h, w_in_b[:, OFF_K:OFF_V], gk, cos_t, sin_t, bd, tm, "k_proj")
    vt3 = _vt_proj(h, w_in_b[:, OFF_V:OFF_Z], tm, tk)
    z = _matmul(h, w_in_b[:, OFF_Z:OFF_XBC], F32, tm, 1024, "z_proj")
    xbc = _matmul(h, w_in_b[:, OFF_XBC:OFF_DT], F32, tm, 1024, "xbc_proj")
    w_dt = jnp.pad(w_in_b[:, OFF_DT:OFF_GATE], ((0, 0), (0, LANES - DT_COLS)))
    dt = _matmul(h, w_dt, F32, tm, LANES, "dt_proj")
    gates = _matmul(h, w_in_b[:, OFF_GATE:], F32, tm, 1024, "gate_proj")

    k3 = k.reshape(n_tok // tk, tk, ATTN_WIDTH)
    lams = [v[layer].reshape(1, QK_DIM) for v in (lambda_q1, lambda_k1, lambda_q2, lambda_k2)]
    o_groups = []
    off = 0
    for b, s in seqs:
        o_groups.append(_attention(q, k3, vt3, lams, off, b, s, tq, tk))
        off += b * s

    seq_starts = []
    off = 0
    for b, s in seqs:
        seq_starts += [(off + i * s, s) for i in range(b)]
        off += b * s
    tile_start = jnp.zeros((n_tok // tp,), jnp.int32)
    tile_end = jnp.zeros((n_tok // tp,), jnp.int32)
    nc = n_tok // CHUNK
    chunk_reset = jnp.zeros((nc,), jnp.int32)
    bwd_idx = jnp.zeros((nc,), jnp.int32)
    for st, s in seq_starts:
        tile_start = tile_start.at[st // tp].set(1)
        tile_end = tile_end.at[(st + s) // tp - 1].set(1)
        c0, c1 = st // CHUNK, (st + s) // CHUNK
        chunk_reset = chunk_reset.at[c0].set(1)
        bwd_idx = bwd_idx.at[c0:c1].set(jnp.arange(c1 - 1, c0 - 1, -1, dtype=jnp.int32))
    fwd_idx = jnp.arange(nc, dtype=jnp.int32)
    xs, bcm = _conv(xbc, conv_w[layer], conv_b[layer], tile_start, tile_end, tp)
    dtt = dt.T
    dtb = dt_bias[layer].reshape(-1)
    alg = a_log[layer].reshape(-1)
    yf = _ssd(xs, bcm, dt, dtt, dtb, alg, fwd_idx, chunk_reset, False)
    yb = _ssd(xs, bcm, dt, dtt, dtb, alg, bwd_idx, chunk_reset, True)

    subg = attn_subln_g[layer].reshape(1, V_DIM)
    dskip = jnp.repeat(d_skip[layer], SSD_HEAD_DIM).reshape(1, D_INNER)
    x1, h2b, h2f = _post(o_groups[0], o_groups[1], yf, yb, xs, z, gates, x, mod_tp, subg, dskip,
                         ssd_norm_g[layer].reshape(1, D_INNER), norm2_g[layer].reshape(1, D_MODEL),
                         w_attn_o[layer].astype(BF16), w_ssd_o[layer].astype(BF16),
                         w_out[layer].astype(BF16), tp)

    n_rt = N_EXPERT_GROUPS + N_EXPERTS
    w_rt = jnp.pad(jnp.concatenate([w_group[layer], w_router[layer]], axis=1), ((0, 0), (0, LANES - n_rt)))
    b_rt = jnp.pad(jnp.concatenate([b_group[layer], b_router[layer]]), (0, LANES - n_rt)).reshape(1, LANES)
    rt = _router(h2f, w_rt, b_rt, tp)
    e_idx = rt[:, :TOP_K_INNER].astype(jnp.int32)
    dest, tok_buf, blk_e, n_used = _dispatch(e_idx, n_tok)
    xb = h2b[tok_buf]
    yb_e = _experts(xb, blk_e, n_used, w_gate_e[layer].astype(BF16), w_up_e[layer].astype(BF16),
                    w_down_e[layer].astype(BF16))
    g0 = yb_e[dest[:, 0]]
    g1 = yb_e[dest[:, 1]]

    outs = []
    off = 0
    for (b, s), (xg, _) in zip(seqs, groups):
        y = _final(x1, mod_tp, rt, g0, g1, off, b * s, tp)
        outs.append(y.reshape(xg.shape))
        off += b * s
    return tuple(outs)
```

```python
import functools
import math

import jax
import jax.numpy as jnp
from jax import lax
from jax.experimental import pallas as pl
from jax.experimental.pallas import tpu as pltpu

F32 = jnp.float32
BF16 = jnp.bfloat16

D_MODEL = 1024
ATTN_HEADS = 8
QK_DIM = 64
V_DIM = 2 * QK_DIM
ATTN_WIDTH = ATTN_HEADS * V_DIM
ROPE_THETA = 10000.0
D_INNER = 2048
SSD_HEAD_DIM = 64
SSD_HEADS = D_INNER // SSD_HEAD_DIM
SSD_GROUPS = 4
HEADS_PER_GROUP = SSD_HEADS // SSD_GROUPS
D_STATE = 128
CONV_K = 5
CONV_CH = D_INNER + 2 * SSD_GROUPS * D_STATE
CHUNK = 128
N_EXPERT_GROUPS = 4
EXPERTS_PER_GROUP = 8
N_EXPERTS = N_EXPERT_GROUPS * EXPERTS_PER_GROUP
TOP_K_INNER = 2
EXPERT_FF = 512
RMS_EPS = 1e-6
LAMBDA_INIT = 0.8 - 0.6 * math.exp(-0.3 * 0)

LANES = 128
SUBLANES = 8
VMEM_LIMIT = 56 * 1024 * 1024

Q_COLS = ATTN_HEADS * 2 * QK_DIM
K_COLS = Q_COLS
V_COLS = ATTN_WIDTH
Z_COLS = D_INNER
XBC_COLS = CONV_CH
DT_COLS = 2 * SSD_HEADS
GATE_COLS = 2 * D_MODEL
OFF_Q = 0
OFF_K = OFF_Q + Q_COLS
OFF_V = OFF_K + K_COLS
OFF_Z = OFF_V + V_COLS
OFF_XBC = OFF_Z + Z_COLS
OFF_DT = OFF_XBC + XBC_COLS
OFF_GATE = OFF_DT + DT_COLS

EXPERT_BLOCK = 256
ONES_ROWS = 16
LOG2_E = math.log2(math.e)


def _params(sem):
    return pltpu.CompilerParams(dimension_semantics=sem, vmem_limit_bytes=VMEM_LIMIT)


def _dot(a, b):
    return jnp.dot(a, b, preferred_element_type=F32)


def _dot_tn(a, b):
    return lax.dot_general(a, b, (((0,), (0,)), ((), ())), preferred_element_type=F32)


def _dot_nt(a, b):
    return lax.dot_general(a, b, (((1,), (1,)), ((), ())), preferred_element_type=F32)


def _split3(a):
    hi = a.astype(BF16)
    r = a - hi.astype(F32)
    mid = r.astype(BF16)
    lo = (r - mid.astype(F32)).astype(BF16)
    return hi, mid, lo


def _dot_left01(m01, a):
    hi, mid, lo = _split3(a)
    return _dot(m01, hi) + _dot(m01, mid) + _dot(m01, lo)


def _dot_right01(a, m01):
    hi, mid, lo = _split3(a)
    return _dot(hi, m01) + _dot(mid, m01) + _dot(lo, m01)


def _spread01(a, m01):
    hi = a.astype(BF16)
    lo = (a - hi.astype(F32)).astype(BF16)
    return _dot(hi, m01) + _dot(lo, m01)


def _dot_f32(a, b):
    a0, a1, a2 = _split3(a)
    b0, b1, b2 = _split3(b)
    return (_dot(a0, b0) + (_dot(a0, b1) + _dot(a1, b0))
            + (_dot(a0, b2) + _dot(a2, b0) + _dot(a1, b1)))


def _dot_f32_3(a, b):
    a0 = a.astype(BF16)
    a1 = (a - a0.astype(F32)).astype(BF16)
    b0 = b.astype(BF16)
    b1 = (b - b0.astype(F32)).astype(BF16)
    return _dot(a0, b0) + (_dot(a0, b1) + _dot(a1, b0))


def _sigmoid(x):
    return 1.0 / (1.0 + jnp.exp(-x))


def _silu(x):
    return x * _sigmoid(x)


def _softplus(x):
    e = jnp.exp(-jnp.abs(x))
    u = 1.0 + e
    log1p_e = jnp.where(u == 1.0, e, jnp.log(u) * (e / (u - 1.0)))
    return jnp.maximum(x, 0.0) + log1p_e


def _ada_kernel(c_ref, w_ref, b_ref, o_ref):
    o_ref[...] = _dot_f32(_silu(c_ref[...]), w_ref[...]) + b_ref[...]


def _ada(c_pad, w_ada, b_ada):
    rows = c_pad.shape[0]
    n = w_ada.shape[1]
    tn = 1024
    return pl.pallas_call(
        _ada_kernel,
        grid=(n // tn,),
        in_specs=[pl.BlockSpec((rows, D_MODEL), lambda j: (0, 0)),
                  pl.BlockSpec((D_MODEL, tn), lambda j: (0, j)),
                  pl.BlockSpec((1, tn), lambda j: (0, j))],
        out_specs=pl.BlockSpec((rows, tn), lambda j: (0, j)),
        out_shape=jax.ShapeDtypeStruct((rows, n), F32),
        compiler_params=_params(("arbitrary",)),
        name="ada_mod",
    )(c_pad, w_ada, b_ada.reshape(1, n))


def _norm_mod_kernel(x_ref, mod_ref, g_ref, o_ref, *, shift_row, scale_row):
    x = x_ref[...]
    r = lax.rsqrt(jnp.mean(x * x, axis=-1, keepdims=True) + RMS_EPS)
    m = mod_ref[0]
    h = ((x * r) * g_ref[...]) * (1.0 + m[scale_row:scale_row + 1]) + m[shift_row:shift_row + 1]
    o_ref[...] = h.astype(o_ref.dtype)


def _norm_mod(x, mod_tiles, g, tm):
    t = x.shape[0]
    return pl.pallas_call(
        functools.partial(_norm_mod_kernel, shift_row=0, scale_row=1),
        grid=(t // tm,),
        in_specs=[pl.BlockSpec((tm, D_MODEL), lambda i: (i, 0)),
                  pl.BlockSpec((1, 6, D_MODEL), lambda i: (i, 0, 0)),
                  pl.BlockSpec((1, D_MODEL), lambda i: (0, 0))],
        out_specs=pl.BlockSpec((tm, D_MODEL), lambda i: (i, 0)),
        out_shape=jax.ShapeDtypeStruct((t, D_MODEL), BF16),
        compiler_params=_params(("parallel",)),
        name="norm1_mod",
    )(x, mod_tiles, g.reshape(1, D_MODEL))


def _mm_kernel(a_ref, w_ref, *rest, act):
    o_ref = rest[-1]
    acc = _dot(a_ref[...], w_ref[...])
    if len(rest) == 2:
        acc = acc + rest[0][...]
    if act is not None:
        acc = act(acc)
    o_ref[...] = acc.astype(o_ref.dtype)


def _matmul(a, w, out_dtype, tm, tn, name, act=None, bias=None):
    m, k = a.shape
    n = w.shape[1]
    in_specs = [pl.BlockSpec((tm, k), lambda j, i: (i, 0)),
                pl.BlockSpec((k, tn), lambda j, i: (0, j))]
    args = [a, w]
    if bias is not None:
        in_specs.append(pl.BlockSpec((1, tn), lambda j, i: (0, j)))
        args.append(bias)
    return pl.pallas_call(
        functools.partial(_mm_kernel, act=act),
        grid=(n // tn, m // tm),
        in_specs=in_specs,
        out_specs=pl.BlockSpec((tm, tn), lambda j, i: (i, j)),
        out_shape=jax.ShapeDtypeStruct((m, n), out_dtype),
        compiler_params=_params(("parallel", "parallel")),
        name=name,
    )(*args)


def _qk_kernel(a_ref, w_ref, g_ref, cos_ref, sin_ref, bd_ref, o_ref):
    acc = _dot(a_ref[...], w_ref[...])
    tm = acc.shape[0]
    lane = lax.broadcasted_iota(jnp.int32, (tm, LANES), 1)
    first = (lane & (QK_DIM // 2)) == 0
    cos = cos_ref[...]
    sin = sin_ref[...]
    g = g_ref[...]
    bd = bd_ref[...]
    for h in range(ATTN_HEADS):
        x = acc[:, h * LANES:(h + 1) * LANES]
        sq = x * x
        hi = sq.astype(BF16)
        lo = (sq - hi.astype(F32)).astype(BF16)
        ss = _dot(hi, bd) + _dot(lo, bd)
        r = lax.rsqrt(ss * (1.0 / QK_DIM) + RMS_EPS)
        xn = (x * r) * g
        partner = jnp.where(first, pltpu.roll(xn, LANES - QK_DIM // 2, 1),
                            pltpu.roll(xn, QK_DIM // 2, 1))
        o_ref[:, h * LANES:(h + 1) * LANES] = (xn * cos + partner * sin).astype(o_ref.dtype)


def _qk_proj(h, w, g128, cos_t, sin_t, bd, tm, name):
    t = h.shape[0]
    n = w.shape[1]
    return pl.pallas_call(
        _qk_kernel,
        grid=(t // tm,),
        in_specs=[pl.BlockSpec((tm, D_MODEL), lambda i: (i, 0)),
                  pl.BlockSpec((D_MODEL, n), lambda i: (0, 0)),
                  pl.BlockSpec((1, LANES), lambda i: (0, 0)),
                  pl.BlockSpec((tm, LANES), lambda i: (i, 0)),
                  pl.BlockSpec((tm, LANES), lambda i: (i, 0)),
                  pl.BlockSpec((LANES, LANES), lambda i: (0, 0))],
        out_specs=pl.BlockSpec((tm, n), lambda i: (i, 0)),
        out_shape=jax.ShapeDtypeStruct((t, n), BF16),
        compiler_params=_params(("parallel",)),
        name=name,
    )(h, w, g128, cos_t, sin_t, bd)


def _vt_kernel(a_ref, w_ref, o_ref, *, tk):
    acc = _dot(a_ref[...], w_ref[...])
    for c in range(acc.shape[0] // tk):
        o_ref[c] = acc[c * tk:(c + 1) * tk, :].T.astype(o_ref.dtype)


def _vt_proj(h, w, tm, tk):
    t = h.shape[0]
    n = w.shape[1]
    return pl.pallas_call(
        functools.partial(_vt_kernel, tk=tk),
        grid=(t // tm,),
        in_specs=[pl.BlockSpec((tm, D_MODEL), lambda i: (i, 0)),
                  pl.BlockSpec((D_MODEL, n), lambda i: (0, 0))],
        out_specs=pl.BlockSpec((tm // tk, n, tk), lambda i: (i, 0, 0)),
        out_shape=jax.ShapeDtypeStruct((t // tk, n, tk), BF16),
        compiler_params=_params(("parallel",)),
        name="v_proj_t",
    )(h, w)


def _attn_kernel(lq1_ref, lk1_ref, lq2_ref, lk2_ref, q_ref, k_ref, vt_ref, o_ref, q2t_s, s_s, m_s, acc_s,
                 *, tq, nkv, unroll):
    tk = k_ref.shape[1]
    qt = q_ref[...].astype(F32).T
    row = lax.broadcasted_iota(jnp.int32, qt.shape, 0)
    q2t_s[:, :tq] = jnp.where(row < QK_DIM, qt, 0.0).astype(BF16)
    q2t_s[:, tq:] = jnp.where(row >= QK_DIM, qt, 0.0).astype(BF16)
    m_s[...] = jnp.full(m_s.shape, -jnp.inf, F32)
    acc_s[...] = jnp.zeros(acc_s.shape, F32)
    ones = jnp.ones((ONES_ROWS, tk), BF16)
    s_s[0] = _dot(k_ref[0], q2t_s[...])

    def step(j, par, compute_next):
        if compute_next:
            s_s[1 - par] = _dot(k_ref[j + 1], q2t_s[...])
        st = s_s[par]
        m_old = m_s[...]
        m_new = jnp.maximum(m_old, jnp.max(st, axis=0, keepdims=True))
        alpha = jnp.exp2(m_old - m_new)
        p = jnp.exp2(st - m_new).astype(BF16)
        lhs = jnp.concatenate([vt_ref[j], ones], axis=0)
        acc_s[...] = alpha * acc_s[...] + _dot(lhs, p)
        m_s[...] = m_new

    n_trips = (nkv - 1) // unroll

    def trip(t, carry):
        for u in range(unroll):
            step(unroll * t + u, u % 2, True)
        return carry

    lax.fori_loop(0, n_trips, trip, 0)
    for j in range(unroll * n_trips, nkv):
        step(j, j % 2, j + 1 < nkv)

    lam = (jnp.exp(jnp.sum(lq1_ref[...] * lk1_ref[...], axis=1, keepdims=True))
           - jnp.exp(jnp.sum(lq2_ref[...] * lk2_ref[...], axis=1, keepdims=True)) + LAMBDA_INIT)
    ot = acc_s[:V_DIM, :] / acc_s[V_DIM:V_DIM + 1, :]
    o_ref[...] = (ot[:, :tq] - lam * ot[:, tq:]).T


def _attention(q, k3, vt3, lams, tok_off, batch, seq, tq, tk):
    nkv = seq // tk
    qb0 = tok_off // tq
    sb0 = tok_off // seq
    nq = seq // tq
    unroll = 4 if nkv >= 16 else 2
    lam_spec = pl.BlockSpec((1, QK_DIM), lambda b, h, i: (0, 0))
    return pl.pallas_call(
        functools.partial(_attn_kernel, tq=tq, nkv=nkv, unroll=unroll),
        grid=(batch, ATTN_HEADS, nq),
        in_specs=[lam_spec, lam_spec, lam_spec, lam_spec,
                  pl.BlockSpec((tq, LANES), lambda b, h, i: (qb0 + b * nq + i, h)),
                  pl.BlockSpec((nkv, tk, LANES), lambda b, h, i: (sb0 + b, 0, h)),
                  pl.BlockSpec((nkv, LANES, tk), lambda b, h, i: (sb0 + b, h, 0))],
        out_specs=pl.BlockSpec((tq, LANES), lambda b, h, i: (b * nq + i, h)),
        out_shape=jax.ShapeDtypeStruct((batch * seq, ATTN_WIDTH), F32),
        scratch_shapes=[pltpu.VMEM((LANES, 2 * tq), BF16),
                        pltpu.VMEM((2, tk, 2 * tq), F32),
                        pltpu.VMEM((1, 2 * tq), F32),
                        pltpu.VMEM((V_DIM + ONES_ROWS, 2 * tq), F32)],
        compiler_params=_params(("parallel", "parallel", "arbitrary")),
        name="diff_attention",
    )(*lams, q, k3, vt3)


def _conv_kernel(start_ref, end_ref, prev_ref, cur_ref, next_ref, w_ref, b_ref, xs_ref, bc_ref, ext_s, *, tm):
    i = pl.program_id(0)
    halo = SUBLANES
    ext_s[0:halo, :] = jnp.where(start_ref[i] == 1, 0.0, prev_ref[...])
    ext_s[halo:halo + tm, :] = cur_ref[...]
    ext_s[halo + tm:2 * halo + tm, :] = jnp.where(end_ref[i] == 1, 0.0, next_ref[...])
    pad = (CONV_K - 1) // 2
    acc = jnp.broadcast_to(b_ref[...], (tm, CONV_CH))
    for d in range(CONV_K):
        acc = acc + w_ref[d:d + 1, :] * ext_s[halo - pad + d:halo - pad + d + tm, :]
    y = _silu(acc)
    xs_ref[...] = y[:, :D_INNER]
    bc_ref[...] = y[:, D_INNER:].astype(bc_ref.dtype)


def _conv(xbc, conv_w, conv_b, start_flags, end_flags, tm):
    t = xbc.shape[0]
    rb = tm // SUBLANES
    last = t // SUBLANES - 1
    w_pad = jnp.zeros((SUBLANES, CONV_CH), F32).at[:CONV_K].set(conv_w)
    grid_spec = pltpu.PrefetchScalarGridSpec(
        num_scalar_prefetch=2,
        grid=(t // tm,),
        in_specs=[pl.BlockSpec((SUBLANES, CONV_CH), lambda i, s, e: (jnp.maximum(i * rb - 1, 0), 0)),
                  pl.BlockSpec((tm, CONV_CH), lambda i, s, e: (i, 0)),
                  pl.BlockSpec((SUBLANES, CONV_CH), lambda i, s, e: (jnp.minimum((i + 1) * rb, last), 0)),
                  pl.BlockSpec((SUBLANES, CONV_CH), lambda i, s, e: (0, 0)),
                  pl.BlockSpec((1, CONV_CH), lambda i, s, e: (0, 0))],
        out_specs=[pl.BlockSpec((tm, D_INNER), lambda i, s, e: (i, 0)),
                   pl.BlockSpec((tm, CONV_CH - D_INNER), lambda i, s, e: (i, 0))],
        scratch_shapes=[pltpu.VMEM((tm + 2 * SUBLANES, CONV_CH), F32)],
    )
    return pl.pallas_call(
        functools.partial(_conv_kernel, tm=tm),
        grid_spec=grid_spec,
        out_shape=[jax.ShapeDtypeStruct((t, D_INNER), F32),
                   jax.ShapeDtypeStruct((t, CONV_CH - D_INNER), BF16)],
        compiler_params=_params(("arbitrary",)),
        name="conv_silu",
    )(start_flags, end_flags, xbc, xbc, xbc, w_pad, conv_b.reshape(1, CONV_CH))


def _ssd_kernel(idx_ref, reset_ref, xs_ref, bc_ref, dt_ref, dtt_ref,
                alog_r_ref, alog_c_ref, tril_ref, triu_ref, e01_ref, y_ref, state_s, *, rev):
    i = pl.program_id(0)
    nh = SSD_HEADS

    @pl.when(reset_ref[i] == 1)
    def _():
        state_s[...] = jnp.zeros(state_s.shape, F32)

    xs = xs_ref[...]
    bc = bc_ref[...]
    dtn = dt_ref[...]
    a = dtn * (-jnp.exp(alog_r_ref[...]))
    pinc = _dot_left01(tril_ref[...], a)
    pex = pinc - a
    tot = pinc[CHUNK - 1:CHUNK, :]

    gw = HEADS_PER_GROUP * SSD_HEAD_DIM
    gn = SSD_GROUPS * D_STATE
    cd = jnp.broadcast_to(jnp.exp(tot), (SUBLANES, LANES))

    def update_state(sc_st, sc_cd):
        w = (xs * sc_st).astype(BF16)
        for g in range(SSD_GROUPS):
            bg = bc[:, g * D_STATE:(g + 1) * D_STATE]
            new = _dot_tn(bg, w[:, g * gw:(g + 1) * gw])
            state_s[:, g * gw:(g + 1) * gw] = (state_s[:, g * gw:(g + 1) * gw] * sc_cd[:, g * gw:(g + 1) * gw]
                                               + new)

    if rev:
        stacked = jnp.concatenate([jnp.exp(pex) * dtn, jnp.exp(tot - pex), cd], axis=0)
        ex = _spread01(stacked, e01_ref[...])
        sc_off = ex[CHUNK:2 * CHUNK]
        for g in range(SSD_GROUPS):
            cg = bc[:, gn + g * D_STATE:gn + (g + 1) * D_STATE]
            st = state_s[:, g * gw:(g + 1) * gw]
            y_ref[:, g * gw:(g + 1) * gw] = _dot(cg, st.astype(BF16)) * sc_off[:, g * gw:(g + 1) * gw]
        update_state(ex[:CHUNK], ex[2 * CHUNK:2 * CHUNK + 1])
        return

    dtnt = dtt_ref[...]
    at = dtnt * (-jnp.exp(alog_c_ref[...]))
    pinct = _dot_right01(at, triu_ref[...])
    pext = pinct - at
    li = lax.broadcasted_iota(jnp.int32, (CHUNK, CHUNK), 0)
    si = lax.broadcasted_iota(jnp.int32, (CHUNK, CHUNK), 1)
    lower = si <= li
    strict_lower = si < li
    strict_upper = si > li
    lane = lax.broadcasted_iota(jnp.int32, (CHUNK, LANES), 1)
    first_head = lane < SSD_HEAD_DIM
    xb = xs.astype(BF16)
    for g in range(SSD_GROUPS):
        bg = bc[:, g * D_STATE:(g + 1) * D_STATE]
        cg = bc[:, gn + g * D_STATE:gn + (g + 1) * D_STATE]
        cg_f = cg.astype(F32)
        cb = _dot_nt(cg, bg)
        for hp in range(HEADS_PER_GROUP // 2):
            col = g * gw + hp * LANES
            rhs = jnp.concatenate([xb[:, col:col + LANES], state_s[:, col:col + LANES].astype(BF16)], axis=0)
            pair = []
            for u in range(2):
                h = g * HEADS_PER_GROUP + 2 * hp + u
                colf = jnp.broadcast_to(pinc[:, h:h + 1], (CHUNK, CHUNK))
                arg = jnp.where(lower, colf - pinct[h:h + 1, :],
                                pext[nh + h:nh + h + 1, :] - pex[:, nh + h:nh + h + 1])
                dtf = dtnt[h:h + 1, :]
                dtb = dtnt[nh + h:nh + h + 1, :]
                coef = jnp.where(strict_lower, dtf, jnp.where(strict_upper, dtb, dtf + dtb))
                mh = (cb * jnp.exp(arg) * coef).astype(BF16)
                dh = (cg_f * jnp.exp(colf)).astype(BF16)
                pair.append(_dot(jnp.concatenate([mh, dh], axis=1), rhs))
            y_ref[:, col:col + LANES] = jnp.where(first_head, pair[0], pair[1])

    stacked = jnp.concatenate([jnp.exp(tot - pinc) * dtn, cd], axis=0)
    ex = _spread01(stacked, e01_ref[...])
    update_state(ex[:CHUNK], ex[CHUNK:CHUNK + 1])


def _ssd(xs, bc, dt, dtt, a_log, idx, reset, rev):
    t = xs.shape[0]
    nc = t // CHUNK
    alog_r = jnp.pad(a_log.reshape(1, DT_COLS), ((0, 0), (0, LANES - DT_COLS)))
    alog_c = alog_r.reshape(LANES, 1)
    r = jnp.arange(CHUNK)
    tril = (r[None, :] <= r[:, None]).astype(BF16)
    triu = (r[:, None] <= r[None, :]).astype(BF16)
    lo = SSD_HEADS if rev else 0
    e01 = (jnp.arange(D_INNER)[None, :] // SSD_HEAD_DIM == jnp.arange(LANES)[:, None] - lo).astype(BF16)
    const = lambda shape: pl.BlockSpec(shape, lambda i, ix, rs: (0, 0))
    grid_spec = pltpu.PrefetchScalarGridSpec(
        num_scalar_prefetch=2,
        grid=(nc,),
        in_specs=[pl.BlockSpec((CHUNK, D_INNER), lambda i, ix, rs: (ix[i], 0)),
                  pl.BlockSpec((CHUNK, CONV_CH - D_INNER), lambda i, ix, rs: (ix[i], 0)),
                  pl.BlockSpec((CHUNK, LANES), lambda i, ix, rs: (ix[i], 0)),
                  pl.BlockSpec((LANES, CHUNK), lambda i, ix, rs: (0, ix[i])),
                  const((1, LANES)), const((LANES, 1)),
                  const((CHUNK, CHUNK)), const((CHUNK, CHUNK)), const((LANES, D_INNER))],
        out_specs=pl.BlockSpec((CHUNK, D_INNER), lambda i, ix, rs: (ix[i], 0)),
        scratch_shapes=[pltpu.VMEM((D_STATE, D_INNER), F32)],
    )
    return pl.pallas_call(
        functools.partial(_ssd_kernel, rev=rev),
        grid_spec=grid_spec,
        out_shape=jax.ShapeDtypeStruct((t, D_INNER), F32),
        compiler_params=_params(("arbitrary",)),
        name="ssd_bwd" if rev else "ssd_fwd",
    )(idx, reset, xs, bc, dt, dtt, alog_r, alog_c, tril, triu, e01)


def _post_kernel(oa_ref, ob_ref, yf_ref, yb_ref, xs_ref, z_ref, gt_ref, x_ref, mod_ref, subg_ref, dskip_ref,
                 ssdg_ref, n2g_ref, wa_ref, ws_ref, wo_ref, x1_ref, h2b_ref, h2f_ref, an_s, yn_s, *, n_first):
    m = mod_ref[0]
    o = jnp.where(pl.program_id(0) < n_first, oa_ref[...], ob_ref[...])
    for h in range(ATTN_HEADS):
        oh = o[:, h * V_DIM:(h + 1) * V_DIM]
        r = lax.rsqrt(jnp.mean(oh * oh, axis=-1, keepdims=True) + RMS_EPS)
        an_s[:, h * V_DIM:(h + 1) * V_DIM] = (((oh * r) * subg_ref[...]) * (1.0 - LAMBDA_INIT)).astype(BF16)
    attn_d = _dot(an_s[...], wa_ref[...])

    y = (yf_ref[...] + yb_ref[...]) + dskip_ref[...] * xs_ref[...]
    y = y * z_ref[...].astype(F32)
    gw = D_INNER // SSD_GROUPS
    for g in range(SSD_GROUPS):
        yg = y[:, g * gw:(g + 1) * gw]
        r = lax.rsqrt(jnp.mean(yg * yg, axis=-1, keepdims=True) + RMS_EPS)
        yn_s[:, g * gw:(g + 1) * gw] = ((yg * r) * ssdg_ref[:, g * gw:(g + 1) * gw]).astype(BF16)
    ssd_d = _dot(yn_s[...], ws_ref[...])

    gt = gt_ref[...].astype(F32)
    mix = gt[:, :D_MODEL] * attn_d + gt[:, D_MODEL:] * ssd_d
    mixed = _dot(mix.astype(BF16), wo_ref[...])
    x1 = x_ref[...] + m[2:3] * mixed
    x1_ref[...] = x1
    r = lax.rsqrt(jnp.mean(x1 * x1, axis=-1, keepdims=True) + RMS_EPS)
    h2 = ((x1 * r) * n2g_ref[...]) * (1.0 + m[4:5]) + m[3:4]
    h2f_ref[...] = h2
    h2b_ref[...] = h2.astype(BF16)


def _post(oa, ob, yf, yb, xs, z, gates, x, mod_tiles, subg, dskip, ssdg, n2g, wa, ws, wo, tm):
    t = x.shape[0]
    n_first = oa.shape[0] // tm
    tok = lambda n: pl.BlockSpec((tm, n), lambda i: (i, 0))
    const = lambda a: pl.BlockSpec(a.shape, lambda i: (0, 0))
    return pl.pallas_call(
        functools.partial(_post_kernel, n_first=n_first),
        grid=(t // tm,),
        in_specs=[pl.BlockSpec((tm, ATTN_WIDTH), lambda i: (jnp.minimum(i, n_first - 1), 0)),
                  pl.BlockSpec((tm, ATTN_WIDTH), lambda i: (jnp.maximum(i - n_first, 0), 0)),
                  tok(D_INNER), tok(D_INNER), tok(D_INNER), tok(D_INNER), tok(GATE_COLS),
                  tok(D_MODEL), pl.BlockSpec((1, 6, D_MODEL), lambda i: (i, 0, 0)),
                  const(subg), const(dskip), const(ssdg), const(n2g), const(wa), const(ws), const(wo)],
        out_specs=[tok(D_MODEL), tok(D_MODEL), tok(D_MODEL)],
        out_shape=[jax.ShapeDtypeStruct((t, D_MODEL), F32),
                   jax.ShapeDtypeStruct((t, D_MODEL), BF16),
                   jax.ShapeDtypeStruct((t, D_MODEL), F32)],
        scratch_shapes=[pltpu.VMEM((tm, ATTN_WIDTH), BF16), pltpu.VMEM((tm, D_INNER), BF16)],
        compiler_params=_params(("parallel",)),
        name="merge_out_proj",
    )(oa, ob, yf, yb, xs, z, gates, x, mod_tiles, subg, dskip, ssdg, n2g, wa, ws, wo)


def _router_kernel(h_ref, w_ref, b_ref, o_ref):
    logits = _dot_f32_3(h_ref[...], w_ref[...]) + b_ref[...]
    tm = logits.shape[0]
    lane = lax.broadcasted_iota(jnp.int32, (tm, LANES), 1)
    lane_f = lane.astype(F32)
    big = float(LANES)
    neg = -jnp.inf
    gl = jnp.where(lane < N_EXPERT_GROUPS, logits, neg)
    gmax = jnp.max(gl, axis=1, keepdims=True)
    g_sel = jnp.min(jnp.where(gl == gmax, lane_f, big), axis=1, keepdims=True)
    g_w = 1.0 / jnp.sum(jnp.exp(gl - gmax), axis=1, keepdims=True)
    e_lane = lane - N_EXPERT_GROUPS
    e_group = (e_lane >> 3).astype(F32)
    in_group = (e_lane >= 0) & (e_lane < N_EXPERTS) & (e_group == g_sel)
    el = jnp.where(in_group, logits, neg)
    v1 = jnp.max(el, axis=1, keepdims=True)
    i1 = jnp.min(jnp.where(el == v1, lane_f, big), axis=1, keepdims=True)
    el2 = jnp.where(lane_f == i1, neg, el)
    v2 = jnp.max(el2, axis=1, keepdims=True)
    i2 = jnp.min(jnp.where(el2 == v2, lane_f, big), axis=1, keepdims=True)
    e2 = jnp.exp(v2 - v1)
    w1 = g_w / (1.0 + e2)
    w2 = g_w * e2 / (1.0 + e2)
    out = jnp.where(lane == 0, i1 - N_EXPERT_GROUPS,
                    jnp.where(lane == 1, i2 - N_EXPERT_GROUPS,
                              jnp.where(lane == 2, w1, jnp.where(lane == 3, w2, 0.0))))
    o_ref[...] = out


def _router(h2f, w_rt, b_rt, tm):
    t = h2f.shape[0]
    return pl.pallas_call(
        _router_kernel,
        grid=(t // tm,),
        in_specs=[pl.BlockSpec((tm, D_MODEL), lambda i: (i, 0)),
                  pl.BlockSpec((D_MODEL, LANES), lambda i: (0, 0)),
                  pl.BlockSpec((1, LANES), lambda i: (0, 0))],
        out_specs=pl.BlockSpec((tm, LANES), lambda i: (i, 0)),
        out_shape=jax.ShapeDtypeStruct((t, LANES), F32),
        compiler_params=_params(("parallel",)),
        name="router_topk",
    )(h2f, w_rt, b_rt)


def _expert_kernel(be_ref, nu_ref, x_ref, wg_ref, wu_ref, wd_ref, o_ref, wg_s, wu_s, wd_s):
    i = pl.program_id(0)
    used = i < nu_ref[0]
    new_expert = (i == 0) | (be_ref[i] != be_ref[jnp.maximum(i - 1, 0)])

    @pl.when(used & new_expert)
    def _():
        wg_s[...] = wg_ref[0].astype(BF16)
        wu_s[...] = wu_ref[0].astype(BF16)
        wd_s[...] = wd_ref[0].astype(BF16)

    @pl.when(used)
    def _():
        x = x_ref[...]
        a = _silu(_dot(x, wg_s[...])) * _dot(x, wu_s[...])
        o_ref[...] = _dot(a.astype(BF16), wd_s[...])

    @pl.when(i >= nu_ref[0])
    def _():
        o_ref[...] = jnp.zeros(o_ref.shape, o_ref.dtype)


def _experts(xb, blk_e, n_used, wg, wu, wd):
    cap = xb.shape[0]
    nb = cap // EXPERT_BLOCK
    grid_spec = pltpu.PrefetchScalarGridSpec(
        num_scalar_prefetch=2,
        grid=(nb,),
        in_specs=[pl.BlockSpec((EXPERT_BLOCK, D_MODEL), lambda i, be, nu: (jnp.minimum(i, nu[0] - 1), 0)),
                  pl.BlockSpec((1, D_MODEL, EXPERT_FF), lambda i, be, nu: (be[i], 0, 0)),
                  pl.BlockSpec((1, D_MODEL, EXPERT_FF), lambda i, be, nu: (be[i], 0, 0)),
                  pl.BlockSpec((1, EXPERT_FF, D_MODEL), lambda i, be, nu: (be[i], 0, 0))],
        out_specs=pl.BlockSpec((EXPERT_BLOCK, D_MODEL), lambda i, be, nu: (i, 0)),
        scratch_shapes=[pltpu.VMEM((D_MODEL, EXPERT_FF), BF16), pltpu.VMEM((D_MODEL, EXPERT_FF), BF16),
                        pltpu.VMEM((EXPERT_FF, D_MODEL), BF16)],
    )
    return pl.pallas_call(
        _expert_kernel,
        grid_spec=grid_spec,
        out_shape=jax.ShapeDtypeStruct((cap, D_MODEL), F32),
        compiler_params=_params(("arbitrary",)),
        name="expert_mlp",
    )(blk_e, n_used, xb, wg, wu, wd)


def _final_kernel(x1_ref, mod_ref, rt_ref, g0_ref, g1_ref, o_ref):
    m = mod_ref[0]
    rt = rt_ref[...]
    moe = g0_ref[...] * rt[:, 2:3] + g1_ref[...] * rt[:, 3:4]
    o_ref[...] = x1_ref[...] + m[5:6] * moe


def _final(x1, mod_tiles, rt, g0, g1, tok_off, n_tok, tm):
    b0 = tok_off // tm
    tok = lambda n: pl.BlockSpec((tm, n), lambda i: (b0 + i, 0))
    return pl.pallas_call(
        _final_kernel,
        grid=(n_tok // tm,),
        in_specs=[tok(D_MODEL), pl.BlockSpec((1, 6, D_MODEL), lambda i: (b0 + i, 0, 0)), tok(LANES),
                  tok(D_MODEL), tok(D_MODEL)],
        out_specs=pl.BlockSpec((tm, D_MODEL), lambda i: (i, 0)),
        out_shape=jax.ShapeDtypeStruct((n_tok, D_MODEL), F32),
        compiler_params=_params(("parallel",)),
        name="moe_combine",
    )(x1, mod_tiles, rt, g0, g1)


def _rope_tables(seq):
    pos = jnp.arange(seq, dtype=F32)
    inv = 1.0 / (ROPE_THETA ** (jnp.arange(0, QK_DIM, 2, dtype=F32) / QK_DIM))
    ang = pos[:, None] * inv[None, :]
    cos, sin = jnp.cos(ang), jnp.sin(ang)
    cos_t = jnp.tile(cos, (1, LANES // (QK_DIM // 2)))
    sin_t = jnp.tile(jnp.concatenate([-sin, sin], axis=1), (1, LANES // QK_DIM))
    return cos_t, sin_t


def _dest_kernel(rt_ref, base_ref, tri_ref, o_ref, carry_s):
    @pl.when(pl.program_id(0) == 0)
    def _():
        carry_s[...] = jnp.zeros(carry_s.shape, F32)

    rt = rt_ref[...]
    tm = rt.shape[0]
    lane = lax.broadcasted_iota(jnp.int32, (tm, LANES), 1)
    lane_f = lane.astype(F32)
    oh0 = lane_f == rt[:, 0:1]
    oh1 = lane_f == rt[:, 1:2]
    both = jnp.where(oh0 | oh1, 1.0, 0.0)
    pos = base_ref[...] + carry_s[0:1, :] + _dot(tri_ref[...], both.astype(BF16))
    d0 = jnp.sum(jnp.where(oh0, pos, 0.0), axis=1, keepdims=True)
    d1 = jnp.sum(jnp.where(oh1, pos, 0.0), axis=1, keepdims=True)
    o_ref[...] = jnp.where(lane == 0, d0, jnp.where(lane == 1, d1, 0.0)).astype(jnp.int32)
    carry_s[...] = carry_s[...] + jnp.sum(both, axis=0, keepdims=True)


def _dispatch(rt, n_tok, tm):
    n_slots = n_tok * TOP_K_INNER
    flat_e = rt[:, :TOP_K_INNER].astype(jnp.int32).reshape(-1)
    counts = jnp.sum((flat_e[:, None] == jnp.arange(N_EXPERTS, dtype=jnp.int32)[None, :]).astype(jnp.int32), axis=0)
    padded = ((counts + EXPERT_BLOCK - 1) // EXPERT_BLOCK) * EXPERT_BLOCK
    pad_end = jnp.cumsum(padded)
    pad_start = pad_end - padded
    base = jnp.pad(pad_start.astype(F32), (0, LANES - N_EXPERTS)).reshape(1, LANES)
    r = jnp.arange(tm)
    tri = (r[None, :] < r[:, None]).astype(BF16)
    dest = pl.pallas_call(
        _dest_kernel,
        grid=(n_tok // tm,),
        in_specs=[pl.BlockSpec((tm, LANES), lambda i: (i, 0)),
                  pl.BlockSpec((1, LANES), lambda i: (0, 0)),
                  pl.BlockSpec((tm, tm), lambda i: (0, 0))],
        out_specs=pl.BlockSpec((tm, LANES), lambda i: (i, 0)),
        out_shape=jax.ShapeDtypeStruct((n_tok, LANES), jnp.int32),
        scratch_shapes=[pltpu.VMEM((SUBLANES, LANES), F32)],
        compiler_params=_params(("arbitrary",)),
        name="dispatch_rows",
    )(rt, base, tri)[:, :TOP_K_INNER]
    cap = n_slots + N_EXPERTS * EXPERT_BLOCK
    nb = cap // EXPERT_BLOCK
    blk_e = jnp.minimum(jnp.searchsorted(pad_end, jnp.arange(nb, dtype=jnp.int32) * EXPERT_BLOCK, side='right'),
                        N_EXPERTS - 1).astype(jnp.int32)
    n_used = (pad_end[-1] // EXPERT_BLOCK).astype(jnp.int32).reshape(1)
    return dest, cap, blk_e, n_used


def kernel(x_prompt, x_sample, c_prompt, c_sample, w_ada, b_ada, norm1_g, w_in, q_norm_g, k_norm_g, lambda_q1, lambda_k1, lambda_q2, lambda_k2, attn_subln_g, w_attn_o, conv_w, conv_b, dt_bias, a_log, d_skip, ssd_norm_g, w_ssd_o, w_out, norm2_g, w_group, b_group, w_router, b_router, w_gate_e, w_up_e, w_down_e):
    groups = [(x_prompt, c_prompt), (x_sample, c_sample)]
    seqs = [(x.shape[0], x.shape[1]) for x, _ in groups]
    n_tok = sum(b * s for b, s in seqs)
    min_seq = min(s for _, s in seqs)
    tm = min(1024, min_seq)
    tp = min(256, min_seq)
    tq = min(512, min_seq)
    tk = min(512, min_seq)
    layer = 0

    x = jnp.concatenate([g[0].reshape(-1, D_MODEL) for g in groups], axis=0)
    c = jnp.concatenate([g[1] for g in groups], axis=0)
    n_batch = c.shape[0]
    c_pad = jnp.pad(c, ((0, (-n_batch) % SUBLANES), (0, 0)))
    mod = _ada(c_pad, w_ada[layer], b_ada[layer]).reshape(-1, 6, D_MODEL)
    tok_batch = jnp.concatenate([jnp.repeat(jnp.arange(b, dtype=jnp.int32), s) + off
                                 for (b, s), off in zip(seqs, [0, seqs[0][0]])])
    mod_tm = mod[tok_batch[::tm]]
    mod_tp = mod[tok_batch[::tp]]

    h = _norm_mod(x, mod_tm, norm1_g[layer], tm)

    w_in_b = w_in[layer].astype(BF16)
    tabs = [_rope_tables(s) for _, s in seqs]
    cos_t = jnp.concatenate([jnp.tile(tb[0], (b, 1)) for tb, (b, _) in zip(tabs, seqs)], axis=0)
    sin_t = jnp.concatenate([jnp.tile(tb[1], (b, 1)) for tb, (b, _) in zip(tabs, seqs)], axis=0)
    half = jnp.arange(LANES) // QK_DIM
    bd = (half[:, None] == half[None, :]).astype(BF16)
    gq = (jnp.tile(q_norm_g[layer], LANES // QK_DIM) * (QK_DIM ** -0.5 * LOG2_E)).reshape(1, LANES)
    gk = jnp.tile(k_norm_g[layer], LANES // QK_DIM).reshape(1, LANES)
    q = _qk_proj(h, w_in_b[:, OFF_Q:OFF_K], gq, cos_t, sin_t, bd, tm, "q_proj")
    k = _qk_proj(h, w_in_b[:, OFF_K:OFF_V], gk, cos_t, sin_t, bd, tm, "k_proj")
    vt3 = _vt_proj(h, w_in_b[:, OFF_V:OFF_Z], tm, tk)
    z_act = _matmul(h, w_in_b[:, OFF_Z:OFF_XBC], BF16, tm, 1024, "z_proj", _silu)
    xbc = _matmul(h, w_in_b[:, OFF_XBC:OFF_DT], F32, tm, 1024, "xbc_proj")
    w_dt = jnp.pad(w_in_b[:, OFF_DT:OFF_GATE], ((0, 0), (0, LANES - DT_COLS)))
    dt_b = jnp.pad(dt_bias[layer].reshape(1, DT_COLS), ((0, 0), (0, LANES - DT_COLS)))
    dt = _matmul(h, w_dt, F32, tm, LANES, "dt_proj", _softplus, dt_b)
    gates = _matmul(h, w_in_b[:, OFF_GATE:], BF16, tm, 1024, "gate_proj", _sigmoid)

    k3 = k.reshape(n_tok // tk, tk, ATTN_WIDTH)
    lams = [v[layer].reshape(1, QK_DIM) for v in (lambda_q1, lambda_k1, lambda_q2, lambda_k2)]
    o_groups = []
    off = 0
    for b, s in seqs:
        o_groups.append(_attention(q, k3, vt3, lams, off, b, s, tq, tk))
        off += b * s

    seq_starts = []
    off = 0
    for b, s in seqs:
        seq_starts += [(off + i * s, s) for i in range(b)]
        off += b * s
    tile_start = jnp.zeros((n_tok // tp,), jnp.int32)
    tile_end = jnp.zeros((n_tok // tp,), jnp.int32)
    nc = n_tok // CHUNK
    chunk_reset = jnp.zeros((nc,), jnp.int32)
    bwd_idx = jnp.zeros((nc,), jnp.int32)
    for st, s in seq_starts:
        tile_start = tile_start.at[st // tp].set(1)
        tile_end = tile_end.at[(st + s) // tp - 1].set(1)
        c0, c1 = st // CHUNK, (st + s) // CHUNK
        chunk_reset = chunk_reset.at[c0].set(1)
        bwd_idx = bwd_idx.at[c0:c1].set(jnp.arange(c1 - 1, c0 - 1, -1, dtype=jnp.int32))
    fwd_idx = jnp.arange(nc, dtype=jnp.int32)
    xs, bcm = _conv(xbc, conv_w[layer], conv_b[layer], tile_start, tile_end, tp)
    dtt = dt.T
    alg = a_log[layer].reshape(-1)
    yf = _ssd(xs, bcm, dt, dtt, alg, fwd_idx, chunk_reset, False)
    yb = _ssd(xs, bcm, dt, dtt, alg, bwd_idx, chunk_reset, True)

    subg = attn_subln_g[layer].reshape(1, V_DIM)
    dskip = jnp.repeat(d_skip[layer], SSD_HEAD_DIM).reshape(1, D_INNER)
    x1, h2b, h2f = _post(o_groups[0], o_groups[1], yf, yb, xs, z_act, gates, x, mod_tp, subg, dskip,
                         ssd_norm_g[layer].reshape(1, D_INNER), norm2_g[layer].reshape(1, D_MODEL),
                         w_attn_o[layer].astype(BF16), w_ssd_o[layer].astype(BF16),
                         w_out[layer].astype(BF16), tp)

    n_rt = N_EXPERT_GROUPS + N_EXPERTS
    w_rt = jnp.pad(jnp.concatenate([w_group[layer], w_router[layer]], axis=1), ((0, 0), (0, LANES - n_rt)))
    b_rt = jnp.pad(jnp.concatenate([b_group[layer], b_router[layer]]), (0, LANES - n_rt)).reshape(1, LANES)
    rt = _router(h2f, w_rt, b_rt, tp)
    dest, cap, blk_e, n_used = _dispatch(rt, n_tok, tp)
    xb = jnp.zeros((cap, D_MODEL), BF16)
    for kk in range(TOP_K_INNER):
        xb = xb.at[dest[:, kk]].set(h2b, unique_indices=True)
    yb_e = _experts(xb, blk_e, n_used, w_gate_e[layer], w_up_e[layer], w_down_e[layer])
    g0 = yb_e[dest[:, 0]]
    g1 = yb_e[dest[:, 1]]

    outs = []
    off = 0
    for (b, s), (xg, _) in zip(seqs, groups):
        y = _final(x1, mod_tp, rt, g0, g1, off, b * s, tp)
        outs.append(y.reshape(xg.shape))
        off += b * s
    return tuple(outs)
```

```python
import functools
import math

import jax
import jax.numpy as jnp
from jax import lax
from jax.experimental import pallas as pl
from jax.experimental.pallas import tpu as pltpu

F32 = jnp.float32
BF16 = jnp.bfloat16

D_MODEL = 1024
ATTN_HEADS = 8
QK_DIM = 64
V_DIM = 2 * QK_DIM
ATTN_WIDTH = ATTN_HEADS * V_DIM
ROPE_THETA = 10000.0
D_INNER = 2048
SSD_HEAD_DIM = 64
SSD_HEADS = D_INNER // SSD_HEAD_DIM
SSD_GROUPS = 4
HEADS_PER_GROUP = SSD_HEADS // SSD_GROUPS
D_STATE = 128
CONV_K = 5
CONV_CH = D_INNER + 2 * SSD_GROUPS * D_STATE
CHUNK = 128
N_EXPERT_GROUPS = 4
EXPERTS_PER_GROUP = 8
N_EXPERTS = N_EXPERT_GROUPS * EXPERTS_PER_GROUP
TOP_K_INNER = 2
EXPERT_FF = 512
RMS_EPS = 1e-6
LAMBDA_INIT = 0.8 - 0.6 * math.exp(-0.3 * 0)

LANES = 128
SUBLANES = 8
VMEM_LIMIT = 56 * 1024 * 1024

Q_COLS = ATTN_HEADS * 2 * QK_DIM
K_COLS = Q_COLS
V_COLS = ATTN_WIDTH
Z_COLS = D_INNER
XBC_COLS = CONV_CH
DT_COLS = 2 * SSD_HEADS
GATE_COLS = 2 * D_MODEL
OFF_Q = 0
OFF_K = OFF_Q + Q_COLS
OFF_V = OFF_K + K_COLS
OFF_Z = OFF_V + V_COLS
OFF_XBC = OFF_Z + Z_COLS
OFF_DT = OFF_XBC + XBC_COLS
OFF_GATE = OFF_DT + DT_COLS

EXPERT_BLOCK = 256
ONES_ROWS = 16
LOG2_E = math.log2(math.e)


def _params(sem):
    return pltpu.CompilerParams(dimension_semantics=sem, vmem_limit_bytes=VMEM_LIMIT)


def _dot(a, b):
    return jnp.dot(a, b, preferred_element_type=F32)


def _dot_tn(a, b):
    return lax.dot_general(a, b, (((0,), (0,)), ((), ())), preferred_element_type=F32)


def _dot_nt(a, b):
    return lax.dot_general(a, b, (((1,), (1,)), ((), ())), preferred_element_type=F32)


def _split3(a):
    hi = a.astype(BF16)
    r = a - hi.astype(F32)
    mid = r.astype(BF16)
    lo = (r - mid.astype(F32)).astype(BF16)
    return hi, mid, lo


def _dot_left01(m01, a):
    hi, mid, lo = _split3(a)
    return _dot(m01, hi) + _dot(m01, mid) + _dot(m01, lo)


def _dot_right01(a, m01):
    hi, mid, lo = _split3(a)
    return _dot(hi, m01) + _dot(mid, m01) + _dot(lo, m01)


def _spread01(a, m01):
    hi = a.astype(BF16)
    lo = (a - hi.astype(F32)).astype(BF16)
    return _dot(hi, m01) + _dot(lo, m01)


def _dot_f32(a, b):
    a0, a1, a2 = _split3(a)
    b0, b1, b2 = _split3(b)
    return (_dot(a0, b0) + (_dot(a0, b1) + _dot(a1, b0))
            + (_dot(a0, b2) + _dot(a2, b0) + _dot(a1, b1)))


def _dot_f32_3(a, b):
    a0 = a.astype(BF16)
    a1 = (a - a0.astype(F32)).astype(BF16)
    b0 = b.astype(BF16)
    b1 = (b - b0.astype(F32)).astype(BF16)
    return _dot(a0, b0) + (_dot(a0, b1) + _dot(a1, b0))


def _sigmoid(x):
    return 1.0 / (1.0 + jnp.exp(-x))


def _silu(x):
    return x * _sigmoid(x)


def _softplus(x):
    e = jnp.exp(-jnp.abs(x))
    u = 1.0 + e
    log1p_e = jnp.where(u == 1.0, e, jnp.log(u) * (e / (u - 1.0)))
    return jnp.maximum(x, 0.0) + log1p_e


def _ada_kernel(c_ref, w_ref, b_ref, o_ref):
    o_ref[...] = _dot_f32(_silu(c_ref[...]), w_ref[...]) + b_ref[...]


def _ada(c_pad, w_ada, b_ada):
    rows = c_pad.shape[0]
    n = w_ada.shape[1]
    tn = 1024
    return pl.pallas_call(
        _ada_kernel,
        grid=(n // tn,),
        in_specs=[pl.BlockSpec((rows, D_MODEL), lambda j: (0, 0)),
                  pl.BlockSpec((D_MODEL, tn), lambda j: (0, j)),
                  pl.BlockSpec((1, tn), lambda j: (0, j))],
        out_specs=pl.BlockSpec((rows, tn), lambda j: (0, j)),
        out_shape=jax.ShapeDtypeStruct((rows, n), F32),
        compiler_params=_params(("arbitrary",)),
        name="ada_mod",
    )(c_pad, w_ada, b_ada.reshape(1, n))


def _norm_mod_kernel(x_ref, mod_ref, g_ref, o_ref, *, shift_row, scale_row):
    x = x_ref[...]
    r = lax.rsqrt(jnp.mean(x * x, axis=-1, keepdims=True) + RMS_EPS)
    m = mod_ref[0]
    h = ((x * r) * g_ref[...]) * (1.0 + m[scale_row:scale_row + 1]) + m[shift_row:shift_row + 1]
    o_ref[...] = h.astype(o_ref.dtype)


def _norm_mod(x, mod_tiles, g, tm):
    t = x.shape[0]
    return pl.pallas_call(
        functools.partial(_norm_mod_kernel, shift_row=0, scale_row=1),
        grid=(t // tm,),
        in_specs=[pl.BlockSpec((tm, D_MODEL), lambda i: (i, 0)),
                  pl.BlockSpec((1, 6, D_MODEL), lambda i: (i, 0, 0)),
                  pl.BlockSpec((1, D_MODEL), lambda i: (0, 0))],
        out_specs=pl.BlockSpec((tm, D_MODEL), lambda i: (i, 0)),
        out_shape=jax.ShapeDtypeStruct((t, D_MODEL), BF16),
        compiler_params=_params(("parallel",)),
        name="norm1_mod",
    )(x, mod_tiles, g.reshape(1, D_MODEL))


def _mm_kernel(a_ref, w_ref, *rest, act):
    o_ref = rest[-1]
    acc = _dot(a_ref[...], w_ref[...])
    if len(rest) == 2:
        acc = acc + rest[0][...]
    if act is not None:
        acc = act(acc)
    o_ref[...] = acc.astype(o_ref.dtype)


def _matmul(a, w, out_dtype, tm, tn, name, act=None, bias=None):
    m, k = a.shape
    n = w.shape[1]
    in_specs = [pl.BlockSpec((tm, k), lambda j, i: (i, 0)),
                pl.BlockSpec((k, tn), lambda j, i: (0, j))]
    args = [a, w]
    if bias is not None:
        in_specs.append(pl.BlockSpec((1, tn), lambda j, i: (0, j)))
        args.append(bias)
    return pl.pallas_call(
        functools.partial(_mm_kernel, act=act),
        grid=(n // tn, m // tm),
        in_specs=in_specs,
        out_specs=pl.BlockSpec((tm, tn), lambda j, i: (i, j)),
        out_shape=jax.ShapeDtypeStruct((m, n), out_dtype),
        compiler_params=_params(("parallel", "parallel")),
        name=name,
    )(*args)


def _qk_kernel(a_ref, w_ref, g_ref, cos_ref, sin_ref, bd_ref, o_ref):
    acc = _dot(a_ref[...], w_ref[...])
    tm = acc.shape[0]
    lane = lax.broadcasted_iota(jnp.int32, (tm, LANES), 1)
    first = (lane & (QK_DIM // 2)) == 0
    cos = cos_ref[...]
    sin = sin_ref[...]
    g = g_ref[...]
    bd = bd_ref[...]
    for h in range(ATTN_HEADS):
        x = acc[:, h * LANES:(h + 1) * LANES]
        sq = x * x
        hi = sq.astype(BF16)
        lo = (sq - hi.astype(F32)).astype(BF16)
        ss = _dot(hi, bd) + _dot(lo, bd)
        r = lax.rsqrt(ss * (1.0 / QK_DIM) + RMS_EPS)
        xn = (x * r) * g
        partner = jnp.where(first, pltpu.roll(xn, LANES - QK_DIM // 2, 1),
                            pltpu.roll(xn, QK_DIM // 2, 1))
        o_ref[:, h * LANES:(h + 1) * LANES] = (xn * cos + partner * sin).astype(o_ref.dtype)


def _qk_proj(h, w, g128, cos_t, sin_t, bd, tm, name):
    t = h.shape[0]
    n = w.shape[1]
    return pl.pallas_call(
        _qk_kernel,
        grid=(t // tm,),
        in_specs=[pl.BlockSpec((tm, D_MODEL), lambda i: (i, 0)),
                  pl.BlockSpec((D_MODEL, n), lambda i: (0, 0)),
                  pl.BlockSpec((1, LANES), lambda i: (0, 0)),
                  pl.BlockSpec((tm, LANES), lambda i: (i, 0)),
                  pl.BlockSpec((tm, LANES), lambda i: (i, 0)),
                  pl.BlockSpec((LANES, LANES), lambda i: (0, 0))],
        out_specs=pl.BlockSpec((tm, n), lambda i: (i, 0)),
        out_shape=jax.ShapeDtypeStruct((t, n), BF16),
        compiler_params=_params(("parallel",)),
        name=name,
    )(h, w, g128, cos_t, sin_t, bd)


def _vt_kernel(a_ref, w_ref, o_ref, *, tk):
    acc = _dot(a_ref[...], w_ref[...])
    for c in range(acc.shape[0] // tk):
        o_ref[c] = acc[c * tk:(c + 1) * tk, :].T.astype(o_ref.dtype)


def _vt_proj(h, w, tm, tk):
    t = h.shape[0]
    n = w.shape[1]
    return pl.pallas_call(
        functools.partial(_vt_kernel, tk=tk),
        grid=(t // tm,),
        in_specs=[pl.BlockSpec((tm, D_MODEL), lambda i: (i, 0)),
                  pl.BlockSpec((D_MODEL, n), lambda i: (0, 0))],
        out_specs=pl.BlockSpec((tm // tk, n, tk), lambda i: (i, 0, 0)),
        out_shape=jax.ShapeDtypeStruct((t // tk, n, tk), BF16),
        compiler_params=_params(("parallel",)),
        name="v_proj_t",
    )(h, w)


def _attn_kernel(lq1_ref, lk1_ref, lq2_ref, lk2_ref, q_ref, k_ref, vt_ref, o_ref, q2t_s, s_s, m_s, acc_s,
                 *, tq, nkv, unroll):
    tk = k_ref.shape[1]
    qt = q_ref[...].astype(F32).T
    row = lax.broadcasted_iota(jnp.int32, qt.shape, 0)
    q2t_s[:, :tq] = jnp.where(row < QK_DIM, qt, 0.0).astype(BF16)
    q2t_s[:, tq:] = jnp.where(row >= QK_DIM, qt, 0.0).astype(BF16)
    m_s[...] = jnp.full(m_s.shape, -jnp.inf, F32)
    acc_s[...] = jnp.zeros(acc_s.shape, F32)
    ones = jnp.ones((ONES_ROWS, tk), BF16)
    s_s[0] = _dot(k_ref[0], q2t_s[...])

    def step(j, par, compute_next):
        if compute_next:
            s_s[1 - par] = _dot(k_ref[j + 1], q2t_s[...])
        st = s_s[par]
        m_old = m_s[...]
        m_new = jnp.maximum(m_old, jnp.max(st, axis=0, keepdims=True))
        alpha = jnp.exp2(m_old - m_new)
        p = jnp.exp2(st - m_new).astype(BF16)
        lhs = jnp.concatenate([vt_ref[j], ones], axis=0)
        acc_s[...] = alpha * acc_s[...] + _dot(lhs, p)
        m_s[...] = m_new

    n_trips = (nkv - 1) // unroll

    def trip(t, carry):
        for u in range(unroll):
            step(unroll * t + u, u % 2, True)
        return carry

    lax.fori_loop(0, n_trips, trip, 0)
    for j in range(unroll * n_trips, nkv):
        step(j, j % 2, j + 1 < nkv)

    lam = (jnp.exp(jnp.sum(lq1_ref[...] * lk1_ref[...], axis=1, keepdims=True))
           - jnp.exp(jnp.sum(lq2_ref[...] * lk2_ref[...], axis=1, keepdims=True)) + LAMBDA_INIT)
    ot = acc_s[:V_DIM, :] / acc_s[V_DIM:V_DIM + 1, :]
    o_ref[...] = (ot[:, :tq] - lam * ot[:, tq:]).T


def _attention(q, k3, vt3, lams, tok_off, batch, seq, tq, tk):
    nkv = seq // tk
    qb0 = tok_off // tq
    sb0 = tok_off // seq
    nq = seq // tq
    unroll = 4 if nkv >= 16 else 2
    lam_spec = pl.BlockSpec((1, QK_DIM), lambda b, h, i: (0, 0))
    return pl.pallas_call(
        functools.partial(_attn_kernel, tq=tq, nkv=nkv, unroll=unroll),
        grid=(batch, ATTN_HEADS, nq),
        in_specs=[lam_spec, lam_spec, lam_spec, lam_spec,
                  pl.BlockSpec((tq, LANES), lambda b, h, i: (qb0 + b * nq + i, h)),
                  pl.BlockSpec((nkv, tk, LANES), lambda b, h, i: (sb0 + b, 0, h)),
                  pl.BlockSpec((nkv, LANES, tk), lambda b, h, i: (sb0 + b, h, 0))],
        out_specs=pl.BlockSpec((tq, LANES), lambda b, h, i: (b * nq + i, h)),
        out_shape=jax.ShapeDtypeStruct((batch * seq, ATTN_WIDTH), F32),
        scratch_shapes=[pltpu.VMEM((LANES, 2 * tq), BF16),
                        pltpu.VMEM((2, tk, 2 * tq), F32),
                        pltpu.VMEM((1, 2 * tq), F32),
                        pltpu.VMEM((V_DIM + ONES_ROWS, 2 * tq), F32)],
        compiler_params=_params(("parallel", "parallel", "arbitrary")),
        name="diff_attention",
    )(*lams, q, k3, vt3)


def _conv_kernel(start_ref, end_ref, prev_ref, cur_ref, next_ref, w_ref, b_ref, xs_ref, bc_ref, ext_s, *, tm):
    i = pl.program_id(0)
    halo = SUBLANES
    ext_s[0:halo, :] = jnp.where(start_ref[i] == 1, 0.0, prev_ref[...])
    ext_s[halo:halo + tm, :] = cur_ref[...]
    ext_s[halo + tm:2 * halo + tm, :] = jnp.where(end_ref[i] == 1, 0.0, next_ref[...])
    pad = (CONV_K - 1) // 2
    acc = jnp.broadcast_to(b_ref[...], (tm, CONV_CH))
    for d in range(CONV_K):
        acc = acc + w_ref[d:d + 1, :] * ext_s[halo - pad + d:halo - pad + d + tm, :]
    y = _silu(acc)
    xs_ref[...] = y[:, :D_INNER]
    bc_ref[...] = y[:, D_INNER:].astype(bc_ref.dtype)


def _conv(xbc, conv_w, conv_b, start_flags, end_flags, tm):
    t = xbc.shape[0]
    rb = tm // SUBLANES
    last = t // SUBLANES - 1
    w_pad = jnp.zeros((SUBLANES, CONV_CH), F32).at[:CONV_K].set(conv_w)
    grid_spec = pltpu.PrefetchScalarGridSpec(
        num_scalar_prefetch=2,
        grid=(t // tm,),
        in_specs=[pl.BlockSpec((SUBLANES, CONV_CH), lambda i, s, e: (jnp.maximum(i * rb - 1, 0), 0)),
                  pl.BlockSpec((tm, CONV_CH), lambda i, s, e: (i, 0)),
                  pl.BlockSpec((SUBLANES, CONV_CH), lambda i, s, e: (jnp.minimum((i + 1) * rb, last), 0)),
                  pl.BlockSpec((SUBLANES, CONV_CH), lambda i, s, e: (0, 0)),
                  pl.BlockSpec((1, CONV_CH), lambda i, s, e: (0, 0))],
        out_specs=[pl.BlockSpec((tm, D_INNER), lambda i, s, e: (i, 0)),
                   pl.BlockSpec((tm, CONV_CH - D_INNER), lambda i, s, e: (i, 0))],
        scratch_shapes=[pltpu.VMEM((tm + 2 * SUBLANES, CONV_CH), F32)],
    )
    return pl.pallas_call(
        functools.partial(_conv_kernel, tm=tm),
        grid_spec=grid_spec,
        out_shape=[jax.ShapeDtypeStruct((t, D_INNER), F32),
                   jax.ShapeDtypeStruct((t, CONV_CH - D_INNER), BF16)],
        compiler_params=_params(("arbitrary",)),
        name="conv_silu",
    )(start_flags, end_flags, xbc, xbc, xbc, w_pad, conv_b.reshape(1, CONV_CH))


def _ssd_kernel(idx_ref, reset_ref, xs_ref, bc_ref, dt_ref, dtt_ref,
                alog_r_ref, alog_c_ref, tril_ref, triu_ref, e01_ref, y_ref, state_s, *, rev):
    i = pl.program_id(0)
    nh = SSD_HEADS

    @pl.when(reset_ref[i] == 1)
    def _():
        state_s[...] = jnp.zeros(state_s.shape, F32)

    xs = xs_ref[...]
    bc = bc_ref[...]
    dtn = dt_ref[...]
    a = dtn * (-jnp.exp(alog_r_ref[...]))
    pinc = _dot_left01(tril_ref[...], a)
    pex = pinc - a
    tot = pinc[CHUNK - 1:CHUNK, :]

    gw = HEADS_PER_GROUP * SSD_HEAD_DIM
    gn = SSD_GROUPS * D_STATE
    cd = jnp.broadcast_to(jnp.exp(tot), (SUBLANES, LANES))

    def update_state(sc_st, sc_cd):
        w = (xs * sc_st).astype(BF16)
        for g in range(SSD_GROUPS):
            bg = bc[:, g * D_STATE:(g + 1) * D_STATE]
            new = _dot_tn(bg, w[:, g * gw:(g + 1) * gw])
            state_s[:, g * gw:(g + 1) * gw] = (state_s[:, g * gw:(g + 1) * gw] * sc_cd[:, g * gw:(g + 1) * gw]
                                               + new)

    if rev:
        stacked = jnp.concatenate([jnp.exp(pex) * dtn, jnp.exp(tot - pex), cd], axis=0)
        ex = _spread01(stacked, e01_ref[...])
        sc_off = ex[CHUNK:2 * CHUNK]
        for g in range(SSD_GROUPS):
            cg = bc[:, gn + g * D_STATE:gn + (g + 1) * D_STATE]
            st = state_s[:, g * gw:(g + 1) * gw]
            y_ref[:, g * gw:(g + 1) * gw] = _dot(cg, st.astype(BF16)) * sc_off[:, g * gw:(g + 1) * gw]
        update_state(ex[:CHUNK], ex[2 * CHUNK:2 * CHUNK + 1])
        return

    dtnt = dtt_ref[...]
    at = dtnt * (-jnp.exp(alog_c_ref[...]))
    pinct = _dot_right01(at, triu_ref[...])
    pext = pinct - at
    li = lax.broadcasted_iota(jnp.int32, (CHUNK, CHUNK), 0)
    si = lax.broadcasted_iota(jnp.int32, (CHUNK, CHUNK), 1)
    lower = si <= li
    strict_lower = si < li
    strict_upper = si > li
    lane = lax.broadcasted_iota(jnp.int32, (CHUNK, LANES), 1)
    first_head = lane < SSD_HEAD_DIM
    xb = xs.astype(BF16)
    for g in range(SSD_GROUPS):
        bg = bc[:, g * D_STATE:(g + 1) * D_STATE]
        cg = bc[:, gn + g * D_STATE:gn + (g + 1) * D_STATE]
        cg_f = cg.astype(F32)
        cb = _dot_nt(cg, bg)
        for hp in range(HEADS_PER_GROUP // 2):
            col = g * gw + hp * LANES
            rhs = jnp.concatenate([xb[:, col:col + LANES], state_s[:, col:col + LANES].astype(BF16)], axis=0)
            pair = []
            for u in range(2):
                h = g * HEADS_PER_GROUP + 2 * hp + u
                colf = jnp.broadcast_to(pinc[:, h:h + 1], (CHUNK, CHUNK))
                arg = jnp.where(lower, colf - pinct[h:h + 1, :],
                                pext[nh + h:nh + h + 1, :] - pex[:, nh + h:nh + h + 1])
                dtf = dtnt[h:h + 1, :]
                dtb = dtnt[nh + h:nh + h + 1, :]
                coef = jnp.where(strict_lower, dtf, jnp.where(strict_upper, dtb, dtf + dtb))
                mh = (cb * jnp.exp(arg) * coef).astype(BF16)
                dh = (cg_f * jnp.exp(colf)).astype(BF16)
                pair.append(_dot(jnp.concatenate([mh, dh], axis=1), rhs))
            y_ref[:, col:col + LANES] = jnp.where(first_head, pair[0], pair[1])

    stacked = jnp.concatenate([jnp.exp(tot - pinc) * dtn, cd], axis=0)
    ex = _spread01(stacked, e01_ref[...])
    update_state(ex[:CHUNK], ex[CHUNK:CHUNK + 1])


def _ssd(xs, bc, dt, dtt, a_log, idx, reset, rev):
    t = xs.shape[0]
    nc = t // CHUNK
    alog_r = jnp.pad(a_log.reshape(1, DT_COLS), ((0, 0), (0, LANES - DT_COLS)))
    alog_c = alog_r.reshape(LANES, 1)
    r = jnp.arange(CHUNK)
    tril = (r[None, :] <= r[:, None]).astype(BF16)
    triu = (r[:, None] <= r[None, :]).astype(BF16)
    lo = SSD_HEADS if rev else 0
    e01 = (jnp.arange(D_INNER)[None, :] // SSD_HEAD_DIM == jnp.arange(LANES)[:, None] - lo).astype(BF16)
    const = lambda shape: pl.BlockSpec(shape, lambda i, ix, rs: (0, 0))
    grid_spec = pltpu.PrefetchScalarGridSpec(
        num_scalar_prefetch=2,
        grid=(nc,),
        in_specs=[pl.BlockSpec((CHUNK, D_INNER), lambda i, ix, rs: (ix[i], 0)),
                  pl.BlockSpec((CHUNK, CONV_CH - D_INNER), lambda i, ix, rs: (ix[i], 0)),
                  pl.BlockSpec((CHUNK, LANES), lambda i, ix, rs: (ix[i], 0)),
                  pl.BlockSpec((LANES, CHUNK), lambda i, ix, rs: (0, ix[i])),
                  const((1, LANES)), const((LANES, 1)),
                  const((CHUNK, CHUNK)), const((CHUNK, CHUNK)), const((LANES, D_INNER))],
        out_specs=pl.BlockSpec((CHUNK, D_INNER), lambda i, ix, rs: (ix[i], 0)),
        scratch_shapes=[pltpu.VMEM((D_STATE, D_INNER), F32)],
    )
    return pl.pallas_call(
        functools.partial(_ssd_kernel, rev=rev),
        grid_spec=grid_spec,
        out_shape=jax.ShapeDtypeStruct((t, D_INNER), F32),
        compiler_params=_params(("arbitrary",)),
        name="ssd_bwd" if rev else "ssd_fwd",
    )(idx, reset, xs, bc, dt, dtt, alog_r, alog_c, tril, triu, e01)


def _post_kernel(oa_ref, ob_ref, yf_ref, yb_ref, xs_ref, z_ref, gt_ref, x_ref, mod_ref, subg_ref, dskip_ref,
                 ssdg_ref, n2g_ref, wa_ref, ws_ref, wo_ref, x1_ref, h2b_ref, h2f_ref, an_s, yn_s, *, n_first):
    m = mod_ref[0]
    o = jnp.where(pl.program_id(0) < n_first, oa_ref[...], ob_ref[...])
    for h in range(ATTN_HEADS):
        oh = o[:, h * V_DIM:(h + 1) * V_DIM]
        r = lax.rsqrt(jnp.mean(oh * oh, axis=-1, keepdims=True) + RMS_EPS)
        an_s[:, h * V_DIM:(h + 1) * V_DIM] = (((oh * r) * subg_ref[...]) * (1.0 - LAMBDA_INIT)).astype(BF16)
    attn_d = _dot(an_s[...], wa_ref[...])

    y = (yf_ref[...] + yb_ref[...]) + dskip_ref[...] * xs_ref[...]
    y = y * z_ref[...].astype(F32)
    gw = D_INNER // SSD_GROUPS
    for g in range(SSD_GROUPS):
        yg = y[:, g * gw:(g + 1) * gw]
        r = lax.rsqrt(jnp.mean(yg * yg, axis=-1, keepdims=True) + RMS_EPS)
        yn_s[:, g * gw:(g + 1) * gw] = ((yg * r) * ssdg_ref[:, g * gw:(g + 1) * gw]).astype(BF16)
    ssd_d = _dot(yn_s[...], ws_ref[...])

    gt = gt_ref[...].astype(F32)
    mix = gt[:, :D_MODEL] * attn_d + gt[:, D_MODEL:] * ssd_d
    mixed = _dot(mix.astype(BF16), wo_ref[...])
    x1 = x_ref[...] + m[2:3] * mixed
    x1_ref[...] = x1
    r = lax.rsqrt(jnp.mean(x1 * x1, axis=-1, keepdims=True) + RMS_EPS)
    h2 = ((x1 * r) * n2g_ref[...]) * (1.0 + m[4:5]) + m[3:4]
    h2f_ref[...] = h2
    h2b_ref[...] = h2.astype(BF16)


def _post(oa, ob, yf, yb, xs, z, gates, x, mod_tiles, subg, dskip, ssdg, n2g, wa, ws, wo, tm):
    t = x.shape[0]
    n_first = oa.shape[0] // tm
    tok = lambda n: pl.BlockSpec((tm, n), lambda i: (i, 0))
    const = lambda a: pl.BlockSpec(a.shape, lambda i: (0, 0))
    return pl.pallas_call(
        functools.partial(_post_kernel, n_first=n_first),
        grid=(t // tm,),
        in_specs=[pl.BlockSpec((tm, ATTN_WIDTH), lambda i: (jnp.minimum(i, n_first - 1), 0)),
                  pl.BlockSpec((tm, ATTN_WIDTH), lambda i: (jnp.maximum(i - n_first, 0), 0)),
                  tok(D_INNER), tok(D_INNER), tok(D_INNER), tok(D_INNER), tok(GATE_COLS),
                  tok(D_MODEL), pl.BlockSpec((1, 6, D_MODEL), lambda i: (i, 0, 0)),
                  const(subg), const(dskip), const(ssdg), const(n2g), const(wa), const(ws), const(wo)],
        out_specs=[tok(D_MODEL), tok(D_MODEL), tok(D_MODEL)],
        out_shape=[jax.ShapeDtypeStruct((t, D_MODEL), F32),
                   jax.ShapeDtypeStruct((t, D_MODEL), BF16),
                   jax.ShapeDtypeStruct((t, D_MODEL), F32)],
        scratch_shapes=[pltpu.VMEM((tm, ATTN_WIDTH), BF16), pltpu.VMEM((tm, D_INNER), BF16)],
        compiler_params=_params(("parallel",)),
        name="merge_out_proj",
    )(oa, ob, yf, yb, xs, z, gates, x, mod_tiles, subg, dskip, ssdg, n2g, wa, ws, wo)


def _router_kernel(h_ref, w_ref, b_ref, o_ref):
    logits = _dot_f32_3(h_ref[...], w_ref[...]) + b_ref[...]
    tm = logits.shape[0]
    lane = lax.broadcasted_iota(jnp.int32, (tm, LANES), 1)
    lane_f = lane.astype(F32)
    big = float(LANES)
    neg = -jnp.inf
    gl = jnp.where(lane < N_EXPERT_GROUPS, logits, neg)
    gmax = jnp.max(gl, axis=1, keepdims=True)
    g_sel = jnp.min(jnp.where(gl == gmax, lane_f, big), axis=1, keepdims=True)
    g_w = 1.0 / jnp.sum(jnp.exp(gl - gmax), axis=1, keepdims=True)
    e_lane = lane - N_EXPERT_GROUPS
    e_group = (e_lane >> 3).astype(F32)
    in_group = (e_lane >= 0) & (e_lane < N_EXPERTS) & (e_group == g_sel)
    el = jnp.where(in_group, logits, neg)
    v1 = jnp.max(el, axis=1, keepdims=True)
    i1 = jnp.min(jnp.where(el == v1, lane_f, big), axis=1, keepdims=True)
    el2 = jnp.where(lane_f == i1, neg, el)
    v2 = jnp.max(el2, axis=1, keepdims=True)
    i2 = jnp.min(jnp.where(el2 == v2, lane_f, big), axis=1, keepdims=True)
    e2 = jnp.exp(v2 - v1)
    w1 = g_w / (1.0 + e2)
    w2 = g_w * e2 / (1.0 + e2)
    out = jnp.where(lane == 0, i1 - N_EXPERT_GROUPS,
                    jnp.where(lane == 1, i2 - N_EXPERT_GROUPS,
                              jnp.where(lane == 2, w1, jnp.where(lane == 3, w2, 0.0))))
    o_ref[...] = out


def _router(h2f, w_rt, b_rt, tm):
    t = h2f.shape[0]
    return pl.pallas_call(
        _router_kernel,
        grid=(t // tm,),
        in_specs=[pl.BlockSpec((tm, D_MODEL), lambda i: (i, 0)),
                  pl.BlockSpec((D_MODEL, LANES), lambda i: (0, 0)),
                  pl.BlockSpec((1, LANES), lambda i: (0, 0))],
        out_specs=pl.BlockSpec((tm, LANES), lambda i: (i, 0)),
        out_shape=jax.ShapeDtypeStruct((t, LANES), F32),
        compiler_params=_params(("parallel",)),
        name="router_topk",
    )(h2f, w_rt, b_rt)


def _expert_kernel(be_ref, nu_ref, x_ref, wg_ref, wu_ref, wd_ref, o_ref, wg_s, wu_s, wd_s):
    i = pl.program_id(0)
    used = i < nu_ref[0]
    new_expert = (i == 0) | (be_ref[i] != be_ref[jnp.maximum(i - 1, 0)])

    @pl.when(used & new_expert)
    def _():
        wg_s[...] = wg_ref[0].astype(BF16)
        wu_s[...] = wu_ref[0].astype(BF16)
        wd_s[...] = wd_ref[0].astype(BF16)

    @pl.when(used)
    def _():
        x = x_ref[...]
        a = _silu(_dot(x, wg_s[...])) * _dot(x, wu_s[...])
        o_ref[...] = _dot(a.astype(BF16), wd_s[...])

    @pl.when(i >= nu_ref[0])
    def _():
        o_ref[...] = jnp.zeros(o_ref.shape, o_ref.dtype)


def _experts(xb, blk_e, n_used, wg, wu, wd):
    cap = xb.shape[0]
    nb = cap // EXPERT_BLOCK
    grid_spec = pltpu.PrefetchScalarGridSpec(
        num_scalar_prefetch=2,
        grid=(nb,),
        in_specs=[pl.BlockSpec((EXPERT_BLOCK, D_MODEL), lambda i, be, nu: (jnp.minimum(i, nu[0] - 1), 0)),
                  pl.BlockSpec((1, D_MODEL, EXPERT_FF), lambda i, be, nu: (be[i], 0, 0)),
                  pl.BlockSpec((1, D_MODEL, EXPERT_FF), lambda i, be, nu: (be[i], 0, 0)),
                  pl.BlockSpec((1, EXPERT_FF, D_MODEL), lambda i, be, nu: (be[i], 0, 0))],
        out_specs=pl.BlockSpec((EXPERT_BLOCK, D_MODEL), lambda i, be, nu: (i, 0)),
        scratch_shapes=[pltpu.VMEM((D_MODEL, EXPERT_FF), BF16), pltpu.VMEM((D_MODEL, EXPERT_FF), BF16),
                        pltpu.VMEM((EXPERT_FF, D_MODEL), BF16)],
    )
    return pl.pallas_call(
        _expert_kernel,
        grid_spec=grid_spec,
        out_shape=jax.ShapeDtypeStruct((cap, D_MODEL), F32),
        compiler_params=_params(("arbitrary",)),
        name="expert_mlp",
    )(blk_e, n_used, xb, wg, wu, wd)


def _final_kernel(x1_ref, mod_ref, rt_ref, g0_ref, g1_ref, o_ref):
    m = mod_ref[0]
    rt = rt_ref[...]
    moe = g0_ref[...] * rt[:, 2:3] + g1_ref[...] * rt[:, 3:4]
    o_ref[...] = x1_ref[...] + m[5:6] * moe


def _final(x1, mod_tiles, rt, g0, g1, tok_off, n_tok, tm):
    b0 = tok_off // tm
    tok = lambda n: pl.BlockSpec((tm, n), lambda i: (b0 + i, 0))
    return pl.pallas_call(
        _final_kernel,
        grid=(n_tok // tm,),
        in_specs=[tok(D_MODEL), pl.BlockSpec((1, 6, D_MODEL), lambda i: (b0 + i, 0, 0)), tok(LANES),
                  tok(D_MODEL), tok(D_MODEL)],
        out_specs=pl.BlockSpec((tm, D_MODEL), lambda i: (i, 0)),
        out_shape=jax.ShapeDtypeStruct((n_tok, D_MODEL), F32),
        compiler_params=_params(("parallel",)),
        name="moe_combine",
    )(x1, mod_tiles, rt, g0, g1)


def _rope_tables(seq):
    pos = jnp.arange(seq, dtype=F32)
    inv = 1.0 / (ROPE_THETA ** (jnp.arange(0, QK_DIM, 2, dtype=F32) / QK_DIM))
    ang = pos[:, None] * inv[None, :]
    cos, sin = jnp.cos(ang), jnp.sin(ang)
    cos_t = jnp.tile(cos, (1, LANES // (QK_DIM // 2)))
    sin_t = jnp.tile(jnp.concatenate([-sin, sin], axis=1), (1, LANES // QK_DIM))
    return cos_t, sin_t


def _dest_kernel(rt_ref, base_ref, tri_ref, o_ref, carry_s):
    @pl.when(pl.program_id(0) == 0)
    def _():
        carry_s[...] = jnp.zeros(carry_s.shape, F32)

    rt = rt_ref[...]
    tm = rt.shape[0]
    lane = lax.broadcasted_iota(jnp.int32, (tm, LANES), 1)
    lane_f = lane.astype(F32)
    oh0 = lane_f == rt[:, 0:1]
    oh1 = lane_f == rt[:, 1:2]
    both = jnp.where(oh0 | oh1, 1.0, 0.0)
    pos = base_ref[...] + carry_s[0:1, :] + _dot(tri_ref[...], both.astype(BF16))
    d0 = jnp.sum(jnp.where(oh0, pos, 0.0), axis=1, keepdims=True)
    d1 = jnp.sum(jnp.where(oh1, pos, 0.0), axis=1, keepdims=True)
    o_ref[...] = jnp.where(lane == 0, d0, jnp.where(lane == 1, d1, 0.0)).astype(jnp.int32)
    carry_s[...] = carry_s[...] + jnp.sum(both, axis=0, keepdims=True)


def _dispatch(rt, n_tok, tm):
    n_slots = n_tok * TOP_K_INNER
    flat_e = rt[:, :TOP_K_INNER].astype(jnp.int32).reshape(-1)
    counts = jnp.sum((flat_e[:, None] == jnp.arange(N_EXPERTS, dtype=jnp.int32)[None, :]).astype(jnp.int32), axis=0)
    padded = ((counts + EXPERT_BLOCK - 1) // EXPERT_BLOCK) * EXPERT_BLOCK
    pad_end = jnp.cumsum(padded)
    pad_start = pad_end - padded
    base = jnp.pad(pad_start.astype(F32), (0, LANES - N_EXPERTS)).reshape(1, LANES)
    r = jnp.arange(tm)
    tri = (r[None, :] < r[:, None]).astype(BF16)
    dest = pl.pallas_call(
        _dest_kernel,
        grid=(n_tok // tm,),
        in_specs=[pl.BlockSpec((tm, LANES), lambda i: (i, 0)),
                  pl.BlockSpec((1, LANES), lambda i: (0, 0)),
                  pl.BlockSpec((tm, tm), lambda i: (0, 0))],
        out_specs=pl.BlockSpec((tm, LANES), lambda i: (i, 0)),
        out_shape=jax.ShapeDtypeStruct((n_tok, LANES), jnp.int32),
        scratch_shapes=[pltpu.VMEM((SUBLANES, LANES), F32)],
        compiler_params=_params(("arbitrary",)),
        name="dispatch_rows",
    )(rt, base, tri)[:, :TOP_K_INNER]
    cap = n_slots + N_EXPERTS * EXPERT_BLOCK
    nb = cap // EXPERT_BLOCK
    blk_row0 = jnp.arange(nb, dtype=jnp.int32) * EXPERT_BLOCK
    blk_e = jnp.minimum(jnp.sum((pad_end[None, :] <= blk_row0[:, None]).astype(jnp.int32), axis=1), N_EXPERTS - 1)
    n_used = (pad_end[-1] // EXPERT_BLOCK).astype(jnp.int32).reshape(1)

    n_pad = cap - n_slots
    seg_len = jnp.concatenate([padded - counts, (cap - pad_end[-1]).reshape(1)])
    seg_end = jnp.cumsum(seg_len)
    seg_start = seg_end - seg_len
    seg_row0 = jnp.concatenate([pad_start + counts, pad_end[-1:]])
    j = jnp.arange(n_pad, dtype=jnp.int32)
    seg_onehot = (jnp.sum((seg_end[None, :] <= j[:, None]).astype(jnp.int32), axis=1)[:, None]
                  == jnp.arange(N_EXPERTS + 1, dtype=jnp.int32)[None, :]).astype(jnp.int32)
    pad_rows = j + jnp.sum(seg_onehot * (seg_row0 - seg_start)[None, :], axis=1)
    rows = jnp.concatenate([dest.reshape(-1), pad_rows.astype(jnp.int32)])
    toks = jnp.concatenate([jnp.arange(n_slots, dtype=jnp.int32) // TOP_K_INNER, jnp.zeros((n_pad,), jnp.int32)])
    _, tok_buf = lax.sort_key_val(rows, toks)
    return dest, tok_buf, blk_e.astype(jnp.int32), n_used


def kernel(x_prompt, x_sample, c_prompt, c_sample, w_ada, b_ada, norm1_g, w_in, q_norm_g, k_norm_g, lambda_q1, lambda_k1, lambda_q2, lambda_k2, attn_subln_g, w_attn_o, conv_w, conv_b, dt_bias, a_log, d_skip, ssd_norm_g, w_ssd_o, w_out, norm2_g, w_group, b_group, w_router, b_router, w_gate_e, w_up_e, w_down_e):
    groups = [(x_prompt, c_prompt), (x_sample, c_sample)]
    seqs = [(x.shape[0], x.shape[1]) for x, _ in groups]
    n_tok = sum(b * s for b, s in seqs)
    min_seq = min(s for _, s in seqs)
    tm = min(1024, min_seq)
    tp = min(256, min_seq)
    tq = min(512, min_seq)
    tk = min(512, min_seq)
    layer = 0

    x = jnp.concatenate([g[0].reshape(-1, D_MODEL) for g in groups], axis=0)
    c = jnp.concatenate([g[1] for g in groups], axis=0)
    n_batch = c.shape[0]
    c_pad = jnp.pad(c, ((0, (-n_batch) % SUBLANES), (0, 0)))
    mod = _ada(c_pad, w_ada[layer], b_ada[layer]).reshape(-1, 6, D_MODEL)
    tok_batch = jnp.concatenate([jnp.repeat(jnp.arange(b, dtype=jnp.int32), s) + off
                                 for (b, s), off in zip(seqs, [0, seqs[0][0]])])
    mod_tm = mod[tok_batch[::tm]]
    mod_tp = mod[tok_batch[::tp]]

    h = _norm_mod(x, mod_tm, norm1_g[layer], tm)

    w_in_b = w_in[layer].astype(BF16)
    tabs = [_rope_tables(s) for _, s in seqs]
    cos_t = jnp.concatenate([jnp.tile(tb[0], (b, 1)) for tb, (b, _) in zip(tabs, seqs)], axis=0)
    sin_t = jnp.concatenate([jnp.tile(tb[1], (b, 1)) for tb, (b, _) in zip(tabs, seqs)], axis=0)
    half = jnp.arange(LANES) // QK_DIM
    bd = (half[:, None] == half[None, :]).astype(BF16)
    gq = (jnp.tile(q_norm_g[layer], LANES // QK_DIM) * (QK_DIM ** -0.5 * LOG2_E)).reshape(1, LANES)
    gk = jnp.tile(k_norm_g[layer], LANES // QK_DIM).reshape(1, LANES)
    q = _qk_proj(h, w_in_b[:, OFF_Q:OFF_K], gq, cos_t, sin_t, bd, tm, "q_proj")
    k = _qk_proj(h, w_in_b[:, OFF_K:OFF_V], gk, cos_t, sin_t, bd, tm, "k_proj")
    vt3 = _vt_proj(h, w_in_b[:, OFF_V:OFF_Z], tm, tk)
    z_act = _matmul(h, w_in_b[:, OFF_Z:OFF_XBC], BF16, tm, 1024, "z_proj", _silu)
    xbc = _matmul(h, w_in_b[:, OFF_XBC:OFF_DT], F32, tm, 1024, "xbc_proj")
    w_dt = jnp.pad(w_in_b[:, OFF_DT:OFF_GATE], ((0, 0), (0, LANES - DT_COLS)))
    dt_b = jnp.pad(dt_bias[layer].reshape(1, DT_COLS), ((0, 0), (0, LANES - DT_COLS)))
    dt = _matmul(h, w_dt, F32, tm, LANES, "dt_proj", _softplus, dt_b)
    gates = _matmul(h, w_in_b[:, OFF_GATE:], BF16, tm, 1024, "gate_proj", _sigmoid)

    k3 = k.reshape(n_tok // tk, tk, ATTN_WIDTH)
    lams = [v[layer].reshape(1, QK_DIM) for v in (lambda_q1, lambda_k1, lambda_q2, lambda_k2)]
    o_groups = []
    off = 0
    for b, s in seqs:
        o_groups.append(_attention(q, k3, vt3, lams, off, b, s, tq, tk))
        off += b * s

    seq_starts = []
    off = 0
    for b, s in seqs:
        seq_starts += [(off + i * s, s) for i in range(b)]
        off += b * s
    tile_start = jnp.zeros((n_tok // tp,), jnp.int32)
    tile_end = jnp.zeros((n_tok // tp,), jnp.int32)
    nc = n_tok // CHUNK
    chunk_reset = jnp.zeros((nc,), jnp.int32)
    bwd_idx = jnp.zeros((nc,), jnp.int32)
    for st, s in seq_starts:
        tile_start = tile_start.at[st // tp].set(1)
        tile_end = tile_end.at[(st + s) // tp - 1].set(1)
        c0, c1 = st // CHUNK, (st + s) // CHUNK
        chunk_reset = chunk_reset.at[c0].set(1)
        bwd_idx = bwd_idx.at[c0:c1].set(jnp.arange(c1 - 1, c0 - 1, -1, dtype=jnp.int32))
    fwd_idx = jnp.arange(nc, dtype=jnp.int32)
    xs, bcm = _conv(xbc, conv_w[layer], conv_b[layer], tile_start, tile_end, tp)
    dtt = dt.T
    alg = a_log[layer].reshape(-1)
    yf = _ssd(xs, bcm, dt, dtt, alg, fwd_idx, chunk_reset, False)
    yb = _ssd(xs, bcm, dt, dtt, alg, bwd_idx, chunk_reset, True)

    subg = attn_subln_g[layer].reshape(1, V_DIM)
    dskip = jnp.repeat(d_skip[layer], SSD_HEAD_DIM).reshape(1, D_INNER)
    x1, h2b, h2f = _post(o_groups[0], o_groups[1], yf, yb, xs, z_act, gates, x, mod_tp, subg, dskip,
                         ssd_norm_g[layer].reshape(1, D_INNER), norm2_g[layer].reshape(1, D_MODEL),
                         w_attn_o[layer].astype(BF16), w_ssd_o[layer].astype(BF16),
                         w_out[layer].astype(BF16), tp)

    n_rt = N_EXPERT_GROUPS + N_EXPERTS
    w_rt = jnp.pad(jnp.concatenate([w_group[layer], w_router[layer]], axis=1), ((0, 0), (0, LANES - n_rt)))
    b_rt = jnp.pad(jnp.concatenate([b_group[layer], b_router[layer]]), (0, LANES - n_rt)).reshape(1, LANES)
    rt = _router(h2f, w_rt, b_rt, tp)
    dest, tok_buf, blk_e, n_used = _dispatch(rt, n_tok, tp)
    xb = h2b[tok_buf]
    yb_e = _experts(xb, blk_e, n_used, w_gate_e[layer], w_up_e[layer], w_down_e[layer])
    g0 = yb_e[dest[:, 0]]
    g1 = yb_e[dest[:, 1]]

    outs = []
    off = 0
    for (b, s), (xg, _) in zip(seqs, groups):
        y = _final(x1, mod_tp, rt, g0, g1, off, b * s, tp)
        outs.append(y.reshape(xg.shape))
        off += b * s
    return tuple(outs)
```

```python
import functools
import math

import jax
import jax.numpy as jnp
from jax import lax
from jax.experimental import pallas as pl
from jax.experimental.pallas import tpu as pltpu

F32 = jnp.float32
BF16 = jnp.bfloat16

D_MODEL = 1024
ATTN_HEADS = 8
QK_DIM = 64
V_DIM = 2 * QK_DIM
ATTN_WIDTH = ATTN_HEADS * V_DIM
ROPE_THETA = 10000.0
D_INNER = 2048
SSD_HEAD_DIM = 64
SSD_HEADS = D_INNER // SSD_HEAD_DIM
SSD_GROUPS = 4
HEADS_PER_GROUP = SSD_HEADS // SSD_GROUPS
D_STATE = 128
CONV_K = 5
CONV_CH = D_INNER + 2 * SSD_GROUPS * D_STATE
CHUNK = 128
N_EXPERT_GROUPS = 4
EXPERTS_PER_GROUP = 8
N_EXPERTS = N_EXPERT_GROUPS * EXPERTS_PER_GROUP
TOP_K_INNER = 2
EXPERT_FF = 512
RMS_EPS = 1e-6
LAMBDA_INIT = 0.8 - 0.6 * math.exp(-0.3 * 0)

LANES = 128
SUBLANES = 8
VMEM_LIMIT = 56 * 1024 * 1024

Q_COLS = ATTN_HEADS * 2 * QK_DIM
K_COLS = Q_COLS
V_COLS = ATTN_WIDTH
Z_COLS = D_INNER
XBC_COLS = CONV_CH
DT_COLS = 2 * SSD_HEADS
GATE_COLS = 2 * D_MODEL
OFF_Q = 0
OFF_K = OFF_Q + Q_COLS
OFF_V = OFF_K + K_COLS
OFF_Z = OFF_V + V_COLS
OFF_XBC = OFF_Z + Z_COLS
OFF_DT = OFF_XBC + XBC_COLS
OFF_GATE = OFF_DT + DT_COLS

EXPERT_BLOCK = 256
ONES_ROWS = 16
LOG2_E = math.log2(math.e)


def _params(sem):
    return pltpu.CompilerParams(dimension_semantics=sem, vmem_limit_bytes=VMEM_LIMIT)


def _dot(a, b):
    return jnp.dot(a, b, preferred_element_type=F32)


def _dot_tn(a, b):
    return lax.dot_general(a, b, (((0,), (0,)), ((), ())), preferred_element_type=F32)


def _dot_nt(a, b):
    return lax.dot_general(a, b, (((1,), (1,)), ((), ())), preferred_element_type=F32)


def _split3(a):
    hi = a.astype(BF16)
    r = a - hi.astype(F32)
    mid = r.astype(BF16)
    lo = (r - mid.astype(F32)).astype(BF16)
    return hi, mid, lo


def _dot_left01(m01, a):
    hi, mid, lo = _split3(a)
    return _dot(m01, hi) + _dot(m01, mid) + _dot(m01, lo)


def _dot_right01(a, m01):
    hi, mid, lo = _split3(a)
    return _dot(hi, m01) + _dot(mid, m01) + _dot(lo, m01)


def _spread01(a, m01):
    hi = a.astype(BF16)
    lo = (a - hi.astype(F32)).astype(BF16)
    return _dot(hi, m01) + _dot(lo, m01)


def _dot_f32(a, b):
    a0, a1, a2 = _split3(a)
    b0, b1, b2 = _split3(b)
    return (_dot(a0, b0) + (_dot(a0, b1) + _dot(a1, b0))
            + (_dot(a0, b2) + _dot(a2, b0) + _dot(a1, b1)))


def _dot_f32_3(a, b):
    a0 = a.astype(BF16)
    a1 = (a - a0.astype(F32)).astype(BF16)
    b0 = b.astype(BF16)
    b1 = (b - b0.astype(F32)).astype(BF16)
    return _dot(a0, b0) + (_dot(a0, b1) + _dot(a1, b0))


def _sigmoid(x):
    return 1.0 / (1.0 + jnp.exp(-x))


def _silu(x):
    return x * _sigmoid(x)


def _softplus(x):
    e = jnp.exp(-jnp.abs(x))
    u = 1.0 + e
    log1p_e = jnp.where(u == 1.0, e, jnp.log(u) * (e / (u - 1.0)))
    return jnp.maximum(x, 0.0) + log1p_e


def _ada_kernel(c_ref, w_ref, b_ref, o_ref):
    o_ref[...] = _dot_f32(_silu(c_ref[...]), w_ref[...]) + b_ref[...]


def _ada(c_pad, w_ada, b_ada):
    rows = c_pad.shape[0]
    n = w_ada.shape[1]
    tn = 1024
    return pl.pallas_call(
        _ada_kernel,
        grid=(n // tn,),
        in_specs=[pl.BlockSpec((rows, D_MODEL), lambda j: (0, 0)),
                  pl.BlockSpec((D_MODEL, tn), lambda j: (0, j)),
                  pl.BlockSpec((1, tn), lambda j: (0, j))],
        out_specs=pl.BlockSpec((rows, tn), lambda j: (0, j)),
        out_shape=jax.ShapeDtypeStruct((rows, n), F32),
        compiler_params=_params(("arbitrary",)),
        name="ada_mod",
    )(c_pad, w_ada, b_ada.reshape(1, n))


def _norm_mod_kernel(x_ref, mod_ref, g_ref, o_ref, *, shift_row, scale_row):
    x = x_ref[...]
    r = lax.rsqrt(jnp.mean(x * x, axis=-1, keepdims=True) + RMS_EPS)
    m = mod_ref[0]
    h = ((x * r) * g_ref[...]) * (1.0 + m[scale_row:scale_row + 1]) + m[shift_row:shift_row + 1]
    o_ref[...] = h.astype(o_ref.dtype)


def _norm_mod(x, mod_tiles, g, tm):
    t = x.shape[0]
    return pl.pallas_call(
        functools.partial(_norm_mod_kernel, shift_row=0, scale_row=1),
        grid=(t // tm,),
        in_specs=[pl.BlockSpec((tm, D_MODEL), lambda i: (i, 0)),
                  pl.BlockSpec((1, 6, D_MODEL), lambda i: (i, 0, 0)),
                  pl.BlockSpec((1, D_MODEL), lambda i: (0, 0))],
        out_specs=pl.BlockSpec((tm, D_MODEL), lambda i: (i, 0)),
        out_shape=jax.ShapeDtypeStruct((t, D_MODEL), BF16),
        compiler_params=_params(("parallel",)),
        name="norm1_mod",
    )(x, mod_tiles, g.reshape(1, D_MODEL))


def _mm_kernel(a_ref, w_ref, *rest, act):
    o_ref = rest[-1]
    acc = _dot(a_ref[...], w_ref[...])
    if len(rest) == 2:
        acc = acc + rest[0][...]
    if act is not None:
        acc = act(acc)
    o_ref[...] = acc.astype(o_ref.dtype)


def _matmul(a, w, out_dtype, tm, tn, name, act=None, bias=None):
    m, k = a.shape
    n = w.shape[1]
    in_specs = [pl.BlockSpec((tm, k), lambda j, i: (i, 0)),
                pl.BlockSpec((k, tn), lambda j, i: (0, j))]
    args = [a, w]
    if bias is not None:
        in_specs.append(pl.BlockSpec((1, tn), lambda j, i: (0, j)))
        args.append(bias)
    return pl.pallas_call(
        functools.partial(_mm_kernel, act=act),
        grid=(n // tn, m // tm),
        in_specs=in_specs,
        out_specs=pl.BlockSpec((tm, tn), lambda j, i: (i, j)),
        out_shape=jax.ShapeDtypeStruct((m, n), out_dtype),
        compiler_params=_params(("parallel", "parallel")),
        name=name,
    )(*args)


def _qk_kernel(a_ref, w_ref, g_ref, cos_ref, sin_ref, bd_ref, o_ref):
    acc = _dot(a_ref[...], w_ref[...])
    tm = acc.shape[0]
    lane = lax.broadcasted_iota(jnp.int32, (tm, LANES), 1)
    first = (lane & (QK_DIM // 2)) == 0
    cos = cos_ref[...]
    sin = sin_ref[...]
    g = g_ref[...]
    bd = bd_ref[...]
    for h in range(ATTN_HEADS):
        x = acc[:, h * LANES:(h + 1) * LANES]
        sq = x * x
        hi = sq.astype(BF16)
        lo = (sq - hi.astype(F32)).astype(BF16)
        ss = _dot(hi, bd) + _dot(lo, bd)
        r = lax.rsqrt(ss * (1.0 / QK_DIM) + RMS_EPS)
        xn = (x * r) * g
        partner = jnp.where(first, pltpu.roll(xn, LANES - QK_DIM // 2, 1),
                            pltpu.roll(xn, QK_DIM // 2, 1))
        o_ref[:, h * LANES:(h + 1) * LANES] = (xn * cos + partner * sin).astype(o_ref.dtype)


def _qk_proj(h, w, g128, cos_t, sin_t, bd, tm, name):
    t = h.shape[0]
    n = w.shape[1]
    return pl.pallas_call(
        _qk_kernel,
        grid=(t // tm,),
        in_specs=[pl.BlockSpec((tm, D_MODEL), lambda i: (i, 0)),
                  pl.BlockSpec((D_MODEL, n), lambda i: (0, 0)),
                  pl.BlockSpec((1, LANES), lambda i: (0, 0)),
                  pl.BlockSpec((tm, LANES), lambda i: (i, 0)),
                  pl.BlockSpec((tm, LANES), lambda i: (i, 0)),
                  pl.BlockSpec((LANES, LANES), lambda i: (0, 0))],
        out_specs=pl.BlockSpec((tm, n), lambda i: (i, 0)),
        out_shape=jax.ShapeDtypeStruct((t, n), BF16),
        compiler_params=_params(("parallel",)),
        name=name,
    )(h, w, g128, cos_t, sin_t, bd)


def _vt_kernel(a_ref, w_ref, o_ref, *, tk):
    acc = _dot(a_ref[...], w_ref[...])
    for c in range(acc.shape[0] // tk):
        o_ref[c] = acc[c * tk:(c + 1) * tk, :].T.astype(o_ref.dtype)


def _vt_proj(h, w, tm, tk):
    t = h.shape[0]
    n = w.shape[1]
    return pl.pallas_call(
        functools.partial(_vt_kernel, tk=tk),
        grid=(t // tm,),
        in_specs=[pl.BlockSpec((tm, D_MODEL), lambda i: (i, 0)),
                  pl.BlockSpec((D_MODEL, n), lambda i: (0, 0))],
        out_specs=pl.BlockSpec((tm // tk, n, tk), lambda i: (i, 0, 0)),
        out_shape=jax.ShapeDtypeStruct((t // tk, n, tk), BF16),
        compiler_params=_params(("parallel",)),
        name="v_proj_t",
    )(h, w)


def _attn_kernel(lq1_ref, lk1_ref, lq2_ref, lk2_ref, q_ref, k_ref, vt_ref, o_ref, q2t_s, s_s, m_s, acc_s,
                 *, tq, nkv, unroll):
    tk = k_ref.shape[1]
    qt = q_ref[...].astype(F32).T
    row = lax.broadcasted_iota(jnp.int32, qt.shape, 0)
    q2t_s[:, :tq] = jnp.where(row < QK_DIM, qt, 0.0).astype(BF16)
    q2t_s[:, tq:] = jnp.where(row >= QK_DIM, qt, 0.0).astype(BF16)
    m_s[...] = jnp.full(m_s.shape, -jnp.inf, F32)
    acc_s[...] = jnp.zeros(acc_s.shape, F32)
    ones = jnp.ones((ONES_ROWS, tk), BF16)
    s_s[0] = _dot(k_ref[0], q2t_s[...])

    def step(j, par, compute_next):
        if compute_next:
            s_s[1 - par] = _dot(k_ref[j + 1], q2t_s[...])
        st = s_s[par]
        m_old = m_s[...]
        m_new = jnp.maximum(m_old, jnp.max(st, axis=0, keepdims=True))
        alpha = jnp.exp2(m_old - m_new)
        p = jnp.exp2(st - m_new).astype(BF16)
        lhs = jnp.concatenate([vt_ref[j], ones], axis=0)
        acc_s[...] = alpha * acc_s[...] + _dot(lhs, p)
        m_s[...] = m_new

    n_trips = (nkv - 1) // unroll

    def trip(t, carry):
        for u in range(unroll):
            step(unroll * t + u, u % 2, True)
        return carry

    lax.fori_loop(0, n_trips, trip, 0)
    for j in range(unroll * n_trips, nkv):
        step(j, j % 2, j + 1 < nkv)

    lam = (jnp.exp(jnp.sum(lq1_ref[...] * lk1_ref[...], axis=1, keepdims=True))
           - jnp.exp(jnp.sum(lq2_ref[...] * lk2_ref[...], axis=1, keepdims=True)) + LAMBDA_INIT)
    ot = acc_s[:V_DIM, :] / acc_s[V_DIM:V_DIM + 1, :]
    o_ref[...] = (ot[:, :tq] - lam * ot[:, tq:]).T


def _attention(q, k3, vt3, lams, tok_off, batch, seq, tq, tk):
    nkv = seq // tk
    qb0 = tok_off // tq
    sb0 = tok_off // seq
    nq = seq // tq
    unroll = 4 if nkv >= 16 else 2
    lam_spec = pl.BlockSpec((1, QK_DIM), lambda b, h, i: (0, 0))
    return pl.pallas_call(
        functools.partial(_attn_kernel, tq=tq, nkv=nkv, unroll=unroll),
        grid=(batch, ATTN_HEADS, nq),
        in_specs=[lam_spec, lam_spec, lam_spec, lam_spec,
                  pl.BlockSpec((tq, LANES), lambda b, h, i: (qb0 + b * nq + i, h)),
                  pl.BlockSpec((nkv, tk, LANES), lambda b, h, i: (sb0 + b, 0, h)),
                  pl.BlockSpec((nkv, LANES, tk), lambda b, h, i: (sb0 + b, h, 0))],
        out_specs=pl.BlockSpec((tq, LANES), lambda b, h, i: (b * nq + i, h)),
        out_shape=jax.ShapeDtypeStruct((batch * seq, ATTN_WIDTH), F32),
        scratch_shapes=[pltpu.VMEM((LANES, 2 * tq), BF16),
                        pltpu.VMEM((2, tk, 2 * tq), F32),
                        pltpu.VMEM((1, 2 * tq), F32),
                        pltpu.VMEM((V_DIM + ONES_ROWS, 2 * tq), F32)],
        compiler_params=_params(("parallel", "parallel", "arbitrary")),
        name="diff_attention",
    )(*lams, q, k3, vt3)


def _conv_kernel(start_ref, end_ref, prev_ref, cur_ref, next_ref, w_ref, b_ref, xs_ref, bc_ref, ext_s, *, tm):
    i = pl.program_id(0)
    halo = SUBLANES
    ext_s[0:halo, :] = jnp.where(start_ref[i] == 1, 0.0, prev_ref[...])
    ext_s[halo:halo + tm, :] = cur_ref[...]
    ext_s[halo + tm:2 * halo + tm, :] = jnp.where(end_ref[i] == 1, 0.0, next_ref[...])
    pad = (CONV_K - 1) // 2
    ext = ext_s[...]
    acc = jnp.broadcast_to(b_ref[...], (tm, CONV_CH))
    for d in range(CONV_K):
        shift = (pad - d) % (tm + 2 * halo)
        src = ext if shift == 0 else pltpu.roll(ext, shift, 0)
        acc = acc + w_ref[d:d + 1, :] * src[halo:halo + tm, :]
    y = _silu(acc)
    xs_ref[...] = y[:, :D_INNER]
    bc_ref[...] = y[:, D_INNER:].astype(bc_ref.dtype)


def _conv(xbc, conv_w, conv_b, start_flags, end_flags, tm):
    t = xbc.shape[0]
    rb = tm // SUBLANES
    last = t // SUBLANES - 1
    w_pad = jnp.zeros((SUBLANES, CONV_CH), F32).at[:CONV_K].set(conv_w)
    grid_spec = pltpu.PrefetchScalarGridSpec(
        num_scalar_prefetch=2,
        grid=(t // tm,),
        in_specs=[pl.BlockSpec((SUBLANES, CONV_CH), lambda i, s, e: (jnp.maximum(i * rb - 1, 0), 0)),
                  pl.BlockSpec((tm, CONV_CH), lambda i, s, e: (i, 0)),
                  pl.BlockSpec((SUBLANES, CONV_CH), lambda i, s, e: (jnp.minimum((i + 1) * rb, last), 0)),
                  pl.BlockSpec((SUBLANES, CONV_CH), lambda i, s, e: (0, 0)),
                  pl.BlockSpec((1, CONV_CH), lambda i, s, e: (0, 0))],
        out_specs=[pl.BlockSpec((tm, D_INNER), lambda i, s, e: (i, 0)),
                   pl.BlockSpec((tm, CONV_CH - D_INNER), lambda i, s, e: (i, 0))],
        scratch_shapes=[pltpu.VMEM((tm + 2 * SUBLANES, CONV_CH), F32)],
    )
    return pl.pallas_call(
        functools.partial(_conv_kernel, tm=tm),
        grid_spec=grid_spec,
        out_shape=[jax.ShapeDtypeStruct((t, D_INNER), F32),
                   jax.ShapeDtypeStruct((t, CONV_CH - D_INNER), BF16)],
        compiler_params=_params(("arbitrary",)),
        name="conv_silu",
    )(start_flags, end_flags, xbc, xbc, xbc, w_pad, conv_b.reshape(1, CONV_CH))


def _ssd_kernel(idx_ref, reset_ref, xs_ref, bc_ref, dt_ref, dtt_ref,
                alog_r_ref, alog_c_ref, tril_ref, triu_ref, e01_ref, y_ref, state_s, *, rev):
    i = pl.program_id(0)
    nh = SSD_HEADS

    @pl.when(reset_ref[i] == 1)
    def _():
        state_s[...] = jnp.zeros(state_s.shape, F32)

    xs = xs_ref[...]
    bc = bc_ref[...]
    dtn = dt_ref[...]
    a = dtn * (-jnp.exp(alog_r_ref[...]))
    pinc = _dot_left01(tril_ref[...], a)
    pex = pinc - a
    tot = pinc[CHUNK - 1:CHUNK, :]

    gw = HEADS_PER_GROUP * SSD_HEAD_DIM
    gn = SSD_GROUPS * D_STATE
    cd = jnp.broadcast_to(jnp.exp(tot), (SUBLANES, LANES))

    def update_state(sc_st, sc_cd):
        w = (xs * sc_st).astype(BF16)
        for g in range(SSD_GROUPS):
            bg = bc[:, g * D_STATE:(g + 1) * D_STATE]
            new = _dot_tn(bg, w[:, g * gw:(g + 1) * gw])
            state_s[:, g * gw:(g + 1) * gw] = (state_s[:, g * gw:(g + 1) * gw] * sc_cd[:, g * gw:(g + 1) * gw]
                                               + new)

    if rev:
        stacked = jnp.concatenate([jnp.exp(pex) * dtn, jnp.exp(tot - pex), cd], axis=0)
        ex = _spread01(stacked, e01_ref[...])
        sc_off = ex[CHUNK:2 * CHUNK]
        for g in range(SSD_GROUPS):
            cg = bc[:, gn + g * D_STATE:gn + (g + 1) * D_STATE]
            st = state_s[:, g * gw:(g + 1) * gw]
            y_ref[:, g * gw:(g + 1) * gw] = _dot(cg, st.astype(BF16)) * sc_off[:, g * gw:(g + 1) * gw]
        update_state(ex[:CHUNK], ex[2 * CHUNK:2 * CHUNK + 1])
        return

    dtnt = dtt_ref[...]
    at = dtnt * (-jnp.exp(alog_c_ref[...]))
    pinct = _dot_right01(at, triu_ref[...])
    pext = pinct - at
    li = lax.broadcasted_iota(jnp.int32, (CHUNK, CHUNK), 0)
    si = lax.broadcasted_iota(jnp.int32, (CHUNK, CHUNK), 1)
    lower = si <= li
    strict_lower = si < li
    strict_upper = si > li
    lane = lax.broadcasted_iota(jnp.int32, (CHUNK, LANES), 1)
    first_head = lane < SSD_HEAD_DIM
    xb = xs.astype(BF16)
    for g in range(SSD_GROUPS):
        bg = bc[:, g * D_STATE:(g + 1) * D_STATE]
        cg = bc[:, gn + g * D_STATE:gn + (g + 1) * D_STATE]
        cg_f = cg.astype(F32)
        cb = _dot_nt(cg, bg)
        for hp in range(HEADS_PER_GROUP // 2):
            col = g * gw + hp * LANES
            rhs = jnp.concatenate([xb[:, col:col + LANES], state_s[:, col:col + LANES].astype(BF16)], axis=0)
            pair = []
            for u in range(2):
                h = g * HEADS_PER_GROUP + 2 * hp + u
                colf = jnp.broadcast_to(pinc[:, h:h + 1], (CHUNK, CHUNK))
                arg = jnp.where(lower, colf - pinct[h:h + 1, :],
                                pext[nh + h:nh + h + 1, :] - pex[:, nh + h:nh + h + 1])
                dtf = dtnt[h:h + 1, :]
                dtb = dtnt[nh + h:nh + h + 1, :]
                coef = jnp.where(strict_lower, dtf, jnp.where(strict_upper, dtb, dtf + dtb))
                mh = (cb * jnp.exp(arg) * coef).astype(BF16)
                dh = (cg_f * jnp.exp(colf)).astype(BF16)
                pair.append(_dot(jnp.concatenate([mh, dh], axis=1), rhs))
            y_ref[:, col:col + LANES] = jnp.where(first_head, pair[0], pair[1])

    stacked = jnp.concatenate([jnp.exp(tot - pinc) * dtn, cd], axis=0)
    ex = _spread01(stacked, e01_ref[...])
    update_state(ex[:CHUNK], ex[CHUNK:CHUNK + 1])


def _ssd(xs, bc, dt, dtt, a_log, idx, reset, rev):
    t = xs.shape[0]
    nc = t // CHUNK
    alog_r = jnp.pad(a_log.reshape(1, DT_COLS), ((0, 0), (0, LANES - DT_COLS)))
    alog_c = alog_r.reshape(LANES, 1)
    r = jnp.arange(CHUNK)
    tril = (r[None, :] <= r[:, None]).astype(BF16)
    triu = (r[:, None] <= r[None, :]).astype(BF16)
    lo = SSD_HEADS if rev else 0
    e01 = (jnp.arange(D_INNER)[None, :] // SSD_HEAD_DIM == jnp.arange(LANES)[:, None] - lo).astype(BF16)
    const = lambda shape: pl.BlockSpec(shape, lambda i, ix, rs: (0, 0))
    grid_spec = pltpu.PrefetchScalarGridSpec(
        num_scalar_prefetch=2,
        grid=(nc,),
        in_specs=[pl.BlockSpec((CHUNK, D_INNER), lambda i, ix, rs: (ix[i], 0)),
                  pl.BlockSpec((CHUNK, CONV_CH - D_INNER), lambda i, ix, rs: (ix[i], 0)),
                  pl.BlockSpec((CHUNK, LANES), lambda i, ix, rs: (ix[i], 0)),
                  pl.BlockSpec((LANES, CHUNK), lambda i, ix, rs: (0, ix[i])),
                  const((1, LANES)), const((LANES, 1)),
                  const((CHUNK, CHUNK)), const((CHUNK, CHUNK)), const((LANES, D_INNER))],
        out_specs=pl.BlockSpec((CHUNK, D_INNER), lambda i, ix, rs: (ix[i], 0)),
        scratch_shapes=[pltpu.VMEM((D_STATE, D_INNER), F32)],
    )
    return pl.pallas_call(
        functools.partial(_ssd_kernel, rev=rev),
        grid_spec=grid_spec,
        out_shape=jax.ShapeDtypeStruct((t, D_INNER), F32),
        compiler_params=_params(("arbitrary",)),
        name="ssd_bwd" if rev else "ssd_fwd",
    )(idx, reset, xs, bc, dt, dtt, alog_r, alog_c, tril, triu, e01)


def _post_kernel(oa_ref, ob_ref, yf_ref, yb_ref, xs_ref, z_ref, gt_ref, x_ref, mod_ref, subg_ref, dskip_ref,
                 ssdg_ref, n2g_ref, wa_ref, ws_ref, wo_ref, x1_ref, h2b_ref, h2f_ref, an_s, yn_s, *, n_first):
    m = mod_ref[0]
    o = jnp.where(pl.program_id(0) < n_first, oa_ref[...], ob_ref[...])
    for h in range(ATTN_HEADS):
        oh = o[:, h * V_DIM:(h + 1) * V_DIM]
        r = lax.rsqrt(jnp.mean(oh * oh, axis=-1, keepdims=True) + RMS_EPS)
        an_s[:, h * V_DIM:(h + 1) * V_DIM] = (((oh * r) * subg_ref[...]) * (1.0 - LAMBDA_INIT)).astype(BF16)
    attn_d = _dot(an_s[...], wa_ref[...])

    y = (yf_ref[...] + yb_ref[...]) + dskip_ref[...] * xs_ref[...]
    y = y * z_ref[...].astype(F32)
    gw = D_INNER // SSD_GROUPS
    for g in range(SSD_GROUPS):
        yg = y[:, g * gw:(g + 1) * gw]
        r = lax.rsqrt(jnp.mean(yg * yg, axis=-1, keepdims=True) + RMS_EPS)
        yn_s[:, g * gw:(g + 1) * gw] = ((yg * r) * ssdg_ref[:, g * gw:(g + 1) * gw]).astype(BF16)
    ssd_d = _dot(yn_s[...], ws_ref[...])

    gt = gt_ref[...].astype(F32)
    mix = gt[:, :D_MODEL] * attn_d + gt[:, D_MODEL:] * ssd_d
    mixed = _dot(mix.astype(BF16), wo_ref[...])
    x1 = x_ref[...] + m[2:3] * mixed
    x1_ref[...] = x1
    r = lax.rsqrt(jnp.mean(x1 * x1, axis=-1, keepdims=True) + RMS_EPS)
    h2 = ((x1 * r) * n2g_ref[...]) * (1.0 + m[4:5]) + m[3:4]
    h2f_ref[...] = h2
    h2b_ref[...] = h2.astype(BF16)


def _post(oa, ob, yf, yb, xs, z, gates, x, mod_tiles, subg, dskip, ssdg, n2g, wa, ws, wo, tm):
    t = x.shape[0]
    n_first = oa.shape[0] // tm
    tok = lambda n: pl.BlockSpec((tm, n), lambda i: (i, 0))
    const = lambda a: pl.BlockSpec(a.shape, lambda i: (0, 0))
    return pl.pallas_call(
        functools.partial(_post_kernel, n_first=n_first),
        grid=(t // tm,),
        in_specs=[pl.BlockSpec((tm, ATTN_WIDTH), lambda i: (jnp.minimum(i, n_first - 1), 0)),
                  pl.BlockSpec((tm, ATTN_WIDTH), lambda i: (jnp.maximum(i - n_first, 0), 0)),
                  tok(D_INNER), tok(D_INNER), tok(D_INNER), tok(D_INNER), tok(GATE_COLS),
                  tok(D_MODEL), pl.BlockSpec((1, 6, D_MODEL), lambda i: (i, 0, 0)),
                  const(subg), const(dskip), const(ssdg), const(n2g), const(wa), const(ws), const(wo)],
        out_specs=[tok(D_MODEL), tok(D_MODEL), tok(D_MODEL)],
        out_shape=[jax.ShapeDtypeStruct((t, D_MODEL), F32),
                   jax.ShapeDtypeStruct((t, D_MODEL), BF16),
                   jax.ShapeDtypeStruct((t, D_MODEL), F32)],
        scratch_shapes=[pltpu.VMEM((tm, ATTN_WIDTH), BF16), pltpu.VMEM((tm, D_INNER), BF16)],
        compiler_params=_params(("parallel",)),
        name="merge_out_proj",
    )(oa, ob, yf, yb, xs, z, gates, x, mod_tiles, subg, dskip, ssdg, n2g, wa, ws, wo)


def _router_kernel(h_ref, w_ref, b_ref, o_ref):
    logits = _dot_f32_3(h_ref[...], w_ref[...]) + b_ref[...]
    tm = logits.shape[0]
    lane = lax.broadcasted_iota(jnp.int32, (tm, LANES), 1)
    lane_f = lane.astype(F32)
    big = float(LANES)
    neg = -jnp.inf
    gl = jnp.where(lane < N_EXPERT_GROUPS, logits, neg)
    gmax = jnp.max(gl, axis=1, keepdims=True)
    g_sel = jnp.min(jnp.where(gl == gmax, lane_f, big), axis=1, keepdims=True)
    g_w = 1.0 / jnp.sum(jnp.exp(gl - gmax), axis=1, keepdims=True)
    e_lane = lane - N_EXPERT_GROUPS
    e_group = (e_lane >> 3).astype(F32)
    in_group = (e_lane >= 0) & (e_lane < N_EXPERTS) & (e_group == g_sel)
    el = jnp.where(in_group, logits, neg)
    v1 = jnp.max(el, axis=1, keepdims=True)
    i1 = jnp.min(jnp.where(el == v1, lane_f, big), axis=1, keepdims=True)
    el2 = jnp.where(lane_f == i1, neg, el)
    v2 = jnp.max(el2, axis=1, keepdims=True)
    i2 = jnp.min(jnp.where(el2 == v2, lane_f, big), axis=1, keepdims=True)
    e2 = jnp.exp(v2 - v1)
    w1 = g_w / (1.0 + e2)
    w2 = g_w * e2 / (1.0 + e2)
    out = jnp.where(lane == 0, i1 - N_EXPERT_GROUPS,
                    jnp.where(lane == 1, i2 - N_EXPERT_GROUPS,
                              jnp.where(lane == 2, w1, jnp.where(lane == 3, w2, 0.0))))
    o_ref[...] = out


def _router(h2f, w_rt, b_rt, tm):
    t = h2f.shape[0]
    return pl.pallas_call(
        _router_kernel,
        grid=(t // tm,),
        in_specs=[pl.BlockSpec((tm, D_MODEL), lambda i: (i, 0)),
                  pl.BlockSpec((D_MODEL, LANES), lambda i: (0, 0)),
                  pl.BlockSpec((1, LANES), lambda i: (0, 0))],
        out_specs=pl.BlockSpec((tm, LANES), lambda i: (i, 0)),
        out_shape=jax.ShapeDtypeStruct((t, LANES), F32),
        compiler_params=_params(("parallel",)),
        name="router_topk",
    )(h2f, w_rt, b_rt)


def _expert_kernel(be_ref, nu_ref, x_ref, wg_ref, wu_ref, wd_ref, o_ref, wg_s, wu_s, wd_s):
    i = pl.program_id(0)
    used = i < nu_ref[0]
    new_expert = (i == 0) | (be_ref[i] != be_ref[jnp.maximum(i - 1, 0)])

    @pl.when(used & new_expert)
    def _():
        wg_s[...] = wg_ref[0].astype(BF16)
        wu_s[...] = wu_ref[0].astype(BF16)
        wd_s[...] = wd_ref[0].astype(BF16)

    @pl.when(used)
    def _():
        x = x_ref[...]
        a = _silu(_dot(x, wg_s[...])) * _dot(x, wu_s[...])
        o_ref[...] = _dot(a.astype(BF16), wd_s[...]).astype(o_ref.dtype)

    @pl.when(i >= nu_ref[0])
    def _():
        o_ref[...] = jnp.zeros(o_ref.shape, o_ref.dtype)


def _experts(xb, blk_e, n_used, wg, wu, wd):
    cap = xb.shape[0]
    nb = cap // EXPERT_BLOCK
    grid_spec = pltpu.PrefetchScalarGridSpec(
        num_scalar_prefetch=2,
        grid=(nb,),
        in_specs=[pl.BlockSpec((EXPERT_BLOCK, D_MODEL), lambda i, be, nu: (jnp.minimum(i, nu[0] - 1), 0)),
                  pl.BlockSpec((1, D_MODEL, EXPERT_FF), lambda i, be, nu: (be[i], 0, 0)),
                  pl.BlockSpec((1, D_MODEL, EXPERT_FF), lambda i, be, nu: (be[i], 0, 0)),
                  pl.BlockSpec((1, EXPERT_FF, D_MODEL), lambda i, be, nu: (be[i], 0, 0))],
        out_specs=pl.BlockSpec((EXPERT_BLOCK, D_MODEL), lambda i, be, nu: (i, 0)),
        scratch_shapes=[pltpu.VMEM((D_MODEL, EXPERT_FF), BF16), pltpu.VMEM((D_MODEL, EXPERT_FF), BF16),
                        pltpu.VMEM((EXPERT_FF, D_MODEL), BF16)],
    )
    return pl.pallas_call(
        _expert_kernel,
        grid_spec=grid_spec,
        out_shape=jax.ShapeDtypeStruct((cap, D_MODEL), BF16),
        compiler_params=_params(("arbitrary",)),
        name="expert_mlp",
    )(blk_e, n_used, xb, wg, wu, wd)


def _final_kernel(x1_ref, mod_ref, rt_ref, g_ref, o_ref):
    m = mod_ref[0]
    rt = rt_ref[...]
    g = g_ref[...].astype(F32)
    moe = g[:, :D_MODEL] * rt[:, 2:3] + g[:, D_MODEL:] * rt[:, 3:4]
    o_ref[...] = x1_ref[...] + m[5:6] * moe


def _final(x1, mod_tiles, rt, g, tok_off, n_tok, tm):
    b0 = tok_off // tm
    tok = lambda n: pl.BlockSpec((tm, n), lambda i: (b0 + i, 0))
    return pl.pallas_call(
        _final_kernel,
        grid=(n_tok // tm,),
        in_specs=[tok(D_MODEL), pl.BlockSpec((1, 6, D_MODEL), lambda i: (b0 + i, 0, 0)), tok(LANES),
                  tok(TOP_K_INNER * D_MODEL)],
        out_specs=pl.BlockSpec((tm, D_MODEL), lambda i: (i, 0)),
        out_shape=jax.ShapeDtypeStruct((n_tok, D_MODEL), F32),
        compiler_params=_params(("parallel",)),
        name="moe_combine",
    )(x1, mod_tiles, rt, g)


def _rope_tables(seq):
    pos = jnp.arange(seq, dtype=F32)
    inv = 1.0 / (ROPE_THETA ** (jnp.arange(0, QK_DIM, 2, dtype=F32) / QK_DIM))
    ang = pos[:, None] * inv[None, :]
    cos, sin = jnp.cos(ang), jnp.sin(ang)
    cos_t = jnp.tile(cos, (1, LANES // (QK_DIM // 2)))
    sin_t = jnp.tile(jnp.concatenate([-sin, sin], axis=1), (1, LANES // QK_DIM))
    return cos_t, sin_t


def _dest_kernel(rt_ref, base_ref, tri_ref, o_ref, carry_s):
    @pl.when(pl.program_id(0) == 0)
    def _():
        carry_s[...] = jnp.zeros(carry_s.shape, F32)

    rt = rt_ref[...]
    tm = rt.shape[0]
    lane = lax.broadcasted_iota(jnp.int32, (tm, LANES), 1)
    lane_f = lane.astype(F32)
    oh0 = lane_f == rt[:, 0:1]
    oh1 = lane_f == rt[:, 1:2]
    both = jnp.where(oh0 | oh1, 1.0, 0.0)
    pos = base_ref[...] + carry_s[0:1, :] + _dot(tri_ref[...], both.astype(BF16))
    d0 = jnp.sum(jnp.where(oh0, pos, 0.0), axis=1, keepdims=True)
    d1 = jnp.sum(jnp.where(oh1, pos, 0.0), axis=1, keepdims=True)
    o_ref[...] = jnp.where(lane == 0, d0, jnp.where(lane == 1, d1, 0.0)).astype(jnp.int32)
    carry_s[...] = carry_s[...] + jnp.sum(both, axis=0, keepdims=True)


def _dispatch(rt, n_tok, tm):
    n_slots = n_tok * TOP_K_INNER
    flat_e = rt[:, :TOP_K_INNER].astype(jnp.int32).reshape(-1)
    counts = jnp.sum((flat_e[:, None] == jnp.arange(N_EXPERTS, dtype=jnp.int32)[None, :]).astype(jnp.int32), axis=0)
    padded = ((counts + EXPERT_BLOCK - 1) // EXPERT_BLOCK) * EXPERT_BLOCK
    pad_end = jnp.cumsum(padded)
    pad_start = pad_end - padded
    base = jnp.pad(pad_start.astype(F32), (0, LANES - N_EXPERTS)).reshape(1, LANES)
    r = jnp.arange(tm)
    tri = (r[None, :] < r[:, None]).astype(BF16)
    dest = pl.pallas_call(
        _dest_kernel,
        grid=(n_tok // tm,),
        in_specs=[pl.BlockSpec((tm, LANES), lambda i: (i, 0)),
                  pl.BlockSpec((1, LANES), lambda i: (0, 0)),
                  pl.BlockSpec((tm, tm), lambda i: (0, 0))],
        out_specs=pl.BlockSpec((tm, LANES), lambda i: (i, 0)),
        out_shape=jax.ShapeDtypeStruct((n_tok, LANES), jnp.int32),
        scratch_shapes=[pltpu.VMEM((SUBLANES, LANES), F32)],
        compiler_params=_params(("arbitrary",)),
        name="dispatch_rows",
    )(rt, base, tri)[:, :TOP_K_INNER]
    cap = n_slots + N_EXPERTS * EXPERT_BLOCK
    nb = cap // EXPERT_BLOCK
    blk_row0 = jnp.arange(nb, dtype=jnp.int32) * EXPERT_BLOCK
    blk_e = jnp.minimum(jnp.sum((pad_end[None, :] <= blk_row0[:, None]).astype(jnp.int32), axis=1), N_EXPERTS - 1)
    n_used = (pad_end[-1] // EXPERT_BLOCK).astype(jnp.int32).reshape(1)

    n_pad = cap - n_slots
    seg_len = jnp.concatenate([padded - counts, (cap - pad_end[-1]).reshape(1)])
    seg_end = jnp.cumsum(seg_len)
    seg_start = seg_end - seg_len
    seg_row0 = jnp.concatenate([pad_start + counts, pad_end[-1:]])
    j = jnp.arange(n_pad, dtype=jnp.int32)
    seg_onehot = (jnp.sum((seg_end[None, :] <= j[:, None]).astype(jnp.int32), axis=1)[:, None]
                  == jnp.arange(N_EXPERTS + 1, dtype=jnp.int32)[None, :]).astype(jnp.int32)
    pad_rows = j + jnp.sum(seg_onehot * (seg_row0 - seg_start)[None, :], axis=1)
    rows = jnp.concatenate([dest.reshape(-1), pad_rows.astype(jnp.int32)])
    toks = jnp.concatenate([jnp.arange(n_slots, dtype=jnp.int32) // TOP_K_INNER, j % n_tok])
    _, tok_buf = lax.sort_key_val(rows, toks)
    return dest, tok_buf, blk_e.astype(jnp.int32), n_used


def kernel(x_prompt, x_sample, c_prompt, c_sample, w_ada, b_ada, norm1_g, w_in, q_norm_g, k_norm_g, lambda_q1, lambda_k1, lambda_q2, lambda_k2, attn_subln_g, w_attn_o, conv_w, conv_b, dt_bias, a_log, d_skip, ssd_norm_g, w_ssd_o, w_out, norm2_g, w_group, b_group, w_router, b_router, w_gate_e, w_up_e, w_down_e):
    groups = [(x_prompt, c_prompt), (x_sample, c_sample)]
    seqs = [(x.shape[0], x.shape[1]) for x, _ in groups]
    n_tok = sum(b * s for b, s in seqs)
    min_seq = min(s for _, s in seqs)
    tm = min(1024, min_seq)
    tp = min(256, min_seq)
    tq = min(512, min_seq)
    tk = min(512, min_seq)
    layer = 0

    x = jnp.concatenate([g[0].reshape(-1, D_MODEL) for g in groups], axis=0)
    c = jnp.concatenate([g[1] for g in groups], axis=0)
    n_batch = c.shape[0]
    c_pad = jnp.pad(c, ((0, (-n_batch) % SUBLANES), (0, 0)))
    mod = _ada(c_pad, w_ada[layer], b_ada[layer]).reshape(-1, 6, D_MODEL)
    tok_batch = jnp.concatenate([jnp.repeat(jnp.arange(b, dtype=jnp.int32), s) + off
                                 for (b, s), off in zip(seqs, [0, seqs[0][0]])])
    mod_tm = mod[tok_batch[::tm]]
    mod_tp = mod[tok_batch[::tp]]

    h = _norm_mod(x, mod_tm, norm1_g[layer], tm)

    w_in_b = w_in[layer].astype(BF16)
    tabs = [_rope_tables(s) for _, s in seqs]
    cos_t = jnp.concatenate([jnp.tile(tb[0], (b, 1)) for tb, (b, _) in zip(tabs, seqs)], axis=0)
    sin_t = jnp.concatenate([jnp.tile(tb[1], (b, 1)) for tb, (b, _) in zip(tabs, seqs)], axis=0)
    half = jnp.arange(LANES) // QK_DIM
    bd = (half[:, None] == half[None, :]).astype(BF16)
    gq = (jnp.tile(q_norm_g[layer], LANES // QK_DIM) * (QK_DIM ** -0.5 * LOG2_E)).reshape(1, LANES)
    gk = jnp.tile(k_norm_g[layer], LANES // QK_DIM).reshape(1, LANES)
    q = _qk_proj(h, w_in_b[:, OFF_Q:OFF_K], gq, cos_t, sin_t, bd, tm, "q_proj")
    k = _qk_proj(h, w_in_b[:, OFF_K:OFF_V], gk, cos_t, sin_t, bd, tm, "k_proj")
    vt3 = _vt_proj(h, w_in_b[:, OFF_V:OFF_Z], tm, tk)
    z_act = _matmul(h, w_in_b[:, OFF_Z:OFF_XBC], BF16, tm, 1024, "z_proj", _silu)
    xbc = _matmul(h, w_in_b[:, OFF_XBC:OFF_DT], F32, tm, 1024, "xbc_proj")
    w_dt = jnp.pad(w_in_b[:, OFF_DT:OFF_GATE], ((0, 0), (0, LANES - DT_COLS)))
    dt_b = jnp.pad(dt_bias[layer].reshape(1, DT_COLS), ((0, 0), (0, LANES - DT_COLS)))
    dt = _matmul(h, w_dt, F32, tm, LANES, "dt_proj", _softplus, dt_b)
    gates = _matmul(h, w_in_b[:, OFF_GATE:], BF16, tm, 1024, "gate_proj", _sigmoid)

    k3 = k.reshape(n_tok // tk, tk, ATTN_WIDTH)
    lams = [v[layer].reshape(1, QK_DIM) for v in (lambda_q1, lambda_k1, lambda_q2, lambda_k2)]
    o_groups = []
    off = 0
    for b, s in seqs:
        o_groups.append(_attention(q, k3, vt3, lams, off, b, s, tq, tk))
        off += b * s

    seq_starts = []
    off = 0
    for b, s in seqs:
        seq_starts += [(off + i * s, s) for i in range(b)]
        off += b * s
    tile_start = jnp.zeros((n_tok // tp,), jnp.int32)
    tile_end = jnp.zeros((n_tok // tp,), jnp.int32)
    nc = n_tok // CHUNK
    chunk_reset = jnp.zeros((nc,), jnp.int32)
    bwd_idx = jnp.zeros((nc,), jnp.int32)
    for st, s in seq_starts:
        tile_start = tile_start.at[st // tp].set(1)
        tile_end = tile_end.at[(st + s) // tp - 1].set(1)
        c0, c1 = st // CHUNK, (st + s) // CHUNK
        chunk_reset = chunk_reset.at[c0].set(1)
        bwd_idx = bwd_idx.at[c0:c1].set(jnp.arange(c1 - 1, c0 - 1, -1, dtype=jnp.int32))
    fwd_idx = jnp.arange(nc, dtype=jnp.int32)
    xs, bcm = _conv(xbc, conv_w[layer], conv_b[layer], tile_start, tile_end, tp)
    dtt = dt.T
    alg = a_log[layer].reshape(-1)
    yf = _ssd(xs, bcm, dt, dtt, alg, fwd_idx, chunk_reset, False)
    yb = _ssd(xs, bcm, dt, dtt, alg, bwd_idx, chunk_reset, True)

    subg = attn_subln_g[layer].reshape(1, V_DIM)
    dskip = jnp.repeat(d_skip[layer], SSD_HEAD_DIM).reshape(1, D_INNER)
    x1, h2b, h2f = _post(o_groups[0], o_groups[1], yf, yb, xs, z_act, gates, x, mod_tp, subg, dskip,
                         ssd_norm_g[layer].reshape(1, D_INNER), norm2_g[layer].reshape(1, D_MODEL),
                         w_attn_o[layer].astype(BF16), w_ssd_o[layer].astype(BF16),
                         w_out[layer].astype(BF16), tp)

    n_rt = N_EXPERT_GROUPS + N_EXPERTS
    w_rt = jnp.pad(jnp.concatenate([w_group[layer], w_router[layer]], axis=1), ((0, 0), (0, LANES - n_rt)))
    b_rt = jnp.pad(jnp.concatenate([b_group[layer], b_router[layer]]), (0, LANES - n_rt)).reshape(1, LANES)
    rt = _router(h2f, w_rt, b_rt, tp)
    dest, tok_buf, blk_e, n_used = _dispatch(rt, n_tok, min(512, min_seq))
    xb = h2b[tok_buf]
    yb_e = _experts(xb, blk_e, n_used, w_gate_e[layer], w_up_e[layer], w_down_e[layer])
    g = yb_e[dest.reshape(-1)].reshape(n_tok, TOP_K_INNER * D_MODEL)

    outs = []
    off = 0
    for (b, s), (xg, _) in zip(seqs, groups):
        y = _final(x1, mod_tp, rt, g, off, b * s, tp)
        outs.append(y.reshape(xg.shape))
        off += b * s
    return tuple(outs)
```

```python
import functools
import math

import jax
import jax.numpy as jnp
from jax import lax
from jax.experimental import pallas as pl
from jax.experimental.pallas import tpu as pltpu

F32 = jnp.float32
BF16 = jnp.bfloat16

D_MODEL = 1024
ATTN_HEADS = 8
QK_DIM = 64
V_DIM = 2 * QK_DIM
ATTN_WIDTH = ATTN_HEADS * V_DIM
ROPE_THETA = 10000.0
D_INNER = 2048
SSD_HEAD_DIM = 64
SSD_HEADS = D_INNER // SSD_HEAD_DIM
SSD_GROUPS = 4
HEADS_PER_GROUP = SSD_HEADS // SSD_GROUPS
D_STATE = 128
CONV_K = 5
CONV_CH = D_INNER + 2 * SSD_GROUPS * D_STATE
CHUNK = 128
N_EXPERT_GROUPS = 4
EXPERTS_PER_GROUP = 8
N_EXPERTS = N_EXPERT_GROUPS * EXPERTS_PER_GROUP
TOP_K_INNER = 2
EXPERT_FF = 512
RMS_EPS = 1e-6
LAMBDA_INIT = 0.8 - 0.6 * math.exp(-0.3 * 0)

LANES = 128
SUBLANES = 8
VMEM_LIMIT = 56 * 1024 * 1024

Q_COLS = ATTN_HEADS * 2 * QK_DIM
K_COLS = Q_COLS
V_COLS = ATTN_WIDTH
Z_COLS = D_INNER
XBC_COLS = CONV_CH
DT_COLS = 2 * SSD_HEADS
GATE_COLS = 2 * D_MODEL
OFF_Q = 0
OFF_K = OFF_Q + Q_COLS
OFF_V = OFF_K + K_COLS
OFF_Z = OFF_V + V_COLS
OFF_XBC = OFF_Z + Z_COLS
OFF_DT = OFF_XBC + XBC_COLS
OFF_GATE = OFF_DT + DT_COLS

EXPERT_BLOCK = 256
ONES_ROWS = 16
LOG2_E = math.log2(math.e)


def _params(sem):
    return pltpu.CompilerParams(dimension_semantics=sem, vmem_limit_bytes=VMEM_LIMIT)


def _dot(a, b):
    return jnp.dot(a, b, preferred_element_type=F32)


def _dot_tn(a, b):
    return lax.dot_general(a, b, (((0,), (0,)), ((), ())), preferred_element_type=F32)


def _dot_nt(a, b):
    return lax.dot_general(a, b, (((1,), (1,)), ((), ())), preferred_element_type=F32)


def _split3(a):
    hi = a.astype(BF16)
    r = a - hi.astype(F32)
    mid = r.astype(BF16)
    lo = (r - mid.astype(F32)).astype(BF16)
    return hi, mid, lo


def _dot_left01(m01, a):
    hi, mid, lo = _split3(a)
    return _dot(m01, hi) + _dot(m01, mid) + _dot(m01, lo)


def _dot_right01(a, m01):
    hi, mid, lo = _split3(a)
    return _dot(hi, m01) + _dot(mid, m01) + _dot(lo, m01)


def _spread01(a, m01):
    hi = a.astype(BF16)
    lo = (a - hi.astype(F32)).astype(BF16)
    return _dot(hi, m01) + _dot(lo, m01)


def _dot_f32(a, b):
    a0, a1, a2 = _split3(a)
    b0, b1, b2 = _split3(b)
    return (_dot(a0, b0) + (_dot(a0, b1) + _dot(a1, b0))
            + (_dot(a0, b2) + _dot(a2, b0) + _dot(a1, b1)))


def _dot_f32_3(a, b):
    a0 = a.astype(BF16)
    a1 = (a - a0.astype(F32)).astype(BF16)
    b0 = b.astype(BF16)
    b1 = (b - b0.astype(F32)).astype(BF16)
    return _dot(a0, b0) + (_dot(a0, b1) + _dot(a1, b0))


def _sigmoid(x):
    return 1.0 / (1.0 + jnp.exp(-x))


def _silu(x):
    return x * _sigmoid(x)


def _softplus(x):
    e = jnp.exp(-jnp.abs(x))
    u = 1.0 + e
    log1p_e = jnp.where(u == 1.0, e, jnp.log(u) * (e / (u - 1.0)))
    return jnp.maximum(x, 0.0) + log1p_e


def _ada_kernel(c_ref, w_ref, b_ref, o_ref):
    o_ref[...] = _dot_f32(_silu(c_ref[...]), w_ref[...]) + b_ref[...]


def _ada(c_pad, w_ada, b_ada):
    rows = c_pad.shape[0]
    n = w_ada.shape[1]
    tn = 1024
    return pl.pallas_call(
        _ada_kernel,
        grid=(n // tn,),
        in_specs=[pl.BlockSpec((rows, D_MODEL), lambda j: (0, 0)),
                  pl.BlockSpec((D_MODEL, tn), lambda j: (0, j)),
                  pl.BlockSpec((1, tn), lambda j: (0, j))],
        out_specs=pl.BlockSpec((rows, tn), lambda j: (0, j)),
        out_shape=jax.ShapeDtypeStruct((rows, n), F32),
        compiler_params=_params(("arbitrary",)),
        name="ada_mod",
    )(c_pad, w_ada, b_ada.reshape(1, n))


def _group_specs(tm, n, n_first):
    return [pl.BlockSpec((tm, n), lambda i: (jnp.minimum(i, n_first - 1), 0)),
            pl.BlockSpec((tm, n), lambda i: (jnp.maximum(i - n_first, 0), 0))]


def _group_pick(a_ref, b_ref, n_first):
    return jnp.where(pl.program_id(0) < n_first, a_ref[...], b_ref[...])


def _norm_mod_kernel(xa_ref, xb_ref, mod_ref, g_ref, o_ref, *, shift_row, scale_row, n_first):
    x = _group_pick(xa_ref, xb_ref, n_first)
    r = lax.rsqrt(jnp.mean(x * x, axis=-1, keepdims=True) + RMS_EPS)
    m = mod_ref[0]
    h = ((x * r) * g_ref[...]) * (1.0 + m[scale_row:scale_row + 1]) + m[shift_row:shift_row + 1]
    o_ref[...] = h.astype(o_ref.dtype)


def _norm_mod(xa, xb, mod_tiles, g, tm):
    t = xa.shape[0] + xb.shape[0]
    n_first = xa.shape[0] // tm
    return pl.pallas_call(
        functools.partial(_norm_mod_kernel, shift_row=0, scale_row=1, n_first=n_first),
        grid=(t // tm,),
        in_specs=_group_specs(tm, D_MODEL, n_first) + [
            pl.BlockSpec((1, 6, D_MODEL), lambda i: (i, 0, 0)),
            pl.BlockSpec((1, D_MODEL), lambda i: (0, 0))],
        out_specs=pl.BlockSpec((tm, D_MODEL), lambda i: (i, 0)),
        out_shape=jax.ShapeDtypeStruct((t, D_MODEL), BF16),
        compiler_params=_params(("parallel",)),
        name="norm1_mod",
    )(xa, xb, mod_tiles, g.reshape(1, D_MODEL))


def _mm_kernel(a_ref, w_ref, *rest, act):
    o_ref = rest[-1]
    acc = _dot(a_ref[...], w_ref[...])
    if len(rest) == 2:
        acc = acc + rest[0][...]
    if act is not None:
        acc = act(acc)
    o_ref[...] = acc.astype(o_ref.dtype)


def _matmul(a, w, out_dtype, tm, tn, name, act=None, bias=None):
    m, k = a.shape
    n = w.shape[1]
    in_specs = [pl.BlockSpec((tm, k), lambda j, i: (i, 0)),
                pl.BlockSpec((k, tn), lambda j, i: (0, j))]
    args = [a, w]
    if bias is not None:
        in_specs.append(pl.BlockSpec((1, tn), lambda j, i: (0, j)))
        args.append(bias)
    return pl.pallas_call(
        functools.partial(_mm_kernel, act=act),
        grid=(n // tn, m // tm),
        in_specs=in_specs,
        out_specs=pl.BlockSpec((tm, tn), lambda j, i: (i, j)),
        out_shape=jax.ShapeDtypeStruct((m, n), out_dtype),
        compiler_params=_params(("parallel", "parallel")),
        name=name,
    )(*args)


def _qk_kernel(a_ref, w_ref, g_ref, cos_ref, sin_ref, bd_ref, o_ref):
    acc = _dot(a_ref[...], w_ref[...])
    tm = acc.shape[0]
    lane = lax.broadcasted_iota(jnp.int32, (tm, LANES), 1)
    first = (lane & (QK_DIM // 2)) == 0
    cos = cos_ref[...]
    sin = sin_ref[...]
    g = g_ref[...]
    bd = bd_ref[...]
    for h in range(ATTN_HEADS):
        x = acc[:, h * LANES:(h + 1) * LANES]
        sq = x * x
        hi = sq.astype(BF16)
        lo = (sq - hi.astype(F32)).astype(BF16)
        ss = _dot(hi, bd) + _dot(lo, bd)
        r = lax.rsqrt(ss * (1.0 / QK_DIM) + RMS_EPS)
        xn = (x * r) * g
        partner = jnp.where(first, pltpu.roll(xn, LANES - QK_DIM // 2, 1),
                            pltpu.roll(xn, QK_DIM // 2, 1))
        o_ref[:, h * LANES:(h + 1) * LANES] = (xn * cos + partner * sin).astype(o_ref.dtype)


def _qk_proj(h, w, g128, cos_t, sin_t, bd, tm, name):
    t = h.shape[0]
    n = w.shape[1]
    return pl.pallas_call(
        _qk_kernel,
        grid=(t // tm,),
        in_specs=[pl.BlockSpec((tm, D_MODEL), lambda i: (i, 0)),
                  pl.BlockSpec((D_MODEL, n), lambda i: (0, 0)),
                  pl.BlockSpec((1, LANES), lambda i: (0, 0)),
                  pl.BlockSpec((tm, LANES), lambda i: (i, 0)),
                  pl.BlockSpec((tm, LANES), lambda i: (i, 0)),
                  pl.BlockSpec((LANES, LANES), lambda i: (0, 0))],
        out_specs=pl.BlockSpec((tm, n), lambda i: (i, 0)),
        out_shape=jax.ShapeDtypeStruct((t, n), BF16),
        compiler_params=_params(("parallel",)),
        name=name,
    )(h, w, g128, cos_t, sin_t, bd)


def _vt_kernel(a_ref, w_ref, o_ref, *, tk):
    acc = _dot(a_ref[...], w_ref[...])
    for c in range(acc.shape[0] // tk):
        o_ref[c] = acc[c * tk:(c + 1) * tk, :].T.astype(o_ref.dtype)


def _vt_proj(h, w, tm, tk):
    t = h.shape[0]
    n = w.shape[1]
    return pl.pallas_call(
        functools.partial(_vt_kernel, tk=tk),
        grid=(t // tm,),
        in_specs=[pl.BlockSpec((tm, D_MODEL), lambda i: (i, 0)),
                  pl.BlockSpec((D_MODEL, n), lambda i: (0, 0))],
        out_specs=pl.BlockSpec((tm // tk, n, tk), lambda i: (i, 0, 0)),
        out_shape=jax.ShapeDtypeStruct((t // tk, n, tk), BF16),
        compiler_params=_params(("parallel",)),
        name="v_proj_t",
    )(h, w)


def _attn_kernel(lq1_ref, lk1_ref, lq2_ref, lk2_ref, q_ref, k_ref, vt_ref, o_ref, q2t_s, s_s, m_s, acc_s,
                 *, tq, nkv, unroll):
    tk = k_ref.shape[1]
    qt = q_ref[...].astype(F32).T
    row = lax.broadcasted_iota(jnp.int32, qt.shape, 0)
    q2t_s[:, :tq] = jnp.where(row < QK_DIM, qt, 0.0).astype(BF16)
    q2t_s[:, tq:] = jnp.where(row >= QK_DIM, qt, 0.0).astype(BF16)
    m_s[...] = jnp.full(m_s.shape, -jnp.inf, F32)
    acc_s[...] = jnp.zeros(acc_s.shape, F32)
    ones = jnp.ones((ONES_ROWS, tk), BF16)
    s_s[0] = _dot(k_ref[0], q2t_s[...])

    def step(j, par, compute_next):
        if compute_next:
            s_s[1 - par] = _dot(k_ref[j + 1], q2t_s[...])
        st = s_s[par]
        m_old = m_s[...]
        m_new = jnp.maximum(m_old, jnp.max(st, axis=0, keepdims=True))
        alpha = jnp.exp2(m_old - m_new)
        p = jnp.exp2(st - m_new).astype(BF16)
        lhs = jnp.concatenate([vt_ref[j], ones], axis=0)
        acc_s[...] = alpha * acc_s[...] + _dot(lhs, p)
        m_s[...] = m_new

    n_trips = (nkv - 1) // unroll

    def trip(t, carry):
        for u in range(unroll):
            step(unroll * t + u, u % 2, True)
        return carry

    lax.fori_loop(0, n_trips, trip, 0)
    for j in range(unroll * n_trips, nkv):
        step(j, j % 2, j + 1 < nkv)

    lam = (jnp.exp(jnp.sum(lq1_ref[...] * lk1_ref[...], axis=1, keepdims=True))
           - jnp.exp(jnp.sum(lq2_ref[...] * lk2_ref[...], axis=1, keepdims=True)) + LAMBDA_INIT)
    ot = acc_s[:V_DIM, :] / acc_s[V_DIM:V_DIM + 1, :]
    o_ref[...] = (ot[:, :tq] - lam * ot[:, tq:]).T


def _attention(q, k3, vt3, lams, tok_off, batch, seq, tq, tk):
    nkv = seq // tk
    qb0 = tok_off // tq
    sb0 = tok_off // seq
    nq = seq // tq
    unroll = 4 if nkv >= 16 else 2
    lam_spec = pl.BlockSpec((1, QK_DIM), lambda b, h, i: (0, 0))
    return pl.pallas_call(
        functools.partial(_attn_kernel, tq=tq, nkv=nkv, unroll=unroll),
        grid=(batch, ATTN_HEADS, nq),
        in_specs=[lam_spec, lam_spec, lam_spec, lam_spec,
                  pl.BlockSpec((tq, LANES), lambda b, h, i: (qb0 + b * nq + i, h)),
                  pl.BlockSpec((nkv, tk, LANES), lambda b, h, i: (sb0 + b, 0, h)),
                  pl.BlockSpec((nkv, LANES, tk), lambda b, h, i: (sb0 + b, h, 0))],
        out_specs=pl.BlockSpec((tq, LANES), lambda b, h, i: (b * nq + i, h)),
        out_shape=jax.ShapeDtypeStruct((batch * seq, ATTN_WIDTH), F32),
        scratch_shapes=[pltpu.VMEM((LANES, 2 * tq), BF16),
                        pltpu.VMEM((2, tk, 2 * tq), F32),
                        pltpu.VMEM((1, 2 * tq), F32),
                        pltpu.VMEM((V_DIM + ONES_ROWS, 2 * tq), F32)],
        compiler_params=_params(("parallel", "parallel", "arbitrary")),
        name="diff_attention",
    )(*lams, q, k3, vt3)


def _conv_kernel(start_ref, end_ref, prev_ref, cur_ref, next_ref, w_ref, b_ref, xs_ref, bc_ref, ext_s, *, tm):
    i = pl.program_id(0)
    halo = SUBLANES
    ext_s[0:halo, :] = jnp.where(start_ref[i] == 1, 0.0, prev_ref[...])
    ext_s[halo:halo + tm, :] = cur_ref[...]
    ext_s[halo + tm:2 * halo + tm, :] = jnp.where(end_ref[i] == 1, 0.0, next_ref[...])
    pad = (CONV_K - 1) // 2
    ext = ext_s[...]
    acc = jnp.broadcast_to(b_ref[...], (tm, CONV_CH))
    for d in range(CONV_K):
        shift = (pad - d) % (tm + 2 * halo)
        src = ext if shift == 0 else pltpu.roll(ext, shift, 0)
        acc = acc + w_ref[d:d + 1, :] * src[halo:halo + tm, :]
    y = _silu(acc)
    xs_ref[...] = y[:, :D_INNER]
    bc_ref[...] = y[:, D_INNER:].astype(bc_ref.dtype)


def _conv(xbc, conv_w, conv_b, start_flags, end_flags, tm):
    t = xbc.shape[0]
    rb = tm // SUBLANES
    last = t // SUBLANES - 1
    w_pad = jnp.zeros((SUBLANES, CONV_CH), F32).at[:CONV_K].set(conv_w)
    grid_spec = pltpu.PrefetchScalarGridSpec(
        num_scalar_prefetch=2,
        grid=(t // tm,),
        in_specs=[pl.BlockSpec((SUBLANES, CONV_CH), lambda i, s, e: (jnp.maximum(i * rb - 1, 0), 0)),
                  pl.BlockSpec((tm, CONV_CH), lambda i, s, e: (i, 0)),
                  pl.BlockSpec((SUBLANES, CONV_CH), lambda i, s, e: (jnp.minimum((i + 1) * rb, last), 0)),
                  pl.BlockSpec((SUBLANES, CONV_CH), lambda i, s, e: (0, 0)),
                  pl.BlockSpec((1, CONV_CH), lambda i, s, e: (0, 0))],
        out_specs=[pl.BlockSpec((tm, D_INNER), lambda i, s, e: (i, 0)),
                   pl.BlockSpec((tm, CONV_CH - D_INNER), lambda i, s, e: (i, 0))],
        scratch_shapes=[pltpu.VMEM((tm + 2 * SUBLANES, CONV_CH), F32)],
    )
    return pl.pallas_call(
        functools.partial(_conv_kernel, tm=tm),
        grid_spec=grid_spec,
        out_shape=[jax.ShapeDtypeStruct((t, D_INNER), F32),
                   jax.ShapeDtypeStruct((t, CONV_CH - D_INNER), BF16)],
        compiler_params=_params(("arbitrary",)),
        name="conv_silu",
    )(start_flags, end_flags, xbc, xbc, xbc, w_pad, conv_b.reshape(1, CONV_CH))


def _ssd_kernel(idx_ref, reset_ref, xs_ref, bc_ref, dt_ref, dtt_ref,
                alog_r_ref, alog_c_ref, tril_ref, triu_ref, e01_ref, *rest, rev):
    i = pl.program_id(0)
    nh = SSD_HEADS
    y_ref, state_s = rest[-2:]

    @pl.when(reset_ref[i] == 1)
    def _():
        state_s[...] = jnp.zeros(state_s.shape, F32)

    xs = xs_ref[...]
    bc = bc_ref[...]
    dtn = dt_ref[...]
    a = dtn * (-jnp.exp(alog_r_ref[...]))
    pinc = _dot_left01(tril_ref[...], a)
    pex = pinc - a
    tot = pinc[CHUNK - 1:CHUNK, :]

    gw = HEADS_PER_GROUP * SSD_HEAD_DIM
    gn = SSD_GROUPS * D_STATE
    cd = jnp.broadcast_to(jnp.exp(tot), (SUBLANES, LANES))

    def update_state(sc_st, sc_cd):
        w = (xs * sc_st).astype(BF16)
        for g in range(SSD_GROUPS):
            bg = bc[:, g * D_STATE:(g + 1) * D_STATE]
            new = _dot_tn(bg, w[:, g * gw:(g + 1) * gw])
            state_s[:, g * gw:(g + 1) * gw] = (state_s[:, g * gw:(g + 1) * gw] * sc_cd[:, g * gw:(g + 1) * gw]
                                               + new)

    if rev:
        stacked = jnp.concatenate([jnp.exp(pex) * dtn, jnp.exp(tot - pex), cd], axis=0)
        ex = _spread01(stacked, e01_ref[...])
        sc_off = ex[CHUNK:2 * CHUNK]
        for g in range(SSD_GROUPS):
            cg = bc[:, gn + g * D_STATE:gn + (g + 1) * D_STATE]
            st = state_s[:, g * gw:(g + 1) * gw]
            y_ref[:, g * gw:(g + 1) * gw] = _dot(cg, st.astype(BF16)) * sc_off[:, g * gw:(g + 1) * gw]
        update_state(ex[:CHUNK], ex[2 * CHUNK:2 * CHUNK + 1])
        return

    yb_ref, dskip_ref = rest[:2]
    dtnt = dtt_ref[...]
    at = dtnt * (-jnp.exp(alog_c_ref[...]))
    pinct = _dot_right01(at, triu_ref[...])
    pext = pinct - at
    li = lax.broadcasted_iota(jnp.int32, (CHUNK, CHUNK), 0)
    si = lax.broadcasted_iota(jnp.int32, (CHUNK, CHUNK), 1)
    lower = si <= li
    strict_lower = si < li
    strict_upper = si > li
    lane = lax.broadcasted_iota(jnp.int32, (CHUNK, LANES), 1)
    first_head = lane < SSD_HEAD_DIM
    xb = xs.astype(BF16)
    for g in range(SSD_GROUPS):
        bg = bc[:, g * D_STATE:(g + 1) * D_STATE]
        cg = bc[:, gn + g * D_STATE:gn + (g + 1) * D_STATE]
        cg_f = cg.astype(F32)
        cb = _dot_nt(cg, bg)
        for hp in range(HEADS_PER_GROUP // 2):
            col = g * gw + hp * LANES
            rhs = jnp.concatenate([xb[:, col:col + LANES], state_s[:, col:col + LANES].astype(BF16)], axis=0)
            pair = []
            for u in range(2):
                h = g * HEADS_PER_GROUP + 2 * hp + u
                colf = jnp.broadcast_to(pinc[:, h:h + 1], (CHUNK, CHUNK))
                arg = jnp.where(lower, colf - pinct[h:h + 1, :],
                                pext[nh + h:nh + h + 1, :] - pex[:, nh + h:nh + h + 1])
                dtf = dtnt[h:h + 1, :]
                dtb = dtnt[nh + h:nh + h + 1, :]
                coef = jnp.where(strict_lower, dtf, jnp.where(strict_upper, dtb, dtf + dtb))
                mh = (cb * jnp.exp(arg) * coef).astype(BF16)
                dh = (cg_f * jnp.exp(colf)).astype(BF16)
                pair.append(_dot(jnp.concatenate([mh, dh], axis=1), rhs))
            y_ref[:, col:col + LANES] = ((jnp.where(first_head, pair[0], pair[1]) + yb_ref[:, col:col + LANES])
                                         + dskip_ref[:, col:col + LANES] * xs[:, col:col + LANES])

    stacked = jnp.concatenate([jnp.exp(tot - pinc) * dtn, cd], axis=0)
    ex = _spread01(stacked, e01_ref[...])
    update_state(ex[:CHUNK], ex[CHUNK:CHUNK + 1])


def _ssd(xs, bc, dt, dtt, a_log, idx, reset, yb=None, dskip=None):
    rev = yb is None
    t = xs.shape[0]
    nc = t // CHUNK
    alog_r = jnp.pad(a_log.reshape(1, DT_COLS), ((0, 0), (0, LANES - DT_COLS)))
    alog_c = alog_r.reshape(LANES, 1)
    r = jnp.arange(CHUNK)
    tril = (r[None, :] <= r[:, None]).astype(BF16)
    triu = (r[:, None] <= r[None, :]).astype(BF16)
    lo = SSD_HEADS if rev else 0
    e01 = (jnp.arange(D_INNER)[None, :] // SSD_HEAD_DIM == jnp.arange(LANES)[:, None] - lo).astype(BF16)
    const = lambda shape: pl.BlockSpec(shape, lambda i, ix, rs: (0, 0))
    in_specs = [pl.BlockSpec((CHUNK, D_INNER), lambda i, ix, rs: (ix[i], 0)),
                pl.BlockSpec((CHUNK, CONV_CH - D_INNER), lambda i, ix, rs: (ix[i], 0)),
                pl.BlockSpec((CHUNK, LANES), lambda i, ix, rs: (ix[i], 0)),
                pl.BlockSpec((LANES, CHUNK), lambda i, ix, rs: (0, ix[i])),
                const((1, LANES)), const((LANES, 1)),
                const((CHUNK, CHUNK)), const((CHUNK, CHUNK)), const((LANES, D_INNER))]
    args = [idx, reset, xs, bc, dt, dtt, alog_r, alog_c, tril, triu, e01]
    if not rev:
        in_specs += [pl.BlockSpec((CHUNK, D_INNER), lambda i, ix, rs: (ix[i], 0)), const((1, D_INNER))]
        args += [yb, dskip]
    grid_spec = pltpu.PrefetchScalarGridSpec(
        num_scalar_prefetch=2,
        grid=(nc,),
        in_specs=in_specs,
        out_specs=pl.BlockSpec((CHUNK, D_INNER), lambda i, ix, rs: (ix[i], 0)),
        scratch_shapes=[pltpu.VMEM((D_STATE, D_INNER), F32)],
    )
    return pl.pallas_call(
        functools.partial(_ssd_kernel, rev=rev),
        grid_spec=grid_spec,
        out_shape=jax.ShapeDtypeStruct((t, D_INNER), F32),
        compiler_params=_params(("arbitrary",)),
        name="ssd_bwd" if rev else "ssd_fwd",
    )(*args)


def _post_kernel(oa_ref, ob_ref, xa_ref, xb_ref, y_ref, z_ref, gt_ref, mod_ref, subg_ref,
                 ssdg_ref, n2g_ref, wa_ref, ws_ref, wo_ref, x1_ref, h2b_ref, h2f_ref, an_s, yn_s,
                 *, n_first):
    m = mod_ref[0]
    o = _group_pick(oa_ref, ob_ref, n_first)
    for h in range(ATTN_HEADS):
        oh = o[:, h * V_DIM:(h + 1) * V_DIM]
        r = lax.rsqrt(jnp.mean(oh * oh, axis=-1, keepdims=True) + RMS_EPS)
        an_s[:, h * V_DIM:(h + 1) * V_DIM] = (((oh * r) * subg_ref[...]) * (1.0 - LAMBDA_INIT)).astype(BF16)
    attn_d = _dot(an_s[...], wa_ref[...])

    y = y_ref[...] * z_ref[...].astype(F32)
    gw = D_INNER // SSD_GROUPS
    for g in range(SSD_GROUPS):
        yg = y[:, g * gw:(g + 1) * gw]
        r = lax.rsqrt(jnp.mean(yg * yg, axis=-1, keepdims=True) + RMS_EPS)
        yn_s[:, g * gw:(g + 1) * gw] = ((yg * r) * ssdg_ref[:, g * gw:(g + 1) * gw]).astype(BF16)
    ssd_d = _dot(yn_s[...], ws_ref[...])

    gt = gt_ref[...].astype(F32)
    mix = gt[:, :D_MODEL] * attn_d + gt[:, D_MODEL:] * ssd_d
    mixed = _dot(mix.astype(BF16), wo_ref[...])
    x1 = _group_pick(xa_ref, xb_ref, n_first) + m[2:3] * mixed
    x1_ref[...] = x1
    r = lax.rsqrt(jnp.mean(x1 * x1, axis=-1, keepdims=True) + RMS_EPS)
    h2 = ((x1 * r) * n2g_ref[...]) * (1.0 + m[4:5]) + m[3:4]
    h2f_ref[...] = h2
    h2b_ref[...] = h2.astype(BF16)


def _post(oa, ob, xa, xb, y, z, gates, mod_tiles, subg, ssdg, n2g, wa, ws, wo, tm):
    t = y.shape[0]
    n_first = oa.shape[0] // tm
    tok = lambda n: pl.BlockSpec((tm, n), lambda i: (i, 0))
    const = lambda a: pl.BlockSpec(a.shape, lambda i: (0, 0))
    return pl.pallas_call(
        functools.partial(_post_kernel, n_first=n_first),
        grid=(t // tm,),
        in_specs=_group_specs(tm, ATTN_WIDTH, n_first) + _group_specs(tm, D_MODEL, n_first) + [
            tok(D_INNER), tok(D_INNER), tok(GATE_COLS),
            pl.BlockSpec((1, 6, D_MODEL), lambda i: (i, 0, 0)),
            const(subg), const(ssdg), const(n2g), const(wa), const(ws), const(wo)],
        out_specs=[tok(D_MODEL), tok(D_MODEL), tok(D_MODEL)],
        out_shape=[jax.ShapeDtypeStruct((t, D_MODEL), F32),
                   jax.ShapeDtypeStruct((t, D_MODEL), BF16),
                   jax.ShapeDtypeStruct((t, D_MODEL), F32)],
        scratch_shapes=[pltpu.VMEM((tm, ATTN_WIDTH), BF16), pltpu.VMEM((tm, D_INNER), BF16)],
        compiler_params=_params(("parallel",)),
        name="merge_out_proj",
    )(oa, ob, xa, xb, y, z, gates, mod_tiles, subg, ssdg, n2g, wa, ws, wo)


def _router_kernel(h_ref, w_ref, b_ref, o_ref):
    logits = _dot_f32_3(h_ref[...], w_ref[...]) + b_ref[...]
    tm = logits.shape[0]
    lane = lax.broadcasted_iota(jnp.int32, (tm, LANES), 1)
    lane_f = lane.astype(F32)
    big = float(LANES)
    neg = -jnp.inf
    gl = jnp.where(lane < N_EXPERT_GROUPS, logits, neg)
    gmax = jnp.max(gl, axis=1, keepdims=True)
    g_sel = jnp.min(jnp.where(gl == gmax, lane_f, big), axis=1, keepdims=True)
    g_w = 1.0 / jnp.sum(jnp.exp(gl - gmax), axis=1, keepdims=True)
    e_lane = lane - N_EXPERT_GROUPS
    e_group = (e_lane >> 3).astype(F32)
    in_group = (e_lane >= 0) & (e_lane < N_EXPERTS) & (e_group == g_sel)
    el = jnp.where(in_group, logits, neg)
    v1 = jnp.max(el, axis=1, keepdims=True)
    i1 = jnp.min(jnp.where(el == v1, lane_f, big), axis=1, keepdims=True)
    el2 = jnp.where(lane_f == i1, neg, el)
    v2 = jnp.max(el2, axis=1, keepdims=True)
    i2 = jnp.min(jnp.where(el2 == v2, lane_f, big), axis=1, keepdims=True)
    e2 = jnp.exp(v2 - v1)
    w1 = g_w / (1.0 + e2)
    w2 = g_w * e2 / (1.0 + e2)
    out = jnp.where(lane == 0, i1 - N_EXPERT_GROUPS,
                    jnp.where(lane == 1, i2 - N_EXPERT_GROUPS,
                              jnp.where(lane == 2, w1, jnp.where(lane == 3, w2, 0.0))))
    o_ref[...] = out


def _router(h2f, w_rt, b_rt, tm):
    t = h2f.shape[0]
    return pl.pallas_call(
        _router_kernel,
        grid=(t // tm,),
        in_specs=[pl.BlockSpec((tm, D_MODEL), lambda i: (i, 0)),
                  pl.BlockSpec((D_MODEL, LANES), lambda i: (0, 0)),
                  pl.BlockSpec((1, LANES), lambda i: (0, 0))],
        out_specs=pl.BlockSpec((tm, LANES), lambda i: (i, 0)),
        out_shape=jax.ShapeDtypeStruct((t, LANES), F32),
        compiler_params=_params(("parallel",)),
        name="router_topk",
    )(h2f, w_rt, b_rt)


def _expert_kernel(be_ref, nu_ref, x_ref, wg_ref, wu_ref, wd_ref, o_ref, wg_s, wu_s, wd_s):
    i = pl.program_id(0)
    used = i < nu_ref[0]
    new_expert = (i == 0) | (be_ref[i] != be_ref[jnp.maximum(i - 1, 0)])

    @pl.when(used & new_expert)
    def _():
        wg_s[...] = wg_ref[0].astype(BF16)
        wu_s[...] = wu_ref[0].astype(BF16)
        wd_s[...] = wd_ref[0].astype(BF16)

    @pl.when(used)
    def _():
        x = x_ref[...]
        a = _silu(_dot(x, wg_s[...])) * _dot(x, wu_s[...])
        o_ref[...] = _dot(a.astype(BF16), wd_s[...]).astype(o_ref.dtype)

    @pl.when(i >= nu_ref[0])
    def _():
        o_ref[...] = jnp.zeros(o_ref.shape, o_ref.dtype)


def _experts(xb, blk_e, n_used, wg, wu, wd):
    cap = xb.shape[0]
    nb = cap // EXPERT_BLOCK
    grid_spec = pltpu.PrefetchScalarGridSpec(
        num_scalar_prefetch=2,
        grid=(nb,),
        in_specs=[pl.BlockSpec((EXPERT_BLOCK, D_MODEL), lambda i, be, nu: (jnp.minimum(i, nu[0] - 1), 0)),
                  pl.BlockSpec((1, D_MODEL, EXPERT_FF), lambda i, be, nu: (be[i], 0, 0)),
                  pl.BlockSpec((1, D_MODEL, EXPERT_FF), lambda i, be, nu: (be[i], 0, 0)),
                  pl.BlockSpec((1, EXPERT_FF, D_MODEL), lambda i, be, nu: (be[i], 0, 0))],
        out_specs=pl.BlockSpec((EXPERT_BLOCK, D_MODEL), lambda i, be, nu: (i, 0)),
        scratch_shapes=[pltpu.VMEM((D_MODEL, EXPERT_FF), BF16), pltpu.VMEM((D_MODEL, EXPERT_FF), BF16),
                        pltpu.VMEM((EXPERT_FF, D_MODEL), BF16)],
    )
    return pl.pallas_call(
        _expert_kernel,
        grid_spec=grid_spec,
        out_shape=jax.ShapeDtypeStruct((cap, D_MODEL), BF16),
        compiler_params=_params(("arbitrary",)),
        name="expert_mlp",
    )(blk_e, n_used, xb, wg, wu, wd)


def _final_kernel(x1_ref, mod_ref, rt_ref, g0_ref, g1_ref, o_ref):
    m = mod_ref[0]
    rt = rt_ref[...]
    moe = g0_ref[...].astype(F32) * rt[:, 2:3] + g1_ref[...].astype(F32) * rt[:, 3:4]
    o_ref[...] = x1_ref[...] + m[5:6] * moe


def _final(x1, mod_tiles, rt, g, tok_off, n_tok, tm):
    b0 = tok_off // tm
    b1 = (g.shape[0] // TOP_K_INNER) // tm
    tok = lambda n: pl.BlockSpec((tm, n), lambda i: (b0 + i, 0))
    return pl.pallas_call(
        _final_kernel,
        grid=(n_tok // tm,),
        in_specs=[tok(D_MODEL), pl.BlockSpec((1, 6, D_MODEL), lambda i: (b0 + i, 0, 0)), tok(LANES),
                  tok(D_MODEL), pl.BlockSpec((tm, D_MODEL), lambda i: (b1 + b0 + i, 0))],
        out_specs=pl.BlockSpec((tm, D_MODEL), lambda i: (i, 0)),
        out_shape=jax.ShapeDtypeStruct((n_tok, D_MODEL), F32),
        compiler_params=_params(("parallel",)),
        name="moe_combine",
    )(x1, mod_tiles, rt, g, g)


def _rope_tables(seq):
    pos = jnp.arange(seq, dtype=F32)
    inv = 1.0 / (ROPE_THETA ** (jnp.arange(0, QK_DIM, 2, dtype=F32) / QK_DIM))
    ang = pos[:, None] * inv[None, :]
    cos, sin = jnp.cos(ang), jnp.sin(ang)
    cos_t = jnp.tile(cos, (1, LANES // (QK_DIM // 2)))
    sin_t = jnp.tile(jnp.concatenate([-sin, sin], axis=1), (1, LANES // QK_DIM))
    return cos_t, sin_t


def _dest_kernel(rt_ref, base_ref, tri_ref, o_ref, carry_s):
    @pl.when(pl.program_id(0) == 0)
    def _():
        carry_s[...] = jnp.zeros(carry_s.shape, F32)

    rt = rt_ref[...]
    tm = rt.shape[0]
    lane = lax.broadcasted_iota(jnp.int32, (tm, LANES), 1)
    lane_f = lane.astype(F32)
    oh0 = lane_f == rt[:, 0:1]
    oh1 = lane_f == rt[:, 1:2]
    both = jnp.where(oh0 | oh1, 1.0, 0.0)
    pos = base_ref[...] + carry_s[0:1, :] + _dot(tri_ref[...], both.astype(BF16))
    d0 = jnp.sum(jnp.where(oh0, pos, 0.0), axis=1, keepdims=True)
    d1 = jnp.sum(jnp.where(oh1, pos, 0.0), axis=1, keepdims=True)
    o_ref[...] = jnp.where(lane == 0, d0, jnp.where(lane == 1, d1, 0.0)).astype(jnp.int32)
    carry_s[...] = carry_s[...] + jnp.sum(both, axis=0, keepdims=True)


def _dispatch(rt, n_tok, tm):
    n_slots = n_tok * TOP_K_INNER
    flat_e = rt[:, :TOP_K_INNER].astype(jnp.int32).reshape(-1)
    counts = jnp.sum((flat_e[:, None] == jnp.arange(N_EXPERTS, dtype=jnp.int32)[None, :]).astype(jnp.int32), axis=0)
    padded = ((counts + EXPERT_BLOCK - 1) // EXPERT_BLOCK) * EXPERT_BLOCK
    pad_end = jnp.cumsum(padded)
    pad_start = pad_end - padded
    base = jnp.pad(pad_start.astype(F32), (0, LANES - N_EXPERTS)).reshape(1, LANES)
    r = jnp.arange(tm)
    tri = (r[None, :] < r[:, None]).astype(BF16)
    dest = pl.pallas_call(
        _dest_kernel,
        grid=(n_tok // tm,),
        in_specs=[pl.BlockSpec((tm, LANES), lambda i: (i, 0)),
                  pl.BlockSpec((1, LANES), lambda i: (0, 0)),
                  pl.BlockSpec((tm, tm), lambda i: (0, 0))],
        out_specs=pl.BlockSpec((tm, LANES), lambda i: (i, 0)),
        out_shape=jax.ShapeDtypeStruct((n_tok, LANES), jnp.int32),
        scratch_shapes=[pltpu.VMEM((SUBLANES, LANES), F32)],
        compiler_params=_params(("arbitrary",)),
        name="dispatch_rows",
    )(rt, base, tri)[:, :TOP_K_INNER]
    cap = n_slots + N_EXPERTS * EXPERT_BLOCK
    nb = cap // EXPERT_BLOCK
    blk_row0 = jnp.arange(nb, dtype=jnp.int32) * EXPERT_BLOCK
    blk_e = jnp.minimum(jnp.sum((pad_end[None, :] <= blk_row0[:, None]).astype(jnp.int32), axis=1), N_EXPERTS - 1)
    n_used = (pad_end[-1] // EXPERT_BLOCK).astype(jnp.int32).reshape(1)

    n_pad = cap - n_slots
    seg_len = jnp.concatenate([padded - counts, (cap - pad_end[-1]).reshape(1)])
    seg_end = jnp.cumsum(seg_len)
    seg_start = seg_end - seg_len
    seg_row0 = jnp.concatenate([pad_start + counts, pad_end[-1:]])
    j = jnp.arange(n_pad, dtype=jnp.int32)
    seg_onehot = (jnp.sum((seg_end[None, :] <= j[:, None]).astype(jnp.int32), axis=1)[:, None]
                  == jnp.arange(N_EXPERTS + 1, dtype=jnp.int32)[None, :]).astype(jnp.int32)
    pad_rows = j + jnp.sum(seg_onehot * (seg_row0 - seg_start)[None, :], axis=1)
    rows = jnp.concatenate([dest.reshape(-1), pad_rows.astype(jnp.int32)])
    toks = jnp.concatenate([jnp.arange(n_slots, dtype=jnp.int32) // TOP_K_INNER, j % n_tok])
    _, tok_buf = lax.sort_key_val(rows, toks)
    return dest, tok_buf, blk_e.astype(jnp.int32), n_used


def kernel(x_prompt, x_sample, c_prompt, c_sample, w_ada, b_ada, norm1_g, w_in, q_norm_g, k_norm_g, lambda_q1, lambda_k1, lambda_q2, lambda_k2, attn_subln_g, w_attn_o, conv_w, conv_b, dt_bias, a_log, d_skip, ssd_norm_g, w_ssd_o, w_out, norm2_g, w_group, b_group, w_router, b_router, w_gate_e, w_up_e, w_down_e):
    groups = [(x_prompt, c_prompt), (x_sample, c_sample)]
    seqs = [(x.shape[0], x.shape[1]) for x, _ in groups]
    n_tok = sum(b * s for b, s in seqs)
    min_seq = min(s for _, s in seqs)
    tm = min(1024, min_seq)
    tp = min(256, min_seq)
    tq = min(512, min_seq)
    tk = min(512, min_seq)
    layer = 0

    xa, xb_in = (g[0].reshape(-1, D_MODEL) for g in groups)
    c = jnp.concatenate([g[1] for g in groups], axis=0)
    n_batch = c.shape[0]
    c_pad = jnp.pad(c, ((0, (-n_batch) % SUBLANES), (0, 0)))
    mod = _ada(c_pad, w_ada[layer], b_ada[layer]).reshape(-1, 6, D_MODEL)
    tok_batch = jnp.concatenate([jnp.repeat(jnp.arange(b, dtype=jnp.int32), s) + off
                                 for (b, s), off in zip(seqs, [0, seqs[0][0]])])
    mod_tm = mod[tok_batch[::tm]]
    mod_tp = mod[tok_batch[::tp]]

    h = _norm_mod(xa, xb_in, mod_tm, norm1_g[layer], tm)

    w_in_b = w_in[layer].astype(BF16)
    tabs = [_rope_tables(s) for _, s in seqs]
    cos_t = jnp.concatenate([jnp.tile(tb[0], (b, 1)) for tb, (b, _) in zip(tabs, seqs)], axis=0)
    sin_t = jnp.concatenate([jnp.tile(tb[1], (b, 1)) for tb, (b, _) in zip(tabs, seqs)], axis=0)
    half = jnp.arange(LANES) // QK_DIM
    bd = (half[:, None] == half[None, :]).astype(BF16)
    gq = (jnp.tile(q_norm_g[layer], LANES // QK_DIM) * (QK_DIM ** -0.5 * LOG2_E)).reshape(1, LANES)
    gk = jnp.tile(k_norm_g[layer], LANES // QK_DIM).reshape(1, LANES)
    q = _qk_proj(h, w_in_b[:, OFF_Q:OFF_K], gq, cos_t, sin_t, bd, tm, "q_proj")
    k = _qk_proj(h, w_in_b[:, OFF_K:OFF_V], gk, cos_t, sin_t, bd, tm, "k_proj")
    vt3 = _vt_proj(h, w_in_b[:, OFF_V:OFF_Z], tm, tk)
    z_act = _matmul(h, w_in_b[:, OFF_Z:OFF_XBC], BF16, tm, 1024, "z_proj", _silu)
    xbc = _matmul(h, w_in_b[:, OFF_XBC:OFF_DT], F32, tm, 1024, "xbc_proj")
    w_dt = jnp.pad(w_in_b[:, OFF_DT:OFF_GATE], ((0, 0), (0, LANES - DT_COLS)))
    dt_b = jnp.pad(dt_bias[layer].reshape(1, DT_COLS), ((0, 0), (0, LANES - DT_COLS)))
    dt = _matmul(h, w_dt, F32, tm, LANES, "dt_proj", _softplus, dt_b)
    gates = _matmul(h, w_in_b[:, OFF_GATE:], BF16, tm, 1024, "gate_proj", _sigmoid)

    k3 = k.reshape(n_tok // tk, tk, ATTN_WIDTH)
    lams = [v[layer].reshape(1, QK_DIM) for v in (lambda_q1, lambda_k1, lambda_q2, lambda_k2)]
    o_groups = []
    off = 0
    for b, s in seqs:
        o_groups.append(_attention(q, k3, vt3, lams, off, b, s, tq, tk))
        off += b * s

    seq_starts = []
    off = 0
    for b, s in seqs:
        seq_starts += [(off + i * s, s) for i in range(b)]
        off += b * s
    tile_start = jnp.zeros((n_tok // tp,), jnp.int32)
    tile_end = jnp.zeros((n_tok // tp,), jnp.int32)
    nc = n_tok // CHUNK
    chunk_reset = jnp.zeros((nc,), jnp.int32)
    bwd_idx = jnp.zeros((nc,), jnp.int32)
    for st, s in seq_starts:
        tile_start = tile_start.at[st // tp].set(1)
        tile_end = tile_end.at[(st + s) // tp - 1].set(1)
        c0, c1 = st // CHUNK, (st + s) // CHUNK
        chunk_reset = chunk_reset.at[c0].set(1)
        bwd_idx = bwd_idx.at[c0:c1].set(jnp.arange(c1 - 1, c0 - 1, -1, dtype=jnp.int32))
    fwd_idx = jnp.arange(nc, dtype=jnp.int32)
    xs, bcm = _conv(xbc, conv_w[layer], conv_b[layer], tile_start, tile_end, tp)
    dtt = dt.T
    alg = a_log[layer].reshape(-1)
    dskip = jnp.repeat(d_skip[layer], SSD_HEAD_DIM).reshape(1, D_INNER)
    yb = _ssd(xs, bcm, dt, dtt, alg, bwd_idx, chunk_reset)
    y_ssd = _ssd(xs, bcm, dt, dtt, alg, fwd_idx, chunk_reset, yb, dskip)

    subg = attn_subln_g[layer].reshape(1, V_DIM)
    x1, h2b, h2f = _post(o_groups[0], o_groups[1], xa, xb_in, y_ssd, z_act, gates, mod_tp, subg,
                         ssd_norm_g[layer].reshape(1, D_INNER), norm2_g[layer].reshape(1, D_MODEL),
                         w_attn_o[layer].astype(BF16), w_ssd_o[layer].astype(BF16),
                         w_out[layer].astype(BF16), tp)

    n_rt = N_EXPERT_GROUPS + N_EXPERTS
    w_rt = jnp.pad(jnp.concatenate([w_group[layer], w_router[layer]], axis=1), ((0, 0), (0, LANES - n_rt)))
    b_rt = jnp.pad(jnp.concatenate([b_group[layer], b_router[layer]]), (0, LANES - n_rt)).reshape(1, LANES)
    rt = _router(h2f, w_rt, b_rt, tp)
    dest, tok_buf, blk_e, n_used = _dispatch(rt, n_tok, min(512, min_seq))
    xb = h2b[tok_buf]
    yb_e = _experts(xb, blk_e, n_used, w_gate_e[layer], w_up_e[layer], w_down_e[layer])
    g = yb_e[dest.T.reshape(-1)]

    outs = []
    off = 0
    for (b, s), (xg, _) in zip(seqs, groups):
        y = _final(x1, mod_tp, rt, g, off, b * s, tp)
        outs.append(y.reshape(xg.shape))
        off += b * s
    return tuple(outs)
```

```python
import functools
import math

import jax
import jax.numpy as jnp
from jax import lax
from jax.experimental import pallas as pl
from jax.experimental.pallas import tpu as pltpu

F32 = jnp.float32
BF16 = jnp.bfloat16

D_MODEL = 1024
ATTN_HEADS = 8
QK_DIM = 64
V_DIM = 2 * QK_DIM
ATTN_WIDTH = ATTN_HEADS * V_DIM
ROPE_THETA = 10000.0
D_INNER = 2048
SSD_HEAD_DIM = 64
SSD_HEADS = D_INNER // SSD_HEAD_DIM
SSD_GROUPS = 4
HEADS_PER_GROUP = SSD_HEADS // SSD_GROUPS
D_STATE = 128
CONV_K = 5
CONV_CH = D_INNER + 2 * SSD_GROUPS * D_STATE
CHUNK = 128
N_EXPERT_GROUPS = 4
EXPERTS_PER_GROUP = 8
N_EXPERTS = N_EXPERT_GROUPS * EXPERTS_PER_GROUP
TOP_K_INNER = 2
EXPERT_FF = 512
RMS_EPS = 1e-6
LAMBDA_INIT = 0.8 - 0.6 * math.exp(-0.3 * 0)

LANES = 128
SUBLANES = 8
VMEM_LIMIT = 56 * 1024 * 1024

Q_COLS = ATTN_HEADS * 2 * QK_DIM
K_COLS = Q_COLS
V_COLS = ATTN_WIDTH
Z_COLS = D_INNER
XBC_COLS = CONV_CH
DT_COLS = 2 * SSD_HEADS
GATE_COLS = 2 * D_MODEL
OFF_Q = 0
OFF_K = OFF_Q + Q_COLS
OFF_V = OFF_K + K_COLS
OFF_Z = OFF_V + V_COLS
OFF_XBC = OFF_Z + Z_COLS
OFF_DT = OFF_XBC + XBC_COLS
OFF_GATE = OFF_DT + DT_COLS

EXPERT_BLOCK = 256
ONES_ROWS = 16
LOG2_E = math.log2(math.e)


def _params(sem):
    return pltpu.CompilerParams(dimension_semantics=sem, vmem_limit_bytes=VMEM_LIMIT)


def _dot(a, b):
    return jnp.dot(a, b, preferred_element_type=F32)


def _dot_tn(a, b):
    return lax.dot_general(a, b, (((0,), (0,)), ((), ())), preferred_element_type=F32)


def _dot_nt(a, b):
    return lax.dot_general(a, b, (((1,), (1,)), ((), ())), preferred_element_type=F32)


def _split3(a):
    hi = a.astype(BF16)
    r = a - hi.astype(F32)
    mid = r.astype(BF16)
    lo = (r - mid.astype(F32)).astype(BF16)
    return hi, mid, lo


def _dot_left01(m01, a):
    hi, mid, lo = _split3(a)
    return _dot(m01, hi) + _dot(m01, mid) + _dot(m01, lo)


def _dot_right01(a, m01):
    hi, mid, lo = _split3(a)
    return _dot(hi, m01) + _dot(mid, m01) + _dot(lo, m01)


def _spread01(a, m01):
    hi = a.astype(BF16)
    lo = (a - hi.astype(F32)).astype(BF16)
    return _dot(hi, m01) + _dot(lo, m01)


def _dot_f32(a, b):
    a0, a1, a2 = _split3(a)
    b0, b1, b2 = _split3(b)
    return (_dot(a0, b0) + (_dot(a0, b1) + _dot(a1, b0))
            + (_dot(a0, b2) + _dot(a2, b0) + _dot(a1, b1)))


def _dot_f32_3(a, b):
    a0 = a.astype(BF16)
    a1 = (a - a0.astype(F32)).astype(BF16)
    b0 = b.astype(BF16)
    b1 = (b - b0.astype(F32)).astype(BF16)
    return _dot(a0, b0) + (_dot(a0, b1) + _dot(a1, b0))


def _sigmoid(x):
    return 1.0 / (1.0 + jnp.exp(-x))


def _silu(x):
    return x * _sigmoid(x)


def _softplus(x):
    e = jnp.exp(-jnp.abs(x))
    u = 1.0 + e
    log1p_e = jnp.where(u == 1.0, e, jnp.log(u) * (e / (u - 1.0)))
    return jnp.maximum(x, 0.0) + log1p_e


def _ada_kernel(c_ref, w_ref, b_ref, o_ref):
    o_ref[...] = _dot_f32(_silu(c_ref[...]), w_ref[...]) + b_ref[...]


def _ada(c_pad, w_ada, b_ada):
    rows = c_pad.shape[0]
    n = w_ada.shape[1]
    tn = 1024
    return pl.pallas_call(
        _ada_kernel,
        grid=(n // tn,),
        in_specs=[pl.BlockSpec((rows, D_MODEL), lambda j: (0, 0)),
                  pl.BlockSpec((D_MODEL, tn), lambda j: (0, j)),
                  pl.BlockSpec((1, tn), lambda j: (0, j))],
        out_specs=pl.BlockSpec((rows, tn), lambda j: (0, j)),
        out_shape=jax.ShapeDtypeStruct((rows, n), F32),
        compiler_params=_params(("arbitrary",)),
        name="ada_mod",
    )(c_pad, w_ada, b_ada.reshape(1, n))


def _group_specs(tm, n, n_first):
    return [pl.BlockSpec((tm, n), lambda i: (jnp.minimum(i, n_first - 1), 0)),
            pl.BlockSpec((tm, n), lambda i: (jnp.maximum(i - n_first, 0), 0))]


def _group_pick(a_ref, b_ref, n_first):
    return jnp.where(pl.program_id(0) < n_first, a_ref[...], b_ref[...])


def _norm_mod_kernel(xa_ref, xb_ref, mod_ref, g_ref, o_ref, *, shift_row, scale_row, n_first):
    x = _group_pick(xa_ref, xb_ref, n_first)
    r = lax.rsqrt(jnp.mean(x * x, axis=-1, keepdims=True) + RMS_EPS)
    m = mod_ref[0]
    h = ((x * r) * g_ref[...]) * (1.0 + m[scale_row:scale_row + 1]) + m[shift_row:shift_row + 1]
    o_ref[...] = h.astype(o_ref.dtype)


def _norm_mod(xa, xb, mod_tiles, g, tm):
    t = xa.shape[0] + xb.shape[0]
    n_first = xa.shape[0] // tm
    return pl.pallas_call(
        functools.partial(_norm_mod_kernel, shift_row=0, scale_row=1, n_first=n_first),
        grid=(t // tm,),
        in_specs=_group_specs(tm, D_MODEL, n_first) + [
            pl.BlockSpec((1, 6, D_MODEL), lambda i: (i, 0, 0)),
            pl.BlockSpec((1, D_MODEL), lambda i: (0, 0))],
        out_specs=pl.BlockSpec((tm, D_MODEL), lambda i: (i, 0)),
        out_shape=jax.ShapeDtypeStruct((t, D_MODEL), BF16),
        compiler_params=_params(("parallel",)),
        name="norm1_mod",
    )(xa, xb, mod_tiles, g.reshape(1, D_MODEL))


def _mm_kernel(a_ref, w_ref, *rest, act):
    o_ref = rest[-1]
    acc = _dot(a_ref[...], w_ref[...])
    if len(rest) == 2:
        acc = acc + rest[0][...]
    if act is not None:
        acc = act(acc)
    o_ref[...] = acc.astype(o_ref.dtype)


def _matmul(a, w, out_dtype, tm, tn, name, act=None, bias=None):
    m, k = a.shape
    n = w.shape[1]
    in_specs = [pl.BlockSpec((tm, k), lambda j, i: (i, 0)),
                pl.BlockSpec((k, tn), lambda j, i: (0, j))]
    args = [a, w]
    if bias is not None:
        in_specs.append(pl.BlockSpec((1, tn), lambda j, i: (0, j)))
        args.append(bias)
    return pl.pallas_call(
        functools.partial(_mm_kernel, act=act),
        grid=(n // tn, m // tm),
        in_specs=in_specs,
        out_specs=pl.BlockSpec((tm, tn), lambda j, i: (i, j)),
        out_shape=jax.ShapeDtypeStruct((m, n), out_dtype),
        compiler_params=_params(("parallel", "parallel")),
        name=name,
    )(*args)


def _qk_kernel(a_ref, w_ref, g_ref, cos_ref, sin_ref, bd_ref, o_ref):
    acc = _dot(a_ref[...], w_ref[...])
    cos = cos_ref[...]
    sin = sin_ref[...]
    g = g_ref[...]
    bd = bd_ref[...]
    for h in range(ATTN_HEADS):
        x = acc[:, h * LANES:(h + 1) * LANES]
        sq = x * x
        hi = sq.astype(BF16)
        lo = (sq - hi.astype(F32)).astype(BF16)
        ss = _dot(hi, bd) + _dot(lo, bd)
        r = lax.rsqrt(ss * (1.0 / QK_DIM) + RMS_EPS)
        xn = (x * r) * g
        partner = pltpu.roll(xn, LANES // 2, 1)
        o_ref[:, h * LANES:(h + 1) * LANES] = (xn * cos + partner * sin).astype(o_ref.dtype)


def _qk_proj(h, w, g128, cos_t, sin_t, bd, tm, name):
    t = h.shape[0]
    n = w.shape[1]
    return pl.pallas_call(
        _qk_kernel,
        grid=(t // tm,),
        in_specs=[pl.BlockSpec((tm, D_MODEL), lambda i: (i, 0)),
                  pl.BlockSpec((D_MODEL, n), lambda i: (0, 0)),
                  pl.BlockSpec((1, LANES), lambda i: (0, 0)),
                  pl.BlockSpec((tm, LANES), lambda i: (i, 0)),
                  pl.BlockSpec((tm, LANES), lambda i: (i, 0)),
                  pl.BlockSpec((LANES, LANES), lambda i: (0, 0))],
        out_specs=pl.BlockSpec((tm, n), lambda i: (i, 0)),
        out_shape=jax.ShapeDtypeStruct((t, n), BF16),
        compiler_params=_params(("parallel",)),
        name=name,
    )(h, w, g128, cos_t, sin_t, bd)


def _vt_kernel(a_ref, w_ref, o_ref, *, tk):
    acc = _dot(a_ref[...], w_ref[...])
    for c in range(acc.shape[0] // tk):
        o_ref[c] = acc[c * tk:(c + 1) * tk, :].T.astype(o_ref.dtype)


def _vt_proj(h, w, tm, tk):
    t = h.shape[0]
    n = w.shape[1]
    return pl.pallas_call(
        functools.partial(_vt_kernel, tk=tk),
        grid=(t // tm,),
        in_specs=[pl.BlockSpec((tm, D_MODEL), lambda i: (i, 0)),
                  pl.BlockSpec((D_MODEL, n), lambda i: (0, 0))],
        out_specs=pl.BlockSpec((tm // tk, n, tk), lambda i: (i, 0, 0)),
        out_shape=jax.ShapeDtypeStruct((t // tk, n, tk), BF16),
        compiler_params=_params(("parallel",)),
        name="v_proj_t",
    )(h, w)


def _attn_kernel(lq1_ref, lk1_ref, lq2_ref, lk2_ref, q_ref, k_ref, vt_ref, o_ref, q2t_s, s_s, m_s, acc_s,
                 *, tq, nkv, unroll):
    tk = k_ref.shape[1]
    qt = q_ref[...].astype(F32).T
    row = lax.broadcasted_iota(jnp.int32, qt.shape, 0)
    first_half = (row & ROT) == 0
    q2t_s[:, :tq] = jnp.where(first_half, qt, 0.0).astype(BF16)
    q2t_s[:, tq:] = jnp.where(first_half, 0.0, qt).astype(BF16)
    m_s[...] = jnp.full(m_s.shape, -jnp.inf, F32)
    acc_s[...] = jnp.zeros(acc_s.shape, F32)
    ones = jnp.ones((ONES_ROWS, tk), BF16)
    s_s[0] = _dot(k_ref[0], q2t_s[...])

    def step(j, par, compute_next):
        if compute_next:
            s_s[1 - par] = _dot(k_ref[j + 1], q2t_s[...])
        st = s_s[par]
        m_old = m_s[...]
        m_new = jnp.maximum(m_old, jnp.max(st, axis=0, keepdims=True))
        alpha = jnp.exp2(m_old - m_new)
        p = jnp.exp2(st - m_new).astype(BF16)
        lhs = jnp.concatenate([vt_ref[j], ones], axis=0)
        acc_s[...] = alpha * acc_s[...] + _dot(lhs, p)
        m_s[...] = m_new

    n_trips = (nkv - 1) // unroll

    def trip(t, carry):
        for u in range(unroll):
            step(unroll * t + u, u % 2, True)
        return carry

    lax.fori_loop(0, n_trips, trip, 0)
    for j in range(unroll * n_trips, nkv):
        step(j, j % 2, j + 1 < nkv)

    lam = (jnp.exp(jnp.sum(lq1_ref[...] * lk1_ref[...], axis=1, keepdims=True))
           - jnp.exp(jnp.sum(lq2_ref[...] * lk2_ref[...], axis=1, keepdims=True)) + LAMBDA_INIT)
    ot = acc_s[:V_DIM, :] / acc_s[V_DIM:V_DIM + 1, :]
    o_ref[...] = (ot[:, :tq] - lam * ot[:, tq:]).T


def _attention(q, k3, vt3, lams, tok_off, batch, seq, tq, tk):
    nkv = seq // tk
    qb0 = tok_off // tq
    sb0 = tok_off // seq
    nq = seq // tq
    unroll = 4 if nkv >= 16 else 2
    lam_spec = pl.BlockSpec((1, QK_DIM), lambda b, h, i: (0, 0))
    return pl.pallas_call(
        functools.partial(_attn_kernel, tq=tq, nkv=nkv, unroll=unroll),
        grid=(batch, ATTN_HEADS, nq),
        in_specs=[lam_spec, lam_spec, lam_spec, lam_spec,
                  pl.BlockSpec((tq, LANES), lambda b, h, i: (qb0 + b * nq + i, h)),
                  pl.BlockSpec((nkv, tk, LANES), lambda b, h, i: (sb0 + b, 0, h)),
                  pl.BlockSpec((nkv, LANES, tk), lambda b, h, i: (sb0 + b, h, 0))],
        out_specs=pl.BlockSpec((tq, LANES), lambda b, h, i: (b * nq + i, h)),
        out_shape=jax.ShapeDtypeStruct((batch * seq, ATTN_WIDTH), F32),
        scratch_shapes=[pltpu.VMEM((LANES, 2 * tq), BF16),
                        pltpu.VMEM((2, tk, 2 * tq), F32),
                        pltpu.VMEM((1, 2 * tq), F32),
                        pltpu.VMEM((V_DIM + ONES_ROWS, 2 * tq), F32)],
        compiler_params=_params(("parallel", "parallel", "arbitrary")),
        name="diff_attention",
    )(*lams, q, k3, vt3)


def _conv_kernel(start_ref, end_ref, prev_ref, cur_ref, next_ref, w_ref, b_ref, xs_ref, bc_ref, ext_s, *, tm):
    i = pl.program_id(0)
    halo = SUBLANES
    ext_s[0:halo, :] = jnp.where(start_ref[i] == 1, 0.0, prev_ref[...])
    ext_s[halo:halo + tm, :] = cur_ref[...]
    ext_s[halo + tm:2 * halo + tm, :] = jnp.where(end_ref[i] == 1, 0.0, next_ref[...])
    pad = (CONV_K - 1) // 2
    ext = ext_s[...]
    acc = jnp.broadcast_to(b_ref[...], (tm, CONV_CH))
    for d in range(CONV_K):
        shift = (pad - d) % (tm + 2 * halo)
        src = ext if shift == 0 else pltpu.roll(ext, shift, 0)
        acc = acc + w_ref[d:d + 1, :] * src[halo:halo + tm, :]
    y = _silu(acc)
    xs_ref[...] = y[:, :D_INNER]
    bc_ref[...] = y[:, D_INNER:].astype(bc_ref.dtype)


def _conv(xbc, conv_w, conv_b, start_flags, end_flags, tm):
    t = xbc.shape[0]
    rb = tm // SUBLANES
    last = t // SUBLANES - 1
    w_pad = jnp.zeros((SUBLANES, CONV_CH), F32).at[:CONV_K].set(conv_w)
    grid_spec = pltpu.PrefetchScalarGridSpec(
        num_scalar_prefetch=2,
        grid=(t // tm,),
        in_specs=[pl.BlockSpec((SUBLANES, CONV_CH), lambda i, s, e: (jnp.maximum(i * rb - 1, 0), 0)),
                  pl.BlockSpec((tm, CONV_CH), lambda i, s, e: (i, 0)),
                  pl.BlockSpec((SUBLANES, CONV_CH), lambda i, s, e: (jnp.minimum((i + 1) * rb, last), 0)),
                  pl.BlockSpec((SUBLANES, CONV_CH), lambda i, s, e: (0, 0)),
                  pl.BlockSpec((1, CONV_CH), lambda i, s, e: (0, 0))],
        out_specs=[pl.BlockSpec((tm, D_INNER), lambda i, s, e: (i, 0)),
                   pl.BlockSpec((tm, CONV_CH - D_INNER), lambda i, s, e: (i, 0))],
        scratch_shapes=[pltpu.VMEM((tm + 2 * SUBLANES, CONV_CH), F32)],
    )
    return pl.pallas_call(
        functools.partial(_conv_kernel, tm=tm),
        grid_spec=grid_spec,
        out_shape=[jax.ShapeDtypeStruct((t, D_INNER), F32),
                   jax.ShapeDtypeStruct((t, CONV_CH - D_INNER), BF16)],
        compiler_params=_params(("arbitrary",)),
        name="conv_silu",
    )(start_flags, end_flags, xbc, xbc, xbc, w_pad, conv_b.reshape(1, CONV_CH))


def _ssd_kernel(idx_ref, reset_ref, xs_ref, bc_ref, dt_ref, dtt_ref,
                alog_r_ref, alog_c_ref, tril_ref, triu_ref, e01_ref, *rest, rev):
    i = pl.program_id(0)
    nh = SSD_HEADS
    y_ref, state_s = rest[-2:]

    @pl.when(reset_ref[i] == 1)
    def _():
        state_s[...] = jnp.zeros(state_s.shape, F32)

    xs = xs_ref[...]
    bc = bc_ref[...]
    dtn = dt_ref[...]
    a = dtn * (-jnp.exp(alog_r_ref[...]))
    pinc = _dot_left01(tril_ref[...], a)
    pex = pinc - a
    tot = pinc[CHUNK - 1:CHUNK, :]

    gw = HEADS_PER_GROUP * SSD_HEAD_DIM
    gn = SSD_GROUPS * D_STATE
    cd = jnp.broadcast_to(jnp.exp(tot), (SUBLANES, LANES))

    def update_state(sc_st, sc_cd):
        w = (xs * sc_st).astype(BF16)
        for g in range(SSD_GROUPS):
            bg = bc[:, g * D_STATE:(g + 1) * D_STATE]
            new = _dot_tn(bg, w[:, g * gw:(g + 1) * gw])
            state_s[:, g * gw:(g + 1) * gw] = (state_s[:, g * gw:(g + 1) * gw] * sc_cd[:, g * gw:(g + 1) * gw]
                                               + new)

    sc_cd = _spread01(cd, e01_ref[...])[0:1]
    if rev:
        stacked = jnp.concatenate([jnp.exp(pex) * dtn, jnp.exp(tot - pex)], axis=0)
        ex = _dot(stacked.astype(BF16), e01_ref[...])
        sc_off = ex[CHUNK:]
        for g in range(SSD_GROUPS):
            cg = bc[:, gn + g * D_STATE:gn + (g + 1) * D_STATE]
            st = state_s[:, g * gw:(g + 1) * gw]
            y_ref[:, g * gw:(g + 1) * gw] = _dot(cg, st.astype(BF16)) * sc_off[:, g * gw:(g + 1) * gw]
        update_state(ex[:CHUNK], sc_cd)
        return

    yb_ref, dskip_ref = rest[:2]
    dtnt = dtt_ref[...]
    at = dtnt * (-jnp.exp(alog_c_ref[...]))
    pinct = _dot_right01(at, triu_ref[...])
    pext = pinct - at
    li = lax.broadcasted_iota(jnp.int32, (CHUNK, CHUNK), 0)
    si = lax.broadcasted_iota(jnp.int32, (CHUNK, CHUNK), 1)
    lower = si <= li
    strict_lower = si < li
    strict_upper = si > li
    lane = lax.broadcasted_iota(jnp.int32, (CHUNK, LANES), 1)
    first_head = lane < SSD_HEAD_DIM
    xb = xs.astype(BF16)
    for g in range(SSD_GROUPS):
        bg = bc[:, g * D_STATE:(g + 1) * D_STATE]
        cg = bc[:, gn + g * D_STATE:gn + (g + 1) * D_STATE]
        cg_f = cg.astype(F32)
        cb = _dot_nt(cg, bg)
        for hp in range(HEADS_PER_GROUP // 2):
            col = g * gw + hp * LANES
            rhs = jnp.concatenate([xb[:, col:col + LANES], state_s[:, col:col + LANES].astype(BF16)], axis=0)
            pair = []
            for u in range(2):
                h = g * HEADS_PER_GROUP + 2 * hp + u
                colf = jnp.broadcast_to(pinc[:, h:h + 1], (CHUNK, CHUNK))
                arg = jnp.where(lower, colf - pinct[h:h + 1, :],
                                pext[nh + h:nh + h + 1, :] - pex[:, nh + h:nh + h + 1])
                dtf = dtnt[h:h + 1, :]
                dtb = dtnt[nh + h:nh + h + 1, :]
                coef = jnp.where(strict_lower, dtf, jnp.where(strict_upper, dtb, dtf + dtb))
                mh = (cb * jnp.exp(arg) * coef).astype(BF16)
                dh = (cg_f * jnp.exp(colf)).astype(BF16)
                pair.append(_dot(jnp.concatenate([mh, dh], axis=1), rhs))
            y_ref[:, col:col + LANES] = ((jnp.where(first_head, pair[0], pair[1]) + yb_ref[:, col:col + LANES])
                                         + dskip_ref[:, col:col + LANES] * xs[:, col:col + LANES])

    update_state(_dot((jnp.exp(tot - pinc) * dtn).astype(BF16), e01_ref[...]), sc_cd)


def _ssd(xs, bc, dt, dtt, a_log, idx, reset, yb=None, dskip=None):
    rev = yb is None
    t = xs.shape[0]
    nc = t // CHUNK
    alog_r = jnp.pad(a_log.reshape(1, DT_COLS), ((0, 0), (0, LANES - DT_COLS)))
    alog_c = alog_r.reshape(LANES, 1)
    r = jnp.arange(CHUNK)
    tril = (r[None, :] <= r[:, None]).astype(BF16)
    triu = (r[:, None] <= r[None, :]).astype(BF16)
    lo = SSD_HEADS if rev else 0
    e01 = (jnp.arange(D_INNER)[None, :] // SSD_HEAD_DIM == jnp.arange(LANES)[:, None] - lo).astype(BF16)
    const = lambda shape: pl.BlockSpec(shape, lambda i, ix, rs: (0, 0))
    in_specs = [pl.BlockSpec((CHUNK, D_INNER), lambda i, ix, rs: (ix[i], 0)),
                pl.BlockSpec((CHUNK, CONV_CH - D_INNER), lambda i, ix, rs: (ix[i], 0)),
                pl.BlockSpec((CHUNK, LANES), lambda i, ix, rs: (ix[i], 0)),
                pl.BlockSpec((LANES, CHUNK), lambda i, ix, rs: (0, ix[i])),
                const((1, LANES)), const((LANES, 1)),
                const((CHUNK, CHUNK)), const((CHUNK, CHUNK)), const((LANES, D_INNER))]
    args = [idx, reset, xs, bc, dt, dtt, alog_r, alog_c, tril, triu, e01]
    if not rev:
        in_specs += [pl.BlockSpec((CHUNK, D_INNER), lambda i, ix, rs: (ix[i], 0)), const((1, D_INNER))]
        args += [yb, dskip]
    grid_spec = pltpu.PrefetchScalarGridSpec(
        num_scalar_prefetch=2,
        grid=(nc,),
        in_specs=in_specs,
        out_specs=pl.BlockSpec((CHUNK, D_INNER), lambda i, ix, rs: (ix[i], 0)),
        scratch_shapes=[pltpu.VMEM((D_STATE, D_INNER), F32)],
    )
    return pl.pallas_call(
        functools.partial(_ssd_kernel, rev=rev),
        grid_spec=grid_spec,
        out_shape=jax.ShapeDtypeStruct((t, D_INNER), F32),
        compiler_params=_params(("arbitrary",)),
        name="ssd_bwd" if rev else "ssd_fwd",
    )(*args)


def _post_kernel(oa_ref, ob_ref, xa_ref, xb_ref, y_ref, z_ref, gt_ref, mod_ref, subg_ref,
                 ssdg_ref, n2g_ref, wa_ref, ws_ref, wo_ref, wrt_ref, brt_ref, x1_ref, h2b_ref, rt_ref, an_s, yn_s,
                 *, n_first):
    m = mod_ref[0]
    o = _group_pick(oa_ref, ob_ref, n_first)
    for h in range(ATTN_HEADS):
        oh = o[:, h * V_DIM:(h + 1) * V_DIM]
        r = lax.rsqrt(jnp.mean(oh * oh, axis=-1, keepdims=True) + RMS_EPS)
        an_s[:, h * V_DIM:(h + 1) * V_DIM] = (((oh * r) * subg_ref[...]) * (1.0 - LAMBDA_INIT)).astype(BF16)
    attn_d = _dot(an_s[...], wa_ref[...])

    y = y_ref[...] * z_ref[...].astype(F32)
    gw = D_INNER // SSD_GROUPS
    for g in range(SSD_GROUPS):
        yg = y[:, g * gw:(g + 1) * gw]
        r = lax.rsqrt(jnp.mean(yg * yg, axis=-1, keepdims=True) + RMS_EPS)
        yn_s[:, g * gw:(g + 1) * gw] = ((yg * r) * ssdg_ref[:, g * gw:(g + 1) * gw]).astype(BF16)
    ssd_d = _dot(yn_s[...], ws_ref[...])

    gt = gt_ref[...].astype(F32)
    mix = gt[:, :D_MODEL] * attn_d + gt[:, D_MODEL:] * ssd_d
    mixed = _dot(mix.astype(BF16), wo_ref[...])
    x1 = _group_pick(xa_ref, xb_ref, n_first) + m[2:3] * mixed
    x1_ref[...] = x1
    r = lax.rsqrt(jnp.mean(x1 * x1, axis=-1, keepdims=True) + RMS_EPS)
    h2 = ((x1 * r) * n2g_ref[...]) * (1.0 + m[4:5]) + m[3:4]
    h2b_ref[...] = h2.astype(BF16)
    rt_ref[...] = _route(_dot_f32_3(h2, wrt_ref[...]) + brt_ref[...])


def _post(oa, ob, xa, xb, y, z, gates, mod_tiles, subg, ssdg, n2g, wa, ws, wo, w_rt, b_rt, tm):
    t = y.shape[0]
    n_first = oa.shape[0] // tm
    tok = lambda n: pl.BlockSpec((tm, n), lambda i: (i, 0))
    const = lambda a: pl.BlockSpec(a.shape, lambda i: (0, 0))
    return pl.pallas_call(
        functools.partial(_post_kernel, n_first=n_first),
        grid=(t // tm,),
        in_specs=_group_specs(tm, ATTN_WIDTH, n_first) + _group_specs(tm, D_MODEL, n_first) + [
            tok(D_INNER), tok(D_INNER), tok(GATE_COLS),
            pl.BlockSpec((1, 6, D_MODEL), lambda i: (i, 0, 0)),
            const(subg), const(ssdg), const(n2g), const(wa), const(ws), const(wo), const(w_rt), const(b_rt)],
        out_specs=[tok(D_MODEL), tok(D_MODEL), tok(LANES)],
        out_shape=[jax.ShapeDtypeStruct((t, D_MODEL), F32),
                   jax.ShapeDtypeStruct((t, D_MODEL), BF16),
                   jax.ShapeDtypeStruct((t, LANES), F32)],
        scratch_shapes=[pltpu.VMEM((tm, ATTN_WIDTH), BF16), pltpu.VMEM((tm, D_INNER), BF16)],
        compiler_params=_params(("parallel",)),
        name="merge_out_proj",
    )(oa, ob, xa, xb, y, z, gates, mod_tiles, subg, ssdg, n2g, wa, ws, wo, w_rt, b_rt)


def _route(logits):
    tm = logits.shape[0]
    lane = lax.broadcasted_iota(jnp.int32, (tm, LANES), 1)
    lane_f = lane.astype(F32)
    big = float(LANES)
    neg = -jnp.inf
    gl = jnp.where(lane < N_EXPERT_GROUPS, logits, neg)
    gmax = jnp.max(gl, axis=1, keepdims=True)
    g_sel = jnp.min(jnp.where(gl == gmax, lane_f, big), axis=1, keepdims=True)
    g_w = 1.0 / jnp.sum(jnp.exp(gl - gmax), axis=1, keepdims=True)
    e_lane = lane - N_EXPERT_GROUPS
    e_group = (e_lane >> 3).astype(F32)
    in_group = (e_lane >= 0) & (e_lane < N_EXPERTS) & (e_group == g_sel)
    el = jnp.where(in_group, logits, neg)
    v1 = jnp.max(el, axis=1, keepdims=True)
    i1 = jnp.min(jnp.where(el == v1, lane_f, big), axis=1, keepdims=True)
    el2 = jnp.where(lane_f == i1, neg, el)
    v2 = jnp.max(el2, axis=1, keepdims=True)
    i2 = jnp.min(jnp.where(el2 == v2, lane_f, big), axis=1, keepdims=True)
    e2 = jnp.exp(v2 - v1)
    w1 = g_w / (1.0 + e2)
    w2 = g_w * e2 / (1.0 + e2)
    return jnp.where(lane == 0, i1 - N_EXPERT_GROUPS,
                     jnp.where(lane == 1, i2 - N_EXPERT_GROUPS,
                               jnp.where(lane == 2, w1, jnp.where(lane == 3, w2, 0.0))))


def _expert_kernel(be_ref, nu_ref, x_ref, wg_ref, wu_ref, wd_ref, o_ref, wg_s, wu_s, wd_s):
    i = pl.program_id(0)
    used = i < nu_ref[0]
    new_expert = (i == 0) | (be_ref[i] != be_ref[jnp.maximum(i - 1, 0)])

    @pl.when(used & new_expert)
    def _():
        wg_s[...] = wg_ref[0].astype(BF16)
        wu_s[...] = wu_ref[0].astype(BF16)
        wd_s[...] = wd_ref[0].astype(BF16)

    @pl.when(used)
    def _():
        x = x_ref[...]
        a = _silu(_dot(x, wg_s[...])) * _dot(x, wu_s[...])
        o_ref[...] = _dot(a.astype(BF16), wd_s[...]).astype(o_ref.dtype)

    @pl.when(i >= nu_ref[0])
    def _():
        o_ref[...] = jnp.zeros(o_ref.shape, o_ref.dtype)


def _experts(xb, blk_e, n_used, wg, wu, wd):
    cap = xb.shape[0]
    nb = cap // EXPERT_BLOCK
    grid_spec = pltpu.PrefetchScalarGridSpec(
        num_scalar_prefetch=2,
        grid=(nb,),
        in_specs=[pl.BlockSpec((EXPERT_BLOCK, D_MODEL), lambda i, be, nu: (jnp.minimum(i, nu[0] - 1), 0)),
                  pl.BlockSpec((1, D_MODEL, EXPERT_FF), lambda i, be, nu: (be[i], 0, 0)),
                  pl.BlockSpec((1, D_MODEL, EXPERT_FF), lambda i, be, nu: (be[i], 0, 0)),
                  pl.BlockSpec((1, EXPERT_FF, D_MODEL), lambda i, be, nu: (be[i], 0, 0))],
        out_specs=pl.BlockSpec((EXPERT_BLOCK, D_MODEL), lambda i, be, nu: (i, 0)),
        scratch_shapes=[pltpu.VMEM((D_MODEL, EXPERT_FF), BF16), pltpu.VMEM((D_MODEL, EXPERT_FF), BF16),
                        pltpu.VMEM((EXPERT_FF, D_MODEL), BF16)],
    )
    return pl.pallas_call(
        _expert_kernel,
        grid_spec=grid_spec,
        out_shape=jax.ShapeDtypeStruct((cap, D_MODEL), BF16),
        compiler_params=_params(("arbitrary",)),
        name="expert_mlp",
    )(blk_e, n_used, xb, wg, wu, wd)


def _final_kernel(x1_ref, mod_ref, rt_ref, g0_ref, g1_ref, o_ref):
    m = mod_ref[0]
    rt = rt_ref[...]
    moe = g0_ref[...].astype(F32) * rt[:, 2:3] + g1_ref[...].astype(F32) * rt[:, 3:4]
    o_ref[...] = x1_ref[...] + m[5:6] * moe


def _final(x1, mod_tiles, rt, g, tok_off, n_tok, tm):
    b0 = tok_off // tm
    b1 = (g.shape[0] // TOP_K_INNER) // tm
    tok = lambda n: pl.BlockSpec((tm, n), lambda i: (b0 + i, 0))
    return pl.pallas_call(
        _final_kernel,
        grid=(n_tok // tm,),
        in_specs=[tok(D_MODEL), pl.BlockSpec((1, 6, D_MODEL), lambda i: (b0 + i, 0, 0)), tok(LANES),
                  tok(D_MODEL), pl.BlockSpec((tm, D_MODEL), lambda i: (b1 + b0 + i, 0))],
        out_specs=pl.BlockSpec((tm, D_MODEL), lambda i: (i, 0)),
        out_shape=jax.ShapeDtypeStruct((n_tok, D_MODEL), F32),
        compiler_params=_params(("parallel",)),
        name="moe_combine",
    )(x1, mod_tiles, rt, g, g)


ROT = QK_DIM // 2


def _permute_head_cols(w):
    k = w.shape[0]
    return w.reshape(k, ATTN_HEADS, 2, 2, ROT).transpose(0, 1, 3, 2, 4).reshape(k, ATTN_HEADS * LANES)


def _permute_gain(g):
    return jnp.broadcast_to(g.reshape(2, 1, ROT), (2, 2, ROT)).reshape(1, LANES)


def _rope_tables(seq):
    pos = jnp.arange(seq, dtype=F32)
    inv = 1.0 / (ROPE_THETA ** (jnp.arange(0, QK_DIM, 2, dtype=F32) / QK_DIM))
    ang = pos[:, None] * inv[None, :]
    cos, sin = jnp.cos(ang), jnp.sin(ang)
    cos_t = jnp.tile(cos, (1, LANES // ROT))
    sin_t = jnp.concatenate([-sin, -sin, sin, sin], axis=1)
    return cos_t, sin_t


def _dest_kernel(rt_ref, base_ref, tri_ref, o_ref, carry_s):
    @pl.when(pl.program_id(0) == 0)
    def _():
        carry_s[...] = jnp.zeros(carry_s.shape, F32)

    rt = rt_ref[...]
    tm = rt.shape[0]
    lane = lax.broadcasted_iota(jnp.int32, (tm, LANES), 1)
    lane_f = lane.astype(F32)
    oh0 = lane_f == rt[:, 0:1]
    oh1 = lane_f == rt[:, 1:2]
    both = jnp.where(oh0 | oh1, 1.0, 0.0)
    pos = base_ref[...] + carry_s[0:1, :] + _dot(tri_ref[...], both.astype(BF16))
    d0 = jnp.sum(jnp.where(oh0, pos, 0.0), axis=1, keepdims=True)
    d1 = jnp.sum(jnp.where(oh1, pos, 0.0), axis=1, keepdims=True)
    o_ref[...] = jnp.where(lane == 0, d0, jnp.where(lane == 1, d1, 0.0)).astype(jnp.int32)
    carry_s[...] = carry_s[...] + jnp.sum(both, axis=0, keepdims=True)


def _dispatch(rt, n_tok, tm):
    n_slots = n_tok * TOP_K_INNER
    flat_e = rt[:, :TOP_K_INNER].astype(jnp.int32).reshape(-1)
    counts = jnp.sum((flat_e[:, None] == jnp.arange(N_EXPERTS, dtype=jnp.int32)[None, :]).astype(jnp.int32), axis=0)
    padded = ((counts + EXPERT_BLOCK - 1) // EXPERT_BLOCK) * EXPERT_BLOCK
    pad_end = jnp.cumsum(padded)
    pad_start = pad_end - padded
    base = jnp.pad(pad_start.astype(F32), (0, LANES - N_EXPERTS)).reshape(1, LANES)
    r = jnp.arange(tm)
    tri = (r[None, :] < r[:, None]).astype(BF16)
    dest = pl.pallas_call(
        _dest_kernel,
        grid=(n_tok // tm,),
        in_specs=[pl.BlockSpec((tm, LANES), lambda i: (i, 0)),
                  pl.BlockSpec((1, LANES), lambda i: (0, 0)),
                  pl.BlockSpec((tm, tm), lambda i: (0, 0))],
        out_specs=pl.BlockSpec((tm, LANES), lambda i: (i, 0)),
        out_shape=jax.ShapeDtypeStruct((n_tok, LANES), jnp.int32),
        scratch_shapes=[pltpu.VMEM((SUBLANES, LANES), F32)],
        compiler_params=_params(("arbitrary",)),
        name="dispatch_rows",
    )(rt, base, tri)[:, :TOP_K_INNER]
    cap = n_slots + N_EXPERTS * EXPERT_BLOCK
    nb = cap // EXPERT_BLOCK
    blk_row0 = jnp.arange(nb, dtype=jnp.int32) * EXPERT_BLOCK
    blk_e = jnp.minimum(jnp.sum((pad_end[None, :] <= blk_row0[:, None]).astype(jnp.int32), axis=1), N_EXPERTS - 1)
    n_used = (pad_end[-1] // EXPERT_BLOCK).astype(jnp.int32).reshape(1)

    n_pad = cap - n_slots
    seg_len = jnp.concatenate([padded - counts, (cap - pad_end[-1]).reshape(1)])
    seg_end = jnp.cumsum(seg_len)
    seg_start = seg_end - seg_len
    seg_row0 = jnp.concatenate([pad_start + counts, pad_end[-1:]])
    j = jnp.arange(n_pad, dtype=jnp.int32)
    seg_onehot = (jnp.sum((seg_end[None, :] <= j[:, None]).astype(jnp.int32), axis=1)[:, None]
                  == jnp.arange(N_EXPERTS + 1, dtype=jnp.int32)[None, :]).astype(jnp.int32)
    pad_rows = j + jnp.sum(seg_onehot * (seg_row0 - seg_start)[None, :], axis=1)
    rows = jnp.concatenate([dest.reshape(-1), pad_rows.astype(jnp.int32)])
    toks = jnp.concatenate([jnp.arange(n_slots, dtype=jnp.int32) // TOP_K_INNER, j % n_tok])
    _, tok_buf = lax.sort_key_val(rows, toks)
    return dest, tok_buf, blk_e.astype(jnp.int32), n_used


def kernel(x_prompt, x_sample, c_prompt, c_sample, w_ada, b_ada, norm1_g, w_in, q_norm_g, k_norm_g, lambda_q1, lambda_k1, lambda_q2, lambda_k2, attn_subln_g, w_attn_o, conv_w, conv_b, dt_bias, a_log, d_skip, ssd_norm_g, w_ssd_o, w_out, norm2_g, w_group, b_group, w_router, b_router, w_gate_e, w_up_e, w_down_e):
    groups = [(x_prompt, c_prompt), (x_sample, c_sample)]
    seqs = [(x.shape[0], x.shape[1]) for x, _ in groups]
    n_tok = sum(b * s for b, s in seqs)
    min_seq = min(s for _, s in seqs)
    tm = min(1024, min_seq)
    tp = min(256, min_seq)
    tq = min(512, min_seq)
    tk = min(512, min_seq)
    layer = 0

    xa, xb_in = (g[0].reshape(-1, D_MODEL) for g in groups)
    c = jnp.concatenate([g[1] for g in groups], axis=0)
    n_batch = c.shape[0]
    c_pad = jnp.pad(c, ((0, (-n_batch) % SUBLANES), (0, 0)))
    mod = _ada(c_pad, w_ada[layer], b_ada[layer]).reshape(-1, 6, D_MODEL)
    tok_batch = jnp.concatenate([jnp.repeat(jnp.arange(b, dtype=jnp.int32), s) + off
                                 for (b, s), off in zip(seqs, [0, seqs[0][0]])])
    mod_tm = mod[tok_batch[::tm]]
    mod_tp = mod[tok_batch[::tp]]

    h = _norm_mod(xa, xb_in, mod_tm, norm1_g[layer], tm)

    w_in_b = w_in[layer].astype(BF16)
    tabs = [_rope_tables(s) for _, s in seqs]
    cos_t = jnp.concatenate([jnp.tile(tb[0], (b, 1)) for tb, (b, _) in zip(tabs, seqs)], axis=0)
    sin_t = jnp.concatenate([jnp.tile(tb[1], (b, 1)) for tb, (b, _) in zip(tabs, seqs)], axis=0)
    half = (jnp.arange(LANES) // ROT) % 2
    bd = (half[:, None] == half[None, :]).astype(BF16)
    gq = _permute_gain(q_norm_g[layer]) * (QK_DIM ** -0.5 * LOG2_E)
    gk = _permute_gain(k_norm_g[layer])
    q = _qk_proj(h, _permute_head_cols(w_in_b[:, OFF_Q:OFF_K]), gq, cos_t, sin_t, bd, tm, "q_proj")
    k = _qk_proj(h, _permute_head_cols(w_in_b[:, OFF_K:OFF_V]), gk, cos_t, sin_t, bd, tm, "k_proj")
    vt3 = _vt_proj(h, w_in_b[:, OFF_V:OFF_Z], tm, tk)
    z_act = _matmul(h, w_in_b[:, OFF_Z:OFF_XBC], BF16, tm, 1024, "z_proj", _silu)
    xbc = _matmul(h, w_in_b[:, OFF_XBC:OFF_DT], F32, tm, 1024, "xbc_proj")
    w_dt = jnp.pad(w_in_b[:, OFF_DT:OFF_GATE], ((0, 0), (0, LANES - DT_COLS)))
    dt_b = jnp.pad(dt_bias[layer].reshape(1, DT_COLS), ((0, 0), (0, LANES - DT_COLS)))
    dt = _matmul(h, w_dt, F32, tm, LANES, "dt_proj", _softplus, dt_b)
    gates = _matmul(h, w_in_b[:, OFF_GATE:], BF16, tm, 1024, "gate_proj", _sigmoid)

    k3 = k.reshape(n_tok // tk, tk, ATTN_WIDTH)
    lams = [v[layer].reshape(1, QK_DIM) for v in (lambda_q1, lambda_k1, lambda_q2, lambda_k2)]
    o_groups = []
    off = 0
    for b, s in seqs:
        o_groups.append(_attention(q, k3, vt3, lams, off, b, s, tq, tk))
        off += b * s

    seq_starts = []
    off = 0
    for b, s in seqs:
        seq_starts += [(off + i * s, s) for i in range(b)]
        off += b * s
    tile_start = jnp.zeros((n_tok // tp,), jnp.int32)
    tile_end = jnp.zeros((n_tok // tp,), jnp.int32)
    nc = n_tok // CHUNK
    chunk_reset = jnp.zeros((nc,), jnp.int32)
    bwd_idx = jnp.zeros((nc,), jnp.int32)
    for st, s in seq_starts:
        tile_start = tile_start.at[st // tp].set(1)
        tile_end = tile_end.at[(st + s) // tp - 1].set(1)
        c0, c1 = st // CHUNK, (st + s) // CHUNK
        chunk_reset = chunk_reset.at[c0].set(1)
        bwd_idx = bwd_idx.at[c0:c1].set(jnp.arange(c1 - 1, c0 - 1, -1, dtype=jnp.int32))
    fwd_idx = jnp.arange(nc, dtype=jnp.int32)
    xs, bcm = _conv(xbc, conv_w[layer], conv_b[layer], tile_start, tile_end, tp)
    dtt = dt.T
    alg = a_log[layer].reshape(-1)
    dskip = jnp.repeat(d_skip[layer], SSD_HEAD_DIM).reshape(1, D_INNER)
    yb = _ssd(xs, bcm, dt, dtt, alg, bwd_idx, chunk_reset)
    y_ssd = _ssd(xs, bcm, dt, dtt, alg, fwd_idx, chunk_reset, yb, dskip)

    subg = attn_subln_g[layer].reshape(1, V_DIM)
    n_rt = N_EXPERT_GROUPS + N_EXPERTS
    w_rt = jnp.pad(jnp.concatenate([w_group[layer], w_router[layer]], axis=1), ((0, 0), (0, LANES - n_rt)))
    b_rt = jnp.pad(jnp.concatenate([b_group[layer], b_router[layer]]), (0, LANES - n_rt)).reshape(1, LANES)
    x1, h2b, rt = _post(o_groups[0], o_groups[1], xa, xb_in, y_ssd, z_act, gates, mod_tp, subg,
                        ssd_norm_g[layer].reshape(1, D_INNER), norm2_g[layer].reshape(1, D_MODEL),
                        w_attn_o[layer].astype(BF16), w_ssd_o[layer].astype(BF16),
                        w_out[layer].astype(BF16), w_rt, b_rt, tp)

    dest, tok_buf, blk_e, n_used = _dispatch(rt, n_tok, min(512, min_seq))
    xb = h2b[tok_buf]
    yb_e = _experts(xb, blk_e, n_used, w_gate_e[layer], w_up_e[layer], w_down_e[layer])
    g = yb_e[dest.T.reshape(-1)]

    outs = []
    off = 0
    for (b, s), (xg, _) in zip(seqs, groups):
        y = _final(x1, mod_tp, rt, g, off, b * s, tp)
        outs.append(y.reshape(xg.shape))
        off += b * s
    return tuple(outs)
```

```python
import functools
import math

import jax
import jax.numpy as jnp
from jax import lax
from jax.experimental import pallas as pl
from jax.experimental.pallas import tpu as pltpu

F32 = jnp.float32
BF16 = jnp.bfloat16

D_MODEL = 1024
ATTN_HEADS = 8
QK_DIM = 64
V_DIM = 2 * QK_DIM
ATTN_WIDTH = ATTN_HEADS * V_DIM
ROPE_THETA = 10000.0
D_INNER = 2048
SSD_HEAD_DIM = 64
SSD_HEADS = D_INNER // SSD_HEAD_DIM
SSD_GROUPS = 4
HEADS_PER_GROUP = SSD_HEADS // SSD_GROUPS
D_STATE = 128
CONV_K = 5
CONV_CH = D_INNER + 2 * SSD_GROUPS * D_STATE
CHUNK = 128
N_EXPERT_GROUPS = 4
EXPERTS_PER_GROUP = 8
N_EXPERTS = N_EXPERT_GROUPS * EXPERTS_PER_GROUP
TOP_K_INNER = 2
EXPERT_FF = 512
RMS_EPS = 1e-6
LAMBDA_INIT = 0.8 - 0.6 * math.exp(-0.3 * 0)

LANES = 128
SUBLANES = 8
VMEM_LIMIT = 56 * 1024 * 1024

Q_COLS = ATTN_HEADS * 2 * QK_DIM
K_COLS = Q_COLS
V_COLS = ATTN_WIDTH
Z_COLS = D_INNER
XBC_COLS = CONV_CH
DT_COLS = 2 * SSD_HEADS
GATE_COLS = 2 * D_MODEL
OFF_Q = 0
OFF_K = OFF_Q + Q_COLS
OFF_V = OFF_K + K_COLS
OFF_Z = OFF_V + V_COLS
OFF_XBC = OFF_Z + Z_COLS
OFF_DT = OFF_XBC + XBC_COLS
OFF_GATE = OFF_DT + DT_COLS

EXPERT_BLOCK = 256
ONES_ROWS = 16
LOG2_E = math.log2(math.e)


def _params(sem):
    return pltpu.CompilerParams(dimension_semantics=sem, vmem_limit_bytes=VMEM_LIMIT)


def _dot(a, b):
    return jnp.dot(a, b, preferred_element_type=F32)


def _dot_tn(a, b):
    return lax.dot_general(a, b, (((0,), (0,)), ((), ())), preferred_element_type=F32)


def _dot_nt(a, b):
    return lax.dot_general(a, b, (((1,), (1,)), ((), ())), preferred_element_type=F32)


def _split3(a):
    hi = a.astype(BF16)
    r = a - hi.astype(F32)
    mid = r.astype(BF16)
    lo = (r - mid.astype(F32)).astype(BF16)
    return hi, mid, lo


def _dot_left01(m01, a):
    hi, mid, lo = _split3(a)
    return _dot(m01, hi) + _dot(m01, mid) + _dot(m01, lo)


def _dot_right01(a, m01):
    hi, mid, lo = _split3(a)
    return _dot(hi, m01) + _dot(mid, m01) + _dot(lo, m01)


def _spread01(a, m01):
    hi = a.astype(BF16)
    lo = (a - hi.astype(F32)).astype(BF16)
    return _dot(hi, m01) + _dot(lo, m01)


def _dot_f32(a, b):
    a0, a1, a2 = _split3(a)
    b0, b1, b2 = _split3(b)
    return (_dot(a0, b0) + (_dot(a0, b1) + _dot(a1, b0))
            + (_dot(a0, b2) + _dot(a2, b0) + _dot(a1, b1)))


def _dot_f32_3(a, b):
    a0 = a.astype(BF16)
    a1 = (a - a0.astype(F32)).astype(BF16)
    b0 = b.astype(BF16)
    b1 = (b - b0.astype(F32)).astype(BF16)
    return _dot(a0, b0) + (_dot(a0, b1) + _dot(a1, b0))


def _sigmoid(x):
    return 1.0 / (1.0 + jnp.exp(-x))


def _silu(x):
    return x * _sigmoid(x)


def _softplus(x):
    e = jnp.exp(-jnp.abs(x))
    u = 1.0 + e
    log1p_e = jnp.where(u == 1.0, e, jnp.log(u) * (e / (u - 1.0)))
    return jnp.maximum(x, 0.0) + log1p_e


def _ada_kernel(c_ref, w_ref, b_ref, o_ref):
    o_ref[...] = _dot_f32(_silu(c_ref[...]), w_ref[...]) + b_ref[...]


def _ada(c_pad, w_ada, b_ada):
    rows = c_pad.shape[0]
    n = w_ada.shape[1]
    tn = 1024
    return pl.pallas_call(
        _ada_kernel,
        grid=(n // tn,),
        in_specs=[pl.BlockSpec((rows, D_MODEL), lambda j: (0, 0)),
                  pl.BlockSpec((D_MODEL, tn), lambda j: (0, j)),
                  pl.BlockSpec((1, tn), lambda j: (0, j))],
        out_specs=pl.BlockSpec((rows, tn), lambda j: (0, j)),
        out_shape=jax.ShapeDtypeStruct((rows, n), F32),
        compiler_params=_params(("arbitrary",)),
        name="ada_mod",
    )(c_pad, w_ada, b_ada.reshape(1, n))


def _group_specs(tm, n, n_first):
    return [pl.BlockSpec((tm, n), lambda i: (jnp.minimum(i, n_first - 1), 0)),
            pl.BlockSpec((tm, n), lambda i: (jnp.maximum(i - n_first, 0), 0))]


def _group_pick(a_ref, b_ref, n_first):
    return jnp.where(pl.program_id(0) < n_first, a_ref[...], b_ref[...])


def _norm_mod_kernel(xa_ref, xb_ref, mod_ref, g_ref, o_ref, *, shift_row, scale_row, n_first):
    x = _group_pick(xa_ref, xb_ref, n_first)
    r = lax.rsqrt(jnp.mean(x * x, axis=-1, keepdims=True) + RMS_EPS)
    m = mod_ref[0]
    h = ((x * r) * g_ref[...]) * (1.0 + m[scale_row:scale_row + 1]) + m[shift_row:shift_row + 1]
    o_ref[...] = h.astype(o_ref.dtype)


def _norm_mod(xa, xb, mod_tiles, g, tm):
    t = xa.shape[0] + xb.shape[0]
    n_first = xa.shape[0] // tm
    return pl.pallas_call(
        functools.partial(_norm_mod_kernel, shift_row=0, scale_row=1, n_first=n_first),
        grid=(t // tm,),
        in_specs=_group_specs(tm, D_MODEL, n_first) + [
            pl.BlockSpec((1, 6, D_MODEL), lambda i: (i, 0, 0)),
            pl.BlockSpec((1, D_MODEL), lambda i: (0, 0))],
        out_specs=pl.BlockSpec((tm, D_MODEL), lambda i: (i, 0)),
        out_shape=jax.ShapeDtypeStruct((t, D_MODEL), BF16),
        compiler_params=_params(("parallel",)),
        name="norm1_mod",
    )(xa, xb, mod_tiles, g.reshape(1, D_MODEL))


def _mm_kernel(a_ref, w_ref, *rest, act):
    o_ref = rest[-1]
    acc = _dot(a_ref[...], w_ref[...])
    if len(rest) == 2:
        acc = acc + rest[0][...]
    if act is not None:
        acc = act(acc)
    o_ref[...] = acc.astype(o_ref.dtype)


def _matmul(a, w, out_dtype, tm, tn, name, act=None, bias=None):
    m, k = a.shape
    n = w.shape[1]
    in_specs = [pl.BlockSpec((tm, k), lambda j, i: (i, 0)),
                pl.BlockSpec((k, tn), lambda j, i: (0, j))]
    args = [a, w]
    if bias is not None:
        in_specs.append(pl.BlockSpec((1, tn), lambda j, i: (0, j)))
        args.append(bias)
    return pl.pallas_call(
        functools.partial(_mm_kernel, act=act),
        grid=(n // tn, m // tm),
        in_specs=in_specs,
        out_specs=pl.BlockSpec((tm, tn), lambda j, i: (i, j)),
        out_shape=jax.ShapeDtypeStruct((m, n), out_dtype),
        compiler_params=_params(("parallel", "parallel")),
        name=name,
    )(*args)


def _qk_kernel(a_ref, w_ref, g_ref, cos_ref, sin_ref, bd_ref, o_ref):
    acc = _dot(a_ref[...], w_ref[...])
    cos = cos_ref[...]
    sin = sin_ref[...]
    g = g_ref[...]
    bd = bd_ref[...]
    for h in range(ATTN_HEADS):
        x = acc[:, h * LANES:(h + 1) * LANES]
        sq = x * x
        hi = sq.astype(BF16)
        lo = (sq - hi.astype(F32)).astype(BF16)
        ss = _dot(hi, bd) + _dot(lo, bd)
        r = lax.rsqrt(ss * (1.0 / QK_DIM) + RMS_EPS)
        xn = (x * r) * g
        partner = pltpu.roll(xn, LANES // 2, 1)
        o_ref[:, h * LANES:(h + 1) * LANES] = (xn * cos + partner * sin).astype(o_ref.dtype)


def _qk_proj(h, w, g128, cos_t, sin_t, bd, tm, name):
    t = h.shape[0]
    n = w.shape[1]
    return pl.pallas_call(
        _qk_kernel,
        grid=(t // tm,),
        in_specs=[pl.BlockSpec((tm, D_MODEL), lambda i: (i, 0)),
                  pl.BlockSpec((D_MODEL, n), lambda i: (0, 0)),
                  pl.BlockSpec((1, LANES), lambda i: (0, 0)),
                  pl.BlockSpec((tm, LANES), lambda i: (i, 0)),
                  pl.BlockSpec((tm, LANES), lambda i: (i, 0)),
                  pl.BlockSpec((LANES, LANES), lambda i: (0, 0))],
        out_specs=pl.BlockSpec((tm, n), lambda i: (i, 0)),
        out_shape=jax.ShapeDtypeStruct((t, n), BF16),
        compiler_params=_params(("parallel",)),
        name=name,
    )(h, w, g128, cos_t, sin_t, bd)


def _vt_kernel(a_ref, w_ref, o_ref, *, tk):
    acc = _dot(a_ref[...], w_ref[...])
    for c in range(acc.shape[0] // tk):
        o_ref[c] = acc[c * tk:(c + 1) * tk, :].T.astype(o_ref.dtype)


def _vt_proj(h, w, tm, tk):
    t = h.shape[0]
    n = w.shape[1]
    return pl.pallas_call(
        functools.partial(_vt_kernel, tk=tk),
        grid=(t // tm,),
        in_specs=[pl.BlockSpec((tm, D_MODEL), lambda i: (i, 0)),
                  pl.BlockSpec((D_MODEL, n), lambda i: (0, 0))],
        out_specs=pl.BlockSpec((tm // tk, n, tk), lambda i: (i, 0, 0)),
        out_shape=jax.ShapeDtypeStruct((t // tk, n, tk), BF16),
        compiler_params=_params(("parallel",)),
        name="v_proj_t",
    )(h, w)


def _attn_kernel(trips_ref, lq1_ref, lk1_ref, lq2_ref, lk2_ref, q_ref, k_ref, vt_ref, o_ref,
                 q2t_s, s_s, m_s, acc_s, *, tq, nkv, unroll):
    tk = k_ref.shape[1]
    qt = q_ref[...].astype(F32).T
    row = lax.broadcasted_iota(jnp.int32, qt.shape, 0)
    first_half = (row & ROT) == 0
    q2t_s[:, :tq] = jnp.where(first_half, qt, 0.0).astype(BF16)
    q2t_s[:, tq:] = jnp.where(first_half, 0.0, qt).astype(BF16)
    m_s[...] = jnp.full(m_s.shape, -jnp.inf, F32)
    acc_s[...] = jnp.zeros(acc_s.shape, F32)
    ones = jnp.ones((ONES_ROWS, tk), BF16)
    s_s[0] = _dot(k_ref[0], q2t_s[...])

    def step(j, par, compute_next):
        if compute_next:
            s_s[1 - par] = _dot(k_ref[j + 1], q2t_s[...])
        st = s_s[par]
        m_old = m_s[...]
        m_new = jnp.maximum(m_old, jnp.max(st, axis=0, keepdims=True))
        alpha = jnp.exp2(m_old - m_new)
        p = jnp.exp2(st - m_new).astype(BF16)
        lhs = jnp.concatenate([vt_ref[j], ones], axis=0)
        acc_s[...] = alpha * acc_s[...] + _dot(lhs, p)
        m_s[...] = m_new

    n_trips = (nkv - 1) // unroll

    def trip(t, carry):
        for u in range(unroll):
            step(unroll * t + u, u % 2, True)
        return carry

    lax.fori_loop(0, trips_ref[0], trip, 0)
    for j in range(unroll * n_trips, nkv):
        step(j, j % 2, j + 1 < nkv)

    lam = (jnp.exp(jnp.sum(lq1_ref[...] * lk1_ref[...], axis=1, keepdims=True))
           - jnp.exp(jnp.sum(lq2_ref[...] * lk2_ref[...], axis=1, keepdims=True)) + LAMBDA_INIT)
    ot = acc_s[:V_DIM, :] / acc_s[V_DIM:V_DIM + 1, :]
    o_ref[...] = (ot[:, :tq] - lam * ot[:, tq:]).T


def _attention(q, k3, vt3, lams, tok_off, batch, seq, tq, tk):
    nkv = seq // tk
    qb0 = tok_off // tq
    sb0 = tok_off // seq
    nq = seq // tq
    unroll = 4 if nkv >= 8 else 2
    n_trips = jnp.full((1,), (nkv - 1) // unroll, jnp.int32)
    lam_spec = pl.BlockSpec((1, QK_DIM), lambda b, h, i, nt: (0, 0))
    grid_spec = pltpu.PrefetchScalarGridSpec(
        num_scalar_prefetch=1,
        grid=(batch, ATTN_HEADS, nq),
        in_specs=[lam_spec, lam_spec, lam_spec, lam_spec,
                  pl.BlockSpec((tq, LANES), lambda b, h, i, nt: (qb0 + b * nq + i, h)),
                  pl.BlockSpec((nkv, tk, LANES), lambda b, h, i, nt: (sb0 + b, 0, h)),
                  pl.BlockSpec((nkv, LANES, tk), lambda b, h, i, nt: (sb0 + b, h, 0))],
        out_specs=pl.BlockSpec((tq, LANES), lambda b, h, i, nt: (b * nq + i, h)),
        scratch_shapes=[pltpu.VMEM((LANES, 2 * tq), BF16),
                        pltpu.VMEM((2, tk, 2 * tq), F32),
                        pltpu.VMEM((1, 2 * tq), F32),
                        pltpu.VMEM((V_DIM + ONES_ROWS, 2 * tq), F32)],
    )
    return pl.pallas_call(
        functools.partial(_attn_kernel, tq=tq, nkv=nkv, unroll=unroll),
        grid_spec=grid_spec,
        out_shape=jax.ShapeDtypeStruct((batch * seq, ATTN_WIDTH), F32),
        compiler_params=_params(("parallel", "parallel", "arbitrary")),
        name="diff_attention",
    )(n_trips, *lams, q, k3, vt3)


HALO = 16
MXU_COLS = 256


def _proj_conv_kernel(start_ref, end_ref, prev_ref, cur_ref, next_ref, w_ref, cw_ref, cb_ref, o_ref, *, tm):
    i = pl.program_id(1)
    prev = jnp.where(start_ref[i] == 1, jnp.zeros_like(prev_ref[...]), prev_ref[...])
    nxt = jnp.where(end_ref[i] == 1, jnp.zeros_like(next_ref[...]), next_ref[...])
    a = jnp.concatenate([prev, cur_ref[...], nxt], axis=0)
    pad = (CONV_K - 1) // 2
    for c in range(0, o_ref.shape[1], MXU_COLS):
        ext = _dot(a, w_ref[:, c:c + MXU_COLS])
        acc = jnp.broadcast_to(cb_ref[:, c:c + MXU_COLS], (tm, MXU_COLS))
        for d in range(CONV_K):
            shift = (pad - d) % (tm + 2 * HALO)
            src = ext if shift == 0 else pltpu.roll(ext, shift, 0)
            acc = acc + cw_ref[d:d + 1, c:c + MXU_COLS] * src[HALO:HALO + tm, :]
        o_ref[:, c:c + MXU_COLS] = _silu(acc).astype(o_ref.dtype)


def _proj_conv(h, w, conv_w, conv_b, start_flags, end_flags, out_dtype, tm, tn, name):
    t = h.shape[0]
    n = w.shape[1]
    rb = tm // HALO
    last = t // HALO - 1
    cw_pad = jnp.zeros((SUBLANES, n), F32).at[:CONV_K].set(conv_w)
    grid_spec = pltpu.PrefetchScalarGridSpec(
        num_scalar_prefetch=2,
        grid=(n // tn, t // tm),
        in_specs=[pl.BlockSpec((HALO, D_MODEL), lambda j, i, s, e: (jnp.maximum(i * rb - 1, 0), 0)),
                  pl.BlockSpec((tm, D_MODEL), lambda j, i, s, e: (i, 0)),
                  pl.BlockSpec((HALO, D_MODEL), lambda j, i, s, e: (jnp.minimum((i + 1) * rb, last), 0)),
                  pl.BlockSpec((D_MODEL, tn), lambda j, i, s, e: (0, j)),
                  pl.BlockSpec((SUBLANES, tn), lambda j, i, s, e: (0, j)),
                  pl.BlockSpec((1, tn), lambda j, i, s, e: (0, j))],
        out_specs=pl.BlockSpec((tm, tn), lambda j, i, s, e: (i, j)),
    )
    return pl.pallas_call(
        functools.partial(_proj_conv_kernel, tm=tm),
        grid_spec=grid_spec,
        out_shape=jax.ShapeDtypeStruct((t, n), out_dtype),
        compiler_params=_params(("parallel", "parallel")),
        name=name,
    )(start_flags, end_flags, h, h, h, w, cw_pad, conv_b.reshape(1, n))


def _ssd_kernel(idx_ref, reset_ref, xs_ref, bc_ref, dt_ref, dtt_ref,
                alog_r_ref, alog_c_ref, tril_ref, triu_ref, e01_ref, *rest, rev):
    i = pl.program_id(0)
    nh = SSD_HEADS
    y_ref, state_s = rest[-2:]

    @pl.when(reset_ref[i] == 1)
    def _():
        state_s[...] = jnp.zeros(state_s.shape, F32)

    xs = xs_ref[...]
    bc = bc_ref[...]
    dtn = dt_ref[...]
    a = dtn * (-jnp.exp(alog_r_ref[...]))
    pinc = _dot_left01(tril_ref[...], a)
    pex = pinc - a
    tot = pinc[CHUNK - 1:CHUNK, :]

    gw = HEADS_PER_GROUP * SSD_HEAD_DIM
    gn = SSD_GROUPS * D_STATE
    cd = jnp.broadcast_to(jnp.exp(tot), (SUBLANES, LANES))

    def update_state(sc_st, sc_cd):
        w = (xs * sc_st).astype(BF16)
        for g in range(SSD_GROUPS):
            bg = bc[:, g * D_STATE:(g + 1) * D_STATE]
            new = _dot_tn(bg, w[:, g * gw:(g + 1) * gw])
            state_s[:, g * gw:(g + 1) * gw] = (state_s[:, g * gw:(g + 1) * gw] * sc_cd[:, g * gw:(g + 1) * gw]
                                               + new)

    sc_cd = _spread01(cd, e01_ref[...])[0:1]
    if rev:
        stacked = jnp.concatenate([jnp.exp(pex) * dtn, jnp.exp(tot - pex)], axis=0)
        ex = _dot(stacked.astype(BF16), e01_ref[...])
        sc_off = ex[CHUNK:]
        for g in range(SSD_GROUPS):
            cg = bc[:, gn + g * D_STATE:gn + (g + 1) * D_STATE]
            st = state_s[:, g * gw:(g + 1) * gw]
            y_ref[:, g * gw:(g + 1) * gw] = _dot(cg, st.astype(BF16)) * sc_off[:, g * gw:(g + 1) * gw]
        update_state(ex[:CHUNK], sc_cd)
        return

    yb_ref, dskip_ref = rest[:2]
    dtnt = dtt_ref[...]
    at = dtnt * (-jnp.exp(alog_c_ref[...]))
    pinct = _dot_right01(at, triu_ref[...])
    pext = pinct - at
    li = lax.broadcasted_iota(jnp.int32, (CHUNK, CHUNK), 0)
    si = lax.broadcasted_iota(jnp.int32, (CHUNK, CHUNK), 1)
    lower = si <= li
    strict_lower = si < li
    strict_upper = si > li
    lane = lax.broadcasted_iota(jnp.int32, (CHUNK, LANES), 1)
    first_head = lane < SSD_HEAD_DIM
    xb = xs.astype(BF16)
    for g in range(SSD_GROUPS):
        bg = bc[:, g * D_STATE:(g + 1) * D_STATE]
        cg = bc[:, gn + g * D_STATE:gn + (g + 1) * D_STATE]
        cg_f = cg.astype(F32)
        cb = _dot_nt(cg, bg)
        for hp in range(HEADS_PER_GROUP // 2):
            col = g * gw + hp * LANES
            rhs = jnp.concatenate([xb[:, col:col + LANES], state_s[:, col:col + LANES].astype(BF16)], axis=0)
            pair = []
            for u in range(2):
                h = g * HEADS_PER_GROUP + 2 * hp + u
                colf = jnp.broadcast_to(pinc[:, h:h + 1], (CHUNK, CHUNK))
                arg = jnp.where(lower, colf - pinct[h:h + 1, :],
                                pext[nh + h:nh + h + 1, :] - pex[:, nh + h:nh + h + 1])
                dtf = dtnt[h:h + 1, :]
                dtb = dtnt[nh + h:nh + h + 1, :]
                coef = jnp.where(strict_lower, dtf, jnp.where(strict_upper, dtb, dtf + dtb))
                mh = (cb * jnp.exp(arg) * coef).astype(BF16)
                dh = (cg_f * jnp.exp(colf)).astype(BF16)
                pair.append(_dot(jnp.concatenate([mh, dh], axis=1), rhs))
            y_ref[:, col:col + LANES] = ((jnp.where(first_head, pair[0], pair[1]) + yb_ref[:, col:col + LANES])
                                         + dskip_ref[:, col:col + LANES] * xs[:, col:col + LANES])

    update_state(_dot((jnp.exp(tot - pinc) * dtn).astype(BF16), e01_ref[...]), sc_cd)


def _ssd(xs, bc, dt, dtt, a_log, idx, reset, yb=None, dskip=None):
    rev = yb is None
    t = xs.shape[0]
    nc = t // CHUNK
    alog_r = jnp.pad(a_log.reshape(1, DT_COLS), ((0, 0), (0, LANES - DT_COLS)))
    alog_c = alog_r.reshape(LANES, 1)
    r = jnp.arange(CHUNK)
    tril = (r[None, :] <= r[:, None]).astype(BF16)
    triu = (r[:, None] <= r[None, :]).astype(BF16)
    lo = SSD_HEADS if rev else 0
    e01 = (jnp.arange(D_INNER)[None, :] // SSD_HEAD_DIM == jnp.arange(LANES)[:, None] - lo).astype(BF16)
    const = lambda shape: pl.BlockSpec(shape, lambda i, ix, rs: (0, 0))
    in_specs = [pl.BlockSpec((CHUNK, D_INNER), lambda i, ix, rs: (ix[i], 0)),
                pl.BlockSpec((CHUNK, CONV_CH - D_INNER), lambda i, ix, rs: (ix[i], 0)),
                pl.BlockSpec((CHUNK, LANES), lambda i, ix, rs: (ix[i], 0)),
                pl.BlockSpec((LANES, CHUNK), lambda i, ix, rs: (0, ix[i])),
                const((1, LANES)), const((LANES, 1)),
                const((CHUNK, CHUNK)), const((CHUNK, CHUNK)), const((LANES, D_INNER))]
    args = [idx, reset, xs, bc, dt, dtt, alog_r, alog_c, tril, triu, e01]
    if not rev:
        in_specs += [pl.BlockSpec((CHUNK, D_INNER), lambda i, ix, rs: (ix[i], 0)), const((1, D_INNER))]
        args += [yb, dskip]
    grid_spec = pltpu.PrefetchScalarGridSpec(
        num_scalar_prefetch=2,
        grid=(nc,),
        in_specs=in_specs,
        out_specs=pl.BlockSpec((CHUNK, D_INNER), lambda i, ix, rs: (ix[i], 0)),
        scratch_shapes=[pltpu.VMEM((D_STATE, D_INNER), F32)],
    )
    return pl.pallas_call(
        functools.partial(_ssd_kernel, rev=rev),
        grid_spec=grid_spec,
        out_shape=jax.ShapeDtypeStruct((t, D_INNER), F32),
        compiler_params=_params(("arbitrary",)),
        name="ssd_bwd" if rev else "ssd_fwd",
    )(*args)


def _post_kernel(oa_ref, ob_ref, xa_ref, xb_ref, y_ref, z_ref, gt_ref, mod_ref, subg_ref,
                 ssdg_ref, n2g_ref, wa_ref, ws_ref, wo_ref, wrt_ref, brt_ref, x1_ref, h2b_ref, rt_ref, an_s, yn_s,
                 *, n_first):
    m = mod_ref[0]
    o = _group_pick(oa_ref, ob_ref, n_first)
    for h in range(ATTN_HEADS):
        oh = o[:, h * V_DIM:(h + 1) * V_DIM]
        r = lax.rsqrt(jnp.mean(oh * oh, axis=-1, keepdims=True) + RMS_EPS)
        an_s[:, h * V_DIM:(h + 1) * V_DIM] = (((oh * r) * subg_ref[...]) * (1.0 - LAMBDA_INIT)).astype(BF16)
    attn_d = _dot(an_s[...], wa_ref[...])

    y = y_ref[...] * z_ref[...].astype(F32)
    gw = D_INNER // SSD_GROUPS
    for g in range(SSD_GROUPS):
        yg = y[:, g * gw:(g + 1) * gw]
        r = lax.rsqrt(jnp.mean(yg * yg, axis=-1, keepdims=True) + RMS_EPS)
        yn_s[:, g * gw:(g + 1) * gw] = ((yg * r) * ssdg_ref[:, g * gw:(g + 1) * gw]).astype(BF16)
    ssd_d = _dot(yn_s[...], ws_ref[...])

    gt = gt_ref[...].astype(F32)
    mix = gt[:, :D_MODEL] * attn_d + gt[:, D_MODEL:] * ssd_d
    mixed = _dot(mix.astype(BF16), wo_ref[...])
    x1 = _group_pick(xa_ref, xb_ref, n_first) + m[2:3] * mixed
    x1_ref[...] = x1
    r = lax.rsqrt(jnp.mean(x1 * x1, axis=-1, keepdims=True) + RMS_EPS)
    h2 = ((x1 * r) * n2g_ref[...]) * (1.0 + m[4:5]) + m[3:4]
    h2b_ref[...] = h2.astype(BF16)
    rt_ref[...] = _route(_dot_f32_3(h2, wrt_ref[...]) + brt_ref[...])


def _post(oa, ob, xa, xb, y, z, gates, mod_tiles, subg, ssdg, n2g, wa, ws, wo, w_rt, b_rt, tm):
    t = y.shape[0]
    n_first = oa.shape[0] // tm
    tok = lambda n: pl.BlockSpec((tm, n), lambda i: (i, 0))
    const = lambda a: pl.BlockSpec(a.shape, lambda i: (0, 0))
    return pl.pallas_call(
        functools.partial(_post_kernel, n_first=n_first),
        grid=(t // tm,),
        in_specs=_group_specs(tm, ATTN_WIDTH, n_first) + _group_specs(tm, D_MODEL, n_first) + [
            tok(D_INNER), tok(D_INNER), tok(GATE_COLS),
            pl.BlockSpec((1, 6, D_MODEL), lambda i: (i, 0, 0)),
            const(subg), const(ssdg), const(n2g), const(wa), const(ws), const(wo), const(w_rt), const(b_rt)],
        out_specs=[tok(D_MODEL), tok(D_MODEL), tok(LANES)],
        out_shape=[jax.ShapeDtypeStruct((t, D_MODEL), F32),
                   jax.ShapeDtypeStruct((t, D_MODEL), BF16),
                   jax.ShapeDtypeStruct((t, LANES), F32)],
        scratch_shapes=[pltpu.VMEM((tm, ATTN_WIDTH), BF16), pltpu.VMEM((tm, D_INNER), BF16)],
        compiler_params=_params(("parallel",)),
        name="merge_out_proj",
    )(oa, ob, xa, xb, y, z, gates, mod_tiles, subg, ssdg, n2g, wa, ws, wo, w_rt, b_rt)


def _route(logits):
    tm = logits.shape[0]
    lane = lax.broadcasted_iota(jnp.int32, (tm, LANES), 1)
    lane_f = lane.astype(F32)
    big = float(LANES)
    neg = -jnp.inf
    gl = jnp.where(lane < N_EXPERT_GROUPS, logits, neg)
    gmax = jnp.max(gl, axis=1, keepdims=True)
    g_sel = jnp.min(jnp.where(gl == gmax, lane_f, big), axis=1, keepdims=True)
    g_w = 1.0 / jnp.sum(jnp.exp(gl - gmax), axis=1, keepdims=True)
    e_lane = lane - N_EXPERT_GROUPS
    e_group = (e_lane >> 3).astype(F32)
    in_group = (e_lane >= 0) & (e_lane < N_EXPERTS) & (e_group == g_sel)
    el = jnp.where(in_group, logits, neg)
    v1 = jnp.max(el, axis=1, keepdims=True)
    i1 = jnp.min(jnp.where(el == v1, lane_f, big), axis=1, keepdims=True)
    el2 = jnp.where(lane_f == i1, neg, el)
    v2 = jnp.max(el2, axis=1, keepdims=True)
    i2 = jnp.min(jnp.where(el2 == v2, lane_f, big), axis=1, keepdims=True)
    e2 = jnp.exp(v2 - v1)
    w1 = g_w / (1.0 + e2)
    w2 = g_w * e2 / (1.0 + e2)
    return jnp.where(lane == 0, i1 - N_EXPERT_GROUPS,
                     jnp.where(lane == 1, i2 - N_EXPERT_GROUPS,
                               jnp.where(lane == 2, w1, jnp.where(lane == 3, w2, 0.0))))


def _expert_kernel(be_ref, nu_ref, x_ref, wg_ref, wu_ref, wd_ref, o_ref, wg_s, wu_s, wd_s):
    i = pl.program_id(0)
    used = i < nu_ref[0]
    new_expert = (i == 0) | (be_ref[i] != be_ref[jnp.maximum(i - 1, 0)])

    @pl.when(used & new_expert)
    def _():
        wg_s[...] = wg_ref[0].astype(BF16)
        wu_s[...] = wu_ref[0].astype(BF16)
        wd_s[...] = wd_ref[0].astype(BF16)

    @pl.when(used)
    def _():
        x = x_ref[...]
        a = _silu(_dot(x, wg_s[...])) * _dot(x, wu_s[...])
        o_ref[...] = _dot(a.astype(BF16), wd_s[...]).astype(o_ref.dtype)

    @pl.when(i >= nu_ref[0])
    def _():
        o_ref[...] = jnp.zeros(o_ref.shape, o_ref.dtype)


def _experts(xb, blk_e, n_used, wg, wu, wd):
    cap = xb.shape[0]
    nb = cap // EXPERT_BLOCK
    grid_spec = pltpu.PrefetchScalarGridSpec(
        num_scalar_prefetch=2,
        grid=(nb,),
        in_specs=[pl.BlockSpec((EXPERT_BLOCK, D_MODEL), lambda i, be, nu: (jnp.minimum(i, nu[0] - 1), 0)),
                  pl.BlockSpec((1, D_MODEL, EXPERT_FF), lambda i, be, nu: (be[i], 0, 0)),
                  pl.BlockSpec((1, D_MODEL, EXPERT_FF), lambda i, be, nu: (be[i], 0, 0)),
                  pl.BlockSpec((1, EXPERT_FF, D_MODEL), lambda i, be, nu: (be[i], 0, 0))],
        out_specs=pl.BlockSpec((EXPERT_BLOCK, D_MODEL), lambda i, be, nu: (i, 0)),
        scratch_shapes=[pltpu.VMEM((D_MODEL, EXPERT_FF), BF16), pltpu.VMEM((D_MODEL, EXPERT_FF), BF16),
                        pltpu.VMEM((EXPERT_FF, D_MODEL), BF16)],
    )
    return pl.pallas_call(
        _expert_kernel,
        grid_spec=grid_spec,
        out_shape=jax.ShapeDtypeStruct((cap, D_MODEL), BF16),
        compiler_params=_params(("arbitrary",)),
        name="expert_mlp",
    )(blk_e, n_used, xb, wg, wu, wd)


def _final_kernel(x1_ref, mod_ref, rt_ref, g0_ref, g1_ref, o_ref):
    m = mod_ref[0]
    rt = rt_ref[...]
    moe = g0_ref[...].astype(F32) * rt[:, 2:3] + g1_ref[...].astype(F32) * rt[:, 3:4]
    o_ref[...] = x1_ref[...] + m[5:6] * moe


def _final(x1, mod_tiles, rt, g, tok_off, n_tok, tm):
    b0 = tok_off // tm
    b1 = (g.shape[0] // TOP_K_INNER) // tm
    tok = lambda n: pl.BlockSpec((tm, n), lambda i: (b0 + i, 0))
    return pl.pallas_call(
        _final_kernel,
        grid=(n_tok // tm,),
        in_specs=[tok(D_MODEL), pl.BlockSpec((1, 6, D_MODEL), lambda i: (b0 + i, 0, 0)), tok(LANES),
                  tok(D_MODEL), pl.BlockSpec((tm, D_MODEL), lambda i: (b1 + b0 + i, 0))],
        out_specs=pl.BlockSpec((tm, D_MODEL), lambda i: (i, 0)),
        out_shape=jax.ShapeDtypeStruct((n_tok, D_MODEL), F32),
        compiler_params=_params(("parallel",)),
        name="moe_combine",
    )(x1, mod_tiles, rt, g, g)


ROT = QK_DIM // 2


def _permute_head_cols(w):
    k = w.shape[0]
    return w.reshape(k, ATTN_HEADS, 2, 2, ROT).transpose(0, 1, 3, 2, 4).reshape(k, ATTN_HEADS * LANES)


def _permute_gain(g):
    return jnp.broadcast_to(g.reshape(2, 1, ROT), (2, 2, ROT)).reshape(1, LANES)


def _rope_tables(seq):
    pos = jnp.arange(seq, dtype=F32)
    inv = 1.0 / (ROPE_THETA ** (jnp.arange(0, QK_DIM, 2, dtype=F32) / QK_DIM))
    ang = pos[:, None] * inv[None, :]
    cos, sin = jnp.cos(ang), jnp.sin(ang)
    cos_t = jnp.tile(cos, (1, LANES // ROT))
    sin_t = jnp.concatenate([-sin, -sin, sin, sin], axis=1)
    return cos_t, sin_t


def _dest_kernel(rt_ref, base_ref, tri_ref, o_ref, carry_s):
    @pl.when(pl.program_id(0) == 0)
    def _():
        carry_s[...] = jnp.zeros(carry_s.shape, F32)

    rt = rt_ref[...]
    tm = rt.shape[0]
    lane = lax.broadcasted_iota(jnp.int32, (tm, LANES), 1)
    lane_f = lane.astype(F32)
    oh0 = lane_f == rt[:, 0:1]
    oh1 = lane_f == rt[:, 1:2]
    both = jnp.where(oh0 | oh1, 1.0, 0.0)
    pos = base_ref[...] + carry_s[0:1, :] + _dot(tri_ref[...], both.astype(BF16))
    d0 = jnp.sum(jnp.where(oh0, pos, 0.0), axis=1, keepdims=True)
    d1 = jnp.sum(jnp.where(oh1, pos, 0.0), axis=1, keepdims=True)
    o_ref[...] = jnp.where(lane == 0, d0, jnp.where(lane == 1, d1, 0.0)).astype(jnp.int32)
    carry_s[...] = carry_s[...] + jnp.sum(both, axis=0, keepdims=True)


def _dispatch(rt, n_tok, tm):
    n_slots = n_tok * TOP_K_INNER
    flat_e = rt[:, :TOP_K_INNER].astype(jnp.int32).reshape(-1)
    counts = jnp.sum((flat_e[:, None] == jnp.arange(N_EXPERTS, dtype=jnp.int32)[None, :]).astype(jnp.int32), axis=0)
    padded = ((counts + EXPERT_BLOCK - 1) // EXPERT_BLOCK) * EXPERT_BLOCK
    pad_end = jnp.cumsum(padded)
    pad_start = pad_end - padded
    base = jnp.pad(pad_start.astype(F32), (0, LANES - N_EXPERTS)).reshape(1, LANES)
    r = jnp.arange(tm)
    tri = (r[None, :] < r[:, None]).astype(BF16)
    dest = pl.pallas_call(
        _dest_kernel,
        grid=(n_tok // tm,),
        in_specs=[pl.BlockSpec((tm, LANES), lambda i: (i, 0)),
                  pl.BlockSpec((1, LANES), lambda i: (0, 0)),
                  pl.BlockSpec((tm, tm), lambda i: (0, 0))],
        out_specs=pl.BlockSpec((tm, LANES), lambda i: (i, 0)),
        out_shape=jax.ShapeDtypeStruct((n_tok, LANES), jnp.int32),
        scratch_shapes=[pltpu.VMEM((SUBLANES, LANES), F32)],
        compiler_params=_params(("arbitrary",)),
        name="dispatch_rows",
    )(rt, base, tri)[:, :TOP_K_INNER]
    cap = n_slots + N_EXPERTS * EXPERT_BLOCK
    nb = cap // EXPERT_BLOCK
    blk_row0 = jnp.arange(nb, dtype=jnp.int32) * EXPERT_BLOCK
    blk_e = jnp.minimum(jnp.sum((pad_end[None, :] <= blk_row0[:, None]).astype(jnp.int32), axis=1), N_EXPERTS - 1)
    n_used = (pad_end[-1] // EXPERT_BLOCK).astype(jnp.int32).reshape(1)

    n_pad = cap - n_slots
    seg_len = jnp.concatenate([padded - counts, (cap - pad_end[-1]).reshape(1)])
    seg_end = jnp.cumsum(seg_len)
    seg_start = seg_end - seg_len
    seg_row0 = jnp.concatenate([pad_start + counts, pad_end[-1:]])
    j = jnp.arange(n_pad, dtype=jnp.int32)
    seg_onehot = (jnp.sum((seg_end[None, :] <= j[:, None]).astype(jnp.int32), axis=1)[:, None]
                  == jnp.arange(N_EXPERTS + 1, dtype=jnp.int32)[None, :]).astype(jnp.int32)
    pad_rows = j + jnp.sum(seg_onehot * (seg_row0 - seg_start)[None, :], axis=1)
    rows = jnp.concatenate([dest.reshape(-1), pad_rows.astype(jnp.int32)])
    toks = jnp.concatenate([jnp.arange(n_slots, dtype=jnp.int32) // TOP_K_INNER, j % n_tok])
    _, tok_buf = lax.sort_key_val(rows, toks)
    return dest, tok_buf, blk_e.astype(jnp.int32), n_used


def kernel(x_prompt, x_sample, c_prompt, c_sample, w_ada, b_ada, norm1_g, w_in, q_norm_g, k_norm_g, lambda_q1, lambda_k1, lambda_q2, lambda_k2, attn_subln_g, w_attn_o, conv_w, conv_b, dt_bias, a_log, d_skip, ssd_norm_g, w_ssd_o, w_out, norm2_g, w_group, b_group, w_router, b_router, w_gate_e, w_up_e, w_down_e):
    groups = [(x_prompt, c_prompt), (x_sample, c_sample)]
    seqs = [(x.shape[0], x.shape[1]) for x, _ in groups]
    n_tok = sum(b * s for b, s in seqs)
    min_seq = min(s for _, s in seqs)
    tm = min(1024, min_seq)
    tp = min(256, min_seq)
    tq = min(512, min_seq)
    tk = min(512, min_seq)
    layer = 0

    xa, xb_in = (g[0].reshape(-1, D_MODEL) for g in groups)
    c = jnp.concatenate([g[1] for g in groups], axis=0)
    n_batch = c.shape[0]
    c_pad = jnp.pad(c, ((0, (-n_batch) % SUBLANES), (0, 0)))
    mod = _ada(c_pad, w_ada[layer], b_ada[layer]).reshape(-1, 6, D_MODEL)
    tok_batch = jnp.concatenate([jnp.repeat(jnp.arange(b, dtype=jnp.int32), s) + off
                                 for (b, s), off in zip(seqs, [0, seqs[0][0]])])
    mod_tm = mod[tok_batch[::tm]]
    mod_tp = mod[tok_batch[::tp]]

    h = _norm_mod(xa, xb_in, mod_tm, norm1_g[layer], tm)

    w_in_b = w_in[layer].astype(BF16)
    tabs = [_rope_tables(s) for _, s in seqs]
    cos_t = jnp.concatenate([jnp.tile(tb[0], (b, 1)) for tb, (b, _) in zip(tabs, seqs)], axis=0)
    sin_t = jnp.concatenate([jnp.tile(tb[1], (b, 1)) for tb, (b, _) in zip(tabs, seqs)], axis=0)
    half = (jnp.arange(LANES) // ROT) % 2
    bd = (half[:, None] == half[None, :]).astype(BF16)
    gq = _permute_gain(q_norm_g[layer]) * (QK_DIM ** -0.5 * LOG2_E)
    gk = _permute_gain(k_norm_g[layer])
    q = _qk_proj(h, _permute_head_cols(w_in_b[:, OFF_Q:OFF_K]), gq, cos_t, sin_t, bd, tm, "q_proj")
    k = _qk_proj(h, _permute_head_cols(w_in_b[:, OFF_K:OFF_V]), gk, cos_t, sin_t, bd, tm, "k_proj")
    vt3 = _vt_proj(h, w_in_b[:, OFF_V:OFF_Z], tm, tk)
    z_act = _matmul(h, w_in_b[:, OFF_Z:OFF_XBC], BF16, tm, 1024, "z_proj", _silu)
    w_dt =jnp.pad(w_in_b[:, OFF_DT:OFF_GATE], ((0, 0), (0, LANES - DT_COLS)))
    dt_b = jnp.pad(dt_bias[layer].reshape(1, DT_COLS), ((0, 0), (0, LANES - DT_COLS)))
    dt = _matmul(h, w_dt, F32, tm, LANES, "dt_proj", _softplus, dt_b)
    gates = _matmul(h, w_in_b[:, OFF_GATE:], BF16, tm, 1024, "gate_proj", _sigmoid)

    k3 = k.reshape(n_tok // tk, tk, ATTN_WIDTH)
    lams = [v[layer].reshape(1, QK_DIM) for v in (lambda_q1, lambda_k1, lambda_q2, lambda_k2)]
    o_groups = []
    off = 0
    for b, s in seqs:
        o_groups.append(_attention(q, k3, vt3, lams, off, b, s, tq, tk))
        off += b * s

    seq_starts = []
    off = 0
    for b, s in seqs:
        seq_starts += [(off + i * s, s) for i in range(b)]
        off += b * s
    tile_start = jnp.zeros((n_tok // tm,), jnp.int32)
    tile_end = jnp.zeros((n_tok // tm,), jnp.int32)
    nc = n_tok // CHUNK
    chunk_reset = jnp.zeros((nc,), jnp.int32)
    bwd_idx = jnp.zeros((nc,), jnp.int32)
    for st, s in seq_starts:
        tile_start = tile_start.at[st // tm].set(1)
        tile_end = tile_end.at[(st + s) // tm - 1].set(1)
        c0, c1 = st // CHUNK, (st + s) // CHUNK
        chunk_reset = chunk_reset.at[c0].set(1)
        bwd_idx = bwd_idx.at[c0:c1].set(jnp.arange(c1 - 1, c0 - 1, -1, dtype=jnp.int32))
    fwd_idx = jnp.arange(nc, dtype=jnp.int32)
    w_x, w_bc = w_in_b[:, OFF_XBC:OFF_XBC + D_INNER], w_in_b[:, OFF_XBC + D_INNER:OFF_DT]
    cw, cbias = conv_w[layer], conv_b[layer]
    xs = _proj_conv(h, w_x, cw[:, :D_INNER], cbias[:D_INNER], tile_start, tile_end, F32, tm, 1024, "x_proj_conv")
    bcm = _proj_conv(h, w_bc, cw[:, D_INNER:], cbias[D_INNER:], tile_start, tile_end, BF16, tm, 1024,
                     "bc_proj_conv")
    dtt = dt.T
    alg = a_log[layer].reshape(-1)
    dskip = jnp.repeat(d_skip[layer], SSD_HEAD_DIM).reshape(1, D_INNER)
    yb = _ssd(xs, bcm, dt, dtt, alg, bwd_idx, chunk_reset)
    y_ssd = _ssd(xs, bcm, dt, dtt, alg, fwd_idx, chunk_reset, yb, dskip)

    subg = attn_subln_g[layer].reshape(1, V_DIM)
    n_rt = N_EXPERT_GROUPS + N_EXPERTS
    w_rt = jnp.pad(jnp.concatenate([w_group[layer], w_router[layer]], axis=1), ((0, 0), (0, LANES - n_rt)))
    b_rt = jnp.pad(jnp.concatenate([b_group[layer], b_router[layer]]), (0, LANES - n_rt)).reshape(1, LANES)
    x1, h2b, rt = _post(o_groups[0], o_groups[1], xa, xb_in, y_ssd, z_act, gates, mod_tp, subg,
                        ssd_norm_g[layer].reshape(1, D_INNER), norm2_g[layer].reshape(1, D_MODEL),
                        w_attn_o[layer].astype(BF16), w_ssd_o[layer].astype(BF16),
                        w_out[layer].astype(BF16), w_rt, b_rt, tp)

    dest, tok_buf, blk_e, n_used = _dispatch(rt, n_tok, min(512, min_seq))
    xb = h2b[tok_buf]
    yb_e = _experts(xb, blk_e, n_used, w_gate_e[layer], w_up_e[layer], w_down_e[layer])
    g = yb_e[dest.T.reshape(-1)]

    outs = []
    off = 0
    for (b, s), (xg, _) in zip(seqs, groups):
        y = _final(x1, mod_tp, rt, g, off, b * s, tp)
        outs.append(y.reshape(xg.shape))
        off += b * s
    return tuple(outs)
```

```python
import functools
import math

import jax
import jax.numpy as jnp
from jax import lax
from jax.experimental import pallas as pl
from jax.experimental.pallas import tpu as pltpu

F32 = jnp.float32
BF16 = jnp.bfloat16

D_MODEL = 1024
ATTN_HEADS = 8
QK_DIM = 64
V_DIM = 2 * QK_DIM
ATTN_WIDTH = ATTN_HEADS * V_DIM
ROPE_THETA = 10000.0
D_INNER = 2048
SSD_HEAD_DIM = 64
SSD_HEADS = D_INNER // SSD_HEAD_DIM
SSD_GROUPS = 4
HEADS_PER_GROUP = SSD_HEADS // SSD_GROUPS
D_STATE = 128
CONV_K = 5
CONV_CH = D_INNER + 2 * SSD_GROUPS * D_STATE
CHUNK = 128
N_EXPERT_GROUPS = 4
EXPERTS_PER_GROUP = 8
N_EXPERTS = N_EXPERT_GROUPS * EXPERTS_PER_GROUP
TOP_K_INNER = 2
EXPERT_FF = 512
RMS_EPS = 1e-6
LAMBDA_INIT = 0.8 - 0.6 * math.exp(-0.3 * 0)

LANES = 128
SUBLANES = 8
VMEM_LIMIT = 56 * 1024 * 1024

Q_COLS = ATTN_HEADS * 2 * QK_DIM
K_COLS = Q_COLS
V_COLS = ATTN_WIDTH
Z_COLS = D_INNER
XBC_COLS = CONV_CH
DT_COLS = 2 * SSD_HEADS
GATE_COLS = 2 * D_MODEL
OFF_Q = 0
OFF_K = OFF_Q + Q_COLS
OFF_V = OFF_K + K_COLS
OFF_Z = OFF_V + V_COLS
OFF_XBC = OFF_Z + Z_COLS
OFF_DT = OFF_XBC + XBC_COLS
OFF_GATE = OFF_DT + DT_COLS

EXPERT_BLOCK = 512
ONES_ROWS = 16
LOG2_E = math.log2(math.e)


def _params(sem):
    return pltpu.CompilerParams(dimension_semantics=sem, vmem_limit_bytes=VMEM_LIMIT)


def _dot(a, b):
    return jnp.dot(a, b, preferred_element_type=F32)


def _dot_tn(a, b):
    return lax.dot_general(a, b, (((0,), (0,)), ((), ())), preferred_element_type=F32)


def _dot_nt(a, b):
    return lax.dot_general(a, b, (((1,), (1,)), ((), ())), preferred_element_type=F32)


def _split3(a):
    hi = a.astype(BF16)
    r = a - hi.astype(F32)
    mid = r.astype(BF16)
    lo = (r - mid.astype(F32)).astype(BF16)
    return hi, mid, lo


def _dot_left01(m01, a):
    hi, mid, lo = _split3(a)
    return _dot(m01, hi) + _dot(m01, mid) + _dot(m01, lo)


def _dot_right01(a, m01):
    hi, mid, lo = _split3(a)
    return _dot(hi, m01) + _dot(mid, m01) + _dot(lo, m01)


def _spread01(a, m01):
    hi = a.astype(BF16)
    lo = (a - hi.astype(F32)).astype(BF16)
    return _dot(hi, m01) + _dot(lo, m01)


def _dot_f32(a, b):
    a0, a1, a2 = _split3(a)
    b0, b1, b2 = _split3(b)
    return (_dot(a0, b0) + (_dot(a0, b1) + _dot(a1, b0))
            + (_dot(a0, b2) + _dot(a2, b0) + _dot(a1, b1)))


def _dot_f32_3(a, b):
    a0 = a.astype(BF16)
    a1 = (a - a0.astype(F32)).astype(BF16)
    b0 = b.astype(BF16)
    b1 = (b - b0.astype(F32)).astype(BF16)
    return _dot(a0, b0) + (_dot(a0, b1) + _dot(a1, b0))


def _sigmoid(x):
    return 1.0 / (1.0 + jnp.exp(-x))


def _silu(x):
    return x * _sigmoid(x)


def _softplus(x):
    e = jnp.exp(-jnp.abs(x))
    u = 1.0 + e
    log1p_e = jnp.where(u == 1.0, e, jnp.log(u) * (e / (u - 1.0)))
    return jnp.maximum(x, 0.0) + log1p_e


def _ada_kernel(c_ref, w_ref, b_ref, o_ref):
    o_ref[...] = _dot_f32(_silu(c_ref[...]), w_ref[...]) + b_ref[...]


def _ada(c_pad, w_ada, b_ada):
    rows = c_pad.shape[0]
    n = w_ada.shape[1]
    tn = 1024
    return pl.pallas_call(
        _ada_kernel,
        grid=(n // tn,),
        in_specs=[pl.BlockSpec((rows, D_MODEL), lambda j: (0, 0)),
                  pl.BlockSpec((D_MODEL, tn), lambda j: (0, j)),
                  pl.BlockSpec((1, tn), lambda j: (0, j))],
        out_specs=pl.BlockSpec((rows, tn), lambda j: (0, j)),
        out_shape=jax.ShapeDtypeStruct((rows, n), F32),
        compiler_params=_params(("arbitrary",)),
        name="ada_mod",
    )(c_pad, w_ada, b_ada.reshape(1, n))


def _group_specs(tm, n, n_first):
    return [pl.BlockSpec((tm, n), lambda i: (jnp.minimum(i, n_first - 1), 0)),
            pl.BlockSpec((tm, n), lambda i: (jnp.maximum(i - n_first, 0), 0))]


def _group_pick(a_ref, b_ref, n_first):
    return jnp.where(pl.program_id(0) < n_first, a_ref[...], b_ref[...])


def _mm_kernel(a_ref, w_ref, *rest, act):
    o_ref = rest[-1]
    acc = _dot(a_ref[...], w_ref[...])
    if len(rest) == 2:
        acc = acc + rest[0][...]
    if act is not None:
        acc = act(acc)
    o_ref[...] = acc.astype(o_ref.dtype)


def _matmul(a, w, out_dtype, tm, tn, name, act=None, bias=None):
    m, k = a.shape
    n = w.shape[1]
    in_specs = [pl.BlockSpec((tm, k), lambda j, i: (i, 0)),
                pl.BlockSpec((k, tn), lambda j, i: (0, j))]
    args = [a, w]
    if bias is not None:
        in_specs.append(pl.BlockSpec((1, tn), lambda j, i: (0, j)))
        args.append(bias)
    return pl.pallas_call(
        functools.partial(_mm_kernel, act=act),
        grid=(n // tn, m // tm),
        in_specs=in_specs,
        out_specs=pl.BlockSpec((tm, tn), lambda j, i: (i, j)),
        out_shape=jax.ShapeDtypeStruct((m, n), out_dtype),
        compiler_params=_params(("parallel", "parallel")),
        name=name,
    )(*args)


def _qk_epilogue(acc, g_ref, cos_ref, sin_ref, bd_ref, o_ref):
    cos = cos_ref[...]
    sin = sin_ref[...]
    g = g_ref[...]
    bd = bd_ref[...]
    for h in range(ATTN_HEADS):
        x = acc[:, h * LANES:(h + 1) * LANES]
        sq = x * x
        hi = sq.astype(BF16)
        lo = (sq - hi.astype(F32)).astype(BF16)
        ss = _dot(hi, bd) + _dot(lo, bd)
        r = lax.rsqrt(ss * (1.0 / QK_DIM) + RMS_EPS)
        xn = (x * r) * g
        partner = pltpu.roll(xn, LANES // 2, 1)
        o_ref[:, h * LANES:(h + 1) * LANES] = (xn * cos + partner * sin).astype(o_ref.dtype)


def _qk_kernel(a_ref, w_ref, g_ref, cos_ref, sin_ref, bd_ref, o_ref):
    _qk_epilogue(_dot(a_ref[...], w_ref[...]), g_ref, cos_ref, sin_ref, bd_ref, o_ref)


def _norm_q_kernel(xa_ref, xb_ref, mod_ref, g1_ref, w_ref, g_ref, cos_ref, sin_ref, bd_ref, o_ref, h_ref,
                   *, n_first):
    x = _group_pick(xa_ref, xb_ref, n_first)
    r = lax.rsqrt(jnp.mean(x * x, axis=-1, keepdims=True) + RMS_EPS)
    m = mod_ref[0]
    h = (((x * r) * g1_ref[...]) * (1.0 + m[1:2]) + m[0:1]).astype(BF16)
    h_ref[...] = h
    _qk_epilogue(_dot(h, w_ref[...]), g_ref, cos_ref, sin_ref, bd_ref, o_ref)


def _qk_specs(tm, n):
    return [pl.BlockSpec((D_MODEL, n), lambda i: (0, 0)),
            pl.BlockSpec((1, LANES), lambda i: (0, 0)),
            pl.BlockSpec((tm, LANES), lambda i: (i, 0)),
            pl.BlockSpec((tm, LANES), lambda i: (i, 0)),
            pl.BlockSpec((LANES, LANES), lambda i: (0, 0))]


def _qk_proj(h, w, g128, cos_t, sin_t, bd, tm, name):
    t = h.shape[0]
    n = w.shape[1]
    return pl.pallas_call(
        _qk_kernel,
        grid=(t // tm,),
        in_specs=[pl.BlockSpec((tm, D_MODEL), lambda i: (i, 0))] + _qk_specs(tm, n),
        out_specs=pl.BlockSpec((tm, n), lambda i: (i, 0)),
        out_shape=jax.ShapeDtypeStruct((t, n), BF16),
        compiler_params=_params(("parallel",)),
        name=name,
    )(h, w, g128, cos_t, sin_t, bd)


def _norm_q_proj(xa, xb, mod_tiles, g1, w, g128, cos_t, sin_t, bd, tm):
    t = xa.shape[0] + xb.shape[0]
    n = w.shape[1]
    n_first = xa.shape[0] // tm
    tok = pl.BlockSpec((tm, n), lambda i: (i, 0))
    return pl.pallas_call(
        functools.partial(_norm_q_kernel, n_first=n_first),
        grid=(t // tm,),
        in_specs=_group_specs(tm, D_MODEL, n_first) + [
            pl.BlockSpec((1, 6, D_MODEL), lambda i: (i, 0, 0)),
            pl.BlockSpec((1, D_MODEL), lambda i: (0, 0))] + _qk_specs(tm, n),
        out_specs=[tok, pl.BlockSpec((tm, D_MODEL), lambda i: (i, 0))],
        out_shape=[jax.ShapeDtypeStruct((t, n), BF16), jax.ShapeDtypeStruct((t, D_MODEL), BF16)],
        compiler_params=_params(("parallel",)),
        name="norm1_q_proj",
    )(xa, xb, mod_tiles, g1.reshape(1, D_MODEL), w, g128, cos_t, sin_t, bd)


def _vt_kernel(a_ref, w_ref, o_ref, *, tk):
    acc = _dot(a_ref[...], w_ref[...])
    for c in range(acc.shape[0] // tk):
        o_ref[c] = acc[c * tk:(c + 1) * tk, :].T.astype(o_ref.dtype)


def _vt_proj(h, w, tm, tk):
    t = h.shape[0]
    n = w.shape[1]
    return pl.pallas_call(
        functools.partial(_vt_kernel, tk=tk),
        grid=(t // tm,),
        in_specs=[pl.BlockSpec((tm, D_MODEL), lambda i: (i, 0)),
                  pl.BlockSpec((D_MODEL, n), lambda i: (0, 0))],
        out_specs=pl.BlockSpec((tm // tk, n, tk), lambda i: (i, 0, 0)),
        out_shape=jax.ShapeDtypeStruct((t // tk, n, tk), BF16),
        compiler_params=_params(("parallel",)),
        name="v_proj_t",
    )(h, w)


def _attn_kernel(trips_ref, lq1_ref, lk1_ref, lq2_ref, lk2_ref, q_ref, k_ref, vt_ref, o_ref,
                 q2t_s, s_s, m_s, acc_s, *, tq, nkv, unroll):
    tk = k_ref.shape[1]
    qt = q_ref[...].astype(F32).T
    row = lax.broadcasted_iota(jnp.int32, qt.shape, 0)
    first_half = (row & ROT) == 0
    q2t_s[:, :tq] = jnp.where(first_half, qt, 0.0).astype(BF16)
    q2t_s[:, tq:] = jnp.where(first_half, 0.0, qt).astype(BF16)
    m_s[...] = jnp.full(m_s.shape, -jnp.inf, F32)
    acc_s[...] = jnp.zeros(acc_s.shape, F32)
    ones = jnp.ones((ONES_ROWS, tk), BF16)
    s_s[0] = _dot(k_ref[0], q2t_s[...])

    def step(j, par, compute_next):
        if compute_next:
            s_s[1 - par] = _dot(k_ref[j + 1], q2t_s[...])
        st = s_s[par]
        m_old = m_s[...]
        m_new = jnp.maximum(m_old, jnp.max(st, axis=0, keepdims=True))
        alpha = jnp.exp2(m_old - m_new)
        p = jnp.exp2(st - m_new).astype(BF16)
        lhs = jnp.concatenate([vt_ref[j], ones], axis=0)
        acc_s[...] = alpha * acc_s[...] + _dot(lhs, p)
        m_s[...] = m_new

    n_trips = (nkv - 1) // unroll

    def trip(t, carry):
        for u in range(unroll):
            step(unroll * t + u, u % 2, True)
        return carry

    lax.fori_loop(0, trips_ref[0], trip, 0)
    for j in range(unroll * n_trips, nkv):
        step(j, j % 2, j + 1 < nkv)

    lam = (jnp.exp(jnp.sum(lq1_ref[...] * lk1_ref[...], axis=1, keepdims=True))
           - jnp.exp(jnp.sum(lq2_ref[...] * lk2_ref[...], axis=1, keepdims=True)) + LAMBDA_INIT)
    ot = acc_s[:V_DIM, :] / acc_s[V_DIM:V_DIM + 1, :]
    o_ref[...] = (ot[:, :tq] - lam * ot[:, tq:]).T


def _attention(q, k3, vt3, lams, tok_off, batch, seq, tq, tk):
    nkv = seq // tk
    qb0 = tok_off // tq
    sb0 = tok_off // seq
    nq = seq // tq
    unroll = 4 if nkv >= 8 else 2
    n_trips = jnp.full((1,), (nkv - 1) // unroll, jnp.int32)
    lam_spec = pl.BlockSpec((1, QK_DIM), lambda b, h, i, nt: (0, 0))
    grid_spec = pltpu.PrefetchScalarGridSpec(
        num_scalar_prefetch=1,
        grid=(batch, ATTN_HEADS, nq),
        in_specs=[lam_spec, lam_spec, lam_spec, lam_spec,
                  pl.BlockSpec((tq, LANES), lambda b, h, i, nt: (qb0 + b * nq + i, h)),
                  pl.BlockSpec((nkv, tk, LANES), lambda b, h, i, nt: (sb0 + b, 0, h)),
                  pl.BlockSpec((nkv, LANES, tk), lambda b, h, i, nt: (sb0 + b, h, 0))],
        out_specs=pl.BlockSpec((tq, LANES), lambda b, h, i, nt: (b * nq + i, h)),
        scratch_shapes=[pltpu.VMEM((LANES, 2 * tq), BF16),
                        pltpu.VMEM((2, tk, 2 * tq), F32),
                        pltpu.VMEM((1, 2 * tq), F32),
                        pltpu.VMEM((V_DIM + ONES_ROWS, 2 * tq), F32)],
    )
    return pl.pallas_call(
        functools.partial(_attn_kernel, tq=tq, nkv=nkv, unroll=unroll),
        grid_spec=grid_spec,
        out_shape=jax.ShapeDtypeStruct((batch * seq, ATTN_WIDTH), F32),
        compiler_params=_params(("parallel", "parallel", "arbitrary")),
        name="diff_attention",
    )(n_trips, *lams, q, k3, vt3)


HALO = 16
MXU_COLS = 256


def _proj_conv_kernel(start_ref, end_ref, prev_ref, cur_ref, next_ref, w_ref, cw_ref, cb_ref, o_ref, *, tm):
    i = pl.program_id(1)
    prev = jnp.where(start_ref[i] == 1, jnp.zeros_like(prev_ref[...]), prev_ref[...])
    nxt = jnp.where(end_ref[i] == 1, jnp.zeros_like(next_ref[...]), next_ref[...])
    a = jnp.concatenate([prev, cur_ref[...], nxt], axis=0)
    pad = (CONV_K - 1) // 2
    for c in range(0, o_ref.shape[1], MXU_COLS):
        ext = _dot(a, w_ref[:, c:c + MXU_COLS])
        acc = jnp.broadcast_to(cb_ref[:, c:c + MXU_COLS], (tm, MXU_COLS))
        for d in range(CONV_K):
            shift = (pad - d) % (tm + 2 * HALO)
            src = ext if shift == 0 else pltpu.roll(ext, shift, 0)
            acc = acc + cw_ref[d:d + 1, c:c + MXU_COLS] * src[HALO:HALO + tm, :]
        o_ref[:, c:c + MXU_COLS] = _silu(acc).astype(o_ref.dtype)


def _proj_conv(h, w, conv_w, conv_b, start_flags, end_flags, out_dtype, tm, tn, name):
    t = h.shape[0]
    n = w.shape[1]
    rb = tm // HALO
    last = t // HALO - 1
    cw_pad = jnp.zeros((SUBLANES, n), F32).at[:CONV_K].set(conv_w)
    grid_spec = pltpu.PrefetchScalarGridSpec(
        num_scalar_prefetch=2,
        grid=(n // tn, t // tm),
        in_specs=[pl.BlockSpec((HALO, D_MODEL), lambda j, i, s, e: (jnp.maximum(i * rb - 1, 0), 0)),
                  pl.BlockSpec((tm, D_MODEL), lambda j, i, s, e: (i, 0)),
                  pl.BlockSpec((HALO, D_MODEL), lambda j, i, s, e: (jnp.minimum((i + 1) * rb, last), 0)),
                  pl.BlockSpec((D_MODEL, tn), lambda j, i, s, e: (0, j)),
                  pl.BlockSpec((SUBLANES, tn), lambda j, i, s, e: (0, j)),
                  pl.BlockSpec((1, tn), lambda j, i, s, e: (0, j))],
        out_specs=pl.BlockSpec((tm, tn), lambda j, i, s, e: (i, j)),
    )
    return pl.pallas_call(
        functools.partial(_proj_conv_kernel, tm=tm),
        grid_spec=grid_spec,
        out_shape=jax.ShapeDtypeStruct((t, n), out_dtype),
        compiler_params=_params(("parallel", "parallel")),
        name=name,
    )(start_flags, end_flags, h, h, h, w, cw_pad, conv_b.reshape(1, n))


def _ssd_kernel(idx_ref, reset_ref, xs_ref, bc_ref, dt_ref, dtt_ref,
                alog_r_ref, alog_c_ref, tril_ref, triu_ref, e01_ref, *rest, rev):
    i = pl.program_id(0)
    nh = SSD_HEADS
    y_ref, state_s = rest[-2:]

    @pl.when(reset_ref[i] == 1)
    def _():
        state_s[...] = jnp.zeros(state_s.shape, F32)

    xs = xs_ref[...]
    bc = bc_ref[...]
    dtn = dt_ref[...]
    a = dtn * (-jnp.exp(alog_r_ref[...]))
    pinc = _dot_left01(tril_ref[...], a)
    pex = pinc - a
    tot = pinc[CHUNK - 1:CHUNK, :]

    gw = HEADS_PER_GROUP * SSD_HEAD_DIM
    gn = SSD_GROUPS * D_STATE
    cd = jnp.broadcast_to(jnp.exp(tot), (SUBLANES, LANES))

    def update_state(sc_st, sc_cd):
        w = (xs * sc_st).astype(BF16)
        for g in range(SSD_GROUPS):
            bg = bc[:, g * D_STATE:(g + 1) * D_STATE]
            new = _dot_tn(bg, w[:, g * gw:(g + 1) * gw])
            state_s[:, g * gw:(g + 1) * gw] = (state_s[:, g * gw:(g + 1) * gw] * sc_cd[:, g * gw:(g + 1) * gw]
                                               + new)

    sc_cd = _spread01(cd, e01_ref[...])[0:1]
    if rev:
        stacked = jnp.concatenate([jnp.exp(pex) * dtn, jnp.exp(tot - pex)], axis=0)
        ex = _dot(stacked.astype(BF16), e01_ref[...])
        sc_off = ex[CHUNK:]
        for g in range(SSD_GROUPS):
            cg = bc[:, gn + g * D_STATE:gn + (g + 1) * D_STATE]
            st = state_s[:, g * gw:(g + 1) * gw]
            y_ref[:, g * gw:(g + 1) * gw] = _dot(cg, st.astype(BF16)) * sc_off[:, g * gw:(g + 1) * gw]
        update_state(ex[:CHUNK], sc_cd)
        return

    yb_ref, dskip_ref = rest[:2]
    dtnt = dtt_ref[...]
    at = dtnt * (-jnp.exp(alog_c_ref[...]))
    pinct = _dot_right01(at, triu_ref[...])
    pext = pinct - at
    li = lax.broadcasted_iota(jnp.int32, (CHUNK, CHUNK), 0)
    si = lax.broadcasted_iota(jnp.int32, (CHUNK, CHUNK), 1)
    lower = si <= li
    strict_lower = si < li
    strict_upper = si > li
    lane = lax.broadcasted_iota(jnp.int32, (CHUNK, LANES), 1)
    first_head = lane < SSD_HEAD_DIM
    xb = xs.astype(BF16)
    for g in range(SSD_GROUPS):
        bg = bc[:, g * D_STATE:(g + 1) * D_STATE]
        cg = bc[:, gn + g * D_STATE:gn + (g + 1) * D_STATE]
        cg_f = cg.astype(F32)
        cb = _dot_nt(cg, bg)
        for hp in range(HEADS_PER_GROUP // 2):
            col = g * gw + hp * LANES
            rhs = jnp.concatenate([xb[:, col:col + LANES], state_s[:, col:col + LANES].astype(BF16)], axis=0)
            pair = []
            for u in range(2):
                h = g * HEADS_PER_GROUP + 2 * hp + u
                colf = jnp.broadcast_to(pinc[:, h:h + 1], (CHUNK, CHUNK))
                arg = jnp.where(lower, colf - pinct[h:h + 1, :],
                                pext[nh + h:nh + h + 1, :] - pex[:, nh + h:nh + h + 1])
                dtf = dtnt[h:h + 1, :]
                dtb = dtnt[nh + h:nh + h + 1, :]
                coef = jnp.where(strict_lower, dtf, jnp.where(strict_upper, dtb, dtf + dtb))
                mh = (cb * jnp.exp(arg) * coef).astype(BF16)
                dh = (cg_f * jnp.exp(colf)).astype(BF16)
                pair.append(_dot(jnp.concatenate([mh, dh], axis=1), rhs))
            y_ref[:, col:col + LANES] = ((jnp.where(first_head, pair[0], pair[1]) + yb_ref[:, col:col + LANES])
                                         + dskip_ref[:, col:col + LANES] * xs[:, col:col + LANES])

    update_state(_dot((jnp.exp(tot - pinc) * dtn).astype(BF16), e01_ref[...]), sc_cd)


def _ssd(xs, bc, dt, dtt, a_log, idx, reset, yb=None, dskip=None):
    rev = yb is None
    t = xs.shape[0]
    nc = t // CHUNK
    alog_r = jnp.pad(a_log.reshape(1, DT_COLS), ((0, 0), (0, LANES - DT_COLS)))
    alog_c = alog_r.reshape(LANES, 1)
    r = jnp.arange(CHUNK)
    tril = (r[None, :] <= r[:, None]).astype(BF16)
    triu = (r[:, None] <= r[None, :]).astype(BF16)
    lo = SSD_HEADS if rev else 0
    e01 = (jnp.arange(D_INNER)[None, :] // SSD_HEAD_DIM == jnp.arange(LANES)[:, None] - lo).astype(BF16)
    const = lambda shape: pl.BlockSpec(shape, lambda i, ix, rs: (0, 0))
    in_specs = [pl.BlockSpec((CHUNK, D_INNER), lambda i, ix, rs: (ix[i], 0)),
                pl.BlockSpec((CHUNK, CONV_CH - D_INNER), lambda i, ix, rs: (ix[i], 0)),
                pl.BlockSpec((CHUNK, LANES), lambda i, ix, rs: (ix[i], 0)),
                pl.BlockSpec((LANES, CHUNK), lambda i, ix, rs: (0, ix[i])),
                const((1, LANES)), const((LANES, 1)),
                const((CHUNK, CHUNK)), const((CHUNK, CHUNK)), const((LANES, D_INNER))]
    args = [idx, reset, xs, bc, dt, dtt, alog_r, alog_c, tril, triu, e01]
    if not rev:
        in_specs += [pl.BlockSpec((CHUNK, D_INNER), lambda i, ix, rs: (ix[i], 0)), const((1, D_INNER))]
        args += [yb, dskip]
    grid_spec = pltpu.PrefetchScalarGridSpec(
        num_scalar_prefetch=2,
        grid=(nc,),
        in_specs=in_specs,
        out_specs=pl.BlockSpec((CHUNK, D_INNER), lambda i, ix, rs: (ix[i], 0)),
        scratch_shapes=[pltpu.VMEM((D_STATE, D_INNER), F32)],
    )
    return pl.pallas_call(
        functools.partial(_ssd_kernel, rev=rev),
        grid_spec=grid_spec,
        out_shape=jax.ShapeDtypeStruct((t, D_INNER), F32),
        compiler_params=_params(("arbitrary",)),
        name="ssd_bwd" if rev else "ssd_fwd",
    )(*args)


def _post_kernel(oa_ref, ob_ref, xa_ref, xb_ref, y_ref, z_ref, gt_ref, mod_ref, subg_ref,
                 ssdg_ref, n2g_ref, wa_ref, ws_ref, wo_ref, wrt_ref, brt_ref, x1_ref, h2b_ref, rt_ref, an_s, yn_s,
                 *, n_first):
    m = mod_ref[0]
    o = _group_pick(oa_ref, ob_ref, n_first)
    for h in range(ATTN_HEADS):
        oh = o[:, h * V_DIM:(h + 1) * V_DIM]
        r = lax.rsqrt(jnp.mean(oh * oh, axis=-1, keepdims=True) + RMS_EPS)
        an_s[:, h * V_DIM:(h + 1) * V_DIM] = (((oh * r) * subg_ref[...]) * (1.0 - LAMBDA_INIT)).astype(BF16)
    attn_d = _dot(an_s[...], wa_ref[...])

    y = y_ref[...] * z_ref[...].astype(F32)
    gw = D_INNER // SSD_GROUPS
    for g in range(SSD_GROUPS):
        yg = y[:, g * gw:(g + 1) * gw]
        r = lax.rsqrt(jnp.mean(yg * yg, axis=-1, keepdims=True) + RMS_EPS)
        yn_s[:, g * gw:(g + 1) * gw] = ((yg * r) * ssdg_ref[:, g * gw:(g + 1) * gw]).astype(BF16)
    ssd_d = _dot(yn_s[...], ws_ref[...])

    gt = gt_ref[...].astype(F32)
    mix = gt[:, :D_MODEL] * attn_d + gt[:, D_MODEL:] * ssd_d
    mixed = _dot(mix.astype(BF16), wo_ref[...])
    x1 = _group_pick(xa_ref, xb_ref, n_first) + m[2:3] * mixed
    x1_ref[...] = x1
    r = lax.rsqrt(jnp.mean(x1 * x1, axis=-1, keepdims=True) + RMS_EPS)
    h2 = ((x1 * r) * n2g_ref[...]) * (1.0 + m[4:5]) + m[3:4]
    h2b_ref[...] = h2.astype(BF16)
    rt_ref[...] = _route(_dot_f32_3(h2, wrt_ref[...]) + brt_ref[...])


def _post(oa, ob, xa, xb, y, z, gates, mod_tiles, subg, ssdg, n2g, wa, ws, wo, w_rt, b_rt, tm):
    t = y.shape[0]
    n_first = oa.shape[0] // tm
    tok = lambda n: pl.BlockSpec((tm, n), lambda i: (i, 0))
    const = lambda a: pl.BlockSpec(a.shape, lambda i: (0, 0))
    return pl.pallas_call(
        functools.partial(_post_kernel, n_first=n_first),
        grid=(t // tm,),
        in_specs=_group_specs(tm, ATTN_WIDTH, n_first) + _group_specs(tm, D_MODEL, n_first) + [
            tok(D_INNER), tok(D_INNER), tok(GATE_COLS),
            pl.BlockSpec((1, 6, D_MODEL), lambda i: (i, 0, 0)),
            const(subg), const(ssdg), const(n2g), const(wa), const(ws), const(wo), const(w_rt), const(b_rt)],
        out_specs=[tok(D_MODEL), tok(D_MODEL), tok(LANES)],
        out_shape=[jax.ShapeDtypeStruct((t, D_MODEL), F32),
                   jax.ShapeDtypeStruct((t, D_MODEL), BF16),
                   jax.ShapeDtypeStruct((t, LANES), F32)],
        scratch_shapes=[pltpu.VMEM((tm, ATTN_WIDTH), BF16), pltpu.VMEM((tm, D_INNER), BF16)],
        compiler_params=_params(("parallel",)),
        name="merge_out_proj",
    )(oa, ob, xa, xb, y, z, gates, mod_tiles, subg, ssdg, n2g, wa, ws, wo, w_rt, b_rt)


def _route(logits):
    tm = logits.shape[0]
    lane = lax.broadcasted_iota(jnp.int32, (tm, LANES), 1)
    lane_f = lane.astype(F32)
    big = float(LANES)
    neg = -jnp.inf
    gl = jnp.where(lane < N_EXPERT_GROUPS, logits, neg)
    gmax = jnp.max(gl, axis=1, keepdims=True)
    g_sel = jnp.min(jnp.where(gl == gmax, lane_f, big), axis=1, keepdims=True)
    g_w = 1.0 / jnp.sum(jnp.exp(gl - gmax), axis=1, keepdims=True)
    e_lane = lane - N_EXPERT_GROUPS
    e_group = (e_lane >> 3).astype(F32)
    in_group = (e_lane >= 0) & (e_lane < N_EXPERTS) & (e_group == g_sel)
    el = jnp.where(in_group, logits, neg)
    v1 = jnp.max(el, axis=1, keepdims=True)
    i1 = jnp.min(jnp.where(el == v1, lane_f, big), axis=1, keepdims=True)
    el2 = jnp.where(lane_f == i1, neg, el)
    v2 = jnp.max(el2, axis=1, keepdims=True)
    i2 = jnp.min(jnp.where(el2 == v2, lane_f, big), axis=1, keepdims=True)
    e2 = jnp.exp(v2 - v1)
    w1 = g_w / (1.0 + e2)
    w2 = g_w * e2 / (1.0 + e2)
    return jnp.where(lane == 0, i1 - N_EXPERT_GROUPS,
                     jnp.where(lane == 1, i2 - N_EXPERT_GROUPS,
                               jnp.where(lane == 2, w1, jnp.where(lane == 3, w2, 0.0))))


def _expert_kernel(be_ref, nu_ref, x_ref, wg_ref, wu_ref, wd_ref, o_ref, wg_s, wu_s, wd_s):
    i = pl.program_id(0)
    used = i < nu_ref[0]
    new_expert = (i == 0) | (be_ref[i] != be_ref[jnp.maximum(i - 1, 0)])

    @pl.when(used & new_expert)
    def _():
        wg_s[...] = wg_ref[0].astype(BF16)
        wu_s[...] = wu_ref[0].astype(BF16)
        wd_s[...] = wd_ref[0].astype(BF16)

    @pl.when(used)
    def _():
        x = x_ref[...]
        a = _silu(_dot(x, wg_s[...])) * _dot(x, wu_s[...])
        o_ref[...] = _dot(a.astype(BF16), wd_s[...]).astype(o_ref.dtype)

    @pl.when(i >= nu_ref[0])
    def _():
        o_ref[...] = jnp.zeros(o_ref.shape, o_ref.dtype)


def _experts(xb, blk_e, n_used, wg, wu, wd):
    cap = xb.shape[0]
    nb = cap // EXPERT_BLOCK
    grid_spec = pltpu.PrefetchScalarGridSpec(
        num_scalar_prefetch=2,
        grid=(nb,),
        in_specs=[pl.BlockSpec((EXPERT_BLOCK, D_MODEL), lambda i, be, nu: (jnp.minimum(i, nu[0] - 1), 0)),
                  pl.BlockSpec((1, D_MODEL, EXPERT_FF), lambda i, be, nu: (be[i], 0, 0)),
                  pl.BlockSpec((1, D_MODEL, EXPERT_FF), lambda i, be, nu: (be[i], 0, 0)),
                  pl.BlockSpec((1, EXPERT_FF, D_MODEL), lambda i, be, nu: (be[i], 0, 0))],
        out_specs=pl.BlockSpec((EXPERT_BLOCK, D_MODEL), lambda i, be, nu: (i, 0)),
        scratch_shapes=[pltpu.VMEM((D_MODEL, EXPERT_FF), BF16), pltpu.VMEM((D_MODEL, EXPERT_FF), BF16),
                        pltpu.VMEM((EXPERT_FF, D_MODEL), BF16)],
    )
    return pl.pallas_call(
        _expert_kernel,
        grid_spec=grid_spec,
        out_shape=jax.ShapeDtypeStruct((cap, D_MODEL), BF16),
        compiler_params=_params(("arbitrary",)),
        name="expert_mlp",
    )(blk_e, n_used, xb, wg, wu, wd)


def _final_kernel(x1_ref, mod_ref, rt_ref, g0_ref, g1_ref, o_ref):
    m = mod_ref[0]
    rt = rt_ref[...]
    moe = g0_ref[...].astype(F32) * rt[:, 2:3] + g1_ref[...].astype(F32) * rt[:, 3:4]
    o_ref[...] = x1_ref[...] + m[5:6] * moe


def _final(x1, mod_tiles, rt, g, tok_off, n_tok, tm):
    b0 = tok_off // tm
    b1 = (g.shape[0] // TOP_K_INNER) // tm
    tok = lambda n: pl.BlockSpec((tm, n), lambda i: (b0 + i, 0))
    return pl.pallas_call(
        _final_kernel,
        grid=(n_tok // tm,),
        in_specs=[tok(D_MODEL), pl.BlockSpec((1, 6, D_MODEL), lambda i: (b0 + i, 0, 0)), tok(LANES),
                  tok(D_MODEL), pl.BlockSpec((tm, D_MODEL), lambda i: (b1 + b0 + i, 0))],
        out_specs=pl.BlockSpec((tm, D_MODEL), lambda i: (i, 0)),
        out_shape=jax.ShapeDtypeStruct((n_tok, D_MODEL), F32),
        compiler_params=_params(("parallel",)),
        name="moe_combine",
    )(x1, mod_tiles, rt, g, g)


ROT = QK_DIM // 2


def _permute_head_cols(w):
    k = w.shape[0]
    return w.reshape(k, ATTN_HEADS, 2, 2, ROT).transpose(0, 1, 3, 2, 4).reshape(k, ATTN_HEADS * LANES)


def _permute_gain(g):
    return jnp.broadcast_to(g.reshape(2, 1, ROT), (2, 2, ROT)).reshape(1, LANES)


def _rope_tables(seq):
    pos = jnp.arange(seq, dtype=F32)
    inv = 1.0 / (ROPE_THETA ** (jnp.arange(0, QK_DIM, 2, dtype=F32) / QK_DIM))
    ang = pos[:, None] * inv[None, :]
    cos, sin = jnp.cos(ang), jnp.sin(ang)
    cos_t = jnp.tile(cos, (1, LANES // ROT))
    sin_t = jnp.concatenate([-sin, -sin, sin, sin], axis=1)
    return cos_t, sin_t


def _dest_kernel(rt_ref, base_ref, tri_ref, o_ref, carry_s):
    @pl.when(pl.program_id(0) == 0)
    def _():
        carry_s[...] = jnp.zeros(carry_s.shape, F32)

    rt = rt_ref[...]
    tm = rt.shape[0]
    lane = lax.broadcasted_iota(jnp.int32, (tm, LANES), 1)
    lane_f = lane.astype(F32)
    oh0 = lane_f == rt[:, 0:1]
    oh1 = lane_f == rt[:, 1:2]
    both = jnp.where(oh0 | oh1, 1.0, 0.0)
    pos = base_ref[...] + carry_s[0:1, :] + _dot(tri_ref[...], both.astype(BF16))
    d0 = jnp.sum(jnp.where(oh0, pos, 0.0), axis=1, keepdims=True)
    d1 = jnp.sum(jnp.where(oh1, pos, 0.0), axis=1, keepdims=True)
    o_ref[...] = jnp.where(lane == 0, d0, jnp.where(lane == 1, d1, 0.0)).astype(jnp.int32)
    carry_s[...] = carry_s[...] + jnp.sum(both, axis=0, keepdims=True)


def _dispatch(rt, n_tok, tm):
    n_slots = n_tok * TOP_K_INNER
    flat_e = rt[:, :TOP_K_INNER].astype(jnp.int32).reshape(-1)
    counts = jnp.sum((flat_e[:, None] == jnp.arange(N_EXPERTS, dtype=jnp.int32)[None, :]).astype(jnp.int32), axis=0)
    padded = ((counts + EXPERT_BLOCK - 1) // EXPERT_BLOCK) * EXPERT_BLOCK
    pad_end = jnp.cumsum(padded)
    pad_start = pad_end - padded
    base = jnp.pad(pad_start.astype(F32), (0, LANES - N_EXPERTS)).reshape(1, LANES)
    r = jnp.arange(tm)
    tri = (r[None, :] < r[:, None]).astype(BF16)
    dest = pl.pallas_call(
        _dest_kernel,
        grid=(n_tok // tm,),
        in_specs=[pl.BlockSpec((tm, LANES), lambda i: (i, 0)),
                  pl.BlockSpec((1, LANES), lambda i: (0, 0)),
                  pl.BlockSpec((tm, tm), lambda i: (0, 0))],
        out_specs=pl.BlockSpec((tm, LANES), lambda i: (i, 0)),
        out_shape=jax.ShapeDtypeStruct((n_tok, LANES), jnp.int32),
        scratch_shapes=[pltpu.VMEM((SUBLANES, LANES), F32)],
        compiler_params=_params(("arbitrary",)),
        name="dispatch_rows",
    )(rt, base, tri)[:, :TOP_K_INNER]
    cap = n_slots + N_EXPERTS * EXPERT_BLOCK
    nb = cap // EXPERT_BLOCK
    blk_row0 = jnp.arange(nb, dtype=jnp.int32) * EXPERT_BLOCK
    blk_e = jnp.minimum(jnp.sum((pad_end[None, :] <= blk_row0[:, None]).astype(jnp.int32), axis=1), N_EXPERTS - 1)
    n_used = (pad_end[-1] // EXPERT_BLOCK).astype(jnp.int32).reshape(1)

    n_pad = cap - n_slots
    seg_len = jnp.concatenate([padded - counts, (cap - pad_end[-1]).reshape(1)])
    seg_end = jnp.cumsum(seg_len)
    seg_start = seg_end - seg_len
    seg_row0 = jnp.concatenate([pad_start + counts, pad_end[-1:]])
    j = jnp.arange(n_pad, dtype=jnp.int32)
    seg_onehot = (jnp.sum((seg_end[None, :] <= j[:, None]).astype(jnp.int32), axis=1)[:, None]
                  == jnp.arange(N_EXPERTS + 1, dtype=jnp.int32)[None, :]).astype(jnp.int32)
    pad_rows = j + jnp.sum(seg_onehot * (seg_row0 - seg_start)[None, :], axis=1)
    rows = jnp.concatenate([dest.reshape(-1), pad_rows.astype(jnp.int32)])
    toks = jnp.concatenate([jnp.arange(n_slots, dtype=jnp.int32) // TOP_K_INNER, j % n_tok])
    _, tok_buf = lax.sort_key_val(rows, toks)
    return dest, tok_buf, blk_e.astype(jnp.int32), n_used


def kernel(x_prompt, x_sample, c_prompt, c_sample, w_ada, b_ada, norm1_g, w_in, q_norm_g, k_norm_g, lambda_q1, lambda_k1, lambda_q2, lambda_k2, attn_subln_g, w_attn_o, conv_w, conv_b, dt_bias, a_log, d_skip, ssd_norm_g, w_ssd_o, w_out, norm2_g, w_group, b_group, w_router, b_router, w_gate_e, w_up_e, w_down_e):
    groups = [(x_prompt, c_prompt), (x_sample, c_sample)]
    seqs = [(x.shape[0], x.shape[1]) for x, _ in groups]
    n_tok = sum(b * s for b, s in seqs)
    min_seq = min(s for _, s in seqs)
    tm = min(1024, min_seq)
    tp = min(256, min_seq)
    tq = min(512, min_seq)
    tk = min(512, min_seq)
    layer = 0

    xa, xb_in = (g[0].reshape(-1, D_MODEL) for g in groups)
    c = jnp.concatenate([g[1] for g in groups], axis=0)
    n_batch = c.shape[0]
    c_pad = jnp.pad(c, ((0, (-n_batch) % SUBLANES), (0, 0)))
    mod = _ada(c_pad, w_ada[layer], b_ada[layer]).reshape(-1, 6, D_MODEL)
    tok_batch = jnp.concatenate([jnp.repeat(jnp.arange(b, dtype=jnp.int32), s) + off
                                 for (b, s), off in zip(seqs, [0, seqs[0][0]])])
    mod_tm = mod[tok_batch[::tm]]
    mod_tp = mod[tok_batch[::tp]]

    w_in_b = w_in[layer].astype(BF16)
    tabs = [_rope_tables(s) for _, s in seqs]
    cos_t = jnp.concatenate([jnp.tile(tb[0], (b, 1)) for tb, (b, _) in zip(tabs, seqs)], axis=0)
    sin_t = jnp.concatenate([jnp.tile(tb[1], (b, 1)) for tb, (b, _) in zip(tabs, seqs)], axis=0)
    half = (jnp.arange(LANES) // ROT) % 2
    bd = (half[:, None] == half[None, :]).astype(BF16)
    gq = _permute_gain(q_norm_g[layer]) * (QK_DIM ** -0.5 * LOG2_E)
    gk = _permute_gain(k_norm_g[layer])
    q, h = _norm_q_proj(xa, xb_in, mod_tm, norm1_g[layer], _permute_head_cols(w_in_b[:, OFF_Q:OFF_K]),
                        gq, cos_t, sin_t, bd, tm)
    k = _qk_proj(h, _permute_head_cols(w_in_b[:, OFF_K:OFF_V]), gk, cos_t, sin_t, bd, tm, "k_proj")
    vt3 = _vt_proj(h, w_in_b[:, OFF_V:OFF_Z], tm, tk)
    z_act = _matmul(h, w_in_b[:, OFF_Z:OFF_XBC], BF16, tm, 1024, "z_proj", _silu)
    w_dt =jnp.pad(w_in_b[:, OFF_DT:OFF_GATE], ((0, 0), (0, LANES - DT_COLS)))
    dt_b = jnp.pad(dt_bias[layer].reshape(1, DT_COLS), ((0, 0), (0, LANES - DT_COLS)))
    dt = _matmul(h, w_dt, F32, tm, LANES, "dt_proj", _softplus, dt_b)
    gates = _matmul(h, w_in_b[:, OFF_GATE:], BF16, tm, 1024, "gate_proj", _sigmoid)

    k3 = k.reshape(n_tok // tk, tk, ATTN_WIDTH)
    lams = [v[layer].reshape(1, QK_DIM) for v in (lambda_q1, lambda_k1, lambda_q2, lambda_k2)]
    o_groups = []
    off = 0
    for b, s in seqs:
        o_groups.append(_attention(q, k3, vt3, lams, off, b, s, tq, tk))
        off += b * s

    seq_starts = []
    off = 0
    for b, s in seqs:
        seq_starts += [(off + i * s, s) for i in range(b)]
        off += b * s
    tile_start = jnp.zeros((n_tok // tm,), jnp.int32)
    tile_end = jnp.zeros((n_tok // tm,), jnp.int32)
    nc = n_tok // CHUNK
    chunk_reset = jnp.zeros((nc,), jnp.int32)
    bwd_idx = jnp.zeros((nc,), jnp.int32)
    for st, s in seq_starts:
        tile_start = tile_start.at[st // tm].set(1)
        tile_end = tile_end.at[(st + s) // tm - 1].set(1)
        c0, c1 = st // CHUNK, (st + s) // CHUNK
        chunk_reset = chunk_reset.at[c0].set(1)
        bwd_idx = bwd_idx.at[c0:c1].set(jnp.arange(c1 - 1, c0 - 1, -1, dtype=jnp.int32))
    fwd_idx = jnp.arange(nc, dtype=jnp.int32)
    w_x, w_bc = w_in_b[:, OFF_XBC:OFF_XBC + D_INNER], w_in_b[:, OFF_XBC + D_INNER:OFF_DT]
    cw, cbias = conv_w[layer], conv_b[layer]
    xs = _proj_conv(h, w_x, cw[:, :D_INNER], cbias[:D_INNER], tile_start, tile_end, F32, tm, 1024, "x_proj_conv")
    bcm = _proj_conv(h, w_bc, cw[:, D_INNER:], cbias[D_INNER:], tile_start, tile_end, BF16, tm, 1024,
                     "bc_proj_conv")
    dtt = dt.T
    alg = a_log[layer].reshape(-1)
    dskip = jnp.repeat(d_skip[layer], SSD_HEAD_DIM).reshape(1, D_INNER)
    yb = _ssd(xs, bcm, dt, dtt, alg, bwd_idx, chunk_reset)
    y_ssd = _ssd(xs, bcm, dt, dtt, alg, fwd_idx, chunk_reset, yb, dskip)

    subg = attn_subln_g[layer].reshape(1, V_DIM)
    n_rt = N_EXPERT_GROUPS + N_EXPERTS
    w_rt = jnp.pad(jnp.concatenate([w_group[layer], w_router[layer]], axis=1), ((0, 0), (0, LANES - n_rt)))
    b_rt = jnp.pad(jnp.concatenate([b_group[layer], b_router[layer]]), (0, LANES - n_rt)).reshape(1, LANES)
    x1, h2b, rt = _post(o_groups[0], o_groups[1], xa, xb_in, y_ssd, z_act, gates, mod_tp, subg,
                        ssd_norm_g[layer].reshape(1, D_INNER), norm2_g[layer].reshape(1, D_MODEL),
                        w_attn_o[layer].astype(BF16), w_ssd_o[layer].astype(BF16),
                        w_out[layer].astype(BF16), w_rt, b_rt, tp)

    dest, tok_buf, blk_e, n_used = _dispatch(rt, n_tok, min(512, min_seq))
    xb = h2b[tok_buf]
    yb_e = _experts(xb, blk_e, n_used, w_gate_e[layer], w_up_e[layer], w_down_e[layer])
    g = yb_e[dest.T.reshape(-1)]

    outs = []
    off = 0
    for (b, s), (xg, _) in zip(seqs, groups):
        y = _final(x1, mod_tp, rt, g, off, b * s, tp)
        outs.append(y.reshape(xg.shape))
        off += b * s
    return tuple(outs)
```

```python
import functools
import math

import jax
import jax.numpy as jnp
from jax import lax
from jax.experimental import pallas as pl
from jax.experimental.pallas import tpu as pltpu

F32 = jnp.float32
BF16 = jnp.bfloat16

D_MODEL = 1024
ATTN_HEADS = 8
QK_DIM = 64
V_DIM = 2 * QK_DIM
ATTN_WIDTH = ATTN_HEADS * V_DIM
ROPE_THETA = 10000.0
D_INNER = 2048
SSD_HEAD_DIM = 64
SSD_HEADS = D_INNER // SSD_HEAD_DIM
SSD_GROUPS = 4
HEADS_PER_GROUP = SSD_HEADS // SSD_GROUPS
D_STATE = 128
CONV_K = 5
CONV_CH = D_INNER + 2 * SSD_GROUPS * D_STATE
CHUNK = 128
N_EXPERT_GROUPS = 4
EXPERTS_PER_GROUP = 8
N_EXPERTS = N_EXPERT_GROUPS * EXPERTS_PER_GROUP
TOP_K_INNER = 2
EXPERT_FF = 512
RMS_EPS = 1e-6
LAMBDA_INIT = 0.8 - 0.6 * math.exp(-0.3 * 0)

LANES = 128
SUBLANES = 8
VMEM_LIMIT = 56 * 1024 * 1024

Q_COLS = ATTN_HEADS * 2 * QK_DIM
K_COLS = Q_COLS
V_COLS = ATTN_WIDTH
Z_COLS = D_INNER
XBC_COLS = CONV_CH
DT_COLS = 2 * SSD_HEADS
GATE_COLS = 2 * D_MODEL
OFF_Q = 0
OFF_K = OFF_Q + Q_COLS
OFF_V = OFF_K + K_COLS
OFF_Z = OFF_V + V_COLS
OFF_XBC = OFF_Z + Z_COLS
OFF_DT = OFF_XBC + XBC_COLS
OFF_GATE = OFF_DT + DT_COLS

EXPERT_BLOCK = 512
ONES_ROWS = 16
LOG2_E = math.log2(math.e)


def _params(sem):
    return pltpu.CompilerParams(dimension_semantics=sem, vmem_limit_bytes=VMEM_LIMIT)


def _dot(a, b):
    return jnp.dot(a, b, preferred_element_type=F32)


def _dot_tn(a, b):
    return lax.dot_general(a, b, (((0,), (0,)), ((), ())), preferred_element_type=F32)


def _dot_nt(a, b):
    return lax.dot_general(a, b, (((1,), (1,)), ((), ())), preferred_element_type=F32)


def _split3(a):
    hi = a.astype(BF16)
    r = a - hi.astype(F32)
    mid = r.astype(BF16)
    lo = (r - mid.astype(F32)).astype(BF16)
    return hi, mid, lo


def _dot_left01(m01, a):
    hi, mid, lo = _split3(a)
    return _dot(m01, hi) + _dot(m01, mid) + _dot(m01, lo)


def _dot_right01(a, m01):
    hi, mid, lo = _split3(a)
    return _dot(hi, m01) + _dot(mid, m01) + _dot(lo, m01)


def _spread01(a, m01):
    hi = a.astype(BF16)
    lo = (a - hi.astype(F32)).astype(BF16)
    return _dot(hi, m01) + _dot(lo, m01)


def _dot_f32(a, b):
    a0, a1, a2 = _split3(a)
    b0, b1, b2 = _split3(b)
    return (_dot(a0, b0) + (_dot(a0, b1) + _dot(a1, b0))
            + (_dot(a0, b2) + _dot(a2, b0) + _dot(a1, b1)))


def _dot_f32_3(a, b):
    a0 = a.astype(BF16)
    a1 = (a - a0.astype(F32)).astype(BF16)
    b0 = b.astype(BF16)
    b1 = (b - b0.astype(F32)).astype(BF16)
    return _dot(a0, b0) + (_dot(a0, b1) + _dot(a1, b0))


def _sigmoid(x):
    return 1.0 / (1.0 + jnp.exp(-x))


def _silu(x):
    return x * _sigmoid(x)


def _softplus(x):
    e = jnp.exp(-jnp.abs(x))
    u = 1.0 + e
    log1p_e = jnp.where(u == 1.0, e, jnp.log(u) * (e / (u - 1.0)))
    return jnp.maximum(x, 0.0) + log1p_e


def _ada_kernel(c_ref, w_ref, b_ref, o_ref):
    o_ref[...] = _dot_f32(_silu(c_ref[...]), w_ref[...]) + b_ref[...]


def _ada(c_pad, w_ada, b_ada):
    rows = c_pad.shape[0]
    n = w_ada.shape[1]
    tn = 1024
    return pl.pallas_call(
        _ada_kernel,
        grid=(n // tn,),
        in_specs=[pl.BlockSpec((rows, D_MODEL), lambda j: (0, 0)),
                  pl.BlockSpec((D_MODEL, tn), lambda j: (0, j)),
                  pl.BlockSpec((1, tn), lambda j: (0, j))],
        out_specs=pl.BlockSpec((rows, tn), lambda j: (0, j)),
        out_shape=jax.ShapeDtypeStruct((rows, n), F32),
        compiler_params=_params(("arbitrary",)),
        name="ada_mod",
    )(c_pad, w_ada, b_ada.reshape(1, n))


def _group_specs(tm, n, n_first):
    return [pl.BlockSpec((tm, n), lambda i: (jnp.minimum(i, n_first - 1), 0)),
            pl.BlockSpec((tm, n), lambda i: (jnp.maximum(i - n_first, 0), 0))]


def _group_pick(a_ref, b_ref, n_first):
    return jnp.where(pl.program_id(0) < n_first, a_ref[...], b_ref[...])


def _mm_kernel(a_ref, w_ref, *rest, act):
    o_ref = rest[-1]
    acc = _dot(a_ref[...], w_ref[...])
    if len(rest) == 2:
        acc = acc + rest[0][...]
    if act is not None:
        acc = act(acc)
    o_ref[...] = acc.astype(o_ref.dtype)


def _matmul(a, w, out_dtype, tm, tn, name, act=None, bias=None):
    m, k = a.shape
    n = w.shape[1]
    in_specs = [pl.BlockSpec((tm, k), lambda j, i: (i, 0)),
                pl.BlockSpec((k, tn), lambda j, i: (0, j))]
    args = [a, w]
    if bias is not None:
        in_specs.append(pl.BlockSpec((1, tn), lambda j, i: (0, j)))
        args.append(bias)
    return pl.pallas_call(
        functools.partial(_mm_kernel, act=act),
        grid=(n // tn, m // tm),
        in_specs=in_specs,
        out_specs=pl.BlockSpec((tm, tn), lambda j, i: (i, j)),
        out_shape=jax.ShapeDtypeStruct((m, n), out_dtype),
        compiler_params=_params(("parallel", "parallel")),
        name=name,
    )(*args)


def _qk_epilogue(acc, g_ref, cos_ref, sin_ref, bd_ref, o_ref):
    cos = cos_ref[...]
    sin = sin_ref[...]
    g = g_ref[...]
    bd = bd_ref[...]
    for h in range(ATTN_HEADS):
        x = acc[:, h * LANES:(h + 1) * LANES]
        sq = x * x
        hi = sq.astype(BF16)
        lo = (sq - hi.astype(F32)).astype(BF16)
        ss = _dot(hi, bd) + _dot(lo, bd)
        r = lax.rsqrt(ss * (1.0 / QK_DIM) + RMS_EPS)
        xn = (x * r) * g
        partner = pltpu.roll(xn, LANES // 2, 1)
        o_ref[:, h * LANES:(h + 1) * LANES] = (xn * cos + partner * sin).astype(o_ref.dtype)


def _qk_kernel(a_ref, w_ref, g_ref, cos_ref, sin_ref, bd_ref, o_ref):
    _qk_epilogue(_dot(a_ref[...], w_ref[...]), g_ref, cos_ref, sin_ref, bd_ref, o_ref)


def _norm_q_kernel(xa_ref, xb_ref, mod_ref, g1_ref, w_ref, g_ref, cos_ref, sin_ref, bd_ref, o_ref, h_ref,
                   *, n_first):
    x = _group_pick(xa_ref, xb_ref, n_first)
    r = lax.rsqrt(jnp.mean(x * x, axis=-1, keepdims=True) + RMS_EPS)
    m = mod_ref[0]
    h = (((x * r) * g1_ref[...]) * (1.0 + m[1:2]) + m[0:1]).astype(BF16)
    h_ref[...] = h
    _qk_epilogue(_dot(h, w_ref[...]), g_ref, cos_ref, sin_ref, bd_ref, o_ref)


def _qk_specs(tm, n):
    return [pl.BlockSpec((D_MODEL, n), lambda i: (0, 0)),
            pl.BlockSpec((1, LANES), lambda i: (0, 0)),
            pl.BlockSpec((tm, LANES), lambda i: (i, 0)),
            pl.BlockSpec((tm, LANES), lambda i: (i, 0)),
            pl.BlockSpec((LANES, LANES), lambda i: (0, 0))]


def _qk_proj(h, w, g128, cos_t, sin_t, bd, tm, name):
    t = h.shape[0]
    n = w.shape[1]
    return pl.pallas_call(
        _qk_kernel,
        grid=(t // tm,),
        in_specs=[pl.BlockSpec((tm, D_MODEL), lambda i: (i, 0))] + _qk_specs(tm, n),
        out_specs=pl.BlockSpec((tm, n), lambda i: (i, 0)),
        out_shape=jax.ShapeDtypeStruct((t, n), BF16),
        compiler_params=_params(("parallel",)),
        name=name,
    )(h, w, g128, cos_t, sin_t, bd)


def _norm_q_proj(xa, xb, mod_tiles, g1, w, g128, cos_t, sin_t, bd, tm):
    t = xa.shape[0] + xb.shape[0]
    n = w.shape[1]
    n_first = xa.shape[0] // tm
    tok = pl.BlockSpec((tm, n), lambda i: (i, 0))
    return pl.pallas_call(
        functools.partial(_norm_q_kernel, n_first=n_first),
        grid=(t // tm,),
        in_specs=_group_specs(tm, D_MODEL, n_first) + [
            pl.BlockSpec((1, 6, D_MODEL), lambda i: (i, 0, 0)),
            pl.BlockSpec((1, D_MODEL), lambda i: (0, 0))] + _qk_specs(tm, n),
        out_specs=[tok, pl.BlockSpec((tm, D_MODEL), lambda i: (i, 0))],
        out_shape=[jax.ShapeDtypeStruct((t, n), BF16), jax.ShapeDtypeStruct((t, D_MODEL), BF16)],
        compiler_params=_params(("parallel",)),
        name="norm1_q_proj",
    )(xa, xb, mod_tiles, g1.reshape(1, D_MODEL), w, g128, cos_t, sin_t, bd)


def _vt_kernel(a_ref, w_ref, o_ref, *, tk):
    acc = _dot(a_ref[...], w_ref[...])
    for c in range(acc.shape[0] // tk):
        o_ref[c] = acc[c * tk:(c + 1) * tk, :].T.astype(o_ref.dtype)


def _vt_proj(h, w, tm, tk):
    t = h.shape[0]
    n = w.shape[1]
    return pl.pallas_call(
        functools.partial(_vt_kernel, tk=tk),
        grid=(t // tm,),
        in_specs=[pl.BlockSpec((tm, D_MODEL), lambda i: (i, 0)),
                  pl.BlockSpec((D_MODEL, n), lambda i: (0, 0))],
        out_specs=pl.BlockSpec((tm // tk, n, tk), lambda i: (i, 0, 0)),
        out_shape=jax.ShapeDtypeStruct((t // tk, n, tk), BF16),
        compiler_params=_params(("parallel",)),
        name="v_proj_t",
    )(h, w)


def _attn_kernel(trips_ref, lq1_ref, lk1_ref, lq2_ref, lk2_ref, q_ref, k_ref, vt_ref, o_ref,
                 q2t_s, s_s, m_s, acc_s, *, tq, nkv, unroll):
    tk = k_ref.shape[1]
    qt = q_ref[...].astype(F32).T
    row = lax.broadcasted_iota(jnp.int32, qt.shape, 0)
    first_half = (row & ROT) == 0
    q2t_s[:, :tq] = jnp.where(first_half, qt, 0.0).astype(BF16)
    q2t_s[:, tq:] = jnp.where(first_half, 0.0, qt).astype(BF16)
    m_s[...] = jnp.full(m_s.shape, -jnp.inf, F32)
    acc_s[...] = jnp.zeros(acc_s.shape, F32)
    ones = jnp.ones((ONES_ROWS, tk), BF16)
    s_s[0] = _dot(k_ref[0], q2t_s[...])

    def step(j, par, compute_next):
        if compute_next:
            s_s[1 - par] = _dot(k_ref[j + 1], q2t_s[...])
        st = s_s[par]
        m_old = m_s[...]
        m_new = jnp.maximum(m_old, jnp.max(st, axis=0, keepdims=True))
        alpha = jnp.exp2(m_old - m_new)
        p = jnp.exp2(st - m_new).astype(BF16)
        lhs = jnp.concatenate([vt_ref[j], ones], axis=0)
        acc_s[...] = alpha * acc_s[...] + _dot(lhs, p)
        m_s[...] = m_new

    n_trips = (nkv - 1) // unroll

    def trip(t, carry):
        for u in range(unroll):
            step(unroll * t + u, u % 2, True)
        return carry

    lax.fori_loop(0, trips_ref[0], trip, 0)
    for j in range(unroll * n_trips, nkv):
        step(j, j % 2, j + 1 < nkv)

    lam = (jnp.exp(jnp.sum(lq1_ref[...] * lk1_ref[...], axis=1, keepdims=True))
           - jnp.exp(jnp.sum(lq2_ref[...] * lk2_ref[...], axis=1, keepdims=True)) + LAMBDA_INIT)
    ot = acc_s[:V_DIM, :] / acc_s[V_DIM:V_DIM + 1, :]
    o_ref[...] = (ot[:, :tq] - lam * ot[:, tq:]).T


def _attention(q, k3, vt3, lams, tok_off, batch, seq, tq, tk):
    nkv = seq // tk
    qb0 = tok_off // tq
    sb0 = tok_off // seq
    nq = seq // tq
    unroll = 4 if nkv >= 8 else 2
    n_trips = jnp.full((1,), (nkv - 1) // unroll, jnp.int32)
    lam_spec = pl.BlockSpec((1, QK_DIM), lambda b, h, i, nt: (0, 0))
    grid_spec = pltpu.PrefetchScalarGridSpec(
        num_scalar_prefetch=1,
        grid=(batch, ATTN_HEADS, nq),
        in_specs=[lam_spec, lam_spec, lam_spec, lam_spec,
                  pl.BlockSpec((tq, LANES), lambda b, h, i, nt: (qb0 + b * nq + i, h)),
                  pl.BlockSpec((nkv, tk, LANES), lambda b, h, i, nt: (sb0 + b, 0, h)),
                  pl.BlockSpec((nkv, LANES, tk), lambda b, h, i, nt: (sb0 + b, h, 0))],
        out_specs=pl.BlockSpec((tq, LANES), lambda b, h, i, nt: (b * nq + i, h)),
        scratch_shapes=[pltpu.VMEM((LANES, 2 * tq), BF16),
                        pltpu.VMEM((2, tk, 2 * tq), F32),
                        pltpu.VMEM((1, 2 * tq), F32),
                        pltpu.VMEM((V_DIM + ONES_ROWS, 2 * tq), F32)],
    )
    return pl.pallas_call(
        functools.partial(_attn_kernel, tq=tq, nkv=nkv, unroll=unroll),
        grid_spec=grid_spec,
        out_shape=jax.ShapeDtypeStruct((batch * seq, ATTN_WIDTH), F32),
        compiler_params=_params(("parallel", "parallel", "arbitrary")),
        name="diff_attention",
    )(n_trips, *lams, q, k3, vt3)


HALO = 16
MXU_COLS = 256


def _proj_conv_kernel(start_ref, end_ref, prev_ref, cur_ref, next_ref, w_ref, cw_ref, cb_ref, o_ref, *, tm):
    i = pl.program_id(1)
    prev = jnp.where(start_ref[i] == 1, jnp.zeros_like(prev_ref[...]), prev_ref[...])
    nxt = jnp.where(end_ref[i] == 1, jnp.zeros_like(next_ref[...]), next_ref[...])
    a = jnp.concatenate([prev, cur_ref[...], nxt], axis=0)
    pad = (CONV_K - 1) // 2
    for c in range(0, o_ref.shape[1], MXU_COLS):
        ext = _dot(a, w_ref[:, c:c + MXU_COLS])
        acc = jnp.broadcast_to(cb_ref[:, c:c + MXU_COLS], (tm, MXU_COLS))
        for d in range(CONV_K):
            shift = (pad - d) % (tm + 2 * HALO)
            src = ext if shift == 0 else pltpu.roll(ext, shift, 0)
            acc = acc + cw_ref[d:d + 1, c:c + MXU_COLS] * src[HALO:HALO + tm, :]
        o_ref[:, c:c + MXU_COLS] = _silu(acc).astype(o_ref.dtype)


def _proj_conv(h, w, conv_w, conv_b, start_flags, end_flags, out_dtype, tm, tn, name):
    t = h.shape[0]
    n = w.shape[1]
    rb = tm // HALO
    last = t // HALO - 1
    cw_pad = jnp.zeros((SUBLANES, n), F32).at[:CONV_K].set(conv_w)
    grid_spec = pltpu.PrefetchScalarGridSpec(
        num_scalar_prefetch=2,
        grid=(n // tn, t // tm),
        in_specs=[pl.BlockSpec((HALO, D_MODEL), lambda j, i, s, e: (jnp.maximum(i * rb - 1, 0), 0)),
                  pl.BlockSpec((tm, D_MODEL), lambda j, i, s, e: (i, 0)),
                  pl.BlockSpec((HALO, D_MODEL), lambda j, i, s, e: (jnp.minimum((i + 1) * rb, last), 0)),
                  pl.BlockSpec((D_MODEL, tn), lambda j, i, s, e: (0, j)),
                  pl.BlockSpec((SUBLANES, tn), lambda j, i, s, e: (0, j)),
                  pl.BlockSpec((1, tn), lambda j, i, s, e: (0, j))],
        out_specs=pl.BlockSpec((tm, tn), lambda j, i, s, e: (i, j)),
    )
    return pl.pallas_call(
        functools.partial(_proj_conv_kernel, tm=tm),
        grid_spec=grid_spec,
        out_shape=jax.ShapeDtypeStruct((t, n), out_dtype),
        compiler_params=_params(("parallel", "parallel")),
        name=name,
    )(start_flags, end_flags, h, h, h, w, cw_pad, conv_b.reshape(1, n))


def _ssd_kernel(idx_ref, reset_ref, xs_ref, bc_ref, dt_ref, dtt_ref,
                alog_r_ref, alog_c_ref, tril_ref, triu_ref, e01_ref, *rest, rev):
    i = pl.program_id(0)
    nh = SSD_HEADS
    y_ref, state_s = rest[-2:]

    @pl.when(reset_ref[i] == 1)
    def _():
        state_s[...] = jnp.zeros(state_s.shape, F32)

    xs = xs_ref[...]
    bc = bc_ref[...]
    dtn = dt_ref[...]
    a = dtn * (-LOG2_E * jnp.exp(alog_r_ref[...]))
    pinc = _dot_left01(tril_ref[...], a)
    pex = pinc - a
    tot = pinc[CHUNK - 1:CHUNK, :]

    gw = HEADS_PER_GROUP * SSD_HEAD_DIM
    gn = SSD_GROUPS * D_STATE
    cd = jnp.broadcast_to(jnp.exp2(tot), (SUBLANES, LANES))

    def update_state(sc_st, sc_cd):
        w = (xs * sc_st).astype(BF16)
        for g in range(SSD_GROUPS):
            bg = bc[:, g * D_STATE:(g + 1) * D_STATE]
            new = _dot_tn(bg, w[:, g * gw:(g + 1) * gw])
            state_s[:, g * gw:(g + 1) * gw] = (state_s[:, g * gw:(g + 1) * gw] * sc_cd[:, g * gw:(g + 1) * gw]
                                               + new)

    sc_cd = _spread01(cd, e01_ref[...])[0:1]
    if rev:
        stacked = jnp.concatenate([jnp.exp2(pex) * dtn, jnp.exp2(tot - pex)], axis=0)
        ex = _dot(stacked.astype(BF16), e01_ref[...])
        sc_off = ex[CHUNK:]
        for g in range(SSD_GROUPS):
            cg = bc[:, gn + g * D_STATE:gn + (g + 1) * D_STATE]
            st = state_s[:, g * gw:(g + 1) * gw]
            y_ref[:, g * gw:(g + 1) * gw] = _dot(cg, st.astype(BF16)) * sc_off[:, g * gw:(g + 1) * gw]
        update_state(ex[:CHUNK], sc_cd)
        return

    yb_ref, dskip_ref = rest[:2]
    dtnt = dtt_ref[...]
    at = dtnt * (-LOG2_E * jnp.exp(alog_c_ref[...]))
    pinct = _dot_right01(at, triu_ref[...])
    pext = pinct - at
    li = lax.broadcasted_iota(jnp.int32, (CHUNK, CHUNK), 0)
    si = lax.broadcasted_iota(jnp.int32, (CHUNK, CHUNK), 1)
    lower = si <= li
    strict_lower = si < li
    strict_upper = si > li
    lane = lax.broadcasted_iota(jnp.int32, (CHUNK, LANES), 1)
    first_head = lane < SSD_HEAD_DIM
    xb = xs.astype(BF16)
    for g in range(SSD_GROUPS):
        bg = bc[:, g * D_STATE:(g + 1) * D_STATE]
        cg = bc[:, gn + g * D_STATE:gn + (g + 1) * D_STATE]
        cg_f = cg.astype(F32)
        cb = _dot_nt(cg, bg)
        for hp in range(HEADS_PER_GROUP // 2):
            col = g * gw + hp * LANES
            rhs = jnp.concatenate([xb[:, col:col + LANES], state_s[:, col:col + LANES].astype(BF16)], axis=0)
            pair = []
            for u in range(2):
                h = g * HEADS_PER_GROUP + 2 * hp + u
                colf = jnp.broadcast_to(pinc[:, h:h + 1], (CHUNK, CHUNK))
                arg = jnp.where(lower, colf - pinct[h:h + 1, :],
                                pext[nh + h:nh + h + 1, :] - pex[:, nh + h:nh + h + 1])
                dtf = dtnt[h:h + 1, :]
                dtb = dtnt[nh + h:nh + h + 1, :]
                coef = jnp.where(strict_lower, dtf, jnp.where(strict_upper, dtb, dtf + dtb))
                mh = (cb * jnp.exp2(arg) * coef).astype(BF16)
                dh = (cg_f * jnp.exp2(colf)).astype(BF16)
                pair.append(_dot(jnp.concatenate([mh, dh], axis=1), rhs))
            y_ref[:, col:col + LANES] = ((jnp.where(first_head, pair[0], pair[1]) + yb_ref[:, col:col + LANES])
                                         + dskip_ref[:, col:col + LANES] * xs[:, col:col + LANES])

    update_state(_dot((jnp.exp2(tot - pinc) * dtn).astype(BF16), e01_ref[...]), sc_cd)


def _ssd(xs, bc, dt, dtt, a_log, idx, reset, yb=None, dskip=None):
    rev = yb is None
    t = xs.shape[0]
    nc = t // CHUNK
    alog_r = jnp.pad(a_log.reshape(1, DT_COLS), ((0, 0), (0, LANES - DT_COLS)))
    alog_c = alog_r.reshape(LANES, 1)
    r = jnp.arange(CHUNK)
    tril = (r[None, :] <= r[:, None]).astype(BF16)
    triu = (r[:, None] <= r[None, :]).astype(BF16)
    lo = SSD_HEADS if rev else 0
    e01 = (jnp.arange(D_INNER)[None, :] // SSD_HEAD_DIM == jnp.arange(LANES)[:, None] - lo).astype(BF16)
    const = lambda shape: pl.BlockSpec(shape, lambda i, ix, rs: (0, 0))
    in_specs = [pl.BlockSpec((CHUNK, D_INNER), lambda i, ix, rs: (ix[i], 0)),
                pl.BlockSpec((CHUNK, CONV_CH - D_INNER), lambda i, ix, rs: (ix[i], 0)),
                pl.BlockSpec((CHUNK, LANES), lambda i, ix, rs: (ix[i], 0)),
                pl.BlockSpec((LANES, CHUNK), lambda i, ix, rs: (0, ix[i])),
                const((1, LANES)), const((LANES, 1)),
                const((CHUNK, CHUNK)), const((CHUNK, CHUNK)), const((LANES, D_INNER))]
    args = [idx, reset, xs, bc, dt, dtt, alog_r, alog_c, tril, triu, e01]
    if not rev:
        in_specs += [pl.BlockSpec((CHUNK, D_INNER), lambda i, ix, rs: (ix[i], 0)), const((1, D_INNER))]
        args += [yb, dskip]
    grid_spec = pltpu.PrefetchScalarGridSpec(
        num_scalar_prefetch=2,
        grid=(nc,),
        in_specs=in_specs,
        out_specs=pl.BlockSpec((CHUNK, D_INNER), lambda i, ix, rs: (ix[i], 0)),
        scratch_shapes=[pltpu.VMEM((D_STATE, D_INNER), F32)],
    )
    return pl.pallas_call(
        functools.partial(_ssd_kernel, rev=rev),
        grid_spec=grid_spec,
        out_shape=jax.ShapeDtypeStruct((t, D_INNER), F32),
        compiler_params=_params(("arbitrary",)),
        name="ssd_bwd" if rev else "ssd_fwd",
    )(*args)


def _post_kernel(oa_ref, ob_ref, xa_ref, xb_ref, y_ref, z_ref, gt_ref, mod_ref, subg_ref,
                 ssdg_ref, n2g_ref, wa_ref, ws_ref, wo_ref, wrt_ref, brt_ref, x1_ref, h2b_ref, rt_ref, an_s, yn_s,
                 *, n_first):
    m = mod_ref[0]
    o = _group_pick(oa_ref, ob_ref, n_first)
    for h in range(ATTN_HEADS):
        oh = o[:, h * V_DIM:(h + 1) * V_DIM]
        r = lax.rsqrt(jnp.mean(oh * oh, axis=-1, keepdims=True) + RMS_EPS)
        an_s[:, h * V_DIM:(h + 1) * V_DIM] = (((oh * r) * subg_ref[...]) * (1.0 - LAMBDA_INIT)).astype(BF16)
    attn_d = _dot(an_s[...], wa_ref[...])

    y = y_ref[...] * z_ref[...].astype(F32)
    gw = D_INNER // SSD_GROUPS
    for g in range(SSD_GROUPS):
        yg = y[:, g * gw:(g + 1) * gw]
        r = lax.rsqrt(jnp.mean(yg * yg, axis=-1, keepdims=True) + RMS_EPS)
        yn_s[:, g * gw:(g + 1) * gw] = ((yg * r) * ssdg_ref[:, g * gw:(g + 1) * gw]).astype(BF16)
    ssd_d = _dot(yn_s[...], ws_ref[...])

    gt = gt_ref[...].astype(F32)
    mix = gt[:, :D_MODEL] * attn_d + gt[:, D_MODEL:] * ssd_d
    mixed = _dot(mix.astype(BF16), wo_ref[...])
    x1 = _group_pick(xa_ref, xb_ref, n_first) + m[2:3] * mixed
    x1_ref[...] = x1
    r = lax.rsqrt(jnp.mean(x1 * x1, axis=-1, keepdims=True) + RMS_EPS)
    h2 = ((x1 * r) * n2g_ref[...]) * (1.0 + m[4:5]) + m[3:4]
    h2b_ref[...] = h2.astype(BF16)
    rt_ref[...] = _route(_dot_f32_3(h2, wrt_ref[...]) + brt_ref[...])


def _post(oa, ob, xa, xb, y, z, gates, mod_tiles, subg, ssdg, n2g, wa, ws, wo, w_rt, b_rt, tm):
    t = y.shape[0]
    n_first = oa.shape[0] // tm
    tok = lambda n: pl.BlockSpec((tm, n), lambda i: (i, 0))
    const = lambda a: pl.BlockSpec(a.shape, lambda i: (0, 0), pipeline_mode=pl.Buffered(1))
    return pl.pallas_call(
        functools.partial(_post_kernel, n_first=n_first),
        grid=(t // tm,),
        in_specs=_group_specs(tm, ATTN_WIDTH, n_first) + _group_specs(tm, D_MODEL, n_first) + [
            tok(D_INNER), tok(D_INNER), tok(GATE_COLS),
            pl.BlockSpec((1, 6, D_MODEL), lambda i: (i, 0, 0)),
            const(subg), const(ssdg), const(n2g), const(wa), const(ws), const(wo), const(w_rt), const(b_rt)],
        out_specs=[tok(D_MODEL), tok(D_MODEL), tok(LANES)],
        out_shape=[jax.ShapeDtypeStruct((t, D_MODEL), F32),
                   jax.ShapeDtypeStruct((t, D_MODEL), BF16),
                   jax.ShapeDtypeStruct((t, LANES), F32)],
        scratch_shapes=[pltpu.VMEM((tm, ATTN_WIDTH), BF16), pltpu.VMEM((tm, D_INNER), BF16)],
        compiler_params=_params(("parallel",)),
        name="merge_out_proj",
    )(oa, ob, xa, xb, y, z, gates, mod_tiles, subg, ssdg, n2g, wa, ws, wo, w_rt, b_rt)


def _route(logits):
    tm = logits.shape[0]
    lane = lax.broadcasted_iota(jnp.int32, (tm, LANES), 1)
    lane_f = lane.astype(F32)
    big = float(LANES)
    neg = -jnp.inf
    gl = jnp.where(lane < N_EXPERT_GROUPS, logits, neg)
    gmax = jnp.max(gl, axis=1, keepdims=True)
    g_sel = jnp.min(jnp.where(gl == gmax, lane_f, big), axis=1, keepdims=True)
    g_w = 1.0 / jnp.sum(jnp.exp(gl - gmax), axis=1, keepdims=True)
    e_lane = lane - N_EXPERT_GROUPS
    e_group = (e_lane >> 3).astype(F32)
    in_group = (e_lane >= 0) & (e_lane < N_EXPERTS) & (e_group == g_sel)
    el = jnp.where(in_group, logits, neg)
    v1 = jnp.max(el, axis=1, keepdims=True)
    i1 = jnp.min(jnp.where(el == v1, lane_f, big), axis=1, keepdims=True)
    el2 = jnp.where(lane_f == i1, neg, el)
    v2 = jnp.max(el2, axis=1, keepdims=True)
    i2 = jnp.min(jnp.where(el2 == v2, lane_f, big), axis=1, keepdims=True)
    e2 = jnp.exp(v2 - v1)
    w1 = g_w / (1.0 + e2)
    w2 = g_w * e2 / (1.0 + e2)
    return jnp.where(lane == 0, i1 - N_EXPERT_GROUPS,
                     jnp.where(lane == 1, i2 - N_EXPERT_GROUPS,
                               jnp.where(lane == 2, w1, jnp.where(lane == 3, w2, 0.0))))


def _expert_kernel(be_ref, nu_ref, x_ref, wg_ref, wu_ref, wd_ref, o_ref, wg_s, wu_s, wd_s):
    i = pl.program_id(0)
    used = i < nu_ref[0]
    new_expert = (i == 0) | (be_ref[i] != be_ref[jnp.maximum(i - 1, 0)])

    @pl.when(used & new_expert)
    def _():
        wg_s[...] = wg_ref[0].astype(BF16)
        wu_s[...] = wu_ref[0].astype(BF16)
        wd_s[...] = wd_ref[0].astype(BF16)

    @pl.when(used)
    def _():
        x = x_ref[...]
        a = _silu(_dot(x, wg_s[...])) * _dot(x, wu_s[...])
        o_ref[...] = _dot(a.astype(BF16), wd_s[...]).astype(o_ref.dtype)

    @pl.when(i >= nu_ref[0])
    def _():
        o_ref[...] = jnp.zeros(o_ref.shape, o_ref.dtype)


def _experts(xb, blk_e, n_used, wg, wu, wd):
    cap = xb.shape[0]
    nb = cap // EXPERT_BLOCK
    grid_spec = pltpu.PrefetchScalarGridSpec(
        num_scalar_prefetch=2,
        grid=(nb,),
        in_specs=[pl.BlockSpec((EXPERT_BLOCK, D_MODEL), lambda i, be, nu: (jnp.minimum(i, nu[0] - 1), 0)),
                  pl.BlockSpec((1, D_MODEL, EXPERT_FF), lambda i, be, nu: (be[i], 0, 0)),
                  pl.BlockSpec((1, D_MODEL, EXPERT_FF), lambda i, be, nu: (be[i], 0, 0)),
                  pl.BlockSpec((1, EXPERT_FF, D_MODEL), lambda i, be, nu: (be[i], 0, 0))],
        out_specs=pl.BlockSpec((EXPERT_BLOCK, D_MODEL), lambda i, be, nu: (i, 0)),
        scratch_shapes=[pltpu.VMEM((D_MODEL, EXPERT_FF), BF16), pltpu.VMEM((D_MODEL, EXPERT_FF), BF16),
                        pltpu.VMEM((EXPERT_FF, D_MODEL), BF16)],
    )
    return pl.pallas_call(
        _expert_kernel,
        grid_spec=grid_spec,
        out_shape=jax.ShapeDtypeStruct((cap, D_MODEL), BF16),
        compiler_params=_params(("arbitrary",)),
        name="expert_mlp",
    )(blk_e, n_used, xb, wg, wu, wd)


def _final_kernel(x1_ref, mod_ref, rt_ref, g0_ref, g1_ref, o_ref):
    m = mod_ref[0]
    rt = rt_ref[...]
    moe = g0_ref[...].astype(F32) * rt[:, 2:3] + g1_ref[...].astype(F32) * rt[:, 3:4]
    o_ref[...] = x1_ref[...] + m[5:6] * moe


def _final(x1, mod_tiles, rt, g, tok_off, n_tok, tm):
    b0 = tok_off // tm
    b1 = (g.shape[0] // TOP_K_INNER) // tm
    tok = lambda n: pl.BlockSpec((tm, n), lambda i: (b0 + i, 0))
    return pl.pallas_call(
        _final_kernel,
        grid=(n_tok // tm,),
        in_specs=[tok(D_MODEL), pl.BlockSpec((1, 6, D_MODEL), lambda i: (b0 + i, 0, 0)), tok(LANES),
                  tok(D_MODEL), pl.BlockSpec((tm, D_MODEL), lambda i: (b1 + b0 + i, 0))],
        out_specs=pl.BlockSpec((tm, D_MODEL), lambda i: (i, 0)),
        out_shape=jax.ShapeDtypeStruct((n_tok, D_MODEL), F32),
        compiler_params=_params(("parallel",)),
        name="moe_combine",
    )(x1, mod_tiles, rt, g, g)


ROT = QK_DIM // 2


def _permute_head_cols(w):
    k = w.shape[0]
    return w.reshape(k, ATTN_HEADS, 2, 2, ROT).transpose(0, 1, 3, 2, 4).reshape(k, ATTN_HEADS * LANES)


def _permute_gain(g):
    return jnp.broadcast_to(g.reshape(2, 1, ROT), (2, 2, ROT)).reshape(1, LANES)


def _rope_tables(seq):
    pos = jnp.arange(seq, dtype=F32)
    inv = 1.0 / (ROPE_THETA ** (jnp.arange(0, QK_DIM, 2, dtype=F32) / QK_DIM))
    ang = pos[:, None] * inv[None, :]
    cos, sin = jnp.cos(ang), jnp.sin(ang)
    cos_t = jnp.tile(cos, (1, LANES // ROT))
    sin_t = jnp.concatenate([-sin, -sin, sin, sin], axis=1)
    return cos_t, sin_t


def _dest_kernel(rt_ref, base_ref, tri_ref, o_ref, carry_s):
    @pl.when(pl.program_id(0) == 0)
    def _():
        carry_s[...] = jnp.zeros(carry_s.shape, F32)

    rt = rt_ref[...]
    tm = rt.shape[0]
    lane = lax.broadcasted_iota(jnp.int32, (tm, LANES), 1)
    lane_f = lane.astype(F32)
    oh0 = lane_f == rt[:, 0:1]
    oh1 = lane_f == rt[:, 1:2]
    both = jnp.where(oh0 | oh1, 1.0, 0.0)
    pos = base_ref[...] + carry_s[0:1, :] + _dot(tri_ref[...], both.astype(BF16))
    d0 = jnp.sum(jnp.where(oh0, pos, 0.0), axis=1, keepdims=True)
    d1 = jnp.sum(jnp.where(oh1, pos, 0.0), axis=1, keepdims=True)
    o_ref[...] = jnp.where(lane == 0, d0, jnp.where(lane == 1, d1, 0.0)).astype(jnp.int32)
    carry_s[...] = carry_s[...] + jnp.sum(both, axis=0, keepdims=True)


def _dispatch(rt, n_tok, tm):
    n_slots = n_tok * TOP_K_INNER
    flat_e = rt[:, :TOP_K_INNER].astype(jnp.int32).reshape(-1)
    counts = jnp.sum((flat_e[:, None] == jnp.arange(N_EXPERTS, dtype=jnp.int32)[None, :]).astype(jnp.int32), axis=0)
    padded = ((counts + EXPERT_BLOCK - 1) // EXPERT_BLOCK) * EXPERT_BLOCK
    pad_end = jnp.cumsum(padded)
    pad_start = pad_end - padded
    base = jnp.pad(pad_start.astype(F32), (0, LANES - N_EXPERTS)).reshape(1, LANES)
    r = jnp.arange(tm)
    tri = (r[None, :] < r[:, None]).astype(BF16)
    dest = pl.pallas_call(
        _dest_kernel,
        grid=(n_tok // tm,),
        in_specs=[pl.BlockSpec((tm, LANES), lambda i: (i, 0)),
                  pl.BlockSpec((1, LANES), lambda i: (0, 0)),
                  pl.BlockSpec((tm, tm), lambda i: (0, 0))],
        out_specs=pl.BlockSpec((tm, LANES), lambda i: (i, 0)),
        out_shape=jax.ShapeDtypeStruct((n_tok, LANES), jnp.int32),
        scratch_shapes=[pltpu.VMEM((SUBLANES, LANES), F32)],
        compiler_params=_params(("arbitrary",)),
        name="dispatch_rows",
    )(rt, base, tri)[:, :TOP_K_INNER]
    cap = n_slots + N_EXPERTS * EXPERT_BLOCK
    nb = cap // EXPERT_BLOCK
    blk_row0 = jnp.arange(nb, dtype=jnp.int32) * EXPERT_BLOCK
    blk_e = jnp.minimum(jnp.sum((pad_end[None, :] <= blk_row0[:, None]).astype(jnp.int32), axis=1), N_EXPERTS - 1)
    n_used = (pad_end[-1] // EXPERT_BLOCK).astype(jnp.int32).reshape(1)

    n_pad = cap - n_slots
    seg_len = jnp.concatenate([padded - counts, (cap - pad_end[-1]).reshape(1)])
    seg_end = jnp.cumsum(seg_len)
    seg_start = seg_end - seg_len
    seg_row0 = jnp.concatenate([pad_start + counts, pad_end[-1:]])
    j = jnp.arange(n_pad, dtype=jnp.int32)
    seg_onehot = (jnp.sum((seg_end[None, :] <= j[:, None]).astype(jnp.int32), axis=1)[:, None]
                  == jnp.arange(N_EXPERTS + 1, dtype=jnp.int32)[None, :]).astype(jnp.int32)
    pad_rows = j + jnp.sum(seg_onehot * (seg_row0 - seg_start)[None, :], axis=1)
    rows = jnp.concatenate([dest.reshape(-1), pad_rows.astype(jnp.int32)])
    toks = jnp.concatenate([jnp.arange(n_slots, dtype=jnp.int32) // TOP_K_INNER, j % n_tok])
    _, tok_buf = lax.sort_key_val(rows, toks)
    return dest, tok_buf, blk_e.astype(jnp.int32), n_used


def kernel(x_prompt, x_sample, c_prompt, c_sample, w_ada, b_ada, norm1_g, w_in, q_norm_g, k_norm_g, lambda_q1, lambda_k1, lambda_q2, lambda_k2, attn_subln_g, w_attn_o, conv_w, conv_b, dt_bias, a_log, d_skip, ssd_norm_g, w_ssd_o, w_out, norm2_g, w_group, b_group, w_router, b_router, w_gate_e, w_up_e, w_down_e):
    groups = [(x_prompt, c_prompt), (x_sample, c_sample)]
    seqs = [(x.shape[0], x.shape[1]) for x, _ in groups]
    n_tok = sum(b * s for b, s in seqs)
    min_seq = min(s for _, s in seqs)
    tm = min(1024, min_seq)
    tp = min(256, min_seq)
    tq = min(512, min_seq)
    tk = min(512, min_seq)
    layer = 0

    xa, xb_in = (g[0].reshape(-1, D_MODEL) for g in groups)
    c = jnp.concatenate([g[1] for g in groups], axis=0)
    n_batch = c.shape[0]
    c_pad = jnp.pad(c, ((0, (-n_batch) % SUBLANES), (0, 0)))
    mod = _ada(c_pad, w_ada[layer], b_ada[layer]).reshape(-1, 6, D_MODEL)
    tok_batch = jnp.concatenate([jnp.repeat(jnp.arange(b, dtype=jnp.int32), s) + off
                                 for (b, s), off in zip(seqs, [0, seqs[0][0]])])
    mod_tm = mod[tok_batch[::tm]]
    mod_tp = mod[tok_batch[::tp]]

    w_in_b = w_in[layer].astype(BF16)
    tabs = [_rope_tables(s) for _, s in seqs]
    cos_t = jnp.concatenate([jnp.tile(tb[0], (b, 1)) for tb, (b, _) in zip(tabs, seqs)], axis=0)
    sin_t = jnp.concatenate([jnp.tile(tb[1], (b, 1)) for tb, (b, _) in zip(tabs, seqs)], axis=0)
    half = (jnp.arange(LANES) // ROT) % 2
    bd = (half[:, None] == half[None, :]).astype(BF16)
    gq = _permute_gain(q_norm_g[layer]) * (QK_DIM ** -0.5 * LOG2_E)
    gk = _permute_gain(k_norm_g[layer])
    q, h = _norm_q_proj(xa, xb_in, mod_tm, norm1_g[layer], _permute_head_cols(w_in_b[:, OFF_Q:OFF_K]),
                        gq, cos_t, sin_t, bd, tm)
    k = _qk_proj(h, _permute_head_cols(w_in_b[:, OFF_K:OFF_V]), gk, cos_t, sin_t, bd, tm, "k_proj")
    vt3 = _vt_proj(h, w_in_b[:, OFF_V:OFF_Z], tm, tk)
    z_act = _matmul(h, w_in_b[:, OFF_Z:OFF_XBC], BF16, tm, 1024, "z_proj", _silu)
    w_dt =jnp.pad(w_in_b[:, OFF_DT:OFF_GATE], ((0, 0), (0, LANES - DT_COLS)))
    dt_b = jnp.pad(dt_bias[layer].reshape(1, DT_COLS), ((0, 0), (0, LANES - DT_COLS)))
    dt = _matmul(h, w_dt, F32, tm, LANES, "dt_proj", _softplus, dt_b)
    gates = _matmul(h, w_in_b[:, OFF_GATE:], BF16, tm, 1024, "gate_proj", _sigmoid)

    k3 = k.reshape(n_tok // tk, tk, ATTN_WIDTH)
    lams = [v[layer].reshape(1, QK_DIM) for v in (lambda_q1, lambda_k1, lambda_q2, lambda_k2)]
    o_groups = []
    off = 0
    for b, s in seqs:
        o_groups.append(_attention(q, k3, vt3, lams, off, b, s, tq, tk))
        off += b * s

    seq_starts = []
    off = 0
    for b, s in seqs:
        seq_starts += [(off + i * s, s) for i in range(b)]
        off += b * s
    tile_start = jnp.zeros((n_tok // tm,), jnp.int32)
    tile_end = jnp.zeros((n_tok // tm,), jnp.int32)
    nc = n_tok // CHUNK
    chunk_reset = jnp.zeros((nc,), jnp.int32)
    bwd_idx = jnp.zeros((nc,), jnp.int32)
    for st, s in seq_starts:
        tile_start = tile_start.at[st // tm].set(1)
        tile_end = tile_end.at[(st + s) // tm - 1].set(1)
        c0, c1 = st // CHUNK, (st + s) // CHUNK
        chunk_reset = chunk_reset.at[c0].set(1)
        bwd_idx = bwd_idx.at[c0:c1].set(jnp.arange(c1 - 1, c0 - 1, -1, dtype=jnp.int32))
    fwd_idx = jnp.arange(nc, dtype=jnp.int32)
    w_x, w_bc = w_in_b[:, OFF_XBC:OFF_XBC + D_INNER], w_in_b[:, OFF_XBC + D_INNER:OFF_DT]
    cw, cbias = conv_w[layer], conv_b[layer]
    xs = _proj_conv(h, w_x, cw[:, :D_INNER], cbias[:D_INNER], tile_start, tile_end, F32, tm, 1024, "x_proj_conv")
    bcm = _proj_conv(h, w_bc, cw[:, D_INNER:], cbias[D_INNER:], tile_start, tile_end, BF16, tm, 1024,
                     "bc_proj_conv")
    dtt = dt.T
    alg = a_log[layer].reshape(-1)
    dskip = jnp.repeat(d_skip[layer], SSD_HEAD_DIM).reshape(1, D_INNER)
    yb = _ssd(xs, bcm, dt, dtt, alg, bwd_idx, chunk_reset)
    y_ssd = _ssd(xs, bcm, dt, dtt, alg, fwd_idx, chunk_reset, yb, dskip)

    subg = attn_subln_g[layer].reshape(1, V_DIM)
    n_rt = N_EXPERT_GROUPS + N_EXPERTS
    w_rt = jnp.pad(jnp.concatenate([w_group[layer], w_router[layer]], axis=1), ((0, 0), (0, LANES - n_rt)))
    b_rt = jnp.pad(jnp.concatenate([b_group[layer], b_router[layer]]), (0, LANES - n_rt)).reshape(1, LANES)
    tg = min(512, min_seq)
    x1, h2b, rt = _post(o_groups[0], o_groups[1], xa, xb_in, y_ssd, z_act, gates, mod[tok_batch[::tg]], subg,
                        ssd_norm_g[layer].reshape(1, D_INNER), norm2_g[layer].reshape(1, D_MODEL),
                        w_attn_o[layer].astype(BF16), w_ssd_o[layer].astype(BF16),
                        w_out[layer].astype(BF16), w_rt, b_rt, tg)

    dest, tok_buf, blk_e, n_used = _dispatch(rt, n_tok, min(512, min_seq))
    xb = h2b[tok_buf]
    yb_e = _experts(xb, blk_e, n_used, w_gate_e[layer], w_up_e[layer], w_down_e[layer])
    g = yb_e[dest.T.reshape(-1)]

    outs = []
    off = 0
    for (b, s), (xg, _) in zip(seqs, groups):
        y = _final(x1, mod_tp, rt, g, off, b * s, tp)
        outs.append(y.reshape(xg.shape))
        off += b * s
    return tuple(outs)
```

```python
import functools
import math

import jax
import jax.numpy as jnp
from jax import lax
from jax.experimental import pallas as pl
from jax.experimental.pallas import tpu as pltpu

F32 = jnp.float32
BF16 = jnp.bfloat16

D_MODEL = 1024
ATTN_HEADS = 8
QK_DIM = 64
V_DIM = 2 * QK_DIM
ATTN_WIDTH = ATTN_HEADS * V_DIM
ROPE_THETA = 10000.0
D_INNER = 2048
SSD_HEAD_DIM = 64
SSD_HEADS = D_INNER // SSD_HEAD_DIM
SSD_GROUPS = 4
HEADS_PER_GROUP = SSD_HEADS // SSD_GROUPS
D_STATE = 128
CONV_K = 5
CONV_CH = D_INNER + 2 * SSD_GROUPS * D_STATE
CHUNK = 128
N_EXPERT_GROUPS = 4
EXPERTS_PER_GROUP = 8
N_EXPERTS = N_EXPERT_GROUPS * EXPERTS_PER_GROUP
TOP_K_INNER = 2
EXPERT_FF = 512
RMS_EPS = 1e-6
LAMBDA_INIT = 0.8 - 0.6 * math.exp(-0.3 * 0)

LANES = 128
SUBLANES = 8
MXU_COLS = 256
VMEM_LIMIT = 56 * 1024 * 1024
ROT = QK_DIM // 2
PROJ_COLS = 1024

Q_COLS = ATTN_HEADS * 2 * QK_DIM
K_COLS = Q_COLS
V_COLS = ATTN_WIDTH
Z_COLS = D_INNER
XBC_COLS = CONV_CH
DT_COLS = 2 * SSD_HEADS
GATE_COLS = 2 * D_MODEL
OFF_Q = 0
OFF_K = OFF_Q + Q_COLS
OFF_V = OFF_K + K_COLS
OFF_Z = OFF_V + V_COLS
OFF_XBC = OFF_Z + Z_COLS
OFF_DT = OFF_XBC + XBC_COLS
OFF_GATE = OFF_DT + DT_COLS

EXPERT_BLOCK = 512
ONES_ROWS = 16
LOG2_E = math.log2(math.e)


def _params(sem):
    return pltpu.CompilerParams(dimension_semantics=sem, vmem_limit_bytes=VMEM_LIMIT)


def _dot(a, b):
    return jnp.dot(a, b, preferred_element_type=F32)


def _dot_tn(a, b):
    return lax.dot_general(a, b, (((0,), (0,)), ((), ())), preferred_element_type=F32)


def _dot_nt(a, b):
    return lax.dot_general(a, b, (((1,), (1,)), ((), ())), preferred_element_type=F32)


def _split3(a):
    hi = a.astype(BF16)
    r = a - hi.astype(F32)
    mid = r.astype(BF16)
    lo = (r - mid.astype(F32)).astype(BF16)
    return hi, mid, lo


def _dot_left01(m01, a):
    hi, mid, lo = _split3(a)
    return _dot(m01, hi) + _dot(m01, mid) + _dot(m01, lo)


def _dot_right01(a, m01):
    hi, mid, lo = _split3(a)
    return _dot(hi, m01) + _dot(mid, m01) + _dot(lo, m01)


def _spread01(a, m01):
    hi = a.astype(BF16)
    lo = (a - hi.astype(F32)).astype(BF16)
    return _dot(hi, m01) + _dot(lo, m01)


def _dot_f32(a, b):
    a0, a1, a2 = _split3(a)
    b0, b1, b2 = _split3(b)
    return (_dot(a0, b0) + (_dot(a0, b1) + _dot(a1, b0))
            + (_dot(a0, b2) + _dot(a2, b0) + _dot(a1, b1)))


def _dot_f32_3(a, b):
    a0 = a.astype(BF16)
    a1 = (a - a0.astype(F32)).astype(BF16)
    b0 = b.astype(BF16)
    b1 = (b - b0.astype(F32)).astype(BF16)
    return _dot(a0, b0) + (_dot(a0, b1) + _dot(a1, b0))


def _sigmoid(x):
    return 1.0 / (1.0 + jnp.exp(-x))


def _silu(x):
    return x * _sigmoid(x)


def _softplus(x):
    e = jnp.exp(-jnp.abs(x))
    u = 1.0 + e
    log1p_e = jnp.where(u == 1.0, e, jnp.log(u) * (e / (u - 1.0)))
    return jnp.maximum(x, 0.0) + log1p_e


def _ada_kernel(c_ref, w_ref, b_ref, o_ref):
    o_ref[...] = _dot_f32(_silu(c_ref[...]), w_ref[...]) + b_ref[...]


def _ada(c_pad, w_ada, b_ada):
    rows = c_pad.shape[0]
    n = w_ada.shape[1]
    tn = PROJ_COLS
    return pl.pallas_call(
        _ada_kernel,
        grid=(n // tn,),
        in_specs=[pl.BlockSpec((rows, D_MODEL), lambda j: (0, 0)),
                  pl.BlockSpec((D_MODEL, tn), lambda j: (0, j)),
                  pl.BlockSpec((1, tn), lambda j: (0, j))],
        out_specs=pl.BlockSpec((rows, tn), lambda j: (0, j)),
        out_shape=jax.ShapeDtypeStruct((rows, n), F32),
        compiler_params=_params(("arbitrary",)),
        name="ada_mod",
    )(c_pad, w_ada, b_ada.reshape(1, n))


def _group_specs(tm, n, n_first):
    return [pl.BlockSpec((tm, n), lambda i: (jnp.minimum(i, n_first - 1), 0)),
            pl.BlockSpec((tm, n), lambda i: (jnp.maximum(i - n_first, 0), 0))]


def _group_pick(a_ref, b_ref, n_first):
    return jnp.where(pl.program_id(0) < n_first, a_ref[...], b_ref[...])


def _mm_kernel(a_ref, w_ref, *rest, act):
    o_ref = rest[-1]
    acc = _dot(a_ref[...], w_ref[...])
    if len(rest) == 2:
        acc = acc + rest[0][...]
    if act is not None:
        acc = act(acc)
    o_ref[...] = acc.astype(o_ref.dtype)


def _matmul(a, w, out_dtype, tm, tn, name, act=None, bias=None):
    m, k = a.shape
    n = w.shape[1]
    in_specs = [pl.BlockSpec((tm, k), lambda j, i: (i, 0)),
                pl.BlockSpec((k, tn), lambda j, i: (0, j))]
    args = [a, w]
    if bias is not None:
        in_specs.append(pl.BlockSpec((1, tn), lambda j, i: (0, j)))
        args.append(bias)
    return pl.pallas_call(
        functools.partial(_mm_kernel, act=act),
        grid=(n // tn, m // tm),
        in_specs=in_specs,
        out_specs=pl.BlockSpec((tm, tn), lambda j, i: (i, j)),
        out_shape=jax.ShapeDtypeStruct((m, n), out_dtype),
        compiler_params=_params(("parallel", "parallel")),
        name=name,
    )(*args)


def _qk_epilogue(acc, g_ref, cos_ref, sin_ref, bd_ref, o_ref):
    cos = cos_ref[...]
    sin = sin_ref[...]
    g = g_ref[...]
    bd = bd_ref[...]
    for h in range(ATTN_HEADS):
        x = acc[:, h * LANES:(h + 1) * LANES]
        sq = x * x
        hi = sq.astype(BF16)
        lo = (sq - hi.astype(F32)).astype(BF16)
        ss = _dot(hi, bd) + _dot(lo, bd)
        r = lax.rsqrt(ss * (1.0 / QK_DIM) + RMS_EPS)
        xn = (x * r) * g
        partner = pltpu.roll(xn, LANES // 2, 1)
        o_ref[:, h * LANES:(h + 1) * LANES] = (xn * cos + partner * sin).astype(o_ref.dtype)


def _qk_kernel(a_ref, w_ref, g_ref, cos_ref, sin_ref, bd_ref, o_ref):
    _qk_epilogue(_dot(a_ref[...], w_ref[...]), g_ref, cos_ref, sin_ref, bd_ref, o_ref)


def _norm_q_kernel(xa_ref, xb_ref, mod_ref, g1_ref, w_ref, g_ref, cos_ref, sin_ref, bd_ref, o_ref, h_ref,
                   *, n_first):
    x = _group_pick(xa_ref, xb_ref, n_first)
    r = lax.rsqrt(jnp.mean(x * x, axis=-1, keepdims=True) + RMS_EPS)
    m = mod_ref[0]
    h = (((x * r) * g1_ref[...]) * (1.0 + m[1:2]) + m[0:1]).astype(BF16)
    h_ref[...] = h
    _qk_epilogue(_dot(h, w_ref[...]), g_ref, cos_ref, sin_ref, bd_ref, o_ref)


def _qk_specs(tm, n):
    return [pl.BlockSpec((D_MODEL, n), lambda i: (0, 0)),
            pl.BlockSpec((1, LANES), lambda i: (0, 0)),
            pl.BlockSpec((tm, LANES), lambda i: (i, 0)),
            pl.BlockSpec((tm, LANES), lambda i: (i, 0)),
            pl.BlockSpec((LANES, LANES), lambda i: (0, 0))]


def _qk_proj(h, w, g128, cos_t, sin_t, bd, tm, name):
    t = h.shape[0]
    n = w.shape[1]
    return pl.pallas_call(
        _qk_kernel,
        grid=(t // tm,),
        in_specs=[pl.BlockSpec((tm, D_MODEL), lambda i: (i, 0))] + _qk_specs(tm, n),
        out_specs=pl.BlockSpec((tm, n), lambda i: (i, 0)),
        out_shape=jax.ShapeDtypeStruct((t, n), BF16),
        compiler_params=_params(("parallel",)),
        name=name,
    )(h, w, g128, cos_t, sin_t, bd)


def _norm_q_proj(xa, xb, mod_tiles, g1, w, g128, cos_t, sin_t, bd, tm):
    t = xa.shape[0] + xb.shape[0]
    n = w.shape[1]
    n_first = xa.shape[0] // tm
    tok = pl.BlockSpec((tm, n), lambda i: (i, 0))
    return pl.pallas_call(
        functools.partial(_norm_q_kernel, n_first=n_first),
        grid=(t // tm,),
        in_specs=_group_specs(tm, D_MODEL, n_first) + [
            pl.BlockSpec((1, 6, D_MODEL), lambda i: (i, 0, 0)),
            pl.BlockSpec((1, D_MODEL), lambda i: (0, 0))] + _qk_specs(tm, n),
        out_specs=[tok, pl.BlockSpec((tm, D_MODEL), lambda i: (i, 0))],
        out_shape=[jax.ShapeDtypeStruct((t, n), BF16), jax.ShapeDtypeStruct((t, D_MODEL), BF16)],
        compiler_params=_params(("parallel",)),
        name="norm1_q_proj",
    )(xa, xb, mod_tiles, g1.reshape(1, D_MODEL), w, g128, cos_t, sin_t, bd)


def _vt_kernel(a_ref, w_ref, o_ref, *, tk):
    acc = _dot(a_ref[...], w_ref[...])
    for c in range(acc.shape[0] // tk):
        o_ref[c] = acc[c * tk:(c + 1) * tk, :].T.astype(o_ref.dtype)


def _vt_proj(h, w, tm, tk):
    t = h.shape[0]
    n = w.shape[1]
    return pl.pallas_call(
        functools.partial(_vt_kernel, tk=tk),
        grid=(t // tm,),
        in_specs=[pl.BlockSpec((tm, D_MODEL), lambda i: (i, 0)),
                  pl.BlockSpec((D_MODEL, n), lambda i: (0, 0))],
        out_specs=pl.BlockSpec((tm // tk, n, tk), lambda i: (i, 0, 0)),
        out_shape=jax.ShapeDtypeStruct((t // tk, n, tk), BF16),
        compiler_params=_params(("parallel",)),
        name="v_proj_t",
    )(h, w)


def _attn_kernel(trips_ref, lq1_ref, lk1_ref, lq2_ref, lk2_ref, q_ref, k_ref, vt_ref, o_ref,
                 q2t_s, s_s, m_s, acc_s, *, tq, nkv, unroll):
    tk = k_ref.shape[1]
    qt = q_ref[...].astype(F32).T
    row = lax.broadcasted_iota(jnp.int32, qt.shape, 0)
    first_half = (row & ROT) == 0
    q2t_s[:, :tq] = jnp.where(first_half, qt, 0.0).astype(BF16)
    q2t_s[:, tq:] = jnp.where(first_half, 0.0, qt).astype(BF16)
    m_s[...] = jnp.full(m_s.shape, -jnp.inf, F32)
    acc_s[...] = jnp.zeros(acc_s.shape, F32)
    ones = jnp.ones((ONES_ROWS, tk), BF16)
    s_s[0] = _dot(k_ref[0], q2t_s[...])

    def step(j, par, compute_next):
        if compute_next:
            s_s[1 - par] = _dot(k_ref[j + 1], q2t_s[...])
        st = s_s[par]
        m_old = m_s[...]
        m_new = jnp.maximum(m_old, jnp.max(st, axis=0, keepdims=True))
        alpha = jnp.exp2(m_old - m_new)
        p = jnp.exp2(st - m_new).astype(BF16)
        lhs = jnp.concatenate([vt_ref[j], ones], axis=0)
        acc_s[...] = alpha * acc_s[...] + _dot(lhs, p)
        m_s[...] = m_new

    n_trips = (nkv - 1) // unroll

    def trip(t, carry):
        for u in range(unroll):
            step(unroll * t + u, u % 2, True)
        return carry

    lax.fori_loop(0, trips_ref[0], trip, 0)
    for j in range(unroll * n_trips, nkv):
        step(j, j % 2, j + 1 < nkv)

    lam = (jnp.exp(jnp.sum(lq1_ref[...] * lk1_ref[...], axis=1, keepdims=True))
           - jnp.exp(jnp.sum(lq2_ref[...] * lk2_ref[...], axis=1, keepdims=True)) + LAMBDA_INIT)
    ot = acc_s[:V_DIM, :] / acc_s[V_DIM:V_DIM + 1, :]
    o_ref[...] = (ot[:, :tq] - lam * ot[:, tq:]).T


def _attention(q, k3, vt3, lams, tok_off, batch, seq, tq, tk):
    nkv = seq // tk
    qb0 = tok_off // tq
    sb0 = tok_off // seq
    nq = seq // tq
    unroll = 4 if nkv >= 8 else 2
    n_trips = jnp.full((1,), (nkv - 1) // unroll, jnp.int32)
    lam_spec = pl.BlockSpec((1, QK_DIM), lambda b, h, i, nt: (0, 0))
    grid_spec = pltpu.PrefetchScalarGridSpec(
        num_scalar_prefetch=1,
        grid=(batch, ATTN_HEADS, nq),
        in_specs=[lam_spec, lam_spec, lam_spec, lam_spec,
                  pl.BlockSpec((tq, LANES), lambda b, h, i, nt: (qb0 + b * nq + i, h)),
                  pl.BlockSpec((nkv, tk, LANES), lambda b, h, i, nt: (sb0 + b, 0, h)),
                  pl.BlockSpec((nkv, LANES, tk), lambda b, h, i, nt: (sb0 + b, h, 0))],
        out_specs=pl.BlockSpec((tq, LANES), lambda b, h, i, nt: (b * nq + i, h)),
        scratch_shapes=[pltpu.VMEM((LANES, 2 * tq), BF16),
                        pltpu.VMEM((2, tk, 2 * tq), F32),
                        pltpu.VMEM((1, 2 * tq), F32),
                        pltpu.VMEM((V_DIM + ONES_ROWS, 2 * tq), F32)],
    )
    return pl.pallas_call(
        functools.partial(_attn_kernel, tq=tq, nkv=nkv, unroll=unroll),
        grid_spec=grid_spec,
        out_shape=jax.ShapeDtypeStruct((batch * seq, ATTN_WIDTH), F32),
        compiler_params=_params(("parallel", "parallel", "arbitrary")),
        name="diff_attention",
    )(n_trips, *lams, q, k3, vt3)


HALO = 16


def _proj_conv_kernel(start_ref, end_ref, prev_ref, cur_ref, next_ref, w_ref, cw_ref, cb_ref, o_ref, *, tm):
    i = pl.program_id(1)
    prev = jnp.where(start_ref[i] == 1, jnp.zeros_like(prev_ref[...]), prev_ref[...])
    nxt = jnp.where(end_ref[i] == 1, jnp.zeros_like(next_ref[...]), next_ref[...])
    a = jnp.concatenate([prev, cur_ref[...], nxt], axis=0)
    pad = (CONV_K - 1) // 2
    for c in range(0, o_ref.shape[1], MXU_COLS):
        ext = _dot(a, w_ref[:, c:c + MXU_COLS])
        acc = jnp.broadcast_to(cb_ref[:, c:c + MXU_COLS], (tm, MXU_COLS))
        for d in range(CONV_K):
            shift = (pad - d) % (tm + 2 * HALO)
            src = ext if shift == 0 else pltpu.roll(ext, shift, 0)
            acc = acc + cw_ref[d:d + 1, c:c + MXU_COLS] * src[HALO:HALO + tm, :]
        o_ref[:, c:c + MXU_COLS] = _silu(acc).astype(o_ref.dtype)


def _proj_conv(h, w, conv_w, conv_b, start_flags, end_flags, out_dtype, tm, tn, name):
    t = h.shape[0]
    n = w.shape[1]
    rb = tm // HALO
    last = t // HALO - 1
    cw_pad = jnp.zeros((SUBLANES, n), F32).at[:CONV_K].set(conv_w)
    grid_spec = pltpu.PrefetchScalarGridSpec(
        num_scalar_prefetch=2,
        grid=(n // tn, t // tm),
        in_specs=[pl.BlockSpec((HALO, D_MODEL), lambda j, i, s, e: (jnp.maximum(i * rb - 1, 0), 0)),
                  pl.BlockSpec((tm, D_MODEL), lambda j, i, s, e: (i, 0)),
                  pl.BlockSpec((HALO, D_MODEL), lambda j, i, s, e: (jnp.minimum((i + 1) * rb, last), 0)),
                  pl.BlockSpec((D_MODEL, tn), lambda j, i, s, e: (0, j)),
                  pl.BlockSpec((SUBLANES, tn), lambda j, i, s, e: (0, j)),
                  pl.BlockSpec((1, tn), lambda j, i, s, e: (0, j))],
        out_specs=pl.BlockSpec((tm, tn), lambda j, i, s, e: (i, j)),
    )
    return pl.pallas_call(
        functools.partial(_proj_conv_kernel, tm=tm),
        grid_spec=grid_spec,
        out_shape=jax.ShapeDtypeStruct((t, n), out_dtype),
        compiler_params=_params(("parallel", "parallel")),
        name=name,
    )(start_flags, end_flags, h, h, h, w, cw_pad, conv_b.reshape(1, n))


def _ssd_kernel(idx_ref, reset_ref, xs_ref, bc_ref, dt_ref, dtt_ref,
                alog_r_ref, alog_c_ref, tril_ref, triu_ref, e01_ref, *rest, rev):
    i = pl.program_id(0)
    nh = SSD_HEADS
    y_ref, state_s = rest[-2:]

    @pl.when(reset_ref[i] == 1)
    def _():
        state_s[...] = jnp.zeros(state_s.shape, F32)

    xs = xs_ref[...]
    bc = bc_ref[...]
    dtn = dt_ref[...]
    a = dtn * (-LOG2_E * jnp.exp(alog_r_ref[...]))
    pinc = _dot_left01(tril_ref[...], a)
    pex = pinc - a
    tot = pinc[CHUNK - 1:CHUNK, :]

    gw = HEADS_PER_GROUP * SSD_HEAD_DIM
    gn = SSD_GROUPS * D_STATE
    cd = jnp.broadcast_to(jnp.exp2(tot), (SUBLANES, LANES))

    def update_state(sc_st, sc_cd):
        w = (xs * sc_st).astype(BF16)
        for g in range(SSD_GROUPS):
            bg = bc[:, g * D_STATE:(g + 1) * D_STATE]
            new = _dot_tn(bg, w[:, g * gw:(g + 1) * gw])
            state_s[:, g * gw:(g + 1) * gw] = (state_s[:, g * gw:(g + 1) * gw] * sc_cd[:, g * gw:(g + 1) * gw]
                                               + new)

    sc_cd = _spread01(cd, e01_ref[...])[0:1]
    if rev:
        stacked = jnp.concatenate([jnp.exp2(pex) * dtn, jnp.exp2(tot - pex)], axis=0)
        ex = _dot(stacked.astype(BF16), e01_ref[...])
        sc_off = ex[CHUNK:]
        for g in range(SSD_GROUPS):
            cg = bc[:, gn + g * D_STATE:gn + (g + 1) * D_STATE]
            st = state_s[:, g * gw:(g + 1) * gw]
            y_ref[:, g * gw:(g + 1) * gw] = _dot(cg, st.astype(BF16)) * sc_off[:, g * gw:(g + 1) * gw]
        update_state(ex[:CHUNK], sc_cd)
        return

    yb_ref, dskip_ref = rest[:2]
    dtnt = dtt_ref[...]
    at = dtnt * (-LOG2_E * jnp.exp(alog_c_ref[...]))
    pinct = _dot_right01(at, triu_ref[...])
    pext = pinct - at
    li = lax.broadcasted_iota(jnp.int32, (CHUNK, CHUNK), 0)
    si = lax.broadcasted_iota(jnp.int32, (CHUNK, CHUNK), 1)
    lower = si <= li
    strict_lower = si < li
    strict_upper = si > li
    lane = lax.broadcasted_iota(jnp.int32, (CHUNK, LANES), 1)
    first_head = lane < SSD_HEAD_DIM
    xb = xs.astype(BF16)
    for g in range(SSD_GROUPS):
        bg = bc[:, g * D_STATE:(g + 1) * D_STATE]
        cg = bc[:, gn + g * D_STATE:gn + (g + 1) * D_STATE]
        cg_f = cg.astype(F32)
        cb = _dot_nt(cg, bg)
        for hp in range(HEADS_PER_GROUP // 2):
            col = g * gw + hp * LANES
            rhs = jnp.concatenate([xb[:, col:col + LANES], state_s[:, col:col + LANES].astype(BF16)], axis=0)
            pair = []
            for u in range(2):
                h = g * HEADS_PER_GROUP + 2 * hp + u
                colf = jnp.broadcast_to(pinc[:, h:h + 1], (CHUNK, CHUNK))
                arg = jnp.where(lower, colf - pinct[h:h + 1, :],
                                pext[nh + h:nh + h + 1, :] - pex[:, nh + h:nh + h + 1])
                dtf = dtnt[h:h + 1, :]
                dtb = dtnt[nh + h:nh + h + 1, :]
                coef = jnp.where(strict_lower, dtf, jnp.where(strict_upper, dtb, dtf + dtb))
                mh = (cb * jnp.exp2(arg) * coef).astype(BF16)
                dh = (cg_f * jnp.exp2(colf)).astype(BF16)
                pair.append(_dot(jnp.concatenate([mh, dh], axis=1), rhs))
            y_ref[:, col:col + LANES] = ((jnp.where(first_head, pair[0], pair[1]) + yb_ref[:, col:col + LANES])
                                         + dskip_ref[:, col:col + LANES] * xs[:, col:col + LANES])

    update_state(_dot((jnp.exp2(tot - pinc) * dtn).astype(BF16), e01_ref[...]), sc_cd)


def _ssd(xs, bc, dt, dtt, a_log, idx, reset, yb=None, dskip=None):
    rev = yb is None
    t = xs.shape[0]
    nc = t // CHUNK
    alog_r = jnp.pad(a_log.reshape(1, DT_COLS), ((0, 0), (0, LANES - DT_COLS)))
    alog_c = alog_r.reshape(LANES, 1)
    r = jnp.arange(CHUNK)
    tril = (r[None, :] <= r[:, None]).astype(BF16)
    triu = (r[:, None] <= r[None, :]).astype(BF16)
    lo = SSD_HEADS if rev else 0
    e01 = (jnp.arange(D_INNER)[None, :] // SSD_HEAD_DIM == jnp.arange(LANES)[:, None] - lo).astype(BF16)
    const = lambda shape: pl.BlockSpec(shape, lambda i, ix, rs: (0, 0))
    in_specs = [pl.BlockSpec((CHUNK, D_INNER), lambda i, ix, rs: (ix[i], 0)),
                pl.BlockSpec((CHUNK, CONV_CH - D_INNER), lambda i, ix, rs: (ix[i], 0)),
                pl.BlockSpec((CHUNK, LANES), lambda i, ix, rs: (ix[i], 0)),
                pl.BlockSpec((LANES, CHUNK), lambda i, ix, rs: (0, ix[i])),
                const((1, LANES)), const((LANES, 1)),
                const((CHUNK, CHUNK)), const((CHUNK, CHUNK)), const((LANES, D_INNER))]
    args = [idx, reset, xs, bc, dt, dtt, alog_r, alog_c, tril, triu, e01]
    if not rev:
        in_specs += [pl.BlockSpec((CHUNK, D_INNER), lambda i, ix, rs: (ix[i], 0)), const((1, D_INNER))]
        args += [yb, dskip]
    grid_spec = pltpu.PrefetchScalarGridSpec(
        num_scalar_prefetch=2,
        grid=(nc,),
        in_specs=in_specs,
        out_specs=pl.BlockSpec((CHUNK, D_INNER), lambda i, ix, rs: (ix[i], 0)),
        scratch_shapes=[pltpu.VMEM((D_STATE, D_INNER), F32)],
    )
    return pl.pallas_call(
        functools.partial(_ssd_kernel, rev=rev),
        grid_spec=grid_spec,
        out_shape=jax.ShapeDtypeStruct((t, D_INNER), F32),
        compiler_params=_params(("arbitrary",)),
        name="ssd_bwd" if rev else "ssd_fwd",
    )(*args)


def _post_kernel(oa_ref, ob_ref, xa_ref, xb_ref, y_ref, z_ref, gt_ref, mod_ref, subg_ref,
                 ssdg_ref, n2g_ref, wa_ref, ws_ref, wo_ref, wrt_ref, brt_ref, x1_ref, h2b_ref, rt_ref, an_s, yn_s,
                 *, n_first):
    m = mod_ref[0]
    o = _group_pick(oa_ref, ob_ref, n_first)
    for h in range(ATTN_HEADS):
        oh = o[:, h * V_DIM:(h + 1) * V_DIM]
        r = lax.rsqrt(jnp.mean(oh * oh, axis=-1, keepdims=True) + RMS_EPS)
        an_s[:, h * V_DIM:(h + 1) * V_DIM] = (((oh * r) * subg_ref[...]) * (1.0 - LAMBDA_INIT)).astype(BF16)
    attn_d = _dot(an_s[...], wa_ref[...])

    y = y_ref[...] * z_ref[...].astype(F32)
    gw = D_INNER // SSD_GROUPS
    for g in range(SSD_GROUPS):
        yg = y[:, g * gw:(g + 1) * gw]
        r = lax.rsqrt(jnp.mean(yg * yg, axis=-1, keepdims=True) + RMS_EPS)
        yn_s[:, g * gw:(g + 1) * gw] = ((yg * r) * ssdg_ref[:, g * gw:(g + 1) * gw]).astype(BF16)
    ssd_d = _dot(yn_s[...], ws_ref[...])

    gt = gt_ref[...].astype(F32)
    mix = gt[:, :D_MODEL] * attn_d + gt[:, D_MODEL:] * ssd_d
    mixed = _dot(mix.astype(BF16), wo_ref[...])
    x1 = _group_pick(xa_ref, xb_ref, n_first) + m[2:3] * mixed
    x1_ref[...] = x1
    r = lax.rsqrt(jnp.mean(x1 * x1, axis=-1, keepdims=True) + RMS_EPS)
    h2 = ((x1 * r) * n2g_ref[...]) * (1.0 + m[4:5]) + m[3:4]
    h2b_ref[...] = h2.astype(BF16)
    rt_ref[...] = _route(_dot_f32_3(h2, wrt_ref[...]) + brt_ref[...])


def _post(oa, ob, xa, xb, y, z, gates, mod_tiles, subg, ssdg, n2g, wa, ws, wo, w_rt, b_rt, tm):
    t = y.shape[0]
    n_first = oa.shape[0] // tm
    tok = lambda n: pl.BlockSpec((tm, n), lambda i: (i, 0))
    const = lambda a: pl.BlockSpec(a.shape, lambda i: (0, 0), pipeline_mode=pl.Buffered(1))
    return pl.pallas_call(
        functools.partial(_post_kernel, n_first=n_first),
        grid=(t // tm,),
        in_specs=_group_specs(tm, ATTN_WIDTH, n_first) + _group_specs(tm, D_MODEL, n_first) + [
            tok(D_INNER), tok(D_INNER), tok(GATE_COLS),
            pl.BlockSpec((1, 6, D_MODEL), lambda i: (i, 0, 0)),
            const(subg), const(ssdg), const(n2g), const(wa), const(ws), const(wo), const(w_rt), const(b_rt)],
        out_specs=[tok(D_MODEL), tok(D_MODEL), tok(LANES)],
        out_shape=[jax.ShapeDtypeStruct((t, D_MODEL), F32),
                   jax.ShapeDtypeStruct((t, D_MODEL), BF16),
                   jax.ShapeDtypeStruct((t, LANES), F32)],
        scratch_shapes=[pltpu.VMEM((tm, ATTN_WIDTH), BF16), pltpu.VMEM((tm, D_INNER), BF16)],
        compiler_params=_params(("parallel",)),
        name="merge_out_proj",
    )(oa, ob, xa, xb, y, z, gates, mod_tiles, subg, ssdg, n2g, wa, ws, wo, w_rt, b_rt)


def _route(logits):
    tm = logits.shape[0]
    lane = lax.broadcasted_iota(jnp.int32, (tm, LANES), 1)
    lane_f = lane.astype(F32)
    big = float(LANES)
    neg = -jnp.inf
    gl = jnp.where(lane < N_EXPERT_GROUPS, logits, neg)
    gmax = jnp.max(gl, axis=1, keepdims=True)
    g_sel = jnp.min(jnp.where(gl == gmax, lane_f, big), axis=1, keepdims=True)
    g_w = 1.0 / jnp.sum(jnp.exp(gl - gmax), axis=1, keepdims=True)
    e_lane = lane - N_EXPERT_GROUPS
    e_group = (e_lane >> 3).astype(F32)
    in_group = (e_lane >= 0) & (e_lane < N_EXPERTS) & (e_group == g_sel)
    el = jnp.where(in_group, logits, neg)
    v1 = jnp.max(el, axis=1, keepdims=True)
    i1 = jnp.min(jnp.where(el == v1, lane_f, big), axis=1, keepdims=True)
    el2 = jnp.where(lane_f == i1, neg, el)
    v2 = jnp.max(el2, axis=1, keepdims=True)
    i2 = jnp.min(jnp.where(el2 == v2, lane_f, big), axis=1, keepdims=True)
    e2 = jnp.exp(v2 - v1)
    w1 = g_w / (1.0 + e2)
    w2 = g_w * e2 / (1.0 + e2)
    return jnp.where(lane == 0, i1 - N_EXPERT_GROUPS,
                     jnp.where(lane == 1, i2 - N_EXPERT_GROUPS,
                               jnp.where(lane == 2, w1, jnp.where(lane == 3, w2, 0.0))))


def _expert_kernel(be_ref, nu_ref, x_ref, wg_ref, wu_ref, wd_ref, o_ref, wg_s, wu_s, wd_s):
    i = pl.program_id(0)
    used = i < nu_ref[0]
    new_expert = (i == 0) | (be_ref[i] != be_ref[jnp.maximum(i - 1, 0)])

    @pl.when(used & new_expert)
    def _():
        wg_s[...] = wg_ref[0].astype(BF16)
        wu_s[...] = wu_ref[0].astype(BF16)
        wd_s[...] = wd_ref[0].astype(BF16)

    @pl.when(used)
    def _():
        x = x_ref[...]
        a = _silu(_dot(x, wg_s[...])) * _dot(x, wu_s[...])
        o_ref[...] = _dot(a.astype(BF16), wd_s[...]).astype(o_ref.dtype)

    @pl.when(i >= nu_ref[0])
    def _():
        o_ref[...] = jnp.zeros(o_ref.shape, o_ref.dtype)


def _experts(xb, blk_e, n_used, wg, wu, wd):
    cap = xb.shape[0]
    nb = cap // EXPERT_BLOCK
    grid_spec = pltpu.PrefetchScalarGridSpec(
        num_scalar_prefetch=2,
        grid=(nb,),
        in_specs=[pl.BlockSpec((EXPERT_BLOCK, D_MODEL), lambda i, be, nu: (jnp.minimum(i, nu[0] - 1), 0)),
                  pl.BlockSpec((1, D_MODEL, EXPERT_FF), lambda i, be, nu: (be[i], 0, 0)),
                  pl.BlockSpec((1, D_MODEL, EXPERT_FF), lambda i, be, nu: (be[i], 0, 0)),
                  pl.BlockSpec((1, EXPERT_FF, D_MODEL), lambda i, be, nu: (be[i], 0, 0))],
        out_specs=pl.BlockSpec((EXPERT_BLOCK, D_MODEL), lambda i, be, nu: (i, 0)),
        scratch_shapes=[pltpu.VMEM((D_MODEL, EXPERT_FF), BF16), pltpu.VMEM((D_MODEL, EXPERT_FF), BF16),
                        pltpu.VMEM((EXPERT_FF, D_MODEL), BF16)],
    )
    return pl.pallas_call(
        _expert_kernel,
        grid_spec=grid_spec,
        out_shape=jax.ShapeDtypeStruct((cap, D_MODEL), BF16),
        compiler_params=_params(("arbitrary",)),
        name="expert_mlp",
    )(blk_e, n_used, xb, wg, wu, wd)


def _final_kernel(x1_ref, mod_ref, rt_ref, g0_ref, g1_ref, o_ref):
    m = mod_ref[0]
    rt = rt_ref[...]
    moe = g0_ref[...].astype(F32) * rt[:, 2:3] + g1_ref[...].astype(F32) * rt[:, 3:4]
    o_ref[...] = x1_ref[...] + m[5:6] * moe


def _final(x1, mod_tiles, rt, g, tok_off, n_tok, tm):
    b0 = tok_off // tm
    b1 = (g.shape[0] // TOP_K_INNER) // tm
    tok = lambda n: pl.BlockSpec((tm, n), lambda i: (b0 + i, 0))
    return pl.pallas_call(
        _final_kernel,
        grid=(n_tok // tm,),
        in_specs=[tok(D_MODEL), pl.BlockSpec((1, 6, D_MODEL), lambda i: (b0 + i, 0, 0)), tok(LANES),
                  tok(D_MODEL), pl.BlockSpec((tm, D_MODEL), lambda i: (b1 + b0 + i, 0))],
        out_specs=pl.BlockSpec((tm, D_MODEL), lambda i: (i, 0)),
        out_shape=jax.ShapeDtypeStruct((n_tok, D_MODEL), F32),
        compiler_params=_params(("parallel",)),
        name="moe_combine",
    )(x1, mod_tiles, rt, g, g)


def _permute_head_cols(w):
    k = w.shape[0]
    return w.reshape(k, ATTN_HEADS, 2, 2, ROT).transpose(0, 1, 3, 2, 4).reshape(k, ATTN_HEADS * LANES)


def _permute_gain(g):
    return jnp.broadcast_to(g.reshape(2, 1, ROT), (2, 2, ROT)).reshape(1, LANES)


def _rope_tables(seq):
    pos = jnp.arange(seq, dtype=F32)
    inv = 1.0 / (ROPE_THETA ** (jnp.arange(0, QK_DIM, 2, dtype=F32) / QK_DIM))
    ang = pos[:, None] * inv[None, :]
    cos, sin = jnp.cos(ang), jnp.sin(ang)
    cos_t = jnp.tile(cos, (1, LANES // ROT))
    sin_t = jnp.concatenate([-sin, -sin, sin, sin], axis=1)
    return cos_t, sin_t


def _dest_kernel(rt_ref, base_ref, tri_ref, o_ref, carry_s):
    @pl.when(pl.program_id(0) == 0)
    def _():
        carry_s[...] = jnp.zeros(carry_s.shape, F32)

    rt = rt_ref[...]
    tm = rt.shape[0]
    lane = lax.broadcasted_iota(jnp.int32, (tm, LANES), 1)
    lane_f = lane.astype(F32)
    oh0 = lane_f == rt[:, 0:1]
    oh1 = lane_f == rt[:, 1:2]
    both = jnp.where(oh0 | oh1, 1.0, 0.0)
    pos = base_ref[...] + carry_s[0:1, :] + _dot(tri_ref[...], both.astype(BF16))
    d0 = jnp.sum(jnp.where(oh0, pos, 0.0), axis=1, keepdims=True)
    d1 = jnp.sum(jnp.where(oh1, pos, 0.0), axis=1, keepdims=True)
    o_ref[...] = jnp.where(lane == 0, d0, jnp.where(lane == 1, d1, 0.0)).astype(jnp.int32)
    carry_s[...] = carry_s[...] + jnp.sum(both, axis=0, keepdims=True)


def _dispatch(rt, n_tok, tm):
    n_slots = n_tok * TOP_K_INNER
    flat_e = rt[:, :TOP_K_INNER].astype(jnp.int32).reshape(-1)
    counts = jnp.sum((flat_e[:, None] == jnp.arange(N_EXPERTS, dtype=jnp.int32)[None, :]).astype(jnp.int32), axis=0)
    padded = ((counts + EXPERT_BLOCK - 1) // EXPERT_BLOCK) * EXPERT_BLOCK
    pad_end = jnp.cumsum(padded)
    pad_start = pad_end - padded
    base = jnp.pad(pad_start.astype(F32), (0, LANES - N_EXPERTS)).reshape(1, LANES)
    r = jnp.arange(tm)
    tri = (r[None, :] < r[:, None]).astype(BF16)
    dest = pl.pallas_call(
        _dest_kernel,
        grid=(n_tok // tm,),
        in_specs=[pl.BlockSpec((tm, LANES), lambda i: (i, 0)),
                  pl.BlockSpec((1, LANES), lambda i: (0, 0)),
                  pl.BlockSpec((tm, tm), lambda i: (0, 0))],
        out_specs=pl.BlockSpec((tm, LANES), lambda i: (i, 0)),
        out_shape=jax.ShapeDtypeStruct((n_tok, LANES), jnp.int32),
        scratch_shapes=[pltpu.VMEM((SUBLANES, LANES), F32)],
        compiler_params=_params(("arbitrary",)),
        name="dispatch_rows",
    )(rt, base, tri)[:, :TOP_K_INNER]
    cap = n_slots + N_EXPERTS * EXPERT_BLOCK
    nb = cap // EXPERT_BLOCK
    blk_row0 = jnp.arange(nb, dtype=jnp.int32) * EXPERT_BLOCK
    blk_e = jnp.minimum(jnp.sum((pad_end[None, :] <= blk_row0[:, None]).astype(jnp.int32), axis=1), N_EXPERTS - 1)
    n_used = (pad_end[-1] // EXPERT_BLOCK).astype(jnp.int32).reshape(1)

    n_pad = cap - n_slots
    seg_len = jnp.concatenate([padded - counts, (cap - pad_end[-1]).reshape(1)])
    seg_end = jnp.cumsum(seg_len)
    seg_start = seg_end - seg_len
    seg_row0 = jnp.concatenate([pad_start + counts, pad_end[-1:]])
    j = jnp.arange(n_pad, dtype=jnp.int32)
    seg_onehot = (jnp.sum((seg_end[None, :] <= j[:, None]).astype(jnp.int32), axis=1)[:, None]
                  == jnp.arange(N_EXPERTS + 1, dtype=jnp.int32)[None, :]).astype(jnp.int32)
    pad_rows = j + jnp.sum(seg_onehot * (seg_row0 - seg_start)[None, :], axis=1)
    rows = jnp.concatenate([dest.reshape(-1), pad_rows.astype(jnp.int32)])
    toks = jnp.concatenate([jnp.arange(n_slots, dtype=jnp.int32) // TOP_K_INNER, j % n_tok])
    _, tok_buf = lax.sort_key_val(rows, toks)
    return dest, tok_buf, blk_e.astype(jnp.int32), n_used


def kernel(x_prompt, x_sample, c_prompt, c_sample, w_ada, b_ada, norm1_g, w_in, q_norm_g, k_norm_g, lambda_q1, lambda_k1, lambda_q2, lambda_k2, attn_subln_g, w_attn_o, conv_w, conv_b, dt_bias, a_log, d_skip, ssd_norm_g, w_ssd_o, w_out, norm2_g, w_group, b_group, w_router, b_router, w_gate_e, w_up_e, w_down_e):
    groups = [(x_prompt, c_prompt), (x_sample, c_sample)]
    seqs = [(x.shape[0], x.shape[1]) for x, _ in groups]
    n_tok = sum(b * s for b, s in seqs)
    min_seq = min(s for _, s in seqs)
    tm = min(1024, min_seq)
    tp = min(256, min_seq)
    tq = min(512, min_seq)
    tk = min(512, min_seq)
    layer = 0

    xa, xb_in = (g[0].reshape(-1, D_MODEL) for g in groups)
    c = jnp.concatenate([g[1] for g in groups], axis=0)
    n_batch = c.shape[0]
    c_pad = jnp.pad(c, ((0, (-n_batch) % SUBLANES), (0, 0)))
    mod = _ada(c_pad, w_ada[layer], b_ada[layer]).reshape(-1, 6, D_MODEL)
    tok_batch = jnp.concatenate([jnp.repeat(jnp.arange(b, dtype=jnp.int32), s) + off
                                 for (b, s), off in zip(seqs, [0, seqs[0][0]])])
    mod_tm = mod[tok_batch[::tm]]
    mod_tp = mod[tok_batch[::tp]]

    w_in_b = w_in[layer].astype(BF16)
    tabs = [_rope_tables(s) for _, s in seqs]
    cos_t = jnp.concatenate([jnp.tile(tb[0], (b, 1)) for tb, (b, _) in zip(tabs, seqs)], axis=0)
    sin_t = jnp.concatenate([jnp.tile(tb[1], (b, 1)) for tb, (b, _) in zip(tabs, seqs)], axis=0)
    half = (jnp.arange(LANES) // ROT) % 2
    bd = (half[:, None] == half[None, :]).astype(BF16)
    gq = _permute_gain(q_norm_g[layer]) * (QK_DIM ** -0.5 * LOG2_E)
    gk = _permute_gain(k_norm_g[layer])
    q, h = _norm_q_proj(xa, xb_in, mod_tm, norm1_g[layer], _permute_head_cols(w_in_b[:, OFF_Q:OFF_K]),
                        gq, cos_t, sin_t, bd, tm)
    k = _qk_proj(h, _permute_head_cols(w_in_b[:, OFF_K:OFF_V]), gk, cos_t, sin_t, bd, tm, "k_proj")
    vt3 = _vt_proj(h, w_in_b[:, OFF_V:OFF_Z], tm, tk)
    z_act = _matmul(h, w_in_b[:, OFF_Z:OFF_XBC], BF16, tm, PROJ_COLS, "z_proj", _silu)
    w_dt =jnp.pad(w_in_b[:, OFF_DT:OFF_GATE], ((0, 0), (0, LANES - DT_COLS)))
    dt_b = jnp.pad(dt_bias[layer].reshape(1, DT_COLS), ((0, 0), (0, LANES - DT_COLS)))
    dt = _matmul(h, w_dt, F32, tm, LANES, "dt_proj", _softplus, dt_b)
    gates = _matmul(h, w_in_b[:, OFF_GATE:], BF16, tm, PROJ_COLS, "gate_proj", _sigmoid)

    k3 = k.reshape(n_tok // tk, tk, ATTN_WIDTH)
    lams = [v[layer].reshape(1, QK_DIM) for v in (lambda_q1, lambda_k1, lambda_q2, lambda_k2)]
    o_groups = []
    off = 0
    for b, s in seqs:
        o_groups.append(_attention(q, k3, vt3, lams, off, b, s, tq, tk))
        off += b * s

    seq_starts = []
    off = 0
    for b, s in seqs:
        seq_starts += [(off + i * s, s) for i in range(b)]
        off += b * s
    tile_start = jnp.zeros((n_tok // tm,), jnp.int32)
    tile_end = jnp.zeros((n_tok // tm,), jnp.int32)
    nc = n_tok // CHUNK
    chunk_reset = jnp.zeros((nc,), jnp.int32)
    bwd_idx = jnp.zeros((nc,), jnp.int32)
    for st, s in seq_starts:
        tile_start = tile_start.at[st // tm].set(1)
        tile_end = tile_end.at[(st + s) // tm - 1].set(1)
        c0, c1 = st // CHUNK, (st + s) // CHUNK
        chunk_reset = chunk_reset.at[c0].set(1)
        bwd_idx = bwd_idx.at[c0:c1].set(jnp.arange(c1 - 1, c0 - 1, -1, dtype=jnp.int32))
    fwd_idx = jnp.arange(nc, dtype=jnp.int32)
    w_x, w_bc = w_in_b[:, OFF_XBC:OFF_XBC + D_INNER], w_in_b[:, OFF_XBC + D_INNER:OFF_DT]
    cw, cbias = conv_w[layer], conv_b[layer]
    xs = _proj_conv(h, w_x, cw[:, :D_INNER], cbias[:D_INNER], tile_start, tile_end, F32, tm, PROJ_COLS,
                    "x_proj_conv")
    bcm = _proj_conv(h, w_bc, cw[:, D_INNER:], cbias[D_INNER:], tile_start, tile_end, BF16, tm, PROJ_COLS,
                     "bc_proj_conv")
    dtt = dt.T
    alg = a_log[layer].reshape(-1)
    dskip = jnp.repeat(d_skip[layer], SSD_HEAD_DIM).reshape(1, D_INNER)
    yb = _ssd(xs, bcm, dt, dtt, alg, bwd_idx, chunk_reset)
    y_ssd = _ssd(xs, bcm, dt, dtt, alg, fwd_idx, chunk_reset, yb, dskip)

    subg = attn_subln_g[layer].reshape(1, V_DIM)
    n_rt = N_EXPERT_GROUPS + N_EXPERTS
    w_rt = jnp.pad(jnp.concatenate([w_group[layer], w_router[layer]], axis=1), ((0, 0), (0, LANES - n_rt)))
    b_rt = jnp.pad(jnp.concatenate([b_group[layer], b_router[layer]]), (0, LANES - n_rt)).reshape(1, LANES)
    tg = min(512, min_seq)
    x1, h2b, rt = _post(o_groups[0], o_groups[1], xa, xb_in, y_ssd, z_act, gates, mod[tok_batch[::tg]], subg,
                        ssd_norm_g[layer].reshape(1, D_INNER), norm2_g[layer].reshape(1, D_MODEL),
                        w_attn_o[layer].astype(BF16), w_ssd_o[layer].astype(BF16),
                        w_out[layer].astype(BF16), w_rt, b_rt, tg)

    dest, tok_buf, blk_e, n_used = _dispatch(rt, n_tok, min(512, min_seq))
    xb = h2b[tok_buf]
    yb_e = _experts(xb, blk_e, n_used, w_gate_e[layer], w_up_e[layer], w_down_e[layer])
    g = yb_e[dest.T.reshape(-1)]

    outs = []
    off = 0
    for (b, s), (xg, _) in zip(seqs, groups):
        y = _final(x1, mod_tp, rt, g, off, b * s, tp)
        outs.append(y.reshape(xg.shape))
        off += b * s
    return tuple(outs)
```

```python
import functools
import math

import jax
import jax.numpy as jnp
from jax import lax
from jax.experimental import pallas as pl
from jax.experimental.pallas import tpu as pltpu

F32 = jnp.float32
BF16 = jnp.bfloat16

D_MODEL = 1024
ATTN_HEADS = 8
QK_DIM = 64
V_DIM = 2 * QK_DIM
ATTN_WIDTH = ATTN_HEADS * V_DIM
ROPE_THETA = 10000.0
D_INNER = 2048
SSD_HEAD_DIM = 64
SSD_HEADS = D_INNER // SSD_HEAD_DIM
SSD_GROUPS = 4
HEADS_PER_GROUP = SSD_HEADS // SSD_GROUPS
D_STATE = 128
CONV_K = 5
CONV_CH = D_INNER + 2 * SSD_GROUPS * D_STATE
CHUNK = 128
N_EXPERT_GROUPS = 4
EXPERTS_PER_GROUP = 8
N_EXPERTS = N_EXPERT_GROUPS * EXPERTS_PER_GROUP
TOP_K_INNER = 2
EXPERT_FF = 512
RMS_EPS = 1e-6
LAMBDA_INIT = 0.8 - 0.6 * math.exp(-0.3 * 0)

LANES = 128
SUBLANES = 8
MXU_COLS = 256
VMEM_LIMIT = 56 * 1024 * 1024
ROT = QK_DIM // 2
PROJ_COLS = 1024

Q_COLS = ATTN_HEADS * 2 * QK_DIM
K_COLS = Q_COLS
V_COLS = ATTN_WIDTH
Z_COLS = D_INNER
XBC_COLS = CONV_CH
DT_COLS = 2 * SSD_HEADS
GATE_COLS = 2 * D_MODEL
OFF_Q = 0
OFF_K = OFF_Q + Q_COLS
OFF_V = OFF_K + K_COLS
OFF_Z = OFF_V + V_COLS
OFF_XBC = OFF_Z + Z_COLS
OFF_DT = OFF_XBC + XBC_COLS
OFF_GATE = OFF_DT + DT_COLS

EXPERT_BLOCK = 512
ONES_ROWS = 16
LOG2_E = math.log2(math.e)


def _params(sem):
    return pltpu.CompilerParams(dimension_semantics=sem, vmem_limit_bytes=VMEM_LIMIT)


def _dot(a, b):
    return jnp.dot(a, b, preferred_element_type=F32)


def _dot_tn(a, b):
    return lax.dot_general(a, b, (((0,), (0,)), ((), ())), preferred_element_type=F32)


def _dot_nt(a, b):
    return lax.dot_general(a, b, (((1,), (1,)), ((), ())), preferred_element_type=F32)


def _split3(a):
    hi = a.astype(BF16)
    r = a - hi.astype(F32)
    mid = r.astype(BF16)
    lo = (r - mid.astype(F32)).astype(BF16)
    return hi, mid, lo


def _dot_left01(m01, a):
    hi, mid, lo = _split3(a)
    return _dot(m01, hi) + _dot(m01, mid) + _dot(m01, lo)


def _dot_right01(a, m01):
    hi, mid, lo = _split3(a)
    return _dot(hi, m01) + _dot(mid, m01) + _dot(lo, m01)


def _spread01(a, m01):
    hi = a.astype(BF16)
    lo = (a - hi.astype(F32)).astype(BF16)
    return _dot(hi, m01) + _dot(lo, m01)


def _dot_f32(a, b):
    a0, a1, a2 = _split3(a)
    b0, b1, b2 = _split3(b)
    return (_dot(a0, b0) + (_dot(a0, b1) + _dot(a1, b0))
            + (_dot(a0, b2) + _dot(a2, b0) + _dot(a1, b1)))


def _dot_f32_3(a, b):
    a0 = a.astype(BF16)
    a1 = (a - a0.astype(F32)).astype(BF16)
    b0 = b.astype(BF16)
    b1 = (b - b0.astype(F32)).astype(BF16)
    return _dot(a0, b0) + (_dot(a0, b1) + _dot(a1, b0))


def _sigmoid(x):
    return 1.0 / (1.0 + jnp.exp(-x))


def _silu(x):
    return x * _sigmoid(x)


def _softplus(x):
    e = jnp.exp(-jnp.abs(x))
    u = 1.0 + e
    log1p_e = jnp.where(u == 1.0, e, jnp.log(u) * (e / (u - 1.0)))
    return jnp.maximum(x, 0.0) + log1p_e


def _ada_kernel(c_ref, w_ref, b_ref, o_ref):
    o_ref[...] = _dot_f32(_silu(c_ref[...]), w_ref[...]) + b_ref[...]


def _ada(c_pad, w_ada, b_ada):
    rows = c_pad.shape[0]
    n = w_ada.shape[1]
    tn = PROJ_COLS
    return pl.pallas_call(
        _ada_kernel,
        grid=(n // tn,),
        in_specs=[pl.BlockSpec((rows, D_MODEL), lambda j: (0, 0)),
                  pl.BlockSpec((D_MODEL, tn), lambda j: (0, j)),
                  pl.BlockSpec((1, tn), lambda j: (0, j))],
        out_specs=pl.BlockSpec((rows, tn), lambda j: (0, j)),
        out_shape=jax.ShapeDtypeStruct((rows, n), F32),
        compiler_params=_params(("arbitrary",)),
        name="ada_mod",
    )(c_pad, w_ada, b_ada.reshape(1, n))


def _group_specs(tm, n, n_first):
    return [pl.BlockSpec((tm, n), lambda i: (jnp.minimum(i, n_first - 1), 0)),
            pl.BlockSpec((tm, n), lambda i: (jnp.maximum(i - n_first, 0), 0))]


def _group_pick(a_ref, b_ref, n_first):
    return jnp.where(pl.program_id(0) < n_first, a_ref[...], b_ref[...])


def _mm_kernel(a_ref, w_ref, *rest, act):
    o_ref = rest[-1]
    acc = _dot(a_ref[...], w_ref[...])
    if len(rest) == 2:
        acc = acc + rest[0][...]
    if act is not None:
        acc = act(acc)
    o_ref[...] = acc.astype(o_ref.dtype)


def _matmul(a, w, out_dtype, tm, tn, name, act=None, bias=None):
    m, k = a.shape
    n = w.shape[1]
    in_specs = [pl.BlockSpec((tm, k), lambda j, i: (i, 0)),
                pl.BlockSpec((k, tn), lambda j, i: (0, j))]
    args = [a, w]
    if bias is not None:
        in_specs.append(pl.BlockSpec((1, tn), lambda j, i: (0, j)))
        args.append(bias)
    return pl.pallas_call(
        functools.partial(_mm_kernel, act=act),
        grid=(n // tn, m // tm),
        in_specs=in_specs,
        out_specs=pl.BlockSpec((tm, tn), lambda j, i: (i, j)),
        out_shape=jax.ShapeDtypeStruct((m, n), out_dtype),
        compiler_params=_params(("parallel", "parallel")),
        name=name,
    )(*args)


def _qk_epilogue(acc, g_ref, cos_ref, sin_ref, bd_ref, o_ref):
    cos = cos_ref[...]
    sin = sin_ref[...]
    g = g_ref[...]
    bd = bd_ref[...]
    for h in range(ATTN_HEADS):
        x = acc[:, h * LANES:(h + 1) * LANES]
        sq = x * x
        hi = sq.astype(BF16)
        lo = (sq - hi.astype(F32)).astype(BF16)
        ss = _dot(hi, bd) + _dot(lo, bd)
        r = lax.rsqrt(ss * (1.0 / QK_DIM) + RMS_EPS)
        xn = (x * r) * g
        partner = pltpu.roll(xn, LANES // 2, 1)
        o_ref[:, h * LANES:(h + 1) * LANES] = (xn * cos + partner * sin).astype(o_ref.dtype)


def _qk_kernel(a_ref, w_ref, g_ref, cos_ref, sin_ref, bd_ref, o_ref):
    _qk_epilogue(_dot(a_ref[...], w_ref[...]), g_ref, cos_ref, sin_ref, bd_ref, o_ref)


def _norm_q_kernel(xa_ref, xb_ref, mod_ref, g1_ref, w_ref, g_ref, cos_ref, sin_ref, bd_ref, o_ref, h_ref,
                   *, n_first):
    x = _group_pick(xa_ref, xb_ref, n_first)
    r = lax.rsqrt(jnp.mean(x * x, axis=-1, keepdims=True) + RMS_EPS)
    m = mod_ref[0]
    h = (((x * r) * g1_ref[...]) * (1.0 + m[1:2]) + m[0:1]).astype(BF16)
    h_ref[...] = h
    _qk_epilogue(_dot(h, w_ref[...]), g_ref, cos_ref, sin_ref, bd_ref, o_ref)


def _qk_specs(tm, n):
    return [pl.BlockSpec((D_MODEL, n), lambda i: (0, 0)),
            pl.BlockSpec((1, LANES), lambda i: (0, 0)),
            pl.BlockSpec((tm, LANES), lambda i: (i, 0)),
            pl.BlockSpec((tm, LANES), lambda i: (i, 0)),
            pl.BlockSpec((LANES, LANES), lambda i: (0, 0))]


def _qk_proj(h, w, g128, cos_t, sin_t, bd, tm, name):
    t = h.shape[0]
    n = w.shape[1]
    return pl.pallas_call(
        _qk_kernel,
        grid=(t // tm,),
        in_specs=[pl.BlockSpec((tm, D_MODEL), lambda i: (i, 0))] + _qk_specs(tm, n),
        out_specs=pl.BlockSpec((tm, n), lambda i: (i, 0)),
        out_shape=jax.ShapeDtypeStruct((t, n), BF16),
        compiler_params=_params(("parallel",)),
        name=name,
    )(h, w, g128, cos_t, sin_t, bd)


def _norm_q_proj(xa, xb, mod_tiles, g1, w, g128, cos_t, sin_t, bd, tm):
    t = xa.shape[0] + xb.shape[0]
    n = w.shape[1]
    n_first = xa.shape[0] // tm
    tok = pl.BlockSpec((tm, n), lambda i: (i, 0))
    return pl.pallas_call(
        functools.partial(_norm_q_kernel, n_first=n_first),
        grid=(t // tm,),
        in_specs=_group_specs(tm, D_MODEL, n_first) + [
            pl.BlockSpec((1, 6, D_MODEL), lambda i: (i, 0, 0)),
            pl.BlockSpec((1, D_MODEL), lambda i: (0, 0))] + _qk_specs(tm, n),
        out_specs=[tok, pl.BlockSpec((tm, D_MODEL), lambda i: (i, 0))],
        out_shape=[jax.ShapeDtypeStruct((t, n), BF16), jax.ShapeDtypeStruct((t, D_MODEL), BF16)],
        compiler_params=_params(("parallel",)),
        name="norm1_q_proj",
    )(xa, xb, mod_tiles, g1.reshape(1, D_MODEL), w, g128, cos_t, sin_t, bd)


def _vt_kernel(a_ref, w_ref, o_ref, *, tk):
    acc = _dot(a_ref[...], w_ref[...])
    for c in range(acc.shape[0] // tk):
        o_ref[c] = acc[c * tk:(c + 1) * tk, :].T.astype(o_ref.dtype)


def _vt_proj(h, w, tm, tk):
    t = h.shape[0]
    n = w.shape[1]
    return pl.pallas_call(
        functools.partial(_vt_kernel, tk=tk),
        grid=(t // tm,),
        in_specs=[pl.BlockSpec((tm, D_MODEL), lambda i: (i, 0)),
                  pl.BlockSpec((D_MODEL, n), lambda i: (0, 0))],
        out_specs=pl.BlockSpec((tm // tk, n, tk), lambda i: (i, 0, 0)),
        out_shape=jax.ShapeDtypeStruct((t // tk, n, tk), BF16),
        compiler_params=_params(("parallel",)),
        name="v_proj_t",
    )(h, w)


def _attn_kernel(trips_ref, lq1_ref, lk1_ref, lq2_ref, lk2_ref, q_ref, k_ref, vt_ref, o_ref,
                 q2t_s, s_s, m_s, acc_s, *, tq, nkv, unroll):
    tk = k_ref.shape[1]
    qt = q_ref[...].astype(F32).T
    row = lax.broadcasted_iota(jnp.int32, qt.shape, 0)
    first_half = (row & ROT) == 0
    q2t_s[:, :tq] = jnp.where(first_half, qt, 0.0).astype(BF16)
    q2t_s[:, tq:] = jnp.where(first_half, 0.0, qt).astype(BF16)
    m_s[...] = jnp.full(m_s.shape, -jnp.inf, F32)
    acc_s[...] = jnp.zeros(acc_s.shape, F32)
    ones = jnp.ones((ONES_ROWS, tk), BF16)
    s_s[0] = _dot(k_ref[0], q2t_s[...])

    def step(j, par, compute_next):
        if compute_next:
            s_s[1 - par] = _dot(k_ref[j + 1], q2t_s[...])
        st = s_s[par]
        m_old = m_s[...]
        m_new = jnp.maximum(m_old, jnp.max(st, axis=0, keepdims=True))
        alpha = jnp.exp2(m_old - m_new)
        p = jnp.exp2(st - m_new).astype(BF16)
        lhs = jnp.concatenate([vt_ref[j], ones], axis=0)
        acc_s[...] = alpha * acc_s[...] + _dot(lhs, p)
        m_s[...] = m_new

    n_trips = (nkv - 1) // unroll

    def trip(t, carry):
        for u in range(unroll):
            step(unroll * t + u, u % 2, True)
        return carry

    lax.fori_loop(0, trips_ref[0], trip, 0)
    for j in range(unroll * n_trips, nkv):
        step(j, j % 2, j + 1 < nkv)

    lam = (jnp.exp(jnp.sum(lq1_ref[...] * lk1_ref[...], axis=1, keepdims=True))
           - jnp.exp(jnp.sum(lq2_ref[...] * lk2_ref[...], axis=1, keepdims=True)) + LAMBDA_INIT)
    ot = acc_s[:V_DIM, :] / acc_s[V_DIM:V_DIM + 1, :]
    o_ref[...] = (ot[:, :tq] - lam * ot[:, tq:]).T


def _attention(q, k3, vt3, lams, tok_off, batch, seq, tq, tk):
    nkv = seq // tk
    qb0 = tok_off // tq
    sb0 = tok_off // seq
    nq = seq // tq
    unroll = 4 if nkv >= 8 else 2
    n_trips = jnp.full((1,), (nkv - 1) // unroll, jnp.int32)
    lam_spec = pl.BlockSpec((1, QK_DIM), lambda b, h, i, nt: (0, 0))
    grid_spec = pltpu.PrefetchScalarGridSpec(
        num_scalar_prefetch=1,
        grid=(batch, ATTN_HEADS, nq),
        in_specs=[lam_spec, lam_spec, lam_spec, lam_spec,
                  pl.BlockSpec((tq, LANES), lambda b, h, i, nt: (qb0 + b * nq + i, h)),
                  pl.BlockSpec((nkv, tk, LANES), lambda b, h, i, nt: (sb0 + b, 0, h)),
                  pl.BlockSpec((nkv, LANES, tk), lambda b, h, i, nt: (sb0 + b, h, 0))],
        out_specs=pl.BlockSpec((tq, LANES), lambda b, h, i, nt: (b * nq + i, h)),
        scratch_shapes=[pltpu.VMEM((LANES, 2 * tq), BF16),
                        pltpu.VMEM((2, tk, 2 * tq), F32),
                        pltpu.VMEM((1, 2 * tq), F32),
                        pltpu.VMEM((V_DIM + ONES_ROWS, 2 * tq), F32)],
    )
    return pl.pallas_call(
        functools.partial(_attn_kernel, tq=tq, nkv=nkv, unroll=unroll),
        grid_spec=grid_spec,
        out_shape=jax.ShapeDtypeStruct((batch * seq, ATTN_WIDTH), F32),
        compiler_params=_params(("parallel", "parallel", "arbitrary")),
        name="diff_attention",
    )(n_trips, *lams, q, k3, vt3)


HALO = 16


def _proj_conv_kernel(start_ref, end_ref, prev_ref, cur_ref, next_ref, w_ref, cw_ref, cb_ref, o_ref, *, tm):
    i = pl.program_id(1)
    prev = jnp.where(start_ref[i] == 1, jnp.zeros_like(prev_ref[...]), prev_ref[...])
    nxt = jnp.where(end_ref[i] == 1, jnp.zeros_like(next_ref[...]), next_ref[...])
    a = jnp.concatenate([prev, cur_ref[...], nxt], axis=0)
    pad = (CONV_K - 1) // 2
    for c in range(0, o_ref.shape[1], MXU_COLS):
        ext = _dot(a, w_ref[:, c:c + MXU_COLS])
        acc = jnp.broadcast_to(cb_ref[:, c:c + MXU_COLS], (tm, MXU_COLS))
        for d in range(CONV_K):
            shift = (pad - d) % (tm + 2 * HALO)
            src = ext if shift == 0 else pltpu.roll(ext, shift, 0)
            acc = acc + cw_ref[d:d + 1, c:c + MXU_COLS] * src[HALO:HALO + tm, :]
        o_ref[:, c:c + MXU_COLS] = _silu(acc).astype(o_ref.dtype)


def _proj_conv(h, w, conv_w, conv_b, start_flags, end_flags, out_dtype, tm, tn, name):
    t = h.shape[0]
    n = w.shape[1]
    rb = tm // HALO
    last = t // HALO - 1
    cw_pad = jnp.zeros((SUBLANES, n), F32).at[:CONV_K].set(conv_w)
    grid_spec = pltpu.PrefetchScalarGridSpec(
        num_scalar_prefetch=2,
        grid=(n // tn, t // tm),
        in_specs=[pl.BlockSpec((HALO, D_MODEL), lambda j, i, s, e: (jnp.maximum(i * rb - 1, 0), 0)),
                  pl.BlockSpec((tm, D_MODEL), lambda j, i, s, e: (i, 0)),
                  pl.BlockSpec((HALO, D_MODEL), lambda j, i, s, e: (jnp.minimum((i + 1) * rb, last), 0)),
                  pl.BlockSpec((D_MODEL, tn), lambda j, i, s, e: (0, j)),
                  pl.BlockSpec((SUBLANES, tn), lambda j, i, s, e: (0, j)),
                  pl.BlockSpec((1, tn), lambda j, i, s, e: (0, j))],
        out_specs=pl.BlockSpec((tm, tn), lambda j, i, s, e: (i, j)),
    )
    return pl.pallas_call(
        functools.partial(_proj_conv_kernel, tm=tm),
        grid_spec=grid_spec,
        out_shape=jax.ShapeDtypeStruct((t, n), out_dtype),
        compiler_params=_params(("parallel", "parallel")),
        name=name,
    )(start_flags, end_flags, h, h, h, w, cw_pad, conv_b.reshape(1, n))


def _ssd_kernel(idx_ref, reset_ref, xs_ref, bc_ref, dt_ref, dtt_ref,
                alog_r_ref, alog_c_ref, tril_ref, triu_ref, e01_ref, *rest, rev):
    i = pl.program_id(0)
    nh = SSD_HEADS
    y_ref, state_s = rest[-2:]

    @pl.when(reset_ref[i] == 1)
    def _():
        state_s[...] = jnp.zeros(state_s.shape, F32)

    xs = xs_ref[...]
    bc = bc_ref[...]
    dtn = dt_ref[...]
    a = dtn * (-LOG2_E * jnp.exp(alog_r_ref[...]))
    pinc = _dot_left01(tril_ref[...], a)
    pex = pinc - a
    tot = pinc[CHUNK - 1:CHUNK, :]

    gw = HEADS_PER_GROUP * SSD_HEAD_DIM
    gn = SSD_GROUPS * D_STATE
    cd = jnp.broadcast_to(jnp.exp2(tot), (SUBLANES, LANES))

    def update_state(sc_st, sc_cd):
        w = (xs * sc_st).astype(BF16)
        for g in range(SSD_GROUPS):
            bg = bc[:, g * D_STATE:(g + 1) * D_STATE]
            new = _dot_tn(bg, w[:, g * gw:(g + 1) * gw])
            state_s[:, g * gw:(g + 1) * gw] = (state_s[:, g * gw:(g + 1) * gw] * sc_cd[:, g * gw:(g + 1) * gw]
                                               + new)

    sc_cd = _spread01(cd, e01_ref[...])[0:1]
    if rev:
        stacked = jnp.concatenate([jnp.exp2(pex) * dtn, jnp.exp2(tot - pex)], axis=0)
        ex = _dot(stacked.astype(BF16), e01_ref[...])
        sc_off = ex[CHUNK:]
        for g in range(SSD_GROUPS):
            cg = bc[:, gn + g * D_STATE:gn + (g + 1) * D_STATE]
            st = state_s[:, g * gw:(g + 1) * gw]
            y_ref[:, g * gw:(g + 1) * gw] = _dot(cg, st.astype(BF16)) * sc_off[:, g * gw:(g + 1) * gw]
        update_state(ex[:CHUNK], sc_cd)
        return

    yb_ref, dskip_ref = rest[:2]
    dtnt = dtt_ref[...]
    at = dtnt * (-LOG2_E * jnp.exp(alog_c_ref[...]))
    pinct = _dot_right01(at, triu_ref[...])
    pext = pinct - at
    li = lax.broadcasted_iota(jnp.int32, (CHUNK, CHUNK), 0)
    si = lax.broadcasted_iota(jnp.int32, (CHUNK, CHUNK), 1)
    lower = si <= li
    strict_lower = si < li
    strict_upper = si > li
    lane = lax.broadcasted_iota(jnp.int32, (CHUNK, LANES), 1)
    first_head = lane < SSD_HEAD_DIM
    xb = xs.astype(BF16)
    for g in range(SSD_GROUPS):
        bg = bc[:, g * D_STATE:(g + 1) * D_STATE]
        cg = bc[:, gn + g * D_STATE:gn + (g + 1) * D_STATE]
        cg_f = cg.astype(F32)
        cb = _dot_nt(cg, bg)
        for hp in range(HEADS_PER_GROUP // 2):
            col = g * gw + hp * LANES
            rhs = jnp.concatenate([xb[:, col:col + LANES], state_s[:, col:col + LANES].astype(BF16)], axis=0)
            pair = []
            for u in range(2):
                h = g * HEADS_PER_GROUP + 2 * hp + u
                colf = jnp.broadcast_to(pinc[:, h:h + 1], (CHUNK, CHUNK))
                arg = jnp.where(lower, colf - pinct[h:h + 1, :],
                                pext[nh + h:nh + h + 1, :] - pex[:, nh + h:nh + h + 1])
                dtf = dtnt[h:h + 1, :]
                dtb = dtnt[nh + h:nh + h + 1, :]
                coef = jnp.where(strict_lower, dtf, jnp.where(strict_upper, dtb, dtf + dtb))
                mh = (cb * jnp.exp2(arg) * coef).astype(BF16)
                dh = (cg_f * jnp.exp2(colf)).astype(BF16)
                pair.append(_dot(jnp.concatenate([mh, dh], axis=1), rhs))
            y_ref[:, col:col + LANES] = ((jnp.where(first_head, pair[0], pair[1]) + yb_ref[:, col:col + LANES])
                                         + dskip_ref[:, col:col + LANES] * xs[:, col:col + LANES])

    update_state(_dot((jnp.exp2(tot - pinc) * dtn).astype(BF16), e01_ref[...]), sc_cd)


def _ssd(xs, bc, dt, dtt, a_log, idx, reset, yb=None, dskip=None):
    rev = yb is None
    t = xs.shape[0]
    nc = t // CHUNK
    alog_r = jnp.pad(a_log.reshape(1, DT_COLS), ((0, 0), (0, LANES - DT_COLS)))
    alog_c = alog_r.reshape(LANES, 1)
    r = jnp.arange(CHUNK)
    tril = (r[None, :] <= r[:, None]).astype(BF16)
    triu = (r[:, None] <= r[None, :]).astype(BF16)
    lo = SSD_HEADS if rev else 0
    e01 = (jnp.arange(D_INNER)[None, :] // SSD_HEAD_DIM == jnp.arange(LANES)[:, None] - lo).astype(BF16)
    const = lambda shape: pl.BlockSpec(shape, lambda i, ix, rs: (0, 0))
    in_specs = [pl.BlockSpec((CHUNK, D_INNER), lambda i, ix, rs: (ix[i], 0)),
                pl.BlockSpec((CHUNK, CONV_CH - D_INNER), lambda i, ix, rs: (ix[i], 0)),
                pl.BlockSpec((CHUNK, LANES), lambda i, ix, rs: (ix[i], 0)),
                pl.BlockSpec((LANES, CHUNK), lambda i, ix, rs: (0, ix[i])),
                const((1, LANES)), const((LANES, 1)),
                const((CHUNK, CHUNK)), const((CHUNK, CHUNK)), const((LANES, D_INNER))]
    args = [idx, reset, xs, bc, dt, dtt, alog_r, alog_c, tril, triu, e01]
    if not rev:
        in_specs += [pl.BlockSpec((CHUNK, D_INNER), lambda i, ix, rs: (ix[i], 0)), const((1, D_INNER))]
        args += [yb, dskip]
    grid_spec = pltpu.PrefetchScalarGridSpec(
        num_scalar_prefetch=2,
        grid=(nc,),
        in_specs=in_specs,
        out_specs=pl.BlockSpec((CHUNK, D_INNER), lambda i, ix, rs: (ix[i], 0)),
        scratch_shapes=[pltpu.VMEM((D_STATE, D_INNER), F32)],
    )
    return pl.pallas_call(
        functools.partial(_ssd_kernel, rev=rev),
        grid_spec=grid_spec,
        out_shape=jax.ShapeDtypeStruct((t, D_INNER), F32),
        compiler_params=_params(("arbitrary",)),
        name="ssd_bwd" if rev else "ssd_fwd",
    )(*args)


def _post_kernel(oa_ref, ob_ref, xa_ref, xb_ref, y_ref, z_ref, gt_ref, mod_ref, subg_ref,
                 ssdg_ref, n2g_ref, wa_ref, ws_ref, wo_ref, wrt_ref, brt_ref, x1_ref, h2b_ref, rt_ref, an_s, yn_s,
                 *, n_first):
    m = mod_ref[0]
    o = _group_pick(oa_ref, ob_ref, n_first)
    for h in range(ATTN_HEADS):
        oh = o[:, h * V_DIM:(h + 1) * V_DIM]
        r = lax.rsqrt(jnp.mean(oh * oh, axis=-1, keepdims=True) + RMS_EPS)
        an_s[:, h * V_DIM:(h + 1) * V_DIM] = (((oh * r) * subg_ref[...]) * (1.0 - LAMBDA_INIT)).astype(BF16)
    attn_d = _dot(an_s[...], wa_ref[...])

    y = y_ref[...] * z_ref[...].astype(F32)
    gw = D_INNER // SSD_GROUPS
    for g in range(SSD_GROUPS):
        yg = y[:, g * gw:(g + 1) * gw]
        r = lax.rsqrt(jnp.mean(yg * yg, axis=-1, keepdims=True) + RMS_EPS)
        yn_s[:, g * gw:(g + 1) * gw] = ((yg * r) * ssdg_ref[:, g * gw:(g + 1) * gw]).astype(BF16)
    ssd_d = _dot(yn_s[...], ws_ref[...])

    gt = gt_ref[...].astype(F32)
    mix = gt[:, :D_MODEL] * attn_d + gt[:, D_MODEL:] * ssd_d
    mixed = _dot(mix.astype(BF16), wo_ref[...])
    x1 = _group_pick(xa_ref, xb_ref, n_first) + m[2:3] * mixed
    x1_ref[...] = x1
    r = lax.rsqrt(jnp.mean(x1 * x1, axis=-1, keepdims=True) + RMS_EPS)
    h2 = ((x1 * r) * n2g_ref[...]) * (1.0 + m[4:5]) + m[3:4]
    h2b_ref[...] = h2.astype(BF16)
    rt_ref[...] = _route(_dot_f32_3(h2, wrt_ref[...]) + brt_ref[...])


def _post(oa, ob, xa, xb, y, z, gates, mod_tiles, subg, ssdg, n2g, wa, ws, wo, w_rt, b_rt, tm):
    t = y.shape[0]
    n_first = oa.shape[0] // tm
    tok = lambda n: pl.BlockSpec((tm, n), lambda i: (i, 0))
    const = lambda a: pl.BlockSpec(a.shape, lambda i: (0, 0), pipeline_mode=pl.Buffered(1))
    return pl.pallas_call(
        functools.partial(_post_kernel, n_first=n_first),
        grid=(t // tm,),
        in_specs=_group_specs(tm, ATTN_WIDTH, n_first) + _group_specs(tm, D_MODEL, n_first) + [
            tok(D_INNER), tok(D_INNER), tok(GATE_COLS),
            pl.BlockSpec((1, 6, D_MODEL), lambda i: (i, 0, 0)),
            const(subg), const(ssdg), const(n2g), const(wa), const(ws), const(wo), const(w_rt), const(b_rt)],
        out_specs=[tok(D_MODEL), tok(D_MODEL), tok(LANES)],
        out_shape=[jax.ShapeDtypeStruct((t, D_MODEL), F32),
                   jax.ShapeDtypeStruct((t, D_MODEL), BF16),
                   jax.ShapeDtypeStruct((t, LANES), F32)],
        scratch_shapes=[pltpu.VMEM((tm, ATTN_WIDTH), BF16), pltpu.VMEM((tm, D_INNER), BF16)],
        compiler_params=_params(("parallel",)),
        name="merge_out_proj",
    )(oa, ob, xa, xb, y, z, gates, mod_tiles, subg, ssdg, n2g, wa, ws, wo, w_rt, b_rt)


def _route(logits):
    tm = logits.shape[0]
    lane = lax.broadcasted_iota(jnp.int32, (tm, LANES), 1)
    lane_f = lane.astype(F32)
    big = float(LANES)
    neg = -jnp.inf
    gl = jnp.where(lane < N_EXPERT_GROUPS, logits, neg)
    gmax = jnp.max(gl, axis=1, keepdims=True)
    g_sel = jnp.min(jnp.where(gl == gmax, lane_f, big), axis=1, keepdims=True)
    g_w = 1.0 / jnp.sum(jnp.exp(gl - gmax), axis=1, keepdims=True)
    e_lane = lane - N_EXPERT_GROUPS
    e_group = (e_lane >> 3).astype(F32)
    in_group = (e_lane >= 0) & (e_lane < N_EXPERTS) & (e_group == g_sel)
    el = jnp.where(in_group, logits, neg)
    v1 = jnp.max(el, axis=1, keepdims=True)
    i1 = jnp.min(jnp.where(el == v1, lane_f, big), axis=1, keepdims=True)
    el2 = jnp.where(lane_f == i1, neg, el)
    v2 = jnp.max(el2, axis=1, keepdims=True)
    i2 = jnp.min(jnp.where(el2 == v2, lane_f, big), axis=1, keepdims=True)
    e2 = jnp.exp(v2 - v1)
    w1 = g_w / (1.0 + e2)
    w2 = g_w * e2 / (1.0 + e2)
    return jnp.where(lane == 0, i1 - N_EXPERT_GROUPS,
                     jnp.where(lane == 1, i2 - N_EXPERT_GROUPS,
                               jnp.where(lane == 2, w1, jnp.where(lane == 3, w2, 0.0))))


def _expert_kernel(be_ref, nu_ref, x_ref, wg_ref, wu_ref, wd_ref, o_ref, wg_s, wu_s, wd_s):
    i = pl.program_id(0)
    used = i < nu_ref[0]
    new_expert = (i == 0) | (be_ref[i] != be_ref[jnp.maximum(i - 1, 0)])

    @pl.when(used & new_expert)
    def _():
        wg_s[...] = wg_ref[0].astype(BF16)
        wu_s[...] = wu_ref[0].astype(BF16)
        wd_s[...] = wd_ref[0].astype(BF16)

    @pl.when(used)
    def _():
        x = x_ref[...]
        a = _silu(_dot(x, wg_s[...])) * _dot(x, wu_s[...])
        o_ref[...] = _dot(a.astype(BF16), wd_s[...]).astype(o_ref.dtype)

    @pl.when(i >= nu_ref[0])
    def _():
        o_ref[...] = jnp.zeros(o_ref.shape, o_ref.dtype)


def _experts(xb, blk_e, n_used, wg, wu, wd):
    cap = xb.shape[0]
    nb = cap // EXPERT_BLOCK
    grid_spec = pltpu.PrefetchScalarGridSpec(
        num_scalar_prefetch=2,
        grid=(nb,),
        in_specs=[pl.BlockSpec((EXPERT_BLOCK, D_MODEL), lambda i, be, nu: (jnp.minimum(i, nu[0] - 1), 0)),
                  pl.BlockSpec((1, D_MODEL, EXPERT_FF), lambda i, be, nu: (be[i], 0, 0)),
                  pl.BlockSpec((1, D_MODEL, EXPERT_FF), lambda i, be, nu: (be[i], 0, 0)),
                  pl.BlockSpec((1, EXPERT_FF, D_MODEL), lambda i, be, nu: (be[i], 0, 0))],
        out_specs=pl.BlockSpec((EXPERT_BLOCK, D_MODEL), lambda i, be, nu: (i, 0)),
        scratch_shapes=[pltpu.VMEM((D_MODEL, EXPERT_FF), BF16), pltpu.VMEM((D_MODEL, EXPERT_FF), BF16),
                        pltpu.VMEM((EXPERT_FF, D_MODEL), BF16)],
    )
    return pl.pallas_call(
        _expert_kernel,
        grid_spec=grid_spec,
        out_shape=jax.ShapeDtypeStruct((cap, D_MODEL), BF16),
        compiler_params=_params(("arbitrary",)),
        name="expert_mlp",
    )(blk_e, n_used, xb, wg, wu, wd)


def _final_kernel(x1_ref, mod_ref, rt_ref, g0_ref, g1_ref, o_ref):
    m = mod_ref[0]
    rt = rt_ref[...]
    moe = g0_ref[...].astype(F32) * rt[:, 2:3] + g1_ref[...].astype(F32) * rt[:, 3:4]
    o_ref[...] = x1_ref[...] + m[5:6] * moe


def _final(x1, mod_tiles, rt, g, tok_off, n_tok, tm):
    b0 = tok_off // tm
    b1 = (g.shape[0] // TOP_K_INNER) // tm
    tok = lambda n: pl.BlockSpec((tm, n), lambda i: (b0 + i, 0))
    return pl.pallas_call(
        _final_kernel,
        grid=(n_tok // tm,),
        in_specs=[tok(D_MODEL), pl.BlockSpec((1, 6, D_MODEL), lambda i: (b0 + i, 0, 0)), tok(LANES),
                  tok(D_MODEL), pl.BlockSpec((tm, D_MODEL), lambda i: (b1 + b0 + i, 0))],
        out_specs=pl.BlockSpec((tm, D_MODEL), lambda i: (i, 0)),
        out_shape=jax.ShapeDtypeStruct((n_tok, D_MODEL), F32),
        compiler_params=_params(("parallel",)),
        name="moe_combine",
    )(x1, mod_tiles, rt, g, g)


def _permute_head_cols(w):
    k = w.shape[0]
    return w.reshape(k, ATTN_HEADS, 2, 2, ROT).transpose(0, 1, 3, 2, 4).reshape(k, ATTN_HEADS * LANES)


def _permute_gain(g):
    return jnp.broadcast_to(g.reshape(2, 1, ROT), (2, 2, ROT)).reshape(1, LANES)


def _rope_tables(seq):
    pos = jnp.arange(seq, dtype=F32)
    inv = 1.0 / (ROPE_THETA ** (jnp.arange(0, QK_DIM, 2, dtype=F32) / QK_DIM))
    ang = pos[:, None] * inv[None, :]
    cos, sin = jnp.cos(ang), jnp.sin(ang)
    cos_t = jnp.tile(cos, (1, LANES // ROT))
    sin_t = jnp.concatenate([-sin, -sin, sin, sin], axis=1)
    return cos_t, sin_t


def _dest_kernel(rt_ref, base_ref, tri_ref, o_ref, carry_s):
    @pl.when(pl.program_id(0) == 0)
    def _():
        carry_s[...] = jnp.zeros(carry_s.shape, F32)

    rt = rt_ref[...]
    tm = rt.shape[0]
    lane = lax.broadcasted_iota(jnp.int32, (tm, LANES), 1)
    lane_f = lane.astype(F32)
    oh0 = lane_f == rt[:, 0:1]
    oh1 = lane_f == rt[:, 1:2]
    both = jnp.where(oh0 | oh1, 1.0, 0.0)
    pos = base_ref[...] + carry_s[0:1, :] + _dot(tri_ref[...], both.astype(BF16))
    d0 = jnp.sum(jnp.where(oh0, pos, 0.0), axis=1, keepdims=True)
    d1 = jnp.sum(jnp.where(oh1, pos, 0.0), axis=1, keepdims=True)
    o_ref[...] = jnp.where(lane == 0, d0, jnp.where(lane == 1, d1, 0.0)).astype(jnp.int32)
    carry_s[...] = carry_s[...] + jnp.sum(both, axis=0, keepdims=True)


def _dispatch(rt, n_tok, tm):
    n_slots = n_tok * TOP_K_INNER
    flat_e = rt[:, :TOP_K_INNER].astype(jnp.int32).reshape(-1)
    counts = jnp.sum((flat_e[:, None] == jnp.arange(N_EXPERTS, dtype=jnp.int32)[None, :]).astype(jnp.int32), axis=0)
    padded = ((counts + EXPERT_BLOCK - 1) // EXPERT_BLOCK) * EXPERT_BLOCK
    pad_end = jnp.cumsum(padded)
    pad_start = pad_end - padded
    base = jnp.pad(pad_start.astype(F32), (0, LANES - N_EXPERTS)).reshape(1, LANES)
    r = jnp.arange(tm)
    tri = (r[None, :] < r[:, None]).astype(BF16)
    dest = pl.pallas_call(
        _dest_kernel,
        grid=(n_tok // tm,),
        in_specs=[pl.BlockSpec((tm, LANES), lambda i: (i, 0)),
                  pl.BlockSpec((1, LANES), lambda i: (0, 0)),
                  pl.BlockSpec((tm, tm), lambda i: (0, 0))],
        out_specs=pl.BlockSpec((tm, LANES), lambda i: (i, 0)),
        out_shape=jax.ShapeDtypeStruct((n_tok, LANES), jnp.int32),
        scratch_shapes=[pltpu.VMEM((SUBLANES, LANES), F32)],
        compiler_params=_params(("arbitrary",)),
        name="dispatch_rows",
    )(rt, base, tri)[:, :TOP_K_INNER]
    cap = n_slots + N_EXPERTS * EXPERT_BLOCK
    nb = cap // EXPERT_BLOCK
    blk_row0 = jnp.arange(nb, dtype=jnp.int32) * EXPERT_BLOCK
    blk_e = jnp.minimum(jnp.sum((pad_end[None, :] <= blk_row0[:, None]).astype(jnp.int32), axis=1), N_EXPERTS - 1)
    n_used = (pad_end[-1] // EXPERT_BLOCK).astype(jnp.int32).reshape(1)

    n_pad = cap - n_slots
    seg_len = jnp.concatenate([padded - counts, (cap - pad_end[-1]).reshape(1)])
    seg_end = jnp.cumsum(seg_len)
    seg_start = seg_end - seg_len
    seg_row0 = jnp.concatenate([pad_start + counts, pad_end[-1:]])
    j = jnp.arange(n_pad, dtype=jnp.int32)
    seg_onehot = (jnp.sum((seg_end[None, :] <= j[:, None]).astype(jnp.int32), axis=1)[:, None]
                  == jnp.arange(N_EXPERTS + 1, dtype=jnp.int32)[None, :]).astype(jnp.int32)
    pad_rows = j + jnp.sum(seg_onehot * (seg_row0 - seg_start)[None, :], axis=1)
    rows = jnp.concatenate([dest.reshape(-1), pad_rows.astype(jnp.int32)])
    toks = jnp.concatenate([jnp.arange(n_slots, dtype=jnp.int32) // TOP_K_INNER, j % n_tok])
    _, tok_buf = lax.sort_key_val(rows, toks)
    return dest, tok_buf, blk_e.astype(jnp.int32), n_used


def kernel(x_prompt, x_sample, c_prompt, c_sample, w_ada, b_ada, norm1_g, w_in, q_norm_g, k_norm_g, lambda_q1, lambda_k1, lambda_q2, lambda_k2, attn_subln_g, w_attn_o, conv_w, conv_b, dt_bias, a_log, d_skip, ssd_norm_g, w_ssd_o, w_out, norm2_g, w_group, b_group, w_router, b_router, w_gate_e, w_up_e, w_down_e):
    groups = [(x_prompt, c_prompt), (x_sample, c_sample)]
    seqs = [(x.shape[0], x.shape[1]) for x, _ in groups]
    n_tok = sum(b * s for b, s in seqs)
    min_seq = min(s for _, s in seqs)
    tm = min(1024, min_seq)
    tp = min(1024, min_seq)
    tq = min(512, min_seq)
    tk = min(512, min_seq)
    layer = 0

    xa, xb_in = (g[0].reshape(-1, D_MODEL) for g in groups)
    c = jnp.concatenate([g[1] for g in groups], axis=0)
    n_batch = c.shape[0]
    c_pad = jnp.pad(c, ((0, (-n_batch) % SUBLANES), (0, 0)))
    mod = _ada(c_pad, w_ada[layer], b_ada[layer]).reshape(-1, 6, D_MODEL)
    tok_batch = jnp.concatenate([jnp.repeat(jnp.arange(b, dtype=jnp.int32), s) + off
                                 for (b, s), off in zip(seqs, [0, seqs[0][0]])])
    mod_tm = mod[tok_batch[::tm]]
    mod_tp = mod[tok_batch[::tp]]

    w_in_b = w_in[layer].astype(BF16)
    tabs = [_rope_tables(s) for _, s in seqs]
    cos_t = jnp.concatenate([jnp.tile(tb[0], (b, 1)) for tb, (b, _) in zip(tabs, seqs)], axis=0)
    sin_t = jnp.concatenate([jnp.tile(tb[1], (b, 1)) for tb, (b, _) in zip(tabs, seqs)], axis=0)
    half = (jnp.arange(LANES) // ROT) % 2
    bd = (half[:, None] == half[None, :]).astype(BF16)
    gq = _permute_gain(q_norm_g[layer]) * (QK_DIM ** -0.5 * LOG2_E)
    gk = _permute_gain(k_norm_g[layer])
    q, h = _norm_q_proj(xa, xb_in, mod_tm, norm1_g[layer], _permute_head_cols(w_in_b[:, OFF_Q:OFF_K]),
                        gq, cos_t, sin_t, bd, tm)
    k = _qk_proj(h, _permute_head_cols(w_in_b[:, OFF_K:OFF_V]), gk, cos_t, sin_t, bd, tm, "k_proj")
    vt3 = _vt_proj(h, w_in_b[:, OFF_V:OFF_Z], tm, tk)
    z_act = _matmul(h, w_in_b[:, OFF_Z:OFF_XBC], BF16, tm, PROJ_COLS, "z_proj", _silu)
    w_dt =jnp.pad(w_in_b[:, OFF_DT:OFF_GATE], ((0, 0), (0, LANES - DT_COLS)))
    dt_b = jnp.pad(dt_bias[layer].reshape(1, DT_COLS), ((0, 0), (0, LANES - DT_COLS)))
    dt = _matmul(h, w_dt, F32, tm, LANES, "dt_proj", _softplus, dt_b)
    gates = _matmul(h, w_in_b[:, OFF_GATE:], BF16, tm, PROJ_COLS, "gate_proj", _sigmoid)

    k3 = k.reshape(n_tok // tk, tk, ATTN_WIDTH)
    lams = [v[layer].reshape(1, QK_DIM) for v in (lambda_q1, lambda_k1, lambda_q2, lambda_k2)]
    o_groups = []
    off = 0
    for b, s in seqs:
        o_groups.append(_attention(q, k3, vt3, lams, off, b, s, tq, tk))
        off += b * s

    seq_starts = []
    off = 0
    for b, s in seqs:
        seq_starts += [(off + i * s, s) for i in range(b)]
        off += b * s
    tile_start = jnp.zeros((n_tok // tm,), jnp.int32)
    tile_end = jnp.zeros((n_tok // tm,), jnp.int32)
    nc = n_tok // CHUNK
    chunk_reset = jnp.zeros((nc,), jnp.int32)
    bwd_idx = jnp.zeros((nc,), jnp.int32)
    for st, s in seq_starts:
        tile_start = tile_start.at[st // tm].set(1)
        tile_end = tile_end.at[(st + s) // tm - 1].set(1)
        c0, c1 = st // CHUNK, (st + s) // CHUNK
        chunk_reset = chunk_reset.at[c0].set(1)
        bwd_idx = bwd_idx.at[c0:c1].set(jnp.arange(c1 - 1, c0 - 1, -1, dtype=jnp.int32))
    fwd_idx = jnp.arange(nc, dtype=jnp.int32)
    w_x, w_bc = w_in_b[:, OFF_XBC:OFF_XBC + D_INNER], w_in_b[:, OFF_XBC + D_INNER:OFF_DT]
    cw, cbias = conv_w[layer], conv_b[layer]
    xs = _proj_conv(h, w_x, cw[:, :D_INNER], cbias[:D_INNER], tile_start, tile_end, F32, tm, PROJ_COLS,
                    "x_proj_conv")
    bcm = _proj_conv(h, w_bc, cw[:, D_INNER:], cbias[D_INNER:], tile_start, tile_end, BF16, tm, PROJ_COLS,
                     "bc_proj_conv")
    dtt = dt.T
    alg = a_log[layer].reshape(-1)
    dskip = jnp.repeat(d_skip[layer], SSD_HEAD_DIM).reshape(1, D_INNER)
    yb = _ssd(xs, bcm, dt, dtt, alg, bwd_idx, chunk_reset)
    y_ssd = _ssd(xs, bcm, dt, dtt, alg, fwd_idx, chunk_reset, yb, dskip)

    subg = attn_subln_g[layer].reshape(1, V_DIM)
    n_rt = N_EXPERT_GROUPS + N_EXPERTS
    w_rt = jnp.pad(jnp.concatenate([w_group[layer], w_router[layer]], axis=1), ((0, 0), (0, LANES - n_rt)))
    b_rt = jnp.pad(jnp.concatenate([b_group[layer], b_router[layer]]), (0, LANES - n_rt)).reshape(1, LANES)
    tg = min(512, min_seq)
    x1, h2b, rt = _post(o_groups[0], o_groups[1], xa, xb_in, y_ssd, z_act, gates, mod[tok_batch[::tg]], subg,
                        ssd_norm_g[layer].reshape(1, D_INNER), norm2_g[layer].reshape(1, D_MODEL),
                        w_attn_o[layer].astype(BF16), w_ssd_o[layer].astype(BF16),
                        w_out[layer].astype(BF16), w_rt, b_rt, tg)

    dest, tok_buf, blk_e, n_used = _dispatch(rt, n_tok, min(512, min_seq))
    xb = h2b[tok_buf]
    yb_e = _experts(xb, blk_e, n_used, w_gate_e[layer], w_up_e[layer], w_down_e[layer])
    g = yb_e[dest.T.reshape(-1)]

    outs = []
    off = 0
    for (b, s), (xg, _) in zip(seqs, groups):
        y = _final(x1, mod_tp, rt, g, off, b * s, tp)
        outs.append(y.reshape(xg.shape))
        off += b * s
    return tuple(outs)
```
